```python
import math
import jax, jax.numpy as jnp
from jax import lax
import numpy as np

D_MODEL = 2048
BATCH = 2
SEQ = 4096
DEPTH = 1
DEC_BATCH = 8
DEC_SEQ = 1
PAST_LEN = 16384
PAGE_SIZE = 128

D_A = D_MODEL // 2
HEAD_DIM_A = 64
N_HEADS_A = D_A // HEAD_DIM_A
DECAY_LORA = 64
AAA_LORA = 64
D_B = D_MODEL // 2
HEAD_DIM_B = 128
N_HEADS_B = D_B // HEAD_DIM_B
ROT_DIM = HEAD_DIM_B // 4
ROPE_THETA = 500000.0
MOBA_BLOCK = 256
MOBA_TOPK = 3
Q_CHUNK = 32
RW_COLS = 4 * D_A + DECAY_LORA + AAA_LORA
N_IN = RW_COLS + 4 * D_B + 2 * D_MODEL
NORM_EPS = 1e-6
GN_EPS = 64e-5
NEG = -1e30

kernel_name = "rwkv7_moba_gated_hybrid_step"


def rmsnorm(x, w):
    xf = x.astype(jnp.float32)
    y = xf * lax.rsqrt(jnp.mean(xf * xf, axis=-1, keepdims=True) + NORM_EPS)
    return (y * w.astype(jnp.float32)).astype(x.dtype)


def partial_rope(x, pos):
    half = ROT_DIM // 2
    inv = jnp.power(jnp.float32(ROPE_THETA), -jnp.arange(half, dtype=jnp.float32) * (2.0 / ROT_DIM))
    ang = pos.astype(jnp.float32)[:, None] * inv[None, :]
    cos = jnp.cos(ang)[None, :, None, :]
    sin = jnp.sin(ang)[None, :, None, :]
    xf = x.astype(jnp.float32)
    x1 = xf[..., :half]
    x2 = xf[..., half:ROT_DIM]
    out = jnp.concatenate([x1 * cos - x2 * sin, x2 * cos + x1 * sin, xf[..., ROT_DIM:]], axis=-1)
    return out.astype(x.dtype)


def wkv7_scan(s0, r, w, k, v, kk, a):
    def step(S, inp):
        r_t, w_t, k_t, v_t, kk_t, a_t = inp
        sa = jnp.einsum('bhij,bhj->bhi', S, -kk_t)
        S = (S * w_t[:, :, None, :]
             + sa[..., None] * (kk_t * a_t)[:, :, None, :]
             + v_t[..., None] * k_t[:, :, None, :])
        y = jnp.einsum('bhij,bhj->bhi', S, r_t)
        return S, y
    xs = tuple(jnp.moveaxis(t, 1, 0) for t in (r, w, k, v, kk, a))
    S, ys = lax.scan(step, s0, xs)
    return jnp.moveaxis(ys, 0, 1), S


def rwkv7_branch(z_rw, z_prev0, s0, mu, w0, w2, a0, a2, k_k, k_a, r_k, lnx_w, lnx_b):
    f32 = jnp.float32
    B, T, _ = z_rw.shape
    z_prev = jnp.concatenate([z_prev0[:, None, :], z_rw[:, :-1, :]], axis=1)
    zm = z_rw + (z_prev - z_rw) * mu
    cuts = np.cumsum([D_A, DECAY_LORA, D_A, D_A, AAA_LORA]).tolist()
    r, w_lo, k, v, a_lo, g = jnp.split(zm, cuts, axis=-1)
    w_log = -jax.nn.softplus(-(w0 + jnp.tanh(w_lo) @ w2).astype(f32)) - 0.5
    decay = jnp.exp(-jnp.exp(w_log))
    a = jax.nn.sigmoid((a0 + a_lo @ a2).astype(f32))
    hs = lambda t: t.astype(f32).reshape(B, T, N_HEADS_A, HEAD_DIM_A)
    kf = hs(k)
    kk = kf * k_k.astype(f32).reshape(N_HEADS_A, HEAD_DIM_A)
    kk = kk / jnp.maximum(jnp.sqrt(jnp.sum(kk * kk, axis=-1, keepdims=True)), 1e-12)
    ah = a.reshape(B, T, N_HEADS_A, HEAD_DIM_A)
    kf = kf * (1.0 + (ah - 1.0) * k_a.astype(f32).reshape(N_HEADS_A, HEAD_DIM_A))
    rf = hs(r)
    vf = hs(v)
    y, S = wkv7_scan(s0.astype(f32), rf, decay.reshape(B, T, N_HEADS_A, HEAD_DIM_A), kf, vf, kk, ah)
    mean = jnp.mean(y, axis=-1, keepdims=True)
    var = jnp.mean(jnp.square(y - mean), axis=-1, keepdims=True)
    yn = ((y - mean) * lax.rsqrt(var + GN_EPS)).reshape(B, T, D_A)
    yn = yn * lnx_w.astype(f32) + lnx_b.astype(f32)
    bonus = jnp.sum(rf * kf * r_k.astype(f32), axis=-1, keepdims=True) * vf
    out = (yn + bonus.reshape(B, T, D_A)) * jax.nn.silu(g.astype(f32))
    return out.astype(z_rw.dtype), S


def moba_attention(q, k_past, k_new, v_past, v_new, q_offset):
    f32 = jnp.float32
    B, Tq, H, hd = q.shape
    L = k_past.shape[1] + Tq
    nb = -(-L // MOBA_BLOCK)
    pad = nb * MOBA_BLOCK - L
    zpad = jnp.zeros((B, pad, H, hd), k_new.dtype)
    kb = jnp.concatenate([k_past, k_new, zpad], axis=1).reshape(B, nb, MOBA_BLOCK, H, hd)
    vb = jnp.concatenate([v_past, v_new, zpad.astype(v_new.dtype)], axis=1).reshape(B, nb, MOBA_BLOCK, H, hd)
    kmean = jnp.mean(kb.astype(f32), axis=2)
    qh = q.transpose(0, 2, 1, 3)
    qpos = q_offset + jnp.arange(Tq)
    qblk = qpos // MOBA_BLOCK
    gate = jnp.einsum('bhtd,bnhd->bhtn', qh.astype(f32), kmean)
    gate = jnp.where(jnp.arange(nb)[None, :] < qblk[:, None], gate, NEG)
    n_sel = min(MOBA_TOPK, nb)
    _, sel = lax.top_k(gate, n_sel)
    valid = sel < qblk[None, None, :, None]
    qc = math.gcd(Q_CHUNK, Tq)
    n_ch = Tq // qc

    def chunks(t):
        return t.reshape(B, H, n_ch, qc, t.shape[-1]).transpose(2, 0, 1, 3, 4)

    starts = q_offset + jnp.arange(n_ch) * qc
    bidx = jnp.arange(B)[:, None, None, None]
    hidx = jnp.arange(H)[None, :, None, None]
    scale = hd ** -0.5

    def step(args):
        qi, si, vi, t0 = args
        blk = t0 // MOBA_BLOCK
        kg = kb[bidx, si, :, hidx, :]
        vg = vb[bidx, si, :, hidx, :]
        ko = lax.dynamic_index_in_dim(kb, blk, axis=1, keepdims=False)
        vo = lax.dynamic_index_in_dim(vb, blk, axis=1, keepdims=False)
        s_sel = jnp.einsum('bhqd,bhqnkd->bhqnk', qi, kg).astype(f32) * scale
        s_sel = jnp.where(vi[..., None], s_sel, NEG)
        qp = t0 + jnp.arange(qc)
        kp = blk * MOBA_BLOCK + jnp.arange(MOBA_BLOCK)
        s_own = jnp.einsum('bhqd,bkhd->bhqk', qi, ko).astype(f32) * scale
        s_own = jnp.where(kp[None, :] <= qp[:, None], s_own, NEG)
        s = jnp.concatenate([s_sel.reshape(B, H, qc, n_sel * MOBA_BLOCK), s_own], axis=-1)
        p = jax.nn.softmax(s, axis=-1).astype(vb.dtype)
        p_sel = p[..., :n_sel * MOBA_BLOCK].reshape(B, H, qc, n_sel, MOBA_BLOCK)
        p_own = p[..., n_sel * MOBA_BLOCK:]
        return (jnp.einsum('bhqnk,bhqnkd->bhqd', p_sel, vg)
                + jnp.einsum('bhqk,bkhd->bhqd', p_own, vo))

    out = lax.map(step, (chunks(qh), chunks(sel), chunks(valid), starts))
    return out.transpose(1, 0, 3, 2, 4).reshape(B, Tq, H, hd)


def hybrid_layer(h, pos, q_offset, shift, s0, k_past, v_past,
                 ln_w, w_in, mu, w0, w2, a0, a2, k_k, k_a, r_k, lnx_w, lnx_b, p_a, p_b, w_o):
    B, T, _ = h.shape
    xn = rmsnorm(h, ln_w)
    z = xn @ w_in
    z_rw = z[..., :RW_COLS]
    z_prev0 = shift.astype(xn.dtype) @ w_in[:, :RW_COLS]
    o_a, s_new = rwkv7_branch(z_rw, z_prev0, s0, mu, w0, w2, a0, a2, k_k, k_a, r_k, lnx_w, lnx_b)
    zq, zk, zv, zg_b = jnp.split(z[..., RW_COLS:RW_COLS + 4 * D_B], 4, axis=-1)
    z_gates = z[..., RW_COLS + 4 * D_B:]
    q = partial_rope(zq.reshape(B, T, N_HEADS_B, HEAD_DIM_B), pos)
    k = partial_rope(zk.reshape(B, T, N_HEADS_B, HEAD_DIM_B), pos)
    v = zv.reshape(B, T, N_HEADS_B, HEAD_DIM_B)
    attn = moba_attention(q, k_past.astype(k.dtype), k, v_past.astype(v.dtype), v, q_offset)
    o_b = attn.reshape(B, T, D_B) * jax.nn.silu(zg_b)
    gate_a = jax.nn.sigmoid(z_gates[..., :D_MODEL].astype(jnp.float32)).astype(h.dtype)
    gate_b = jax.nn.sigmoid(z_gates[..., D_MODEL:].astype(jnp.float32)).astype(h.dtype)
    merged = gate_a * (o_a @ p_a) + gate_b * (o_b @ p_b)
    h = h + merged @ w_o
    return h, xn[:, -1, :], s_new, k, v


def setup_inputs(seed: int = 0) -> dict:
    key = jax.random.key(seed)
    ks = jax.random.split(key, 24)
    f32 = jnp.float32
    nrm = lambda k, shape, s: jax.random.normal(k, shape, f32) * s
    n_pages = PAST_LEN // PAGE_SIZE
    n_used = DEC_BATCH * n_pages
    n_pool = n_used + max(1, n_used // 4)
    page_table = jax.random.permutation(ks[6], n_pool)[:n_used].reshape(DEC_BATCH, n_pages).astype(jnp.int32)
    return {
        "x_prompt": nrm(ks[0], (BATCH, SEQ, D_MODEL), 1.0),
        "x_sample": nrm(ks[1], (DEC_BATCH, DEC_SEQ, D_MODEL), 1.0),
        "state_shift": nrm(ks[2], (DEPTH, DEC_BATCH, D_MODEL), 1.0),
        "state_wkv": nrm(ks[3], (DEPTH, DEC_BATCH, N_HEADS_A, HEAD_DIM_A, HEAD_DIM_A), 0.3),
        "cache_k": nrm(ks[4], (DEPTH, n_pool, PAGE_SIZE, N_HEADS_B, HEAD_DIM_B), 1.0),
        "cache_v": nrm(ks[5], (DEPTH, n_pool, PAGE_SIZE, N_HEADS_B, HEAD_DIM_B), 1.0),
        "page_table": page_table,
        "ln_w": 1.0 + nrm(ks[7], (DEPTH, D_MODEL), 0.02),
        "w_in": nrm(ks[8], (DEPTH, D_MODEL, N_IN), D_MODEL ** -0.5),
        "mu": jax.random.uniform(ks[9], (DEPTH, RW_COLS), f32),
        "w0": nrm(ks[10], (DEPTH, D_A), 0.5) - 0.5,
        "w2": nrm(ks[11], (DEPTH, DECAY_LORA, D_A), DECAY_LORA ** -0.5),
        "a0": nrm(ks[12], (DEPTH, D_A), 0.1),
        "a2": nrm(ks[13], (DEPTH, AAA_LORA, D_A), AAA_LORA ** -0.5),
        "k_k": 0.85 + nrm(ks[14], (DEPTH, D_A), 0.02),
        "k_a": 1.0 + nrm(ks[15], (DEPTH, D_A), 0.02),
        "r_k": nrm(ks[16], (DEPTH, N_HEADS_A, HEAD_DIM_A), 0.1),
        "lnx_w": 1.0 + nrm(ks[17], (DEPTH, D_A), 0.02),
        "lnx_b": nrm(ks[18], (DEPTH, D_A), 0.02),
        "p_a": nrm(ks[19], (DEPTH, D_A, D_MODEL), D_A ** -0.5),
        "p_b": nrm(ks[20], (DEPTH, D_B, D_MODEL), D_B ** -0.5),
        "w_o": nrm(ks[21], (DEPTH, D_MODEL, D_MODEL), D_MODEL ** -0.5),
        "final_w": 1.0 + nrm(ks[22], (D_MODEL,), 0.02),
    }


def reference(x_prompt, x_sample, state_shift, state_wkv, cache_k, cache_v, page_table,
              ln_w, w_in, mu, w0, w2, a0, a2, k_k, k_a, r_k, lnx_w, lnx_b, p_a, p_b, w_o, final_w):
    B, T, _ = x_prompt.shape
    DB, TS, _ = x_sample.shape
    n_pages = page_table.shape[1]
    past = n_pages * cache_k.shape[2]
    pos_p = jnp.arange(T)
    pos_s = past + jnp.arange(TS)
    hp, hs = x_prompt, x_sample
    sh_p, wk_p, kp_l, vp_l = [], [], [], []
    sh_s, wk_s, ks_l, vs_l = [], [], [], []
    for l in range(DEPTH):
        lw = (ln_w[l], w_in[l], mu[l], w0[l], w2[l], a0[l], a2[l], k_k[l], k_a[l], r_k[l],
              lnx_w[l], lnx_b[l], p_a[l], p_b[l], w_o[l])
        empty = jnp.zeros((B, 0, N_HEADS_B, HEAD_DIM_B), x_prompt.dtype)
        hp, s1, s2, s3, s4 = hybrid_layer(
            hp, pos_p, 0, jnp.zeros((B, D_MODEL), x_prompt.dtype),
            jnp.zeros((B, N_HEADS_A, HEAD_DIM_A, HEAD_DIM_A), jnp.float32), empty, empty, *lw)
        sh_p.append(s1.astype(state_shift.dtype))
        wk_p.append(s2.astype(state_wkv.dtype))
        kp_l.append(s3.astype(cache_k.dtype))
        vp_l.append(s4.astype(cache_v.dtype))
        k_past = cache_k[l][page_table].reshape(DB, past, N_HEADS_B, HEAD_DIM_B)
        v_past = cache_v[l][page_table].reshape(DB, past, N_HEADS_B, HEAD_DIM_B)
        hs, t1, t2, t3, t4 = hybrid_layer(
            hs, pos_s, past, state_shift[l], state_wkv[l], k_past, v_past, *lw)
        sh_s.append(t1.astype(state_shift.dtype))
        wk_s.append(t2.astype(state_wkv.dtype))
        ks_l.append(t3.astype(cache_k.dtype))
        vs_l.append(t4.astype(cache_v.dtype))
    y_prompt = rmsnorm(hp, final_w)
    y_sample = rmsnorm(hs, final_w)
    shift_prompt = jnp.stack(sh_p)
    wkv_prompt = jnp.stack(wk_p)
    k_prompt = jnp.stack(kp_l)
    v_prompt = jnp.stack(vp_l)
    shift_sample = jnp.stack(sh_s)
    wkv_sample = jnp.stack(wk_s)
    k_sample = jnp.stack(ks_l)
    v_sample = jnp.stack(vs_l)
    return (y_prompt, y_sample, shift_prompt, wkv_prompt, k_prompt, v_prompt,
            shift_sample, wkv_sample, k_sample, v_sample)
```

```python
import functools

import jax
import jax.numpy as jnp
from jax import lax
from jax.experimental import pallas as pl
from jax.experimental.pallas import tpu as pltpu

F32 = jnp.float32
BF16 = jnp.bfloat16

D_MODEL = 2048
D_A = D_MODEL // 2
HEAD_A = 64
N_HEADS_A = D_A // HEAD_A
LORA = 64
D_B = D_MODEL // 2
HEAD_B = 128
N_HEADS_B = D_B // HEAD_B
ROT_DIM = HEAD_B // 4
ROPE_THETA = 500000.0
MOBA_BLOCK = 256
MOBA_TOPK = 3
NORM_EPS = 1e-6
GN_EPS = 64e-5
NEG = -1e30

COL_R, COL_K, COL_V, COL_G = 0, D_A, 2 * D_A, 3 * D_A
COL_Q, COL_KB, COL_VB, COL_GB = 4 * D_A, 4 * D_A + D_B, 4 * D_A + 2 * D_B, 4 * D_A + 3 * D_B
COL_GA = 4 * D_A + 4 * D_B
COL_GM = COL_GA + D_MODEL
N_MAIN = COL_GM + D_MODEL

GROUP = 256
HEADS_PER_GROUP = GROUP // HEAD_A
CHUNK = 64
assert CHUNK == HEAD_A
VMEM_LIMIT = 56 * 1024 * 1024


def _cparams(sem):
    return pltpu.CompilerParams(dimension_semantics=sem, vmem_limit_bytes=VMEM_LIMIT)


def _bdot(a, b):
    return jnp.dot(a.astype(BF16), b.astype(BF16), preferred_element_type=F32)


def _bdot_nt(a, b):
    return lax.dot_general(a.astype(BF16), b.astype(BF16), (((1,), (1,)), ((), ())),
                           preferred_element_type=F32)


def _bdot_tn(a, b):
    return lax.dot_general(a.astype(BF16), b.astype(BF16), (((0,), (0,)), ((), ())),
                           preferred_element_type=F32)


def _split3(x):
    hi = x.astype(BF16)
    r1 = x - hi.astype(F32)
    mid = r1.astype(BF16)
    lo = (r1 - mid.astype(F32)).astype(BF16)
    return hi, mid, lo


def _sigmoid(x):
    return 1.0 / (1.0 + jnp.exp(-x))


def _silu(x):
    return x * _sigmoid(x)


def _softplus(x):
    return jnp.maximum(x, 0.0) + jnp.log(1.0 + jnp.exp(-jnp.abs(x)))


def _rmsnorm_rows_kernel(x_ref, w_ref, o_ref):
    x = x_ref[...]
    ms = jnp.mean(x * x, axis=-1, keepdims=True)
    o_ref[...] = (x * lax.rsqrt(ms + NORM_EPS)) * w_ref[...]


def rmsnorm_rows(x, w):
    return pl.pallas_call(
        _rmsnorm_rows_kernel,
        out_shape=jax.ShapeDtypeStruct(x.shape, F32),
        name="rmsnorm_rows",
    )(x, w.reshape(1, -1))


def _proj_kernel(x_ref, lnw_ref, w_ref, wwa_ref, z_ref, zwa_ref, xn_ref, *, normalize):
    @pl.when(pl.program_id(1) == 0)
    def _():
        x = x_ref[...]
        if normalize:
            ms = jnp.mean(x * x, axis=-1, keepdims=True)
            x = (x * lax.rsqrt(ms + NORM_EPS)) * lnw_ref[...]
        xn_ref[...] = x.astype(BF16)
        zwa_ref[...] = jnp.dot(xn_ref[...], wwa_ref[...], preferred_element_type=F32)

    z_ref[...] = jnp.dot(xn_ref[...], w_ref[...], preferred_element_type=F32)


def input_projection(x, ln_w, w_main, w_wa, *, normalize):
    m = x.shape[0]
    tm = min(m, 1024)
    tn = 1024
    assert m % tm == 0 and N_MAIN % tn == 0
    return pl.pallas_call(
        functools.partial(_proj_kernel, normalize=normalize),
        grid=(m // tm, N_MAIN // tn),
        in_specs=[
            pl.BlockSpec((tm, D_MODEL), lambda i, j: (i, 0)),
            pl.BlockSpec((1, D_MODEL), lambda i, j: (0, 0)),
            pl.BlockSpec((D_MODEL, tn), lambda i, j: (0, j)),
            pl.BlockSpec((D_MODEL, 2 * LORA), lambda i, j: (0, 0)),
        ],
        out_specs=[
            pl.BlockSpec((tm, tn), lambda i, j: (i, j)),
            pl.BlockSpec((tm, 2 * LORA), lambda i, j: (i, 0)),
        ],
        out_shape=[
            jax.ShapeDtypeStruct((m, N_MAIN), F32),
            jax.ShapeDtypeStruct((m, 2 * LORA), F32),
        ],
        scratch_shapes=[pltpu.VMEM((tm, D_MODEL), BF16)],
        compiler_params=_cparams(("arbitrary", "arbitrary")),
        name="input_projection",
    )(x, ln_w.reshape(1, -1), w_main, w_wa)


def _head_mask(n):
    r = lax.broadcasted_iota(jnp.int32, (n, n), 0) // HEAD_A
    c = lax.broadcasted_iota(jnp.int32, (n, n), 1) // HEAD_A
    return r == c


def _segsum(x, ones_bd):
    hi = x.astype(BF16)
    lo = (x - hi.astype(F32)).astype(BF16)
    return (jnp.dot(hi, ones_bd, preferred_element_type=F32)
            + jnp.dot(lo, ones_bd, preferred_element_type=F32))


def _rwkv_prep(zm_r, zm_k, zm_v, zm_wa, w0, a0, k_k, k_a, w2p, a2p, ones_bd):
    lane = lax.broadcasted_iota(jnp.int32, zm_wa.shape, 1)
    lora_in = jnp.where(lane < LORA, jnp.tanh(zm_wa), zm_wa)
    ww = _bdot(lora_in, w2p)
    aa = _bdot(lora_in, a2p)
    w_log = -_softplus(-(w0 + ww)) - 0.5
    logw = -jnp.exp(w_log)
    a = _sigmoid(a0 + aa)
    kk = zm_k * k_k
    ss = _segsum(kk * kk, ones_bd)
    kk = kk / jnp.maximum(jnp.sqrt(ss), 1e-12)
    kf = zm_k * (1.0 + (a - 1.0) * k_a)
    return zm_r, kf, zm_v, kk, kk * a, logw


def _rwkv_post(y, r, kf, v, zm_g, r_k, lnx_w, lnx_b, ones_bd):
    inv_n = 1.0 / HEAD_A
    mean = _segsum(y, ones_bd) * inv_n
    d = y - mean
    var = _segsum(d * d, ones_bd) * inv_n
    yn = d * lax.rsqrt(var + GN_EPS) * lnx_w + lnx_b
    bonus = _segsum(r * kf * r_k, ones_bd) * v
    return (yn + bonus) * _silu(zm_g)


def _stack_heads(x, lane_head):
    return jnp.concatenate(
        [jnp.where(lane_head == h, x, 0.0) for h in range(HEADS_PER_GROUP)], axis=0)


def _rwkv_chunk(S, r, kf, v, kk, b, logw, cl, consts):
    strict, incl, eye, lane_head, bd_mask = consts
    c = CHUNK
    cl_last = cl[c - 1:c, :]
    e_pos = jnp.exp(cl)
    e_prev = jnp.exp(cl - logw)
    e_neg = jnp.exp(-cl)
    e_end = jnp.exp(cl_last - cl)
    rp = r * e_pos
    kkp = kk * e_prev
    bq = b * e_neg
    kq = kf * e_neg

    lhs = jnp.concatenate([kkp, rp], axis=0).astype(BF16)
    wt = jnp.concatenate([_stack_heads(bq, lane_head), _stack_heads(kq, lane_head)], axis=0)
    att = _bdot_nt(lhs, wt)
    a_ab = jnp.where(strict, att[:c, :4 * c], 0.0)
    a_ak = jnp.where(strict, att[:c, 4 * c:], 0.0)
    m_rb = jnp.where(incl, att[c:, :4 * c], 0.0)
    m_rk = jnp.where(incl, att[c:, 4 * c:], 0.0)

    sh = _bdot_nt(lhs, S)
    v_bd = _stack_heads(v, lane_head).astype(BF16)
    rhs = sh[:c] + _bdot(a_ak, v_bd)

    def bd(p):
        return jnp.where(bd_mask, jnp.concatenate([p] * HEADS_PER_GROUP, axis=0), 0.0).astype(BF16)

    p = a_ab
    x = eye - a_ab
    p = _bdot(p, bd(p))
    for _ in range(4):
        px = _bdot(jnp.concatenate([p, x], axis=0), bd(p))
        p = px[:c]
        x = x + px[c:]
    x = x + _bdot(x, bd(p))

    u = -_bdot(x, _stack_heads(rhs, lane_head))
    u_bd = _stack_heads(u, lane_head).astype(BF16)
    y = sh[c:] + _bdot(jnp.concatenate([m_rb, m_rk], axis=1),
                       jnp.concatenate([u_bd, v_bd], axis=0))
    upd = _bdot_tn(jnp.concatenate([u, v], axis=0),
                   jnp.concatenate([b * e_end, kf * e_end], axis=0))
    s_new = S * jnp.exp(cl_last) + jnp.where(bd_mask, upd, 0.0)
    return y, s_new


def _rwkv_prompt_kernel(zr_ref, zk_ref, zv_ref, zg_ref, zwa_ref,
                        mur_ref, muk_ref, muv_ref, mug_ref, muwa_ref,
                        w0_ref, a0_ref, kk_ref, ka_ref, rk_ref, lnw_ref, lnb_ref,
                        w2_ref, a2_ref,
                        o_ref, s_out_ref,
                        s_ref, pr_ref, pk_ref, pv_ref, pg_ref, pwa_ref):
    t = pl.program_id(2)
    tr = zr_ref.shape[0]

    @pl.when(t == 0)
    def _():
        s_ref[...] = jnp.zeros_like(s_ref)
        pr_ref[...] = jnp.zeros_like(pr_ref)
        pk_ref[...] = jnp.zeros_like(pk_ref)
        pv_ref[...] = jnp.zeros_like(pv_ref)
        pg_ref[...] = jnp.zeros_like(pg_ref)
        pwa_ref[...] = jnp.zeros_like(pwa_ref)

    def shifted(z_ref, prev_ref, mu_ref):
        z = z_ref[...]
        row = lax.broadcasted_iota(jnp.int32, z.shape, 0)
        prev = jnp.where(row == 0, prev_ref[...], pltpu.roll(z, 1, 0))
        prev_ref[...] = z[tr - 1:tr, :]
        return z + (prev - z) * mu_ref[...]

    zm_r = shifted(zr_ref, pr_ref, mur_ref)
    zm_k = shifted(zk_ref, pk_ref, muk_ref)
    zm_v = shifted(zv_ref, pv_ref, muv_ref)
    zm_g = shifted(zg_ref, pg_ref, mug_ref)
    zm_wa = shifted(zwa_ref, pwa_ref, muwa_ref)

    bd_mask = _head_mask(GROUP)
    ones_bd = jnp.where(bd_mask, 1.0, 0.0).astype(BF16)
    r, kf, v, kk, b, logw = _rwkv_prep(zm_r, zm_k, zm_v, zm_wa, w0_ref[...], a0_ref[...],
                                       kk_ref[...], ka_ref[...], w2_ref[...], a2_ref[...], ones_bd)

    ri = lax.broadcasted_iota(jnp.int32, (tr, tr), 0)
    ci = lax.broadcasted_iota(jnp.int32, (tr, tr), 1)
    tri = jnp.where((ri // CHUNK == ci // CHUNK) & (ci <= ri), 1.0, 0.0).astype(BF16)
    cl = sum(jnp.dot(tri, piece, preferred_element_type=F32) for piece in _split3(logw))

    rc = lax.broadcasted_iota(jnp.int32, (CHUNK, GROUP), 0)
    lc = lax.broadcasted_iota(jnp.int32, (CHUNK, GROUP), 1)
    sc = lc % CHUNK
    consts = (sc < rc, sc <= rc, jnp.where(sc == rc, 1.0, 0.0), lc // HEAD_A, bd_mask)

    S = s_ref[...]
    ys = []
    for ch in range(tr // CHUNK):
        sl = slice(ch * CHUNK, (ch + 1) * CHUNK)
        y, S = _rwkv_chunk(S, r[sl], kf[sl], v[sl], kk[sl], b[sl], logw[sl], cl[sl], consts)
        ys.append(y)
    s_ref[...] = S
    y = jnp.concatenate(ys, axis=0)

    o_ref[...] = _rwkv_post(y, r, kf, v, zm_g, rk_ref[...], lnw_ref[...], lnb_ref[...],
                            ones_bd).astype(o_ref.dtype)

    @pl.when(t == pl.num_programs(2) - 1)
    def _():
        for h in range(HEADS_PER_GROUP):
            s_out_ref[h] = S[h * HEAD_A:(h + 1) * HEAD_A, h * HEAD_A:(h + 1) * HEAD_A]


def rwkv_prompt(z_main, z_wa, batch, seq, mu_main, mu_wa, vecs, w2p, a2p):
    w0, a0, k_k, k_a, r_k, lnx_w, lnx_b = vecs
    tr = min(seq, 256)
    assert seq % tr == 0 and tr % CHUNK == 0
    nt = seq // tr
    ng = D_A // GROUP

    def zspec(col):
        return pl.BlockSpec((tr, GROUP), lambda b, g, t, c=col // GROUP: (b * nt + t, c + g))

    def vspec(col=0):
        return pl.BlockSpec((1, GROUP), lambda b, g, t, c=col // GROUP: (0, c + g))

    in_specs = [
        zspec(COL_R), zspec(COL_K), zspec(COL_V), zspec(COL_G),
        pl.BlockSpec((tr, 2 * LORA), lambda b, g, t: (b * nt + t, 0)),
        vspec(COL_R), vspec(COL_K), vspec(COL_V), vspec(COL_G),
        pl.BlockSpec((1, 2 * LORA), lambda b, g, t: (0, 0)),
        vspec(), vspec(), vspec(), vspec(), vspec(), vspec(), vspec(),
        pl.BlockSpec((2 * LORA, GROUP), lambda b, g, t: (0, g)),
        pl.BlockSpec((2 * LORA, GROUP), lambda b, g, t: (0, g)),
    ]
    out_specs = [
        pl.BlockSpec((tr, GROUP), lambda b, g, t: (b * nt + t, g)),
        pl.BlockSpec((HEADS_PER_GROUP, HEAD_A, HEAD_A), lambda b, g, t: (b * ng + g, 0, 0)),
    ]
    row = lambda n: pltpu.VMEM((1, n), F32)
    return pl.pallas_call(
        _rwkv_prompt_kernel,
        grid=(batch, ng, nt),
        in_specs=in_specs,
        out_specs=out_specs,
        out_shape=[
            jax.ShapeDtypeStruct((batch * seq, D_A), BF16),
            jax.ShapeDtypeStruct((batch * N_HEADS_A, HEAD_A, HEAD_A), F32),
        ],
        scratch_shapes=[pltpu.VMEM((GROUP, GROUP), F32),
                        row(GROUP), row(GROUP), row(GROUP), row(GROUP), row(2 * LORA)],
        compiler_params=_cparams(("arbitrary", "arbitrary", "arbitrary")),
        name="rwkv_prompt",
    )(z_main, z_main, z_main, z_main, z_wa,
      mu_main, mu_main, mu_main, mu_main, mu_wa,
      w0, a0, k_k, k_a, r_k, lnx_w, lnx_b, w2p, a2p)


def _rwkv_rows_kernel(z_ref, zp_ref, zwa_ref, zpwa_ref, mu_ref, muwa_ref,
                      w0_ref, a0_ref, kk_ref, ka_ref, w2_ref, a2_ref,
                      r_ref, kf_ref, v_ref, kkn_ref, b_ref, w_ref, g_ref):
    def shifted(z, zp, mu):
        return z + (zp - z) * mu

    mu = mu_ref[...]
    z = z_ref[...]
    zp = zp_ref[...]
    zm = [shifted(z[:, c:c + D_A], zp[:, c:c + D_A], mu[:, c:c + D_A])
          for c in (COL_R, COL_K, COL_V, COL_G)]
    zm_wa = shifted(zwa_ref[...], zpwa_ref[...], muwa_ref[...])
    ones_bd = jnp.where(_head_mask(D_A), 1.0, 0.0).astype(BF16)
    r, kf, v, kk, b, logw = _rwkv_prep(zm[0], zm[1], zm[2], zm_wa, w0_ref[...], a0_ref[...],
                                       kk_ref[...], ka_ref[...], w2_ref[...], a2_ref[...], ones_bd)
    r_ref[...] = r
    kf_ref[...] = kf
    v_ref[...] = v
    kkn_ref[...] = kk
    b_ref[...] = b
    w_ref[...] = jnp.exp(logw)
    g_ref[...] = zm[3]


def rwkv_rows(z_rw, zp_rw, z_wa, zp_wa, mu_rw, mu_wa, w0, a0, k_k, k_a, w2p, a2p):
    n = z_rw.shape[0]
    out = jax.ShapeDtypeStruct((n, D_A), F32)
    return pl.pallas_call(
        _rwkv_rows_kernel,
        out_shape=[out] * 7,
        compiler_params=pltpu.CompilerParams(vmem_limit_bytes=VMEM_LIMIT),
        name="rwkv_rows",
    )(z_rw, zp_rw, z_wa, zp_wa, mu_rw, mu_wa, w0, a0, k_k, k_a, w2p, a2p)


def _rwkv_step_kernel(s_ref, w_ref, kk_ref, b_ref, kf_ref, r_ref, v_ref, g_ref,
                      rk_ref, lnw_ref, lnb_ref, s_out_ref, o_ref):
    S = s_ref[...]
    w = w_ref[...]
    kk = kk_ref[...]
    b = b_ref[...]
    kf = kf_ref[...]
    r = r_ref[...]
    v = v_ref[...]
    sa = -jnp.sum(S * kk, axis=-1, keepdims=True)
    S = S * w + sa * b + v * kf
    s_out_ref[...] = S
    y = jnp.sum(S * r, axis=-1, keepdims=True)
    mean = jnp.mean(y, axis=1, keepdims=True)
    d = y - mean
    var = jnp.mean(d * d, axis=1, keepdims=True)
    yn = d * lax.rsqrt(var + GN_EPS) * lnw_ref[...] + lnb_ref[...]
    bonus = jnp.sum(r * kf * rk_ref[...], axis=-1, keepdims=True) * v
    o_ref[...] = (yn + bonus) * _silu(g_ref[...])


def rwkv_step(state, w, kk, b, kf, r, v, g, r_k, lnx_w, lnx_b):
    n = state.shape[0]
    h = N_HEADS_A
    sq = pl.Squeezed()
    lane_vec = pl.BlockSpec((sq, h, 1, HEAD_A), lambda i: (i, 0, 0, 0))
    col_vec = pl.BlockSpec((sq, h, HEAD_A, 1), lambda i: (i, 0, 0, 0))
    return pl.pallas_call(
        _rwkv_step_kernel,
        grid=(n,),
        in_specs=[pl.BlockSpec((sq, h, HEAD_A, HEAD_A), lambda i: (i, 0, 0, 0)),
                  lane_vec, lane_vec, lane_vec, lane_vec, lane_vec, col_vec, col_vec,
                  pl.BlockSpec((h, 1, HEAD_A), lambda i: (0, 0, 0)),
                  pl.BlockSpec((h, HEAD_A, 1), lambda i: (0, 0, 0)),
                  pl.BlockSpec((h, HEAD_A, 1), lambda i: (0, 0, 0))],
        out_specs=[pl.BlockSpec((sq, h, HEAD_A, HEAD_A), lambda i: (i, 0, 0, 0)), col_vec],
        out_shape=[jax.ShapeDtypeStruct((n, h, HEAD_A, HEAD_A), F32),
                   jax.ShapeDtypeStruct((n, h, HEAD_A, 1), F32)],
        compiler_params=_cparams(("arbitrary",)),
        name="rwkv_step",
    )(state, w, kk, b, kf, r, v, g, r_k, lnx_w, lnx_b)


def _rope(x, cos_t, sin_t, lane):
    partner = jnp.where(lane < ROT_DIM // 2, pltpu.roll(x, HEAD_B - ROT_DIM // 2, 1),
                        pltpu.roll(x, ROT_DIM // 2, 1))
    return x * cos_t + partner * sin_t


def _attn_prep_kernel(zq_ref, zk_ref, zv_ref, cos_ref, sin_ref,
                      q_ref, kh_ref, vh_ref, ko_ref, vo_ref, km_ref):
    cos_t = cos_ref[...]
    sin_t = sin_ref[...]
    lane = lax.broadcasted_iota(jnp.int32, cos_t.shape, 1)
    vo_ref[...] = zv_ref[...]
    for h in range(N_HEADS_B):
        sl = slice(h * HEAD_B, (h + 1) * HEAD_B)
        q = _rope(zq_ref[:, sl], cos_t, sin_t, lane)
        k = _rope(zk_ref[:, sl], cos_t, sin_t, lane)
        q_ref[h] = q.astype(q_ref.dtype)
        kh_ref[h] = k.astype(kh_ref.dtype)
        vh_ref[h] = zv_ref[:, sl].astype(vh_ref.dtype)
        ko_ref[:, sl] = k
        km_ref[:, sl] = jnp.mean(k, axis=0, keepdims=True)


def attn_prep(z_main, cos_t, sin_t, batch, seq):
    tr = min(seq, MOBA_BLOCK)
    nt = seq // tr
    m = batch * seq
    sq = pl.Squeezed()
    zspec = lambda col: pl.BlockSpec((tr, D_B), lambda b, t, c=col // D_B: (b * nt + t, c))
    tab = pl.BlockSpec((tr, HEAD_B), lambda b, t: (t, 0))
    hm = pl.BlockSpec((sq, N_HEADS_B, tr, HEAD_B), lambda b, t: (b, 0, t, 0))
    rows = pl.BlockSpec((tr, D_B), lambda b, t: (b * nt + t, 0))
    hm_shape = jax.ShapeDtypeStruct((batch, N_HEADS_B, seq, HEAD_B), BF16)
    return pl.pallas_call(
        _attn_prep_kernel,
        grid=(batch, nt),
        in_specs=[zspec(COL_Q), zspec(COL_KB), zspec(COL_VB), tab, tab],
        out_specs=[hm, hm, hm, rows, rows,
                   pl.BlockSpec((sq, 1, D_B), lambda b, t: (b * nt + t, 0, 0))],
        out_shape=[hm_shape, hm_shape, hm_shape,
                   jax.ShapeDtypeStruct((m, D_B), F32), jax.ShapeDtypeStruct((m, D_B), F32),
                   jax.ShapeDtypeStruct((batch * nt, 1, D_B), F32)],
        compiler_params=_cparams(("arbitrary", "arbitrary")),
        name="attn_prep",
    )(z_main, z_main, z_main, cos_t, sin_t)


def _moba_prompt_kernel(q_ref, k_ref, v_ref, km_ref, zg_ref, o_ref, sel_ref):
    qb = pl.program_id(2)
    q = q_ref[...]
    tq = q.shape[0]
    nb = km_ref.shape[0]
    scale = HEAD_B ** -0.5

    gate = sum(lax.dot_general(q, piece, (((1,), (1,)), ((), ())), preferred_element_type=F32)
               for piece in _split3(km_ref[...]))
    blk = lax.broadcasted_iota(jnp.int32, (tq, nb), 1)
    past = blk < qb
    for n in range(nb):
        g_n = gate[:, n:n + 1]
        beats = (gate > g_n) | ((gate == g_n) & (blk < n))
        cnt = jnp.sum(jnp.where(beats & past, 1.0, 0.0), axis=1, keepdims=True)
        sel_ref[n] = jnp.where(cnt < MOBA_TOPK, 1.0, 0.0)

    def attend(s, v_blk, carry):
        m, l, acc = carry
        m_new = jnp.maximum(m, jnp.max(s, axis=1, keepdims=True))
        alpha = jnp.exp(m - m_new)
        p = jnp.exp(s - m_new)
        l = alpha * l + jnp.sum(p, axis=1, keepdims=True)
        acc = alpha * acc + jnp.dot(p.astype(BF16), v_blk, preferred_element_type=F32)
        return m_new, l, acc

    def body(n, carry):
        start = pl.multiple_of(n * MOBA_BLOCK, MOBA_BLOCK)
        k_blk = k_ref[pl.ds(start, MOBA_BLOCK), :]
        v_blk = v_ref[pl.ds(start, MOBA_BLOCK), :]
        s = lax.dot_general(q, k_blk, (((1,), (1,)), ((), ())), preferred_element_type=F32) * scale
        s = jnp.where(sel_ref[n] > 0.0, s, NEG)
        return attend(s, v_blk, carry)

    init = (jnp.full((tq, 1), NEG, F32), jnp.zeros((tq, 1), F32), jnp.zeros((tq, HEAD_B), F32))
    carry = lax.fori_loop(0, qb, body, init)

    start = pl.multiple_of(qb * MOBA_BLOCK, MOBA_BLOCK)
    k_blk = k_ref[pl.ds(start, tq), :]
    v_blk = v_ref[pl.ds(start, tq), :]
    s = lax.dot_general(q, k_blk, (((1,), (1,)), ((), ())), preferred_element_type=F32) * scale
    ri = lax.broadcasted_iota(jnp.int32, (tq, tq), 0)
    ci = lax.broadcasted_iota(jnp.int32, (tq, tq), 1)
    s = jnp.where(ci <= ri, s, NEG)
    _, l, acc = attend(s, v_blk, carry)
    o_ref[...] = ((acc / l) * _silu(zg_ref[...])).astype(o_ref.dtype)


def moba_prompt(q_hm, k_hm, v_hm, kmean, z_main, batch, seq):
    tq = min(seq, MOBA_BLOCK)
    nb = seq // tq
    sq = pl.Squeezed()
    return pl.pallas_call(
        _moba_prompt_kernel,
        grid=(batch, N_HEADS_B, nb),
        in_specs=[
            pl.BlockSpec((sq, sq, tq, HEAD_B), lambda b, h, i: (b, h, i, 0)),
            pl.BlockSpec((sq, sq, seq, HEAD_B), lambda b, h, i: (b, h, 0, 0)),
            pl.BlockSpec((sq, sq, seq, HEAD_B), lambda b, h, i: (b, h, 0, 0)),
            pl.BlockSpec((sq, nb, HEAD_B), lambda b, h, i: (b, 0, h)),
            pl.BlockSpec((tq, HEAD_B), lambda b, h, i: (b * nb + i, COL_GB // HEAD_B + h)),
        ],
        out_specs=pl.BlockSpec((tq, HEAD_B), lambda b, h, i: (b * nb + i, h)),
        out_shape=jax.ShapeDtypeStruct((batch * seq, D_B), BF16),
        scratch_shapes=[pltpu.VMEM((nb, tq, 1), F32)],
        compiler_params=_cparams(("arbitrary", "arbitrary", "arbitrary")),
        name="moba_prompt",
    )(q_hm, k_hm, v_hm, kmean, z_main)


def _cache_kmean_kernel(pt_ref, k_ref, o_ref, *, pages_per_block):
    p = pl.program_id(1)
    part = jnp.sum(k_ref[...], axis=0, keepdims=True) * (1.0 / MOBA_BLOCK)

    @pl.when(p % pages_per_block == 0)
    def _():
        o_ref[...] = part

    @pl.when(p % pages_per_block != 0)
    def _():
        o_ref[...] = o_ref[...] + part


def cache_kmean(cache_k, page_table):
    n, n_pages = page_table.shape
    page = cache_k.shape[1]
    ppb = MOBA_BLOCK // page
    sq = pl.Squeezed()
    return pl.pallas_call(
        functools.partial(_cache_kmean_kernel, pages_per_block=ppb),
        grid_spec=pltpu.PrefetchScalarGridSpec(
            num_scalar_prefetch=1,
            grid=(n, n_pages),
            in_specs=[pl.BlockSpec((sq, page, D_B), lambda b, p, pt: (pt[b, p], 0, 0))],
            out_specs=pl.BlockSpec((sq, sq, 1, D_B), lambda b, p, pt: (b, p // ppb, 0, 0)),
        ),
        out_shape=jax.ShapeDtypeStruct((n, n_pages // ppb, 1, D_B), F32),
        compiler_params=_cparams(("arbitrary", "arbitrary")),
        name="cache_kmean",
    )(page_table, cache_k)


def _sample_select_kernel(q_ref, km_ref, sel_ref):
    nb = km_ref.shape[0]
    ri = lax.broadcasted_iota(jnp.int32, (nb, nb), 0)
    ci = lax.broadcasted_iota(jnp.int32, (nb, nb), 1)
    lane = lax.broadcasted_iota(jnp.int32, (1, 128), 1)
    blk_row = lax.broadcasted_iota(jnp.int32, (1, nb), 1).astype(F32)
    for h in range(N_HEADS_B):
        km = km_ref[:, h * HEAD_B:(h + 1) * HEAD_B]
        g_col = jnp.sum(km * q_ref[h:h + 1, :], axis=1, keepdims=True)
        g_row = jnp.sum(jnp.where(ri == ci, g_col, 0.0), axis=0, keepdims=True)
        beats = (g_col > g_row) | ((g_col == g_row) & (ri < ci))
        rank = jnp.sum(jnp.where(beats, 1.0, 0.0), axis=0, keepdims=True)
        out = jnp.zeros((1, 128), F32)
        for r in range(MOBA_TOPK):
            idx = jnp.sum(jnp.where(rank == float(r), blk_row, 0.0), axis=1, keepdims=True)
            out = jnp.where(lane == r, idx, out)
        sel_ref[h:h + 1, :] = out.astype(jnp.int32)


def sample_select(q_s, kmean_s):
    n, nb, _ = kmean_s.shape
    sq = pl.Squeezed()
    return pl.pallas_call(
        _sample_select_kernel,
        grid=(n,),
        in_specs=[pl.BlockSpec((sq, N_HEADS_B, HEAD_B), lambda b: (b, 0, 0)),
                  pl.BlockSpec((sq, nb, D_B), lambda b: (b, 0, 0))],
        out_specs=pl.BlockSpec((sq, N_HEADS_B, 128), lambda b: (b, 0, 0)),
        out_shape=jax.ShapeDtypeStruct((n, N_HEADS_B, 128), jnp.int32),
        compiler_params=_cparams(("arbitrary",)),
        name="sample_select",
    )(q_s, kmean_s)


def _sample_attn_kernel(sel_ref, pt_ref, q_ref, k_ref, v_ref, kn_ref, vn_ref, zg_ref, o_ref,
                        m_ref, l_ref, acc_ref):
    j = pl.program_id(2)
    scale = HEAD_B ** -0.5
    q = q_ref[...]

    @pl.when(j == 0)
    def _():
        m_ref[...] = jnp.full_like(m_ref, NEG)
        l_ref[...] = jnp.zeros_like(l_ref)
        acc_ref[...] = jnp.zeros_like(acc_ref)

    def attend(s, v):
        m = m_ref[...]
        m_new = jnp.maximum(m, jnp.max(s, axis=0, keepdims=True))
        alpha = jnp.exp(m - m_new)
        p = jnp.exp(s - m_new)
        l_ref[...] = alpha * l_ref[...] + jnp.sum(p, axis=0, keepdims=True)
        acc_ref[...] = alpha * acc_ref[...] + jnp.sum(p * v, axis=0, keepdims=True)
        m_ref[...] = m_new

    s = jnp.sum(k_ref[...] * q, axis=1, keepdims=True) * scale
    attend(s, v_ref[...])

    @pl.when(j == pl.num_programs(2) - 1)
    def _():
        attend(jnp.sum(kn_ref[...] * q, axis=1, keepdims=True) * scale, vn_ref[...])
        o_ref[...] = (acc_ref[...] / l_ref[...]) * _silu(zg_ref[...])


def sample_attn(sel, page_table, q_s, cache_k, cache_v, k_new, v_new, zgb):
    n = q_s.shape[0]
    page = cache_k.shape[1]
    ppb = MOBA_BLOCK // page
    sq = pl.Squeezed()
    vec = pl.BlockSpec((sq, sq, 1, HEAD_B), lambda b, h, j, sel, pt: (b, h, 0, 0))

    def page_map(b, h, j, sel, pt):
        return (pt[b, sel[b, h, j // ppb] * ppb + j % ppb], 0, h)

    kv = pl.BlockSpec((sq, page, HEAD_B), page_map)
    return pl.pallas_call(
        _sample_attn_kernel,
        grid_spec=pltpu.PrefetchScalarGridSpec(
            num_scalar_prefetch=2,
            grid=(n, N_HEADS_B, MOBA_TOPK * ppb),
            in_specs=[vec, kv, kv, vec, vec, vec],
            out_specs=vec,
            scratch_shapes=[pltpu.VMEM((1, 1), F32), pltpu.VMEM((1, 1), F32),
                            pltpu.VMEM((1, HEAD_B), F32)],
        ),
        out_shape=jax.ShapeDtypeStruct((n, N_HEADS_B, 1, HEAD_B), F32),
        compiler_params=_cparams(("arbitrary", "arbitrary", "arbitrary")),
        name="sample_attn",
    )(sel, page_table, q_s, cache_k, cache_v, k_new, v_new, zgb)


def _merge_kernel(oa_ref, ob_ref, pa_ref, pb_ref, zga_ref, zgb_ref, o_ref):
    ya = jnp.dot(oa_ref[...], pa_ref[...], preferred_element_type=F32)
    yb = jnp.dot(ob_ref[...], pb_ref[...], preferred_element_type=F32)
    o_ref[...] = (_sigmoid(zga_ref[...]) * ya + _sigmoid(zgb_ref[...]) * yb).astype(o_ref.dtype)


def merge(o_a, o_b, p_a, p_b, z_main):
    m = o_a.shape[0]
    tm = min(m, 1024)
    tn = 512
    return pl.pallas_call(
        _merge_kernel,
        grid=(m // tm, D_MODEL // tn),
        in_specs=[
            pl.BlockSpec((tm, D_A), lambda i, j: (i, 0)),
            pl.BlockSpec((tm, D_B), lambda i, j: (i, 0)),
            pl.BlockSpec((D_A, tn), lambda i, j: (0, j)),
            pl.BlockSpec((D_B, tn), lambda i, j: (0, j)),
            pl.BlockSpec((tm, tn), lambda i, j: (i, COL_GA // tn + j)),
            pl.BlockSpec((tm, tn), lambda i, j: (i, COL_GM // tn + j)),
        ],
        out_specs=pl.BlockSpec((tm, tn), lambda i, j: (i, j)),
        out_shape=jax.ShapeDtypeStruct((m, D_MODEL), BF16),
        compiler_params=_cparams(("arbitrary", "arbitrary")),
        name="merge",
    )(o_a, o_b, p_a, p_b, z_main, z_main)


def _out_kernel(mg_ref, wo_ref, x_ref, fw_ref, y_ref):
    h = x_ref[...] + jnp.dot(mg_ref[...], wo_ref[...], preferred_element_type=F32)
    ms = jnp.mean(h * h, axis=-1, keepdims=True)
    y_ref[...] = (h * lax.rsqrt(ms + NORM_EPS)) * fw_ref[...]


def output_projection(merged, w_o, x, final_w):
    m = x.shape[0]
    tm = min(m, 512)
    return pl.pallas_call(
        _out_kernel,
        grid=(m // tm,),
        in_specs=[
            pl.BlockSpec((tm, D_MODEL), lambda i: (i, 0)),
            pl.BlockSpec((D_MODEL, D_MODEL), lambda i: (0, 0)),
            pl.BlockSpec((tm, D_MODEL), lambda i: (i, 0)),
            pl.BlockSpec((1, D_MODEL), lambda i: (0, 0)),
        ],
        out_specs=pl.BlockSpec((tm, D_MODEL), lambda i: (i, 0)),
        out_shape=jax.ShapeDtypeStruct((m, D_MODEL), F32),
        compiler_params=_cparams(("arbitrary",)),
        name="output_projection",
    )(merged, w_o, x, final_w.reshape(1, -1))


def _rope_tables(pos):
    half = ROT_DIM // 2
    inv = jnp.power(jnp.float32(ROPE_THETA), -jnp.arange(half, dtype=F32) * (2.0 / ROT_DIM))
    ang = pos.astype(F32)[:, None] * inv[None, :]
    cos, sin = jnp.cos(ang), jnp.sin(ang)
    n = pos.shape[0]
    rest = HEAD_B - ROT_DIM
    cos_t = jnp.concatenate([cos, cos, jnp.ones((n, rest), F32)], axis=1)
    sin_t = jnp.concatenate([-sin, sin, jnp.zeros((n, rest), F32)], axis=1)
    return cos_t, sin_t


def _reorder_rw(a):
    r, w_lo, k, v, a_lo, g = jnp.split(
        a, [D_A, D_A + LORA, 2 * D_A + LORA, 3 * D_A + LORA, 3 * D_A + 2 * LORA], axis=-1)
    return jnp.concatenate([r, k, v, g], axis=-1), jnp.concatenate([w_lo, a_lo], axis=-1)


def kernel(x_prompt, x_sample, state_shift, state_wkv, cache_k, cache_v, page_table, ln_w, w_in, mu,
           w0, w2, a0, a2, k_k, k_a, r_k, lnx_w, lnx_b, p_a, p_b, w_o, final_w):
    depth = ln_w.shape[0]
    assert depth == 1, "single-layer trunk"
    B, T, _ = x_prompt.shape
    DB, TS, _ = x_sample.shape
    assert TS == 1
    n_pages = page_table.shape[1]
    page = cache_k.shape[2]
    past = n_pages * page
    assert past % MOBA_BLOCK == 0 and MOBA_BLOCK % page == 0
    l = 0
    rw_cols = 4 * D_A + 2 * LORA

    w_rw, w_wa = _reorder_rw(w_in[l][:, :rw_cols])
    w_main = jnp.concatenate([w_rw, w_in[l][:, rw_cols:]], axis=1).astype(BF16)
    w_wa = w_wa.astype(BF16)
    mu_rw, mu_wa = _reorder_rw(mu[l][None, :])
    zeros = jnp.zeros((LORA, D_A), F32)
    w2p = jnp.concatenate([w2[l], zeros], axis=0).astype(BF16)
    a2p = jnp.concatenate([zeros, a2[l]], axis=0).astype(BF16)
    row = lambda a: a.reshape(1, -1)
    vecs = (row(w0[l]), row(a0[l]), row(k_k[l]), row(k_a[l]), row(r_k[l]), row(lnx_w[l]), row(lnx_b[l]))
    pa_bf, pb_bf, wo_bf = p_a[l].astype(BF16), p_b[l].astype(BF16), w_o[l].astype(BF16)

    xp = x_prompt.reshape(B * T, D_MODEL)
    z_p, zwa_p = input_projection(xp, ln_w[l], w_main, w_wa, normalize=True)
    o_a_p, wkv_p = rwkv_prompt(z_p, zwa_p, B, T, mu_rw, mu_wa, vecs, w2p, a2p)
    cos_p, sin_p = _rope_tables(jnp.arange(T))
    q_hm, k_hm, v_hm, k_rows_p, v_rows_p, kmean_p = attn_prep(z_p, cos_p, sin_p, B, T)
    nb_p = T // min(T, MOBA_BLOCK)
    o_b_p = moba_prompt(q_hm, k_hm, v_hm, kmean_p.reshape(B, nb_p, D_B), z_p, B, T)
    merged_p = merge(o_a_p, o_b_p, pa_bf, pb_bf, z_p)
    y_prompt = output_projection(merged_p, wo_bf, xp, final_w).reshape(B, T, D_MODEL)

    rows = jnp.concatenate([x_prompt[:, -1, :], x_sample[:, 0, :]], axis=0)
    xn_rows = rmsnorm_rows(rows, ln_w[l])
    shift_prompt = xn_rows[:B]
    xn_s = xn_rows[B:]
    z_s2, zwa_s2 = input_projection(jnp.concatenate([xn_s, state_shift[l]], axis=0), ln_w[l],
                                    w_main, w_wa, normalize=False)
    z_s, zprev_s = z_s2[:DB], z_s2[DB:]
    r_s, kf_s, v_s, kk_s, b_s, w_s, g_s = rwkv_rows(
        z_s[:, :4 * D_A], zprev_s[:, :4 * D_A], zwa_s2[:DB], zwa_s2[DB:], mu_rw, mu_wa,
        vecs[0], vecs[1], vecs[2], vecs[3], w2p, a2p)
    hs = lambda a: a.reshape(-1, N_HEADS_A, 1, HEAD_A)
    col = lambda a: a.reshape(-1, N_HEADS_A, HEAD_A, 1)
    wkv_s, o_a_s = rwkv_step(state_wkv[l], hs(w_s), hs(kk_s), hs(b_s), hs(kf_s), hs(r_s),
                             col(v_s), col(g_s), hs(r_k[l])[0], col(lnx_w[l])[0], col(lnx_b[l])[0])
    o_a_s = o_a_s.reshape(DB, D_A).astype(BF16)

    cos_s, sin_s = _rope_tables(past + jnp.arange(TS))
    cos_s = jnp.broadcast_to(cos_s, (DB, HEAD_B))
    sin_s = jnp.broadcast_to(sin_s, (DB, HEAD_B))
    q_s_hm, _, _, k_rows_s, v_rows_s, _ = attn_prep(z_s, cos_s, sin_s, 1, DB)
    q_s = q_s_hm[0].transpose(1, 0, 2).astype(F32)
    ck = cache_k[l].reshape(-1, page, D_B)
    cv = cache_v[l].reshape(-1, page, D_B)
    kmean_s = cache_kmean(ck, page_table)
    sel = sample_select(q_s, kmean_s.reshape(DB, past // MOBA_BLOCK, D_B))[:, :, :MOBA_TOPK]
    hv = lambda a: a.reshape(DB, N_HEADS_B, 1, HEAD_B)
    o_b_s = sample_attn(sel, page_table, hv(q_s), ck, cv, hv(k_rows_s), hv(v_rows_s),
                        hv(z_s[:, COL_GB:COL_GB + D_B]))
    o_b_s = o_b_s.reshape(DB, D_B).astype(BF16)
    merged_s = merge(o_a_s, o_b_s, pa_bf, pb_bf, z_s)
    y_sample = output_projection(merged_s, wo_bf, x_sample.reshape(DB, D_MODEL), final_w)

    return (y_prompt,
            y_sample.reshape(DB, TS, D_MODEL),
            shift_prompt[None],
            wkv_p.reshape(1, B, N_HEADS_A, HEAD_A, HEAD_A),
            k_rows_p.reshape(1, B, T, N_HEADS_B, HEAD_B),
            v_rows_p.reshape(1, B, T, N_HEADS_B, HEAD_B),
            xn_s[None],
            wkv_s[None],
            k_rows_s.reshape(1, DB, TS, N_HEADS_B, HEAD_B),
            v_rows_s.reshape(1, DB, TS, N_HEADS_B, HEAD_B))
```

```python
import functools

import jax
import jax.numpy as jnp
from jax import lax
from jax.experimental import pallas as pl
from jax.experimental.pallas import tpu as pltpu

F32 = jnp.float32
BF16 = jnp.bfloat16

D_MODEL = 2048
D_A = D_MODEL // 2
HEAD_A = 64
N_HEADS_A = D_A // HEAD_A
LORA = 64
D_B = D_MODEL // 2
HEAD_B = 128
N_HEADS_B = D_B // HEAD_B
ROT_DIM = HEAD_B // 4
ROPE_THETA = 500000.0
MOBA_BLOCK = 256
MOBA_TOPK = 3
NORM_EPS = 1e-6
GN_EPS = 64e-5
NEG = -1e30

COL_R, COL_K, COL_V, COL_G = 0, D_A, 2 * D_A, 3 * D_A
COL_Q, COL_KB, COL_VB, COL_GB = 4 * D_A, 4 * D_A + D_B, 4 * D_A + 2 * D_B, 4 * D_A + 3 * D_B
COL_GA = 4 * D_A + 4 * D_B
COL_GM = COL_GA + D_MODEL
N_MAIN = COL_GM + D_MODEL

GROUP = 256
HEADS_PER_GROUP = GROUP // HEAD_A
CHUNK = 64
assert CHUNK == HEAD_A
VMEM_LIMIT = 56 * 1024 * 1024


def _cparams(sem):
    return pltpu.CompilerParams(dimension_semantics=sem, vmem_limit_bytes=VMEM_LIMIT)


def _bdot(a, b):
    return jnp.dot(a.astype(BF16), b.astype(BF16), preferred_element_type=F32)


def _bdot_nt(a, b):
    return lax.dot_general(a.astype(BF16), b.astype(BF16), (((1,), (1,)), ((), ())),
                           preferred_element_type=F32)


def _bdot_tn(a, b):
    return lax.dot_general(a.astype(BF16), b.astype(BF16), (((0,), (0,)), ((), ())),
                           preferred_element_type=F32)


def _split3(x):
    hi = x.astype(BF16)
    r1 = x - hi.astype(F32)
    mid = r1.astype(BF16)
    lo = (r1 - mid.astype(F32)).astype(BF16)
    return hi, mid, lo


def _sigmoid(x):
    return 1.0 / (1.0 + jnp.exp(-x))


def _silu(x):
    return x * _sigmoid(x)


def _softplus(x):
    return jnp.maximum(x, 0.0) + jnp.log(1.0 + jnp.exp(-jnp.abs(x)))


def _rmsnorm_rows_kernel(x_ref, w_ref, o_ref):
    x = x_ref[...]
    ms = jnp.mean(x * x, axis=-1, keepdims=True)
    o_ref[...] = (x * lax.rsqrt(ms + NORM_EPS)) * w_ref[...]


def rmsnorm_rows(x, w):
    return pl.pallas_call(
        _rmsnorm_rows_kernel,
        out_shape=jax.ShapeDtypeStruct(x.shape, F32),
        name="rmsnorm_rows",
    )(x, w.reshape(1, -1))


def _proj_kernel(x_ref, lnw_ref, w_ref, wwa_ref, z_ref, zwa_ref, xn_ref, *, normalize):
    @pl.when(pl.program_id(1) == 0)
    def _():
        x = x_ref[...]
        if normalize:
            ms = jnp.mean(x * x, axis=-1, keepdims=True)
            x = (x * lax.rsqrt(ms + NORM_EPS)) * lnw_ref[...]
        xn_ref[...] = x.astype(BF16)
        zwa_ref[...] = jnp.dot(xn_ref[...], wwa_ref[...], preferred_element_type=F32)

    z_ref[...] = jnp.dot(xn_ref[...], w_ref[...], preferred_element_type=F32)


def input_projection(x, ln_w, w_main, w_wa, *, normalize):
    m = x.shape[0]
    tm = min(m, 1024)
    tn = 1024
    assert m % tm == 0 and N_MAIN % tn == 0
    return pl.pallas_call(
        functools.partial(_proj_kernel, normalize=normalize),
        grid=(m // tm, N_MAIN // tn),
        in_specs=[
            pl.BlockSpec((tm, D_MODEL), lambda i, j: (i, 0)),
            pl.BlockSpec((1, D_MODEL), lambda i, j: (0, 0)),
            pl.BlockSpec((D_MODEL, tn), lambda i, j: (0, j)),
            pl.BlockSpec((D_MODEL, 2 * LORA), lambda i, j: (0, 0)),
        ],
        out_specs=[
            pl.BlockSpec((tm, tn), lambda i, j: (i, j)),
            pl.BlockSpec((tm, 2 * LORA), lambda i, j: (i, 0)),
        ],
        out_shape=[
            jax.ShapeDtypeStruct((m, N_MAIN), F32),
            jax.ShapeDtypeStruct((m, 2 * LORA), F32),
        ],
        scratch_shapes=[pltpu.VMEM((tm, D_MODEL), BF16)],
        compiler_params=_cparams(("arbitrary", "arbitrary")),
        name="input_projection",
    )(x, ln_w.reshape(1, -1), w_main, w_wa)


def _head_mask(n):
    r = lax.broadcasted_iota(jnp.int32, (n, n), 0) // HEAD_A
    c = lax.broadcasted_iota(jnp.int32, (n, n), 1) // HEAD_A
    return r == c


def _segsum(x, ones_bd):
    hi = x.astype(BF16)
    lo = (x - hi.astype(F32)).astype(BF16)
    return (jnp.dot(hi, ones_bd, preferred_element_type=F32)
            + jnp.dot(lo, ones_bd, preferred_element_type=F32))


def _rwkv_prep(zm_r, zm_k, zm_v, zm_wa, w0, a0, k_k, k_a, w2p, a2p, ones_bd):
    lane = lax.broadcasted_iota(jnp.int32, zm_wa.shape, 1)
    lora_in = jnp.where(lane < LORA, jnp.tanh(zm_wa), zm_wa)
    ww = _bdot(lora_in, w2p)
    aa = _bdot(lora_in, a2p)
    w_log = -_softplus(-(w0 + ww)) - 0.5
    logw = -jnp.exp(w_log)
    a = _sigmoid(a0 + aa)
    kk = zm_k * k_k
    ss = _segsum(kk * kk, ones_bd)
    kk = kk / jnp.maximum(jnp.sqrt(ss), 1e-12)
    kf = zm_k * (1.0 + (a - 1.0) * k_a)
    return zm_r, kf, zm_v, kk, kk * a, logw


def _rwkv_post(y, r, kf, v, zm_g, r_k, lnx_w, lnx_b, ones_bd):
    inv_n = 1.0 / HEAD_A
    mean = _segsum(y, ones_bd) * inv_n
    d = y - mean
    var = _segsum(d * d, ones_bd) * inv_n
    yn = d * lax.rsqrt(var + GN_EPS) * lnx_w + lnx_b
    bonus = _segsum(r * kf * r_k, ones_bd) * v
    return (yn + bonus) * _silu(zm_g)


def _stack_heads(x, lane_head):
    return jnp.concatenate(
        [jnp.where(lane_head == h, x, 0.0) for h in range(HEADS_PER_GROUP)], axis=0)


def _rwkv_chunk(S, r, kf, v, kk, b, logw, cl, consts):
    strict, incl, eye, lane_head, bd_mask = consts
    c = CHUNK
    cl_last = cl[c - 1:c, :]
    e_pos = jnp.exp(cl)
    e_prev = jnp.exp(cl - logw)
    e_neg = jnp.exp(-cl)
    e_end = jnp.exp(cl_last - cl)
    rp = r * e_pos
    kkp = kk * e_prev
    bq = b * e_neg
    kq = kf * e_neg

    lhs = jnp.concatenate([kkp, rp], axis=0).astype(BF16)
    wt = jnp.concatenate([_stack_heads(bq, lane_head), _stack_heads(kq, lane_head)], axis=0)
    att = _bdot_nt(lhs, wt)
    a_ab = jnp.where(strict, att[:c, :4 * c], 0.0)
    a_ak = jnp.where(strict, att[:c, 4 * c:], 0.0)
    m_rb = jnp.where(incl, att[c:, :4 * c], 0.0)
    m_rk = jnp.where(incl, att[c:, 4 * c:], 0.0)

    sh = _bdot_nt(lhs, S)
    v_bd = _stack_heads(v, lane_head).astype(BF16)
    rhs = sh[:c] + _bdot(a_ak, v_bd)

    def bd(p):
        return jnp.where(bd_mask, jnp.concatenate([p] * HEADS_PER_GROUP, axis=0), 0.0).astype(BF16)

    p = a_ab
    x = eye - a_ab
    p = _bdot(p, bd(p))
    for _ in range(4):
        px = _bdot(jnp.concatenate([p, x], axis=0), bd(p))
        p = px[:c]
        x = x + px[c:]
    x = x + _bdot(x, bd(p))

    u = -_bdot(x, _stack_heads(rhs, lane_head))
    u_bd = _stack_heads(u, lane_head).astype(BF16)
    y = sh[c:] + _bdot(jnp.concatenate([m_rb, m_rk], axis=1),
                       jnp.concatenate([u_bd, v_bd], axis=0))
    upd = _bdot_tn(jnp.concatenate([u, v], axis=0),
                   jnp.concatenate([b * e_end, kf * e_end], axis=0))
    s_new = S * jnp.exp(cl_last) + jnp.where(bd_mask, upd, 0.0)
    return y, s_new


def _rwkv_prompt_kernel(zr_ref, zk_ref, zv_ref, zg_ref, zwa_ref,
                        mur_ref, muk_ref, muv_ref, mug_ref, muwa_ref,
                        w0_ref, a0_ref, kk_ref, ka_ref, rk_ref, lnw_ref, lnb_ref,
                        w2_ref, a2_ref,
                        o_ref, s_out_ref,
                        s_ref, pr_ref, pk_ref, pv_ref, pg_ref, pwa_ref):
    t = pl.program_id(2)
    tr = zr_ref.shape[0]

    @pl.when(t == 0)
    def _():
        s_ref[...] = jnp.zeros_like(s_ref)
        pr_ref[...] = jnp.zeros_like(pr_ref)
        pk_ref[...] = jnp.zeros_like(pk_ref)
        pv_ref[...] = jnp.zeros_like(pv_ref)
        pg_ref[...] = jnp.zeros_like(pg_ref)
        pwa_ref[...] = jnp.zeros_like(pwa_ref)

    def shifted(z_ref, prev_ref, mu_ref):
        z = z_ref[...]
        row = lax.broadcasted_iota(jnp.int32, z.shape, 0)
        prev = jnp.where(row == 0, prev_ref[...], pltpu.roll(z, 1, 0))
        prev_ref[...] = z[tr - 1:tr, :]
        return z + (prev - z) * mu_ref[...]

    zm_r = shifted(zr_ref, pr_ref, mur_ref)
    zm_k = shifted(zk_ref, pk_ref, muk_ref)
    zm_v = shifted(zv_ref, pv_ref, muv_ref)
    zm_g = shifted(zg_ref, pg_ref, mug_ref)
    zm_wa = shifted(zwa_ref, pwa_ref, muwa_ref)

    bd_mask = _head_mask(GROUP)
    ones_bd = jnp.where(bd_mask, 1.0, 0.0).astype(BF16)
    r, kf, v, kk, b, logw = _rwkv_prep(zm_r, zm_k, zm_v, zm_wa, w0_ref[...], a0_ref[...],
                                       kk_ref[...], ka_ref[...], w2_ref[...], a2_ref[...], ones_bd)

    ri = lax.broadcasted_iota(jnp.int32, (tr, tr), 0)
    ci = lax.broadcasted_iota(jnp.int32, (tr, tr), 1)
    tri = jnp.where((ri // CHUNK == ci // CHUNK) & (ci <= ri), 1.0, 0.0).astype(BF16)
    cl = sum(jnp.dot(tri, piece, preferred_element_type=F32) for piece in _split3(logw))

    rc = lax.broadcasted_iota(jnp.int32, (CHUNK, GROUP), 0)
    lc = lax.broadcasted_iota(jnp.int32, (CHUNK, GROUP), 1)
    sc = lc % CHUNK
    consts = (sc < rc, sc <= rc, jnp.where(sc == rc, 1.0, 0.0), lc // HEAD_A, bd_mask)

    S = s_ref[...]
    ys = []
    for ch in range(tr // CHUNK):
        sl = slice(ch * CHUNK, (ch + 1) * CHUNK)
        y, S = _rwkv_chunk(S, r[sl], kf[sl], v[sl], kk[sl], b[sl], logw[sl], cl[sl], consts)
        ys.append(y)
    s_ref[...] = S
    y = jnp.concatenate(ys, axis=0)

    o_ref[...] = _rwkv_post(y, r, kf, v, zm_g, rk_ref[...], lnw_ref[...], lnb_ref[...],
                            ones_bd).astype(o_ref.dtype)

    @pl.when(t == pl.num_programs(2) - 1)
    def _():
        for h in range(HEADS_PER_GROUP):
            s_out_ref[h] = S[h * HEAD_A:(h + 1) * HEAD_A, h * HEAD_A:(h + 1) * HEAD_A]


def rwkv_prompt(z_main, z_wa, batch, seq, mu_main, mu_wa, vecs, w2p, a2p):
    w0, a0, k_k, k_a, r_k, lnx_w, lnx_b = vecs
    tr = min(seq, 256)
    assert seq % tr == 0 and tr % CHUNK == 0
    nt = seq // tr
    ng = D_A // GROUP

    def zspec(col):
        return pl.BlockSpec((tr, GROUP), lambda b, g, t, c=col // GROUP: (b * nt + t, c + g))

    def vspec(col=0):
        return pl.BlockSpec((1, GROUP), lambda b, g, t, c=col // GROUP: (0, c + g))

    in_specs = [
        zspec(COL_R), zspec(COL_K), zspec(COL_V), zspec(COL_G),
        pl.BlockSpec((tr, 2 * LORA), lambda b, g, t: (b * nt + t, 0)),
        vspec(COL_R), vspec(COL_K), vspec(COL_V), vspec(COL_G),
        pl.BlockSpec((1, 2 * LORA), lambda b, g, t: (0, 0)),
        vspec(), vspec(), vspec(), vspec(), vspec(), vspec(), vspec(),
        pl.BlockSpec((2 * LORA, GROUP), lambda b, g, t: (0, g)),
        pl.BlockSpec((2 * LORA, GROUP), lambda b, g, t: (0, g)),
    ]
    out_specs = [
        pl.BlockSpec((tr, GROUP), lambda b, g, t: (b * nt + t, g)),
        pl.BlockSpec((HEADS_PER_GROUP, HEAD_A, HEAD_A), lambda b, g, t: (b * ng + g, 0, 0)),
    ]
    row = lambda n: pltpu.VMEM((1, n), F32)
    return pl.pallas_call(
        _rwkv_prompt_kernel,
        grid=(batch, ng, nt),
        in_specs=in_specs,
        out_specs=out_specs,
        out_shape=[
            jax.ShapeDtypeStruct((batch * seq, D_A), BF16),
            jax.ShapeDtypeStruct((batch * N_HEADS_A, HEAD_A, HEAD_A), F32),
        ],
        scratch_shapes=[pltpu.VMEM((GROUP, GROUP), F32),
                        row(GROUP), row(GROUP), row(GROUP), row(GROUP), row(2 * LORA)],
        compiler_params=_cparams(("arbitrary", "arbitrary", "arbitrary")),
        name="rwkv_prompt",
    )(z_main, z_main, z_main, z_main, z_wa,
      mu_main, mu_main, mu_main, mu_main, mu_wa,
      w0, a0, k_k, k_a, r_k, lnx_w, lnx_b, w2p, a2p)


def _rwkv_rows_kernel(z_ref, zp_ref, zwa_ref, zpwa_ref, mu_ref, muwa_ref,
                      w0_ref, a0_ref, kk_ref, ka_ref, w2_ref, a2_ref,
                      r_ref, kf_ref, v_ref, kkn_ref, b_ref, w_ref, g_ref):
    def shifted(z, zp, mu):
        return z + (zp - z) * mu

    mu = mu_ref[...]
    z = z_ref[...]
    zp = zp_ref[...]
    zm = [shifted(z[:, c:c + D_A], zp[:, c:c + D_A], mu[:, c:c + D_A])
          for c in (COL_R, COL_K, COL_V, COL_G)]
    zm_wa = shifted(zwa_ref[...], zpwa_ref[...], muwa_ref[...])
    ones_bd = jnp.where(_head_mask(D_A), 1.0, 0.0).astype(BF16)
    r, kf, v, kk, b, logw = _rwkv_prep(zm[0], zm[1], zm[2], zm_wa, w0_ref[...], a0_ref[...],
                                       kk_ref[...], ka_ref[...], w2_ref[...], a2_ref[...], ones_bd)
    r_ref[...] = r
    kf_ref[...] = kf
    v_ref[...] = v
    kkn_ref[...] = kk
    b_ref[...] = b
    w_ref[...] = jnp.exp(logw)
    g_ref[...] = zm[3]


def rwkv_rows(z_rw, zp_rw, z_wa, zp_wa, mu_rw, mu_wa, w0, a0, k_k, k_a, w2p, a2p):
    n = z_rw.shape[0]
    out = jax.ShapeDtypeStruct((n, D_A), F32)
    return pl.pallas_call(
        _rwkv_rows_kernel,
        out_shape=[out] * 7,
        compiler_params=pltpu.CompilerParams(vmem_limit_bytes=VMEM_LIMIT),
        name="rwkv_rows",
    )(z_rw, zp_rw, z_wa, zp_wa, mu_rw, mu_wa, w0, a0, k_k, k_a, w2p, a2p)


def _rwkv_step_kernel(s_ref, w_ref, kk_ref, b_ref, kf_ref, r_ref, v_ref, g_ref,
                      rk_ref, lnw_ref, lnb_ref, s_out_ref, o_ref):
    S = s_ref[...]
    w = w_ref[...]
    kk = kk_ref[...]
    b = b_ref[...]
    kf = kf_ref[...]
    r = r_ref[...]
    v = v_ref[...]
    sa = -jnp.sum(S * kk, axis=-1, keepdims=True)
    S = S * w + sa * b + v * kf
    s_out_ref[...] = S
    y = jnp.sum(S * r, axis=-1, keepdims=True)
    mean = jnp.mean(y, axis=1, keepdims=True)
    d = y - mean
    var = jnp.mean(d * d, axis=1, keepdims=True)
    yn = d * lax.rsqrt(var + GN_EPS) * lnw_ref[...] + lnb_ref[...]
    bonus = jnp.sum(r * kf * rk_ref[...], axis=-1, keepdims=True) * v
    o_ref[...] = (yn + bonus) * _silu(g_ref[...])


def rwkv_step(state, w, kk, b, kf, r, v, g, r_k, lnx_w, lnx_b):
    n = state.shape[0]
    h = N_HEADS_A
    sq = pl.Squeezed()
    lane_vec = pl.BlockSpec((sq, h, 1, HEAD_A), lambda i: (i, 0, 0, 0))
    col_vec = pl.BlockSpec((sq, h, HEAD_A, 1), lambda i: (i, 0, 0, 0))
    return pl.pallas_call(
        _rwkv_step_kernel,
        grid=(n,),
        in_specs=[pl.BlockSpec((sq, h, HEAD_A, HEAD_A), lambda i: (i, 0, 0, 0)),
                  lane_vec, lane_vec, lane_vec, lane_vec, lane_vec, col_vec, col_vec,
                  pl.BlockSpec((h, 1, HEAD_A), lambda i: (0, 0, 0)),
                  pl.BlockSpec((h, HEAD_A, 1), lambda i: (0, 0, 0)),
                  pl.BlockSpec((h, HEAD_A, 1), lambda i: (0, 0, 0))],
        out_specs=[pl.BlockSpec((sq, h, HEAD_A, HEAD_A), lambda i: (i, 0, 0, 0)), col_vec],
        out_shape=[jax.ShapeDtypeStruct((n, h, HEAD_A, HEAD_A), F32),
                   jax.ShapeDtypeStruct((n, h, HEAD_A, 1), F32)],
        compiler_params=_cparams(("arbitrary",)),
        name="rwkv_step",
    )(state, w, kk, b, kf, r, v, g, r_k, lnx_w, lnx_b)


def _rope(x, cos_t, sin_t, lane):
    partner = jnp.where(lane < ROT_DIM // 2, pltpu.roll(x, HEAD_B - ROT_DIM // 2, 1),
                        pltpu.roll(x, ROT_DIM // 2, 1))
    return x * cos_t + partner * sin_t


def _attn_prep_kernel(zq_ref, zk_ref, zv_ref, cos_ref, sin_ref,
                      q_ref, kh_ref, vt_ref, ko_ref, vo_ref, km_ref):
    cos_t = cos_ref[...]
    sin_t = sin_ref[...]
    lane = lax.broadcasted_iota(jnp.int32, cos_t.shape, 1)
    vo_ref[...] = zv_ref[...]
    for h in range(N_HEADS_B):
        sl = slice(h * HEAD_B, (h + 1) * HEAD_B)
        q = _rope(zq_ref[:, sl], cos_t, sin_t, lane)
        k = _rope(zk_ref[:, sl], cos_t, sin_t, lane)
        q_ref[h] = q.astype(q_ref.dtype)
        kh_ref[h] = k.astype(kh_ref.dtype)
        vt_ref[h] = zv_ref[:, sl].T.astype(vt_ref.dtype)
        ko_ref[:, sl] = k
        km_ref[:, sl] = jnp.mean(k, axis=0, keepdims=True)


def attn_prep(z_main, cos_t, sin_t, batch, seq):
    tr = min(seq, MOBA_BLOCK)
    nt = seq // tr
    m = batch * seq
    sq = pl.Squeezed()
    zspec = lambda col: pl.BlockSpec((tr, D_B), lambda b, t, c=col // D_B: (b * nt + t, c))
    tab = pl.BlockSpec((tr, HEAD_B), lambda b, t: (t, 0))
    hm = pl.BlockSpec((sq, N_HEADS_B, tr, HEAD_B), lambda b, t: (b, 0, t, 0))
    hm_t = pl.BlockSpec((sq, N_HEADS_B, HEAD_B, tr), lambda b, t: (b, 0, 0, t))
    rows = pl.BlockSpec((tr, D_B), lambda b, t: (b * nt + t, 0))
    hm_shape = jax.ShapeDtypeStruct((batch, N_HEADS_B, seq, HEAD_B), BF16)
    return pl.pallas_call(
        _attn_prep_kernel,
        grid=(batch, nt),
        in_specs=[zspec(COL_Q), zspec(COL_KB), zspec(COL_VB), tab, tab],
        out_specs=[hm, hm, hm_t, rows, rows,
                   pl.BlockSpec((sq, 1, D_B), lambda b, t: (b * nt + t, 0, 0))],
        out_shape=[hm_shape, hm_shape,
                   jax.ShapeDtypeStruct((batch, N_HEADS_B, HEAD_B, seq), BF16),
                   jax.ShapeDtypeStruct((m, D_B), F32), jax.ShapeDtypeStruct((m, D_B), F32),
                   jax.ShapeDtypeStruct((batch * nt, 1, D_B), F32)],
        compiler_params=_cparams(("arbitrary", "arbitrary")),
        name="attn_prep",
    )(z_main, z_main, z_main, cos_t, sin_t)


def _rope_rows_kernel(zq_ref, zk_ref, cos_ref, sin_ref, q_ref, k_ref):
    cos_t = cos_ref[...]
    sin_t = sin_ref[...]
    lane = lax.broadcasted_iota(jnp.int32, cos_t.shape, 1)
    for h in range(N_HEADS_B):
        sl = slice(h * HEAD_B, (h + 1) * HEAD_B)
        q_ref[:, sl] = _rope(zq_ref[:, sl], cos_t, sin_t, lane).astype(BF16).astype(F32)
        k_ref[:, sl] = _rope(zk_ref[:, sl], cos_t, sin_t, lane)


def rope_rows(zq, zk, cos_t, sin_t):
    out = jax.ShapeDtypeStruct(zq.shape, F32)
    return pl.pallas_call(_rope_rows_kernel, out_shape=[out, out], name="rope_rows")(zq, zk, cos_t, sin_t)


def _moba_prompt_kernel(q_ref, k_ref, vt_ref, km_ref, zg_ref, o_ref, sel_ref):
    qb = pl.program_id(2)
    q = q_ref[...]
    tq = q.shape[0]
    nb = km_ref.shape[0]
    scale = HEAD_B ** -0.5
    nt_dims = (((1,), (1,)), ((), ()))

    gate = sum(lax.dot_general(piece, q, nt_dims, preferred_element_type=F32)
               for piece in _split3(km_ref[...]))
    blk = lax.broadcasted_iota(jnp.int32, (nb, tq), 0)
    past = blk < qb
    for n in range(nb):
        g_n = gate[n:n + 1, :]
        beats = (gate > g_n) | ((gate == g_n) & (blk < n))
        cnt = jnp.sum(jnp.where(beats & past, 1.0, 0.0), axis=0, keepdims=True)
        sel_ref[n:n + 1, :] = jnp.where(cnt < MOBA_TOPK, 1.0, 0.0)

    def scores(n):
        start = pl.multiple_of(n * tq, tq)
        return lax.dot_general(k_ref[pl.ds(start, tq), :], q, nt_dims, preferred_element_type=F32)

    def attend(s, n, m, l, acc):
        m_new = jnp.maximum(m, jnp.max(s, axis=0, keepdims=True))
        alpha = jnp.exp(m - m_new)
        p = jnp.exp(s - m_new)
        l = alpha * l + jnp.sum(p, axis=0, keepdims=True)
        start = pl.multiple_of(n * tq, tq)
        acc = alpha * acc + jnp.dot(vt_ref[:, pl.ds(start, tq)], p.astype(BF16),
                                    preferred_element_type=F32)
        return m_new, l, acc

    def body(n, carry):
        s_cur, m, l, acc = carry
        s_next = scores(n + 1)
        s = jnp.where(sel_ref[pl.ds(n, 1), :] > 0.0, s_cur * scale, NEG)
        m, l, acc = attend(s, n, m, l, acc)
        return s_next, m, l, acc

    init = (scores(0), jnp.full((1, tq), NEG, F32), jnp.zeros((1, tq), F32),
            jnp.zeros((HEAD_B, tq), F32))
    s_own, m, l, acc = lax.fori_loop(0, qb, body, init)
    ki = lax.broadcasted_iota(jnp.int32, (tq, tq), 0)
    qi = lax.broadcasted_iota(jnp.int32, (tq, tq), 1)
    s = jnp.where(ki <= qi, s_own * scale, NEG)
    _, l, acc = attend(s, qb, m, l, acc)
    o_ref[...] = ((acc / l).T * _silu(zg_ref[...])).astype(o_ref.dtype)


def moba_prompt(q_hm, k_hm, vt_hm, kmean, z_main, batch, seq):
    tq = min(seq, MOBA_BLOCK)
    nb = seq // tq
    sq = pl.Squeezed()
    return pl.pallas_call(
        _moba_prompt_kernel,
        grid=(batch, N_HEADS_B, nb),
        in_specs=[
            pl.BlockSpec((sq, sq, tq, HEAD_B), lambda b, h, i: (b, h, i, 0)),
            pl.BlockSpec((sq, sq, seq, HEAD_B), lambda b, h, i: (b, h, 0, 0)),
            pl.BlockSpec((sq, sq, HEAD_B, seq), lambda b, h, i: (b, h, 0, 0)),
            pl.BlockSpec((sq, nb, HEAD_B), lambda b, h, i: (b, 0, h)),
            pl.BlockSpec((tq, HEAD_B), lambda b, h, i: (b * nb + i, COL_GB // HEAD_B + h)),
        ],
        out_specs=pl.BlockSpec((tq, HEAD_B), lambda b, h, i: (b * nb + i, h)),
        out_shape=jax.ShapeDtypeStruct((batch * seq, D_B), BF16),
        scratch_shapes=[pltpu.VMEM((nb, tq), F32)],
        compiler_params=_cparams(("arbitrary", "arbitrary", "arbitrary")),
        name="moba_prompt",
    )(q_hm, k_hm, vt_hm, kmean, z_main)


KMEAN_PAGES = 8


def _cache_kmean_kernel(pt_ref, *refs, pages_per_block):
    del pt_ref
    k_refs, o_ref = refs[:-1], refs[-1]
    sums = [jnp.sum(k_ref[...], axis=0) for k_ref in k_refs]
    for j in range(len(k_refs) // pages_per_block):
        blk = sums[j * pages_per_block]
        for i in range(1, pages_per_block):
            blk = blk + sums[j * pages_per_block + i]
        o_ref[j] = blk * (1.0 / MOBA_BLOCK)


def cache_kmean(cache_k, page_table, layer):
    n, n_pages = page_table.shape
    page = cache_k.shape[2]
    ppb = MOBA_BLOCK // page
    pps = min(KMEAN_PAGES, n_pages)
    assert n_pages % pps == 0 and pps % ppb == 0
    sq = pl.Squeezed()
    page_spec = lambda i: pl.BlockSpec(
        (sq, sq, page, N_HEADS_B, HEAD_B), lambda b, s, pt, i=i: (layer, pt[b, s * pps + i], 0, 0, 0))
    return pl.pallas_call(
        functools.partial(_cache_kmean_kernel, pages_per_block=ppb),
        grid_spec=pltpu.PrefetchScalarGridSpec(
            num_scalar_prefetch=1,
            grid=(n, n_pages // pps),
            in_specs=[page_spec(i) for i in range(pps)],
            out_specs=pl.BlockSpec((sq, pps // ppb, N_HEADS_B, HEAD_B), lambda b, s, pt: (b, s, 0, 0)),
        ),
        out_shape=jax.ShapeDtypeStruct((n, n_pages // ppb, N_HEADS_B, HEAD_B), F32),
        compiler_params=_cparams(("arbitrary", "arbitrary")),
        name="cache_kmean",
    )(page_table, *([cache_k] * pps))


def _sample_select_kernel(q_ref, km_ref, sel_ref):
    nb = km_ref.shape[0]
    ri = lax.broadcasted_iota(jnp.int32, (nb, nb), 0)
    ci = lax.broadcasted_iota(jnp.int32, (nb, nb), 1)
    lane = lax.broadcasted_iota(jnp.int32, (1, 128), 1)
    blk_row = lax.broadcasted_iota(jnp.int32, (1, nb), 1).astype(F32)
    for h in range(N_HEADS_B):
        km = km_ref[:, h, :]
        g_col = jnp.sum(km * q_ref[h:h + 1, :], axis=1, keepdims=True)
        g_row = jnp.sum(jnp.where(ri == ci, g_col, 0.0), axis=0, keepdims=True)
        beats = (g_col > g_row) | ((g_col == g_row) & (ri < ci))
        rank = jnp.sum(jnp.where(beats, 1.0, 0.0), axis=0, keepdims=True)
        out = jnp.zeros((1, 128), F32)
        for r in range(MOBA_TOPK):
            idx = jnp.sum(jnp.where(rank == float(r), blk_row, 0.0), axis=1, keepdims=True)
            out = jnp.where(lane == r, idx, out)
        sel_ref[h:h + 1, :] = out.astype(jnp.int32)


def sample_select(q_s, kmean_s):
    n, nb = kmean_s.shape[:2]
    sq = pl.Squeezed()
    return pl.pallas_call(
        _sample_select_kernel,
        grid=(n,),
        in_specs=[pl.BlockSpec((sq, N_HEADS_B, HEAD_B), lambda b: (b, 0, 0)),
                  pl.BlockSpec((sq, nb, N_HEADS_B, HEAD_B), lambda b: (b, 0, 0, 0))],
        out_specs=pl.BlockSpec((sq, N_HEADS_B, 128), lambda b: (b, 0, 0)),
        out_shape=jax.ShapeDtypeStruct((n, N_HEADS_B, 128), jnp.int32),
        compiler_params=_cparams(("arbitrary",)),
        name="sample_select",
    )(q_s, kmean_s)


def _sample_attn_kernel(sel_ref, pt_ref, q_ref, kn_ref, vn_ref, zg_ref, ck_ref, cv_ref, o_ref,
                        kbuf, vbuf, sem, *, layer, page, ppb):
    b = pl.program_id(0)
    nb_steps = pl.num_programs(0)
    n_sel = MOBA_TOPK * ppb
    scale = HEAD_B ** -0.5

    def copies(bb, slot):
        out = []
        for h in range(N_HEADS_B):
            for j in range(n_sel):
                pg = pt_ref[bb, sel_ref[bb, h, j // ppb] * ppb + j % ppb]
                dst = pl.ds(j * page, page)
                out.append(pltpu.make_async_copy(ck_ref.at[layer, pg, :, h, :],
                                                 kbuf.at[slot, h, dst, :], sem.at[slot, 0]))
                out.append(pltpu.make_async_copy(cv_ref.at[layer, pg, :, h, :],
                                                 vbuf.at[slot, h, dst, :], sem.at[slot, 1]))
        return out

    slot = b % 2

    @pl.when(b == 0)
    def _():
        for c in copies(0, 0):
            c.start()

    @pl.when(b + 1 < nb_steps)
    def _():
        for c in copies(b + 1, 1 - slot):
            c.start()

    for c in copies(b, slot):
        c.wait()

    for h in range(N_HEADS_B):
        q = q_ref[h]
        k = kbuf[slot, h]
        v = vbuf[slot, h]
        s = jnp.sum(k * q, axis=1, keepdims=True) * scale
        s_own = jnp.sum(kn_ref[h] * q, axis=1, keepdims=True) * scale
        m = jnp.maximum(jnp.max(s, axis=0, keepdims=True), s_own)
        p = jnp.exp(s - m)
        p_own = jnp.exp(s_own - m)
        l = jnp.sum(p, axis=0, keepdims=True) + p_own
        acc = jnp.sum(p * v, axis=0, keepdims=True) + p_own * vn_ref[h]
        o_ref[h] = (acc / l) * _silu(zg_ref[h])


def sample_attn(sel, page_table, q_s, k_new, v_new, zgb, cache_k, cache_v, layer):
    n = q_s.shape[0]
    page = cache_k.shape[2]
    ppb = MOBA_BLOCK // page
    rows = MOBA_TOPK * ppb * page
    sq = pl.Squeezed()
    vec = pl.BlockSpec((sq, N_HEADS_B, 1, HEAD_B), lambda b, sel, pt: (b, 0, 0, 0))
    hbm = pl.BlockSpec(memory_space=pl.ANY)
    return pl.pallas_call(
        functools.partial(_sample_attn_kernel, layer=layer, page=page, ppb=ppb),
        grid_spec=pltpu.PrefetchScalarGridSpec(
            num_scalar_prefetch=2,
            grid=(n,),
            in_specs=[vec, vec, vec, vec, hbm, hbm],
            out_specs=vec,
            scratch_shapes=[pltpu.VMEM((2, N_HEADS_B, rows, HEAD_B), F32),
                            pltpu.VMEM((2, N_HEADS_B, rows, HEAD_B), F32),
                            pltpu.SemaphoreType.DMA((2, 2))],
        ),
        out_shape=jax.ShapeDtypeStruct((n, N_HEADS_B, 1, HEAD_B), F32),
        compiler_params=_cparams(("arbitrary",)),
        name="sample_attn",
    )(sel, page_table, q_s, k_new, v_new, zgb, cache_k, cache_v)


def _merge_kernel(oa_ref, ob_ref, pa_ref, pb_ref, zga_ref, zgb_ref, o_ref):
    ya = jnp.dot(oa_ref[...], pa_ref[...], preferred_element_type=F32)
    yb = jnp.dot(ob_ref[...], pb_ref[...], preferred_element_type=F32)
    o_ref[...] = (_sigmoid(zga_ref[...]) * ya + _sigmoid(zgb_ref[...]) * yb).astype(o_ref.dtype)


def merge(o_a, o_b, p_a, p_b, z_main):
    m = o_a.shape[0]
    tm = min(m, 1024)
    tn = 512
    return pl.pallas_call(
        _merge_kernel,
        grid=(m // tm, D_MODEL // tn),
        in_specs=[
            pl.BlockSpec((tm, D_A), lambda i, j: (i, 0)),
            pl.BlockSpec((tm, D_B), lambda i, j: (i, 0)),
            pl.BlockSpec((D_A, tn), lambda i, j: (0, j)),
            pl.BlockSpec((D_B, tn), lambda i, j: (0, j)),
            pl.BlockSpec((tm, tn), lambda i, j: (i, COL_GA // tn + j)),
            pl.BlockSpec((tm, tn), lambda i, j: (i, COL_GM // tn + j)),
        ],
        out_specs=pl.BlockSpec((tm, tn), lambda i, j: (i, j)),
        out_shape=jax.ShapeDtypeStruct((m, D_MODEL), BF16),
        compiler_params=_cparams(("arbitrary", "arbitrary")),
        name="merge",
    )(o_a, o_b, p_a, p_b, z_main, z_main)


def _out_kernel(mg_ref, wo_ref, x_ref, fw_ref, y_ref):
    h = x_ref[...] + jnp.dot(mg_ref[...], wo_ref[...], preferred_element_type=F32)
    ms = jnp.mean(h * h, axis=-1, keepdims=True)
    y_ref[...] = (h * lax.rsqrt(ms + NORM_EPS)) * fw_ref[...]


def output_projection(merged, w_o, x, final_w):
    m = x.shape[0]
    tm = min(m, 512)
    return pl.pallas_call(
        _out_kernel,
        grid=(m // tm,),
        in_specs=[
            pl.BlockSpec((tm, D_MODEL), lambda i: (i, 0)),
            pl.BlockSpec((D_MODEL, D_MODEL), lambda i: (0, 0)),
            pl.BlockSpec((tm, D_MODEL), lambda i: (i, 0)),
            pl.BlockSpec((1, D_MODEL), lambda i: (0, 0)),
        ],
        out_specs=pl.BlockSpec((tm, D_MODEL), lambda i: (i, 0)),
        out_shape=jax.ShapeDtypeStruct((m, D_MODEL), F32),
        compiler_params=_cparams(("arbitrary",)),
        name="output_projection",
    )(merged, w_o, x, final_w.reshape(1, -1))


def _rope_tables(pos):
    half = ROT_DIM // 2
    inv = jnp.power(jnp.float32(ROPE_THETA), -jnp.arange(half, dtype=F32) * (2.0 / ROT_DIM))
    ang = pos.astype(F32)[:, None] * inv[None, :]
    cos, sin = jnp.cos(ang), jnp.sin(ang)
    n = pos.shape[0]
    rest = HEAD_B - ROT_DIM
    cos_t = jnp.concatenate([cos, cos, jnp.ones((n, rest), F32)], axis=1)
    sin_t = jnp.concatenate([-sin, sin, jnp.zeros((n, rest), F32)], axis=1)
    return cos_t, sin_t


def _reorder_rw(a):
    r, w_lo, k, v, a_lo, g = jnp.split(
        a, [D_A, D_A + LORA, 2 * D_A + LORA, 3 * D_A + LORA, 3 * D_A + 2 * LORA], axis=-1)
    return jnp.concatenate([r, k, v, g], axis=-1), jnp.concatenate([w_lo, a_lo], axis=-1)


def kernel(x_prompt, x_sample, state_shift, state_wkv, cache_k, cache_v, page_table, ln_w, w_in, mu,
           w0, w2, a0, a2, k_k, k_a, r_k, lnx_w, lnx_b, p_a, p_b, w_o, final_w):
    depth = ln_w.shape[0]
    assert depth == 1, "single-layer trunk"
    B, T, _ = x_prompt.shape
    DB, TS, _ = x_sample.shape
    assert TS == 1
    n_pages = page_table.shape[1]
    page = cache_k.shape[2]
    past = n_pages * page
    assert past % MOBA_BLOCK == 0 and MOBA_BLOCK % page == 0
    l = 0
    rw_cols = 4 * D_A + 2 * LORA

    w_rw, w_wa = _reorder_rw(w_in[l][:, :rw_cols])
    w_main = jnp.concatenate([w_rw, w_in[l][:, rw_cols:]], axis=1).astype(BF16)
    w_wa = w_wa.astype(BF16)
    mu_rw, mu_wa = _reorder_rw(mu[l][None, :])
    zeros = jnp.zeros((LORA, D_A), F32)
    w2p = jnp.concatenate([w2[l], zeros], axis=0).astype(BF16)
    a2p = jnp.concatenate([zeros, a2[l]], axis=0).astype(BF16)
    row = lambda a: a.reshape(1, -1)
    vecs = (row(w0[l]), row(a0[l]), row(k_k[l]), row(k_a[l]), row(r_k[l]), row(lnx_w[l]), row(lnx_b[l]))
    pa_bf, pb_bf, wo_bf = p_a[l].astype(BF16), p_b[l].astype(BF16), w_o[l].astype(BF16)

    xp = x_prompt.reshape(B * T, D_MODEL)
    z_p, zwa_p = input_projection(xp, ln_w[l], w_main, w_wa, normalize=True)
    o_a_p, wkv_p = rwkv_prompt(z_p, zwa_p, B, T, mu_rw, mu_wa, vecs, w2p, a2p)
    cos_p, sin_p = _rope_tables(jnp.arange(T))
    q_hm, k_hm, vt_hm, k_rows_p, v_rows_p, kmean_p = attn_prep(z_p, cos_p, sin_p, B, T)
    nb_p = T // min(T, MOBA_BLOCK)
    o_b_p = moba_prompt(q_hm, k_hm, vt_hm, kmean_p.reshape(B, nb_p, D_B), z_p, B, T)
    merged_p = merge(o_a_p, o_b_p, pa_bf, pb_bf, z_p)
    y_prompt = output_projection(merged_p, wo_bf, xp, final_w).reshape(B, T, D_MODEL)

    rows = jnp.concatenate([x_prompt[:, -1, :], x_sample[:, 0, :]], axis=0)
    xn_rows = rmsnorm_rows(rows, ln_w[l])
    shift_prompt = xn_rows[:B]
    xn_s = xn_rows[B:]
    z_s2, zwa_s2 = input_projection(jnp.concatenate([xn_s, state_shift[l]], axis=0), ln_w[l],
                                    w_main, w_wa, normalize=False)
    z_s, zprev_s = z_s2[:DB], z_s2[DB:]
    r_s, kf_s, v_s, kk_s, b_s, w_s, g_s = rwkv_rows(
        z_s[:, :4 * D_A], zprev_s[:, :4 * D_A], zwa_s2[:DB], zwa_s2[DB:], mu_rw, mu_wa,
        vecs[0], vecs[1], vecs[2], vecs[3], w2p, a2p)
    hs = lambda a: a.reshape(-1, N_HEADS_A, 1, HEAD_A)
    col = lambda a: a.reshape(-1, N_HEADS_A, HEAD_A, 1)
    wkv_s, o_a_s = rwkv_step(state_wkv[l], hs(w_s), hs(kk_s), hs(b_s), hs(kf_s), hs(r_s),
                             col(v_s), col(g_s), hs(r_k[l])[0], col(lnx_w[l])[0], col(lnx_b[l])[0])
    o_a_s = o_a_s.reshape(DB, D_A).astype(BF16)

    cos_s, sin_s = _rope_tables(past + jnp.arange(TS))
    cos_s = jnp.broadcast_to(cos_s, (DB, HEAD_B))
    sin_s = jnp.broadcast_to(sin_s, (DB, HEAD_B))
    q_rows_s, k_rows_s = rope_rows(z_s[:, COL_Q:COL_Q + D_B], z_s[:, COL_KB:COL_KB + D_B], cos_s, sin_s)
    v_rows_s = z_s[:, COL_VB:COL_VB + D_B]
    kmean_s = cache_kmean(cache_k, page_table, l)
    sel = sample_select(q_rows_s.reshape(DB, N_HEADS_B, HEAD_B), kmean_s)[:, :, :MOBA_TOPK]
    hv = lambda a: a.reshape(DB, N_HEADS_B, 1, HEAD_B)
    o_b_s = sample_attn(sel, page_table, hv(q_rows_s), hv(k_rows_s), hv(v_rows_s),
                        hv(z_s[:, COL_GB:COL_GB + D_B]), cache_k, cache_v, l)
    o_b_s = o_b_s.reshape(DB, D_B).astype(BF16)
    merged_s = merge(o_a_s, o_b_s, pa_bf, pb_bf, z_s)
    y_sample = output_projection(merged_s, wo_bf, x_sample.reshape(DB, D_MODEL), final_w)

    return (y_prompt,
            y_sample.reshape(DB, TS, D_MODEL),
            shift_prompt[None],
            wkv_p.reshape(1, B, N_HEADS_A, HEAD_A, HEAD_A),
            k_rows_p.reshape(1, B, T, N_HEADS_B, HEAD_B),
            v_rows_p.reshape(1, B, T, N_HEADS_B, HEAD_B),
            xn_s[None],
            wkv_s[None],
            k_rows_s.reshape(1, DB, TS, N_HEADS_B, HEAD_B),
            v_rows_s.reshape(1, DB, TS, N_HEADS_B, HEAD_B))
```

```python
import functools

import jax
import jax.numpy as jnp
from jax import lax
from jax.experimental import pallas as pl
from jax.experimental.pallas import tpu as pltpu

F32 = jnp.float32
BF16 = jnp.bfloat16

D_MODEL = 2048
D_A = D_MODEL // 2
HEAD_A = 64
N_HEADS_A = D_A // HEAD_A
LORA = 64
D_B = D_MODEL // 2
HEAD_B = 128
N_HEADS_B = D_B // HEAD_B
ROT_DIM = HEAD_B // 4
ROPE_THETA = 500000.0
MOBA_BLOCK = 256
MOBA_TOPK = 3
NORM_EPS = 1e-6
GN_EPS = 64e-5
NEG = -1e30

COL_R, COL_K, COL_V, COL_G = 0, D_A, 2 * D_A, 3 * D_A
COL_Q, COL_KB, COL_VB, COL_GB = 4 * D_A, 4 * D_A + D_B, 4 * D_A + 2 * D_B, 4 * D_A + 3 * D_B
COL_GA = 4 * D_A + 4 * D_B
COL_GM = COL_GA + D_MODEL
N_MAIN = COL_GM + D_MODEL

GROUP = 256
HEADS_PER_GROUP = GROUP // HEAD_A
CHUNK = 64
assert CHUNK == HEAD_A
VMEM_LIMIT = 56 * 1024 * 1024


def _cparams(sem):
    return pltpu.CompilerParams(dimension_semantics=sem, vmem_limit_bytes=VMEM_LIMIT)


def _bdot(a, b):
    return jnp.dot(a.astype(BF16), b.astype(BF16), preferred_element_type=F32)


def _bdot_nt(a, b):
    return lax.dot_general(a.astype(BF16), b.astype(BF16), (((1,), (1,)), ((), ())),
                           preferred_element_type=F32)


def _bdot_tn(a, b):
    return lax.dot_general(a.astype(BF16), b.astype(BF16), (((0,), (0,)), ((), ())),
                           preferred_element_type=F32)


def _split3(x):
    hi = x.astype(BF16)
    r1 = x - hi.astype(F32)
    mid = r1.astype(BF16)
    lo = (r1 - mid.astype(F32)).astype(BF16)
    return hi, mid, lo


def _sigmoid(x):
    return 1.0 / (1.0 + jnp.exp(-x))


def _silu(x):
    return x * _sigmoid(x)


def _softplus(x):
    return jnp.maximum(x, 0.0) + jnp.log(1.0 + jnp.exp(-jnp.abs(x)))


def _rmsnorm_rows_kernel(x_ref, w_ref, o_ref):
    x = x_ref[...]
    ms = jnp.mean(x * x, axis=-1, keepdims=True)
    o_ref[...] = (x * lax.rsqrt(ms + NORM_EPS)) * w_ref[...]


def rmsnorm_rows(x, w):
    return pl.pallas_call(
        _rmsnorm_rows_kernel,
        out_shape=jax.ShapeDtypeStruct(x.shape, F32),
        name="rmsnorm_rows",
    )(x, w.reshape(1, -1))


def _proj_kernel(x_ref, lnw_ref, w_ref, wwa_ref, z_ref, zwa_ref, xn_ref, *, normalize):
    @pl.when(pl.program_id(1) == 0)
    def _():
        x = x_ref[...]
        if normalize:
            ms = jnp.mean(x * x, axis=-1, keepdims=True)
            x = (x * lax.rsqrt(ms + NORM_EPS)) * lnw_ref[...]
        xn_ref[...] = x.astype(BF16)
        zwa_ref[...] = jnp.dot(xn_ref[...], wwa_ref[...], preferred_element_type=F32)

    z_ref[...] = jnp.dot(xn_ref[...], w_ref[...], preferred_element_type=F32)


def input_projection(x, ln_w, w_main, w_wa, *, normalize):
    m = x.shape[0]
    tm = min(m, 1024)
    tn = 1024 if m >= 1024 else 2048
    assert m % tm == 0 and N_MAIN % tn == 0
    return pl.pallas_call(
        functools.partial(_proj_kernel, normalize=normalize),
        grid=(m // tm, N_MAIN // tn),
        in_specs=[
            pl.BlockSpec((tm, D_MODEL), lambda i, j: (i, 0)),
            pl.BlockSpec((1, D_MODEL), lambda i, j: (0, 0)),
            pl.BlockSpec((D_MODEL, tn), lambda i, j: (0, j)),
            pl.BlockSpec((D_MODEL, 2 * LORA), lambda i, j: (0, 0)),
        ],
        out_specs=[
            pl.BlockSpec((tm, tn), lambda i, j: (i, j)),
            pl.BlockSpec((tm, 2 * LORA), lambda i, j: (i, 0)),
        ],
        out_shape=[
            jax.ShapeDtypeStruct((m, N_MAIN), F32),
            jax.ShapeDtypeStruct((m, 2 * LORA), F32),
        ],
        scratch_shapes=[pltpu.VMEM((tm, D_MODEL), BF16)],
        compiler_params=_cparams(("arbitrary", "arbitrary")),
        name="input_projection",
    )(x, ln_w.reshape(1, -1), w_main, w_wa)


def _head_mask(n):
    r = lax.broadcasted_iota(jnp.int32, (n, n), 0) // HEAD_A
    c = lax.broadcasted_iota(jnp.int32, (n, n), 1) // HEAD_A
    return r == c


def _segsum(x, ones_bd):
    hi = x.astype(BF16)
    lo = (x - hi.astype(F32)).astype(BF16)
    return (jnp.dot(hi, ones_bd, preferred_element_type=F32)
            + jnp.dot(lo, ones_bd, preferred_element_type=F32))


def _rwkv_prep(zm_r, zm_k, zm_v, zm_wa, w0, a0, k_k, k_a, w2p, a2p, ones_bd):
    lane = lax.broadcasted_iota(jnp.int32, zm_wa.shape, 1)
    lora_in = jnp.where(lane < LORA, jnp.tanh(zm_wa), zm_wa)
    ww = _bdot(lora_in, w2p)
    aa = _bdot(lora_in, a2p)
    w_log = -_softplus(-(w0 + ww)) - 0.5
    logw = -jnp.exp(w_log)
    a = _sigmoid(a0 + aa)
    kk = zm_k * k_k
    ss = _segsum(kk * kk, ones_bd)
    kk = kk / jnp.maximum(jnp.sqrt(ss), 1e-12)
    kf = zm_k * (1.0 + (a - 1.0) * k_a)
    return zm_r, kf, zm_v, kk, kk * a, logw


def _rwkv_post(y, r, kf, v, zm_g, r_k, lnx_w, lnx_b, ones_bd):
    inv_n = 1.0 / HEAD_A
    mean = _segsum(y, ones_bd) * inv_n
    d = y - mean
    var = _segsum(d * d, ones_bd) * inv_n
    yn = d * lax.rsqrt(var + GN_EPS) * lnx_w + lnx_b
    bonus = _segsum(r * kf * r_k, ones_bd) * v
    return (yn + bonus) * _silu(zm_g)


def _stack_heads(x, lane_head):
    return jnp.concatenate(
        [jnp.where(lane_head == h, x, 0.0) for h in range(HEADS_PER_GROUP)], axis=0)


def _chunk_precompute(insts, consts):
    strict, incl, eye, lane_head, bd_mask = consts
    c = CHUNK
    n = len(insts)

    def bd(p):
        return jnp.where(bd_mask, jnp.concatenate([p] * HEADS_PER_GROUP, axis=0), 0.0).astype(BF16)

    lhs, wt, v_bd, bdkd, p_end = [], [], [], [], []
    for r, kf, v, kk, b, logw, cl in insts:
        cl_last = cl[c - 1:c, :]
        e_neg = jnp.exp(-cl)
        e_end = jnp.exp(cl_last - cl)
        lhs.append(jnp.concatenate([kk * jnp.exp(cl - logw), r * jnp.exp(cl)], axis=0).astype(BF16))
        wt.append(jnp.concatenate([_stack_heads(b * e_neg, lane_head),
                                   _stack_heads(kf * e_neg, lane_head)], axis=0).astype(BF16))
        v_bd.append(_stack_heads(v, lane_head).astype(BF16))
        bdkd.append(jnp.concatenate([b * e_end, kf * e_end], axis=0).astype(BF16))
        p_end.append(jnp.exp(cl_last))
    att = [_bdot_nt(lhs[i], wt[i]) for i in range(n)]
    a_ab = [jnp.where(strict, att[i][:c, :4 * c], 0.0) for i in range(n)]
    a_ak = [jnp.where(strict, att[i][:c, 4 * c:], 0.0) for i in range(n)]
    m_cat = [jnp.concatenate([jnp.where(incl, att[i][c:, :4 * c], 0.0),
                              jnp.where(incl, att[i][c:, 4 * c:], 0.0)], axis=1).astype(BF16)
             for i in range(n)]
    av = [_bdot(a_ak[i], v_bd[i]) for i in range(n)]

    x = [eye - a_ab[i] for i in range(n)]
    p = [_bdot(a_ab[i], bd(a_ab[i])) for i in range(n)]
    for _ in range(4):
        px = [_bdot(jnp.concatenate([p[i], x[i]], axis=0), bd(p[i])) for i in range(n)]
        p = [px[i][:c] for i in range(n)]
        x = [x[i] + px[i][c:] for i in range(n)]
    x = [(x[i] + _bdot(x[i], bd(p[i]))).astype(BF16) for i in range(n)]
    return [dict(lhs=lhs[i], av=av[i], t=x[i], m_cat=m_cat[i], v_bd=v_bd[i], v=insts[i][2],
                 bdkd=bdkd[i], p_end=p_end[i]) for i in range(n)]


def _chunk_state_step(states, pres, consts):
    lane_head, bd_mask = consts[3], consts[4]
    c = CHUNK
    n = len(states)
    sh = [_bdot_nt(pres[i]["lhs"], states[i]) for i in range(n)]
    rhs = [sh[i][:c] + pres[i]["av"] for i in range(n)]
    u = [-_bdot(pres[i]["t"], _stack_heads(rhs[i], lane_head)) for i in range(n)]
    y = [sh[i][c:] + _bdot(pres[i]["m_cat"],
                           jnp.concatenate([_stack_heads(u[i], lane_head).astype(BF16),
                                            pres[i]["v_bd"]], axis=0)) for i in range(n)]
    upd = [_bdot_tn(jnp.concatenate([u[i], pres[i]["v"]], axis=0), pres[i]["bdkd"]) for i in range(n)]
    new = [states[i] * pres[i]["p_end"] + jnp.where(bd_mask, upd[i], 0.0) for i in range(n)]
    return y, new


def _rwkv_prompt_kernel(zr_ref, zk_ref, zv_ref, zg_ref, zwa_ref,
                        mur_ref, muk_ref, muv_ref, mug_ref, muwa_ref,
                        w0_ref, a0_ref, kk_ref, ka_ref, rk_ref, lnw_ref, lnb_ref,
                        w2_ref, a2_ref,
                        o_ref, s_out_ref,
                        s_ref, pr_ref, pk_ref, pv_ref, pg_ref, pwa_ref):
    t = pl.program_id(1)
    nseq, tr = zr_ref.shape[0], zr_ref.shape[1]

    @pl.when(t == 0)
    def _():
        for ref in (s_ref, pr_ref, pk_ref, pv_ref, pg_ref, pwa_ref):
            ref[...] = jnp.zeros_like(ref)

    def shifted(z_ref, prev_ref, mu_ref, i):
        z = z_ref[i]
        row = lax.broadcasted_iota(jnp.int32, z.shape, 0)
        prev = jnp.where(row == 0, prev_ref[i], pltpu.roll(z, 1, 0))
        prev_ref[i] = z[tr - 1:tr, :]
        return z + (prev - z) * mu_ref[...]

    bd_mask = _head_mask(GROUP)
    ones_bd = jnp.where(bd_mask, 1.0, 0.0).astype(BF16)
    ri = lax.broadcasted_iota(jnp.int32, (tr, tr), 0)
    ci = lax.broadcasted_iota(jnp.int32, (tr, tr), 1)
    tri = jnp.where((ri // CHUNK == ci // CHUNK) & (ci <= ri), 1.0, 0.0).astype(BF16)
    rc = lax.broadcasted_iota(jnp.int32, (CHUNK, GROUP), 0)
    lc = lax.broadcasted_iota(jnp.int32, (CHUNK, GROUP), 1)
    sc = lc % CHUNK
    consts = (sc < rc, sc <= rc, jnp.where(sc == rc, 1.0, 0.0), lc // HEAD_A, bd_mask)
    last = t == pl.num_programs(1) - 1

    n_ch = tr // CHUNK
    prepped, insts = [], []
    for i in range(nseq):
        zm_r = shifted(zr_ref, pr_ref, mur_ref, i)
        zm_k = shifted(zk_ref, pk_ref, muk_ref, i)
        zm_v = shifted(zv_ref, pv_ref, muv_ref, i)
        zm_g = shifted(zg_ref, pg_ref, mug_ref, i)
        zm_wa = shifted(zwa_ref, pwa_ref, muwa_ref, i)
        r, kf, v, kk, b, logw = _rwkv_prep(zm_r, zm_k, zm_v, zm_wa, w0_ref[...], a0_ref[...],
                                           kk_ref[...], ka_ref[...], w2_ref[...], a2_ref[...], ones_bd)
        cl = sum(jnp.dot(tri, piece, preferred_element_type=F32) for piece in _split3(logw))
        prepped.append((r, kf, v, zm_g))
        for ch in range(n_ch):
            sl = slice(ch * CHUNK, (ch + 1) * CHUNK)
            insts.append((r[sl], kf[sl], v[sl], kk[sl], b[sl], logw[sl], cl[sl]))
    pres = _chunk_precompute(insts, consts)

    states = [s_ref[i] for i in range(nseq)]
    ys = [[] for _ in range(nseq)]
    for ch in range(n_ch):
        y, states = _chunk_state_step(states, [pres[i * n_ch + ch] for i in range(nseq)], consts)
        for i in range(nseq):
            ys[i].append(y[i])
    for i in range(nseq):
        s_ref[i] = states[i]
        r, kf, v, zm_g = prepped[i]
        o_ref[i] = _rwkv_post(jnp.concatenate(ys[i], axis=0), r, kf, v, zm_g, rk_ref[...],
                              lnw_ref[...], lnb_ref[...], ones_bd).astype(o_ref.dtype)

    @pl.when(last)
    def _():
        for i in range(nseq):
            for h in range(HEADS_PER_GROUP):
                s_out_ref[i, h] = s_ref[i, h * HEAD_A:(h + 1) * HEAD_A, h * HEAD_A:(h + 1) * HEAD_A]


def rwkv_prompt(z_main, z_wa, mu_main, mu_wa, vecs, w2p, a2p):
    w0, a0, k_k, k_a, r_k, lnx_w, lnx_b = vecs
    batch, seq, _ = z_main.shape
    tr = min(seq, 256)
    assert seq % tr == 0 and tr % CHUNK == 0
    nt = seq // tr
    ng = D_A // GROUP

    def zspec(col):
        return pl.BlockSpec((batch, tr, GROUP), lambda g, t, c=col // GROUP: (0, t, c + g))

    def vspec(col=0):
        return pl.BlockSpec((1, GROUP), lambda g, t, c=col // GROUP: (0, c + g))

    in_specs = [
        zspec(COL_R), zspec(COL_K), zspec(COL_V), zspec(COL_G),
        pl.BlockSpec((batch, tr, 2 * LORA), lambda g, t: (0, t, 0)),
        vspec(COL_R), vspec(COL_K), vspec(COL_V), vspec(COL_G),
        pl.BlockSpec((1, 2 * LORA), lambda g, t: (0, 0)),
        vspec(), vspec(), vspec(), vspec(), vspec(), vspec(), vspec(),
        pl.BlockSpec((2 * LORA, GROUP), lambda g, t: (0, g)),
        pl.BlockSpec((2 * LORA, GROUP), lambda g, t: (0, g)),
    ]
    out_specs = [
        pl.BlockSpec((batch, tr, GROUP), lambda g, t: (0, t, g)),
        pl.BlockSpec((batch, HEADS_PER_GROUP, HEAD_A, HEAD_A), lambda g, t: (0, g, 0, 0)),
    ]
    row = lambda n: pltpu.VMEM((batch, 1, n), F32)
    return pl.pallas_call(
        _rwkv_prompt_kernel,
        grid=(ng, nt),
        in_specs=in_specs,
        out_specs=out_specs,
        out_shape=[
            jax.ShapeDtypeStruct((batch, seq, D_A), BF16),
            jax.ShapeDtypeStruct((batch, N_HEADS_A, HEAD_A, HEAD_A), F32),
        ],
        scratch_shapes=[pltpu.VMEM((batch, GROUP, GROUP), F32),
                        row(GROUP), row(GROUP), row(GROUP), row(GROUP), row(2 * LORA)],
        compiler_params=_cparams(("arbitrary", "arbitrary")),
        name="rwkv_prompt",
    )(z_main, z_main, z_main, z_main, z_wa,
      mu_main, mu_main, mu_main, mu_main, mu_wa,
      w0, a0, k_k, k_a, r_k, lnx_w, lnx_b, w2p, a2p)


def _rwkv_rows_kernel(z_ref, zp_ref, zwa_ref, zpwa_ref, mu_ref, muwa_ref,
                      w0_ref, a0_ref, kk_ref, ka_ref, w2_ref, a2_ref,
                      r_ref, kf_ref, v_ref, kkn_ref, b_ref, w_ref, g_ref):
    def shifted(z, zp, mu):
        return z + (zp - z) * mu

    mu = mu_ref[...]
    z = z_ref[...]
    zp = zp_ref[...]
    zm = [shifted(z[:, c:c + D_A], zp[:, c:c + D_A], mu[:, c:c + D_A])
          for c in (COL_R, COL_K, COL_V, COL_G)]
    zm_wa = shifted(zwa_ref[...], zpwa_ref[...], muwa_ref[...])
    ones_bd = jnp.where(_head_mask(D_A), 1.0, 0.0).astype(BF16)
    r, kf, v, kk, b, logw = _rwkv_prep(zm[0], zm[1], zm[2], zm_wa, w0_ref[...], a0_ref[...],
                                       kk_ref[...], ka_ref[...], w2_ref[...], a2_ref[...], ones_bd)
    r_ref[...] = r
    kf_ref[...] = kf
    v_ref[...] = v
    kkn_ref[...] = kk
    b_ref[...] = b
    w_ref[...] = jnp.exp(logw)
    g_ref[...] = zm[3]


def rwkv_rows(z_rw, zp_rw, z_wa, zp_wa, mu_rw, mu_wa, w0, a0, k_k, k_a, w2p, a2p):
    n = z_rw.shape[0]
    out = jax.ShapeDtypeStruct((n, D_A), F32)
    return pl.pallas_call(
        _rwkv_rows_kernel,
        out_shape=[out] * 7,
        compiler_params=pltpu.CompilerParams(vmem_limit_bytes=VMEM_LIMIT),
        name="rwkv_rows",
    )(z_rw, zp_rw, z_wa, zp_wa, mu_rw, mu_wa, w0, a0, k_k, k_a, w2p, a2p)


def _rwkv_step_kernel(s_ref, w_ref, kk_ref, b_ref, kf_ref, r_ref, v_ref, g_ref,
                      rk_ref, lnw_ref, lnb_ref, s_out_ref, o_ref):
    S = s_ref[...]
    w = w_ref[...]
    kk = kk_ref[...]
    b = b_ref[...]
    kf = kf_ref[...]
    r = r_ref[...]
    v = v_ref[...]
    sa = -jnp.sum(S * kk, axis=-1, keepdims=True)
    S = S * w + sa * b + v * kf
    s_out_ref[...] = S
    y = jnp.sum(S * r, axis=-1, keepdims=True)
    mean = jnp.mean(y, axis=1, keepdims=True)
    d = y - mean
    var = jnp.mean(d * d, axis=1, keepdims=True)
    yn = d * lax.rsqrt(var + GN_EPS) * lnw_ref[...] + lnb_ref[...]
    bonus = jnp.sum(r * kf * rk_ref[...], axis=-1, keepdims=True) * v
    o_ref[...] = (yn + bonus) * _silu(g_ref[...])


def rwkv_step(state, w, kk, b, kf, r, v, g, r_k, lnx_w, lnx_b):
    n = state.shape[0]
    h = N_HEADS_A
    sq = pl.Squeezed()
    lane_vec = pl.BlockSpec((sq, h, 1, HEAD_A), lambda i: (i, 0, 0, 0))
    col_vec = pl.BlockSpec((sq, h, HEAD_A, 1), lambda i: (i, 0, 0, 0))
    return pl.pallas_call(
        _rwkv_step_kernel,
        grid=(n,),
        in_specs=[pl.BlockSpec((sq, h, HEAD_A, HEAD_A), lambda i: (i, 0, 0, 0)),
                  lane_vec, lane_vec, lane_vec, lane_vec, lane_vec, col_vec, col_vec,
                  pl.BlockSpec((h, 1, HEAD_A), lambda i: (0, 0, 0)),
                  pl.BlockSpec((h, HEAD_A, 1), lambda i: (0, 0, 0)),
                  pl.BlockSpec((h, HEAD_A, 1), lambda i: (0, 0, 0))],
        out_specs=[pl.BlockSpec((sq, h, HEAD_A, HEAD_A), lambda i: (i, 0, 0, 0)), col_vec],
        out_shape=[jax.ShapeDtypeStruct((n, h, HEAD_A, HEAD_A), F32),
                   jax.ShapeDtypeStruct((n, h, HEAD_A, 1), F32)],
        compiler_params=_cparams(("arbitrary",)),
        name="rwkv_step",
    )(state, w, kk, b, kf, r, v, g, r_k, lnx_w, lnx_b)


def _rope(x, cos_t, sin_t, lane):
    partner = jnp.where(lane < ROT_DIM // 2, pltpu.roll(x, HEAD_B - ROT_DIM // 2, 1),
                        pltpu.roll(x, ROT_DIM // 2, 1))
    return x * cos_t + partner * sin_t


def _attn_prep_kernel(zq_ref, zk_ref, zv_ref, cos_ref, sin_ref,
                      q_ref, kh_ref, vt_ref, ko_ref, vo_ref, km_ref):
    cos_t = cos_ref[...]
    sin_t = sin_ref[...]
    lane = lax.broadcasted_iota(jnp.int32, cos_t.shape, 1)
    vo_ref[...] = zv_ref[...]
    for h in range(N_HEADS_B):
        sl = slice(h * HEAD_B, (h + 1) * HEAD_B)
        q = _rope(zq_ref[:, sl], cos_t, sin_t, lane)
        k = _rope(zk_ref[:, sl], cos_t, sin_t, lane)
        q_ref[h] = q.astype(q_ref.dtype)
        kh_ref[h] = k.astype(kh_ref.dtype)
        vt_ref[h] = zv_ref[:, sl].T.astype(vt_ref.dtype)
        ko_ref[:, sl] = k
        km_ref[:, sl] = jnp.mean(k, axis=0, keepdims=True)


def attn_prep(z_main, cos_t, sin_t, batch, seq):
    tr = min(seq, MOBA_BLOCK)
    nt = seq // tr
    m = batch * seq
    sq = pl.Squeezed()
    zspec = lambda col: pl.BlockSpec((tr, D_B), lambda b, t, c=col // D_B: (b * nt + t, c))
    tab = pl.BlockSpec((tr, HEAD_B), lambda b, t: (t, 0))
    hm = pl.BlockSpec((sq, N_HEADS_B, tr, HEAD_B), lambda b, t: (b, 0, t, 0))
    hm_t = pl.BlockSpec((sq, N_HEADS_B, HEAD_B, tr), lambda b, t: (b, 0, 0, t))
    rows = pl.BlockSpec((tr, D_B), lambda b, t: (b * nt + t, 0))
    hm_shape = jax.ShapeDtypeStruct((batch, N_HEADS_B, seq, HEAD_B), BF16)
    return pl.pallas_call(
        _attn_prep_kernel,
        grid=(batch, nt),
        in_specs=[zspec(COL_Q), zspec(COL_KB), zspec(COL_VB), tab, tab],
        out_specs=[hm, hm, hm_t, rows, rows,
                   pl.BlockSpec((sq, 1, D_B), lambda b, t: (b * nt + t, 0, 0))],
        out_shape=[hm_shape, hm_shape,
                   jax.ShapeDtypeStruct((batch, N_HEADS_B, HEAD_B, seq), BF16),
                   jax.ShapeDtypeStruct((m, D_B), F32), jax.ShapeDtypeStruct((m, D_B), F32),
                   jax.ShapeDtypeStruct((batch * nt, 1, D_B), F32)],
        compiler_params=_cparams(("arbitrary", "arbitrary")),
        name="attn_prep",
    )(z_main, z_main, z_main, cos_t, sin_t)


def _rope_rows_kernel(zq_ref, zk_ref, cos_ref, sin_ref, q_ref, k_ref):
    cos_t = cos_ref[...]
    sin_t = sin_ref[...]
    lane = lax.broadcasted_iota(jnp.int32, cos_t.shape, 1)
    for h in range(N_HEADS_B):
        sl = slice(h * HEAD_B, (h + 1) * HEAD_B)
        q_ref[:, sl] = _rope(zq_ref[:, sl], cos_t, sin_t, lane).astype(BF16).astype(F32)
        k_ref[:, sl] = _rope(zk_ref[:, sl], cos_t, sin_t, lane)


def rope_rows(zq, zk, cos_t, sin_t):
    out = jax.ShapeDtypeStruct(zq.shape, F32)
    return pl.pallas_call(_rope_rows_kernel, out_shape=[out, out], name="rope_rows")(zq, zk, cos_t, sin_t)


def _moba_prompt_kernel(q_ref, k_ref, vt_ref, km_ref, zg_ref, o_ref, sel_ref):
    qb = pl.program_id(1)
    nseq, tq = q_ref.shape[0], q_ref.shape[1]
    nb = km_ref.shape[1]
    scale = HEAD_B ** -0.5
    nt_dims = (((1,), (1,)), ((), ()))
    qs = [q_ref[i] for i in range(nseq)]

    blk = lax.broadcasted_iota(jnp.int32, (nb, tq), 0)
    past = blk < qb
    for i in range(nseq):
        gate = sum(lax.dot_general(piece, qs[i], nt_dims, preferred_element_type=F32)
                   for piece in _split3(km_ref[i]))
        for n in range(nb):
            g_n = gate[n:n + 1, :]
            beats = (gate > g_n) | ((gate == g_n) & (blk < n))
            cnt = jnp.sum(jnp.where(beats & past, 1.0, 0.0), axis=0, keepdims=True)
            sel_ref[i, n:n + 1, :] = jnp.where(cnt < MOBA_TOPK, 1.0, 0.0)

    def scores(i, n):
        start = pl.multiple_of(n * tq, tq)
        return lax.dot_general(k_ref[i, pl.ds(start, tq), :], qs[i], nt_dims,
                               preferred_element_type=F32)

    def attend(i, s, n, m, l, acc):
        m_new = jnp.maximum(m, jnp.max(s, axis=0, keepdims=True))
        alpha = jnp.exp(m - m_new)
        p = jnp.exp(s - m_new)
        l = alpha * l + jnp.sum(p, axis=0, keepdims=True)
        start = pl.multiple_of(n * tq, tq)
        acc = alpha * acc + jnp.dot(vt_ref[i, :, pl.ds(start, tq)], p.astype(BF16),
                                    preferred_element_type=F32)
        return m_new, l, acc

    def body(n, carry):
        out = []
        for i in range(nseq):
            s_cur, m, l, acc = carry[i]
            s_next = scores(i, n + 1)
            s = jnp.where(sel_ref[i, pl.ds(n, 1), :] > 0.0, s_cur * scale, NEG)
            out.append((s_next,) + attend(i, s, n, m, l, acc))
        return tuple(out)

    init = tuple((scores(i, 0), jnp.full((1, tq), NEG, F32), jnp.zeros((1, tq), F32),
                  jnp.zeros((HEAD_B, tq), F32)) for i in range(nseq))
    carry = lax.fori_loop(0, qb, body, init)
    ki = lax.broadcasted_iota(jnp.int32, (tq, tq), 0)
    qi = lax.broadcasted_iota(jnp.int32, (tq, tq), 1)
    for i in range(nseq):
        s_own, m, l, acc = carry[i]
        s = jnp.where(ki <= qi, s_own * scale, NEG)
        _, l, acc = attend(i, s, qb, m, l, acc)
        o_ref[i] = ((acc / l).T * _silu(zg_ref[i])).astype(o_ref.dtype)


def moba_prompt(q_hm, k_hm, vt_hm, kmean, z_main):
    batch, _, seq, _ = q_hm.shape
    tq = min(seq, MOBA_BLOCK)
    nb = seq // tq
    sq = pl.Squeezed()
    return pl.pallas_call(
        _moba_prompt_kernel,
        grid=(N_HEADS_B, nb),
        in_specs=[
            pl.BlockSpec((batch, sq, tq, HEAD_B), lambda h, i: (0, h, i, 0)),
            pl.BlockSpec((batch, sq, seq, HEAD_B), lambda h, i: (0, h, 0, 0)),
            pl.BlockSpec((batch, sq, HEAD_B, seq), lambda h, i: (0, h, 0, 0)),
            pl.BlockSpec((batch, nb, HEAD_B), lambda h, i: (0, 0, h)),
            pl.BlockSpec((batch, tq, HEAD_B), lambda h, i: (0, i, COL_GB // HEAD_B + h)),
        ],
        out_specs=pl.BlockSpec((batch, tq, HEAD_B), lambda h, i: (0, i, h)),
        out_shape=jax.ShapeDtypeStruct((batch, seq, D_B), BF16),
        scratch_shapes=[pltpu.VMEM((batch, nb, tq), F32)],
        compiler_params=_cparams(("arbitrary", "arbitrary")),
        name="moba_prompt",
    )(q_hm, k_hm, vt_hm, kmean, z_main)


KMEAN_PAGES = 8


def _cache_kmean_kernel(pt_ref, *refs, pages_per_block):
    del pt_ref
    k_refs, o_ref = refs[:-1], refs[-1]
    sums = [jnp.sum(k_ref[...], axis=0) for k_ref in k_refs]
    for j in range(len(k_refs) // pages_per_block):
        blk = sums[j * pages_per_block]
        for i in range(1, pages_per_block):
            blk = blk + sums[j * pages_per_block + i]
        o_ref[j] = blk * (1.0 / MOBA_BLOCK)


def cache_kmean(cache_k, page_table, layer):
    n, n_pages = page_table.shape
    page = cache_k.shape[2]
    ppb = MOBA_BLOCK // page
    pps = min(KMEAN_PAGES, n_pages)
    assert n_pages % pps == 0 and pps % ppb == 0
    sq = pl.Squeezed()
    page_spec = lambda i: pl.BlockSpec(
        (sq, sq, page, N_HEADS_B, HEAD_B), lambda b, s, pt, i=i: (layer, pt[b, s * pps + i], 0, 0, 0))
    return pl.pallas_call(
        functools.partial(_cache_kmean_kernel, pages_per_block=ppb),
        grid_spec=pltpu.PrefetchScalarGridSpec(
            num_scalar_prefetch=1,
            grid=(n, n_pages // pps),
            in_specs=[page_spec(i) for i in range(pps)],
            out_specs=pl.BlockSpec((sq, pps // ppb, N_HEADS_B, HEAD_B), lambda b, s, pt: (b, s, 0, 0)),
        ),
        out_shape=jax.ShapeDtypeStruct((n, n_pages // ppb, N_HEADS_B, HEAD_B), F32),
        compiler_params=_cparams(("arbitrary", "arbitrary")),
        name="cache_kmean",
    )(page_table, *([cache_k] * pps))


def _sample_select_kernel(q_ref, km_ref, sel_ref):
    nb = km_ref.shape[0]
    ri = lax.broadcasted_iota(jnp.int32, (nb, nb), 0)
    ci = lax.broadcasted_iota(jnp.int32, (nb, nb), 1)
    lane = lax.broadcasted_iota(jnp.int32, (1, 128), 1)
    blk_row = lax.broadcasted_iota(jnp.int32, (1, nb), 1).astype(F32)
    for h in range(N_HEADS_B):
        km = km_ref[:, h, :]
        g_col = jnp.sum(km * q_ref[h:h + 1, :], axis=1, keepdims=True)
        g_row = jnp.sum(jnp.where(ri == ci, g_col, 0.0), axis=0, keepdims=True)
        beats = (g_col > g_row) | ((g_col == g_row) & (ri < ci))
        rank = jnp.sum(jnp.where(beats, 1.0, 0.0), axis=0, keepdims=True)
        out = jnp.zeros((1, 128), F32)
        for r in range(MOBA_TOPK):
            idx = jnp.sum(jnp.where(rank == float(r), blk_row, 0.0), axis=1, keepdims=True)
            out = jnp.where(lane == r, idx, out)
        sel_ref[h:h + 1, :] = out.astype(jnp.int32)


def sample_select(q_s, kmean_s):
    n, nb = kmean_s.shape[:2]
    sq = pl.Squeezed()
    return pl.pallas_call(
        _sample_select_kernel,
        grid=(n,),
        in_specs=[pl.BlockSpec((sq, N_HEADS_B, HEAD_B), lambda b: (b, 0, 0)),
                  pl.BlockSpec((sq, nb, N_HEADS_B, HEAD_B), lambda b: (b, 0, 0, 0))],
        out_specs=pl.BlockSpec((sq, N_HEADS_B, 128), lambda b: (b, 0, 0)),
        out_shape=jax.ShapeDtypeStruct((n, N_HEADS_B, 128), jnp.int32),
        compiler_params=_cparams(("arbitrary",)),
        name="sample_select",
    )(q_s, kmean_s)


def _sample_attn_kernel(sel_ref, pt_ref, q_ref, kn_ref, vn_ref, zg_ref, ck_ref, cv_ref, o_ref,
                        kbuf, vbuf, sem, *, layer, page, ppb):
    b = pl.program_id(0)
    nb_steps = pl.num_programs(0)
    n_sel = MOBA_TOPK * ppb
    scale = HEAD_B ** -0.5

    def copies(bb, slot):
        out = []
        for h in range(N_HEADS_B):
            for j in range(n_sel):
                pg = pt_ref[bb, sel_ref[bb, h, j // ppb] * ppb + j % ppb]
                dst = pl.ds(j * page, page)
                out.append(pltpu.make_async_copy(ck_ref.at[layer, pg, :, h, :],
                                                 kbuf.at[slot, h, dst, :], sem.at[slot, 0]))
                out.append(pltpu.make_async_copy(cv_ref.at[layer, pg, :, h, :],
                                                 vbuf.at[slot, h, dst, :], sem.at[slot, 1]))
        return out

    slot = b % 2

    @pl.when(b == 0)
    def _():
        for c in copies(0, 0):
            c.start()

    @pl.when(b + 1 < nb_steps)
    def _():
        for c in copies(b + 1, 1 - slot):
            c.start()

    for c in copies(b, slot):
        c.wait()

    for h in range(N_HEADS_B):
        q = q_ref[h]
        k = kbuf[slot, h]
        v = vbuf[slot, h]
        s = jnp.sum(k * q, axis=1, keepdims=True) * scale
        s_own = jnp.sum(kn_ref[h] * q, axis=1, keepdims=True) * scale
        m = jnp.maximum(jnp.max(s, axis=0, keepdims=True), s_own)
        p = jnp.exp(s - m)
        p_own = jnp.exp(s_own - m)
        l = jnp.sum(p, axis=0, keepdims=True) + p_own
        acc = jnp.sum(p * v, axis=0, keepdims=True) + p_own * vn_ref[h]
        o_ref[h] = (acc / l) * _silu(zg_ref[h])


def sample_attn(sel, page_table, q_s, k_new, v_new, zgb, cache_k, cache_v, layer):
    n = q_s.shape[0]
    page = cache_k.shape[2]
    ppb = MOBA_BLOCK // page
    rows = MOBA_TOPK * ppb * page
    sq = pl.Squeezed()
    vec = pl.BlockSpec((sq, N_HEADS_B, 1, HEAD_B), lambda b, sel, pt: (b, 0, 0, 0))
    hbm = pl.BlockSpec(memory_space=pl.ANY)
    return pl.pallas_call(
        functools.partial(_sample_attn_kernel, layer=layer, page=page, ppb=ppb),
        grid_spec=pltpu.PrefetchScalarGridSpec(
            num_scalar_prefetch=2,
            grid=(n,),
            in_specs=[vec, vec, vec, vec, hbm, hbm],
            out_specs=vec,
            scratch_shapes=[pltpu.VMEM((2, N_HEADS_B, rows, HEAD_B), F32),
                            pltpu.VMEM((2, N_HEADS_B, rows, HEAD_B), F32),
                            pltpu.SemaphoreType.DMA((2, 2))],
        ),
        out_shape=jax.ShapeDtypeStruct((n, N_HEADS_B, 1, HEAD_B), F32),
        compiler_params=_cparams(("arbitrary",)),
        name="sample_attn",
    )(sel, page_table, q_s, k_new, v_new, zgb, cache_k, cache_v)


def _merge_kernel(oa_ref, ob_ref, pa_ref, pb_ref, zga_ref, zgb_ref, o_ref):
    ya = jnp.dot(oa_ref[...], pa_ref[...], preferred_element_type=F32)
    yb = jnp.dot(ob_ref[...], pb_ref[...], preferred_element_type=F32)
    o_ref[...] = (_sigmoid(zga_ref[...]) * ya + _sigmoid(zgb_ref[...]) * yb).astype(o_ref.dtype)


def merge(o_a, o_b, p_a, p_b, z_main):
    m = o_a.shape[0]
    tm = min(m, 1024)
    tn = 512
    return pl.pallas_call(
        _merge_kernel,
        grid=(m // tm, D_MODEL // tn),
        in_specs=[
            pl.BlockSpec((tm, D_A), lambda i, j: (i, 0)),
            pl.BlockSpec((tm, D_B), lambda i, j: (i, 0)),
            pl.BlockSpec((D_A, tn), lambda i, j: (0, j)),
            pl.BlockSpec((D_B, tn), lambda i, j: (0, j)),
            pl.BlockSpec((tm, tn), lambda i, j: (i, COL_GA // tn + j)),
            pl.BlockSpec((tm, tn), lambda i, j: (i, COL_GM // tn + j)),
        ],
        out_specs=pl.BlockSpec((tm, tn), lambda i, j: (i, j)),
        out_shape=jax.ShapeDtypeStruct((m, D_MODEL), BF16),
        compiler_params=_cparams(("arbitrary", "arbitrary")),
        name="merge",
    )(o_a, o_b, p_a, p_b, z_main, z_main)


def _out_kernel(mg_ref, wo_ref, x_ref, fw_ref, y_ref):
    h = x_ref[...] + jnp.dot(mg_ref[...], wo_ref[...], preferred_element_type=F32)
    ms = jnp.mean(h * h, axis=-1, keepdims=True)
    y_ref[...] = (h * lax.rsqrt(ms + NORM_EPS)) * fw_ref[...]


def output_projection(merged, w_o, x, final_w):
    m = x.shape[0]
    tm = min(m, 512)
    return pl.pallas_call(
        _out_kernel,
        grid=(m // tm,),
        in_specs=[
            pl.BlockSpec((tm, D_MODEL), lambda i: (i, 0)),
            pl.BlockSpec((D_MODEL, D_MODEL), lambda i: (0, 0)),
            pl.BlockSpec((tm, D_MODEL), lambda i: (i, 0)),
            pl.BlockSpec((1, D_MODEL), lambda i: (0, 0)),
        ],
        out_specs=pl.BlockSpec((tm, D_MODEL), lambda i: (i, 0)),
        out_shape=jax.ShapeDtypeStruct((m, D_MODEL), F32),
        compiler_params=_cparams(("arbitrary",)),
        name="output_projection",
    )(merged, w_o, x, final_w.reshape(1, -1))


def _rope_tables(pos):
    half = ROT_DIM // 2
    inv = jnp.power(jnp.float32(ROPE_THETA), -jnp.arange(half, dtype=F32) * (2.0 / ROT_DIM))
    ang = pos.astype(F32)[:, None] * inv[None, :]
    cos, sin = jnp.cos(ang), jnp.sin(ang)
    n = pos.shape[0]
    rest = HEAD_B - ROT_DIM
    cos_t = jnp.concatenate([cos, cos, jnp.ones((n, rest), F32)], axis=1)
    sin_t = jnp.concatenate([-sin, sin, jnp.zeros((n, rest), F32)], axis=1)
    return cos_t, sin_t


def _reorder_rw(a):
    r, w_lo, k, v, a_lo, g = jnp.split(
        a, [D_A, D_A + LORA, 2 * D_A + LORA, 3 * D_A + LORA, 3 * D_A + 2 * LORA], axis=-1)
    return jnp.concatenate([r, k, v, g], axis=-1), jnp.concatenate([w_lo, a_lo], axis=-1)


def kernel(x_prompt, x_sample, state_shift, state_wkv, cache_k, cache_v, page_table, ln_w, w_in, mu,
           w0, w2, a0, a2, k_k, k_a, r_k, lnx_w, lnx_b, p_a, p_b, w_o, final_w):
    depth = ln_w.shape[0]
    assert depth == 1, "single-layer trunk"
    B, T, _ = x_prompt.shape
    DB, TS, _ = x_sample.shape
    assert TS == 1
    n_pages = page_table.shape[1]
    page = cache_k.shape[2]
    past = n_pages * page
    assert past % MOBA_BLOCK == 0 and MOBA_BLOCK % page == 0
    l = 0
    rw_cols = 4 * D_A + 2 * LORA

    w_rw, w_wa = _reorder_rw(w_in[l][:, :rw_cols])
    w_main = jnp.concatenate([w_rw, w_in[l][:, rw_cols:]], axis=1).astype(BF16)
    w_wa = w_wa.astype(BF16)
    mu_rw, mu_wa = _reorder_rw(mu[l][None, :])
    zeros = jnp.zeros((LORA, D_A), F32)
    w2p = jnp.concatenate([w2[l], zeros], axis=0).astype(BF16)
    a2p = jnp.concatenate([zeros, a2[l]], axis=0).astype(BF16)
    row = lambda a: a.reshape(1, -1)
    vecs = (row(w0[l]), row(a0[l]), row(k_k[l]), row(k_a[l]), row(r_k[l]), row(lnx_w[l]), row(lnx_b[l]))
    pa_bf, pb_bf, wo_bf = p_a[l].astype(BF16), p_b[l].astype(BF16), w_o[l].astype(BF16)

    rows = jnp.concatenate([x_prompt[:, -1, :], x_sample[:, 0, :]], axis=0)
    xn_rows = rmsnorm_rows(rows, ln_w[l])
    shift_prompt = xn_rows[:B]
    xn_s = xn_rows[B:]
    xp = x_prompt.reshape(B * T, D_MODEL)
    z_p, zwa_p = input_projection(xp, ln_w[l], w_main, w_wa, normalize=True)
    z_s2, zwa_s2 = input_projection(jnp.concatenate([xn_s, state_shift[l]], axis=0), ln_w[l],
                                    w_main, w_wa, normalize=False)

    z_p3 = z_p.reshape(B, T, N_MAIN)
    o_a_p, wkv_p = rwkv_prompt(z_p3, zwa_p.reshape(B, T, 2 * LORA), mu_rw, mu_wa, vecs, w2p, a2p)
    cos_p, sin_p = _rope_tables(jnp.arange(T))
    q_hm, k_hm, vt_hm, k_rows_p, v_rows_p, kmean_p = attn_prep(z_p, cos_p, sin_p, B, T)
    nb_p = T // min(T, MOBA_BLOCK)
    o_b_p = moba_prompt(q_hm, k_hm, vt_hm, kmean_p.reshape(B, nb_p, D_B), z_p3)
    merged_p = merge(o_a_p.reshape(B * T, D_A), o_b_p.reshape(B * T, D_B), pa_bf, pb_bf, z_p)
    y_prompt = output_projection(merged_p, wo_bf, xp, final_w).reshape(B, T, D_MODEL)

    z_s, zprev_s = z_s2[:DB], z_s2[DB:]
    r_s, kf_s, v_s, kk_s, b_s, w_s, g_s = rwkv_rows(
        z_s[:, :4 * D_A], zprev_s[:, :4 * D_A], zwa_s2[:DB], zwa_s2[DB:], mu_rw, mu_wa,
        vecs[0], vecs[1], vecs[2], vecs[3], w2p, a2p)
    hs = lambda a: a.reshape(-1, N_HEADS_A, 1, HEAD_A)
    col = lambda a: a.reshape(-1, N_HEADS_A, HEAD_A, 1)
    wkv_s, o_a_s = rwkv_step(state_wkv[l], hs(w_s), hs(kk_s), hs(b_s), hs(kf_s), hs(r_s),
                             col(v_s), col(g_s), hs(r_k[l])[0], col(lnx_w[l])[0], col(lnx_b[l])[0])
    o_a_s = o_a_s.reshape(DB, D_A).astype(BF16)

    cos_s, sin_s = _rope_tables(past + jnp.arange(TS))
    cos_s = jnp.broadcast_to(cos_s, (DB, HEAD_B))
    sin_s = jnp.broadcast_to(sin_s, (DB, HEAD_B))
    q_rows_s, k_rows_s = rope_rows(z_s[:, COL_Q:COL_Q + D_B], z_s[:, COL_KB:COL_KB + D_B], cos_s, sin_s)
    v_rows_s = z_s[:, COL_VB:COL_VB + D_B]
    kmean_s = cache_kmean(cache_k, page_table, l)
    sel = sample_select(q_rows_s.reshape(DB, N_HEADS_B, HEAD_B), kmean_s)[:, :, :MOBA_TOPK]
    hv = lambda a: a.reshape(DB, N_HEADS_B, 1, HEAD_B)
    o_b_s = sample_attn(sel, page_table, hv(q_rows_s), hv(k_rows_s), hv(v_rows_s),
                        hv(z_s[:, COL_GB:COL_GB + D_B]), cache_k, cache_v, l)
    o_b_s = o_b_s.reshape(DB, D_B).astype(BF16)
    merged_s = merge(o_a_s, o_b_s, pa_bf, pb_bf, z_s)
    y_sample = output_projection(merged_s, wo_bf, x_sample.reshape(DB, D_MODEL), final_w)

    return (y_prompt,
            y_sample.reshape(DB, TS, D_MODEL),
            shift_prompt[None],
            wkv_p.reshape(1, B, N_HEADS_A, HEAD_A, HEAD_A),
            k_rows_p.reshape(1, B, T, N_HEADS_B, HEAD_B),
            v_rows_p.reshape(1, B, T, N_HEADS_B, HEAD_B),
            xn_s[None],
            wkv_s[None],
            k_rows_s.reshape(1, DB, TS, N_HEADS_B, HEAD_B),
            v_rows_s.reshape(1, DB, TS, N_HEADS_B, HEAD_B))
```

```python
import functools

import jax
import jax.numpy as jnp
from jax import lax
from jax.experimental import pallas as pl
from jax.experimental.pallas import tpu as pltpu

F32 = jnp.float32
BF16 = jnp.bfloat16

D_MODEL = 2048
D_A = D_MODEL // 2
HEAD_A = 64
N_HEADS_A = D_A // HEAD_A
LORA = 64
D_B = D_MODEL // 2
HEAD_B = 128
N_HEADS_B = D_B // HEAD_B
ROT_DIM = HEAD_B // 4
ROPE_THETA = 500000.0
MOBA_BLOCK = 256
MOBA_TOPK = 3
NORM_EPS = 1e-6
GN_EPS = 64e-5
NEG = -1e30
LOG2_E = 1.4426950408889634

COL_R, COL_K, COL_V, COL_G = 0, D_A, 2 * D_A, 3 * D_A
COL_Q, COL_KB, COL_VB, COL_GB = 4 * D_A, 4 * D_A + D_B, 4 * D_A + 2 * D_B, 4 * D_A + 3 * D_B
COL_GA = 4 * D_A + 4 * D_B
COL_GM = COL_GA + D_MODEL
N_MAIN = COL_GM + D_MODEL

GROUP = 256
HEADS_PER_GROUP = GROUP // HEAD_A
CHUNK = 64
assert CHUNK == HEAD_A
VMEM_LIMIT = 56 * 1024 * 1024


def _cparams(sem):
    return pltpu.CompilerParams(dimension_semantics=sem, vmem_limit_bytes=VMEM_LIMIT)


def _bdot(a, b):
    return jnp.dot(a.astype(BF16), b.astype(BF16), preferred_element_type=F32)


def _bdot_nt(a, b):
    return lax.dot_general(a.astype(BF16), b.astype(BF16), (((1,), (1,)), ((), ())),
                           preferred_element_type=F32)


def _bdot_tn(a, b):
    return lax.dot_general(a.astype(BF16), b.astype(BF16), (((0,), (0,)), ((), ())),
                           preferred_element_type=F32)


def _split3(x):
    hi = x.astype(BF16)
    r1 = x - hi.astype(F32)
    mid = r1.astype(BF16)
    lo = (r1 - mid.astype(F32)).astype(BF16)
    return hi, mid, lo


def _sigmoid(x):
    return 1.0 / (1.0 + jnp.exp(-x))


def _silu(x):
    return x * _sigmoid(x)


def _softplus(x):
    return jnp.maximum(x, 0.0) + jnp.log(1.0 + jnp.exp(-jnp.abs(x)))


def _rmsnorm_rows_kernel(x_ref, w_ref, o_ref):
    x = x_ref[...]
    ms = jnp.mean(x * x, axis=-1, keepdims=True)
    o_ref[...] = (x * lax.rsqrt(ms + NORM_EPS)) * w_ref[...]


def rmsnorm_rows(x, w):
    return pl.pallas_call(
        _rmsnorm_rows_kernel,
        out_shape=jax.ShapeDtypeStruct(x.shape, F32),
        name="rmsnorm_rows",
    )(x, w.reshape(1, -1))


def _proj_kernel(x_ref, lnw_ref, w_ref, wwa_ref, z_ref, zwa_ref, xn_ref, *, normalize):
    @pl.when(pl.program_id(1) == 0)
    def _():
        x = x_ref[...]
        if normalize:
            ms = jnp.mean(x * x, axis=-1, keepdims=True)
            x = (x * lax.rsqrt(ms + NORM_EPS)) * lnw_ref[...]
        xn_ref[...] = x.astype(BF16)
        zwa_ref[...] = jnp.dot(xn_ref[...], wwa_ref[...], preferred_element_type=F32)

    z_ref[...] = jnp.dot(xn_ref[...], w_ref[...], preferred_element_type=F32)


def input_projection(x, ln_w, w_main, w_wa, *, normalize):
    m = x.shape[0]
    tm = min(m, 1024)
    tn = 1024 if m >= 1024 else 2048
    assert m % tm == 0 and N_MAIN % tn == 0
    return pl.pallas_call(
        functools.partial(_proj_kernel, normalize=normalize),
        grid=(m // tm, N_MAIN // tn),
        in_specs=[
            pl.BlockSpec((tm, D_MODEL), lambda i, j: (i, 0)),
            pl.BlockSpec((1, D_MODEL), lambda i, j: (0, 0)),
            pl.BlockSpec((D_MODEL, tn), lambda i, j: (0, j)),
            pl.BlockSpec((D_MODEL, 2 * LORA), lambda i, j: (0, 0)),
        ],
        out_specs=[
            pl.BlockSpec((tm, tn), lambda i, j: (i, j)),
            pl.BlockSpec((tm, 2 * LORA), lambda i, j: (i, 0)),
        ],
        out_shape=[
            jax.ShapeDtypeStruct((m, N_MAIN), F32),
            jax.ShapeDtypeStruct((m, 2 * LORA), F32),
        ],
        scratch_shapes=[pltpu.VMEM((tm, D_MODEL), BF16)],
        compiler_params=_cparams(("arbitrary", "arbitrary")),
        name="input_projection",
    )(x, ln_w.reshape(1, -1), w_main, w_wa)


def _head_mask(n):
    r = lax.broadcasted_iota(jnp.int32, (n, n), 0) // HEAD_A
    c = lax.broadcasted_iota(jnp.int32, (n, n), 1) // HEAD_A
    return r == c


def _segsum(x, ones_bd):
    hi = x.astype(BF16)
    lo = (x - hi.astype(F32)).astype(BF16)
    return (jnp.dot(hi, ones_bd, preferred_element_type=F32)
            + jnp.dot(lo, ones_bd, preferred_element_type=F32))


def _rwkv_prep(zm_r, zm_k, zm_v, zm_wa, w0, a0, k_k, k_a, w2p, a2p, ones_bd):
    lane = lax.broadcasted_iota(jnp.int32, zm_wa.shape, 1)
    lora_in = jnp.where(lane < LORA, jnp.tanh(zm_wa), zm_wa)
    ww = _bdot(lora_in, w2p)
    aa = _bdot(lora_in, a2p)
    w_log = -_softplus(-(w0 + ww)) - 0.5
    logw = -jnp.exp(w_log)
    a = _sigmoid(a0 + aa)
    kk = zm_k * k_k
    ss = _segsum(kk * kk, ones_bd)
    kk = kk / jnp.maximum(jnp.sqrt(ss), 1e-12)
    kf = zm_k * (1.0 + (a - 1.0) * k_a)
    return zm_r, kf, zm_v, kk, kk * a, logw


def _rwkv_post(y, r, kf, v, zm_g, r_k, lnx_w, lnx_b, ones_bd):
    inv_n = 1.0 / HEAD_A
    mean = _segsum(y, ones_bd) * inv_n
    d = y - mean
    var = _segsum(d * d, ones_bd) * inv_n
    yn = d * lax.rsqrt(var + GN_EPS) * lnx_w + lnx_b
    bonus = _segsum(r * kf * r_k, ones_bd) * v
    return (yn + bonus) * _silu(zm_g)


def _stack_heads(x, lane_head):
    return jnp.concatenate(
        [jnp.where(lane_head == h, x, 0.0) for h in range(HEADS_PER_GROUP)], axis=0)


def _chunk_precompute(insts, consts):
    strict, incl, eye, lane_head, bd_mask = consts
    c = CHUNK
    n = len(insts)

    def bd(p):
        return jnp.where(bd_mask, jnp.concatenate([p] * HEADS_PER_GROUP, axis=0), 0.0).astype(BF16)

    lhs, wt, v_bd, bdkd, p_end = [], [], [], [], []
    for r, kf, v, kk, b, logw, cl in insts:
        cl_last = cl[c - 1:c, :]
        e_neg = jnp.exp(-cl)
        e_end = jnp.exp(cl_last - cl)
        lhs.append(jnp.concatenate([kk * jnp.exp(cl - logw), r * jnp.exp(cl)], axis=0).astype(BF16))
        wt.append(jnp.concatenate([_stack_heads(b * e_neg, lane_head),
                                   _stack_heads(kf * e_neg, lane_head)], axis=0).astype(BF16))
        v_bd.append(_stack_heads(v, lane_head).astype(BF16))
        bdkd.append(jnp.concatenate([b * e_end, kf * e_end], axis=0).astype(BF16))
        p_end.append(jnp.exp(cl_last))
    att = [_bdot_nt(lhs[i], wt[i]) for i in range(n)]
    a_ab = [jnp.where(strict, att[i][:c, :4 * c], 0.0) for i in range(n)]
    a_ak = [jnp.where(strict, att[i][:c, 4 * c:], 0.0) for i in range(n)]
    m_cat = [jnp.concatenate([jnp.where(incl, att[i][c:, :4 * c], 0.0),
                              jnp.where(incl, att[i][c:, 4 * c:], 0.0)], axis=1).astype(BF16)
             for i in range(n)]
    av = [_bdot(a_ak[i], v_bd[i]) for i in range(n)]

    x = [eye - a_ab[i] for i in range(n)]
    p = [_bdot(a_ab[i], bd(a_ab[i])) for i in range(n)]
    for _ in range(4):
        px = [_bdot(jnp.concatenate([p[i], x[i]], axis=0), bd(p[i])) for i in range(n)]
        p = [px[i][:c] for i in range(n)]
        x = [x[i] + px[i][c:] for i in range(n)]
    x = [(x[i] + _bdot(x[i], bd(p[i]))).astype(BF16) for i in range(n)]
    return [dict(lhs=lhs[i], av=av[i], t=x[i], m_cat=m_cat[i], v_bd=v_bd[i], v=insts[i][2],
                 bdkd=bdkd[i], p_end=p_end[i]) for i in range(n)]


def _chunk_state_step(states, pres, consts):
    lane_head, bd_mask = consts[3], consts[4]
    c = CHUNK
    n = len(states)
    sh = [_bdot_nt(pres[i]["lhs"], states[i]) for i in range(n)]
    rhs = [sh[i][:c] + pres[i]["av"] for i in range(n)]
    u = [-_bdot(pres[i]["t"], _stack_heads(rhs[i], lane_head)) for i in range(n)]
    y = [sh[i][c:] + _bdot(pres[i]["m_cat"],
                           jnp.concatenate([_stack_heads(u[i], lane_head).astype(BF16),
                                            pres[i]["v_bd"]], axis=0)) for i in range(n)]
    upd = [_bdot_tn(jnp.concatenate([u[i], pres[i]["v"]], axis=0), pres[i]["bdkd"]) for i in range(n)]
    new = [states[i] * pres[i]["p_end"] + jnp.where(bd_mask, upd[i], 0.0) for i in range(n)]
    return y, new


def _page_block_means(page_refs, km_ref, pages_per_block):
    sums = [jnp.sum(ref[...], axis=0) for ref in page_refs]
    for j in range(len(page_refs) // pages_per_block):
        blk = sums[j * pages_per_block]
        for i in range(1, pages_per_block):
            blk = blk + sums[j * pages_per_block + i]
        km_ref[j] = blk * (1.0 / MOBA_BLOCK)


def _rwkv_prompt_kernel(*refs, n_pages, pages_per_block):
    if n_pages:
        refs = refs[1:]
    (zr_ref, zk_ref, zv_ref, zg_ref, zwa_ref, mur_ref, muk_ref, muv_ref, mug_ref, muwa_ref,
     w0_ref, a0_ref, kk_ref, ka_ref, rk_ref, lnw_ref, lnb_ref, w2_ref, a2_ref) = refs[:19]
    page_refs = refs[19:19 + n_pages]
    refs = refs[19 + n_pages:]
    o_ref, s_out_ref = refs[:2]
    if n_pages:
        _page_block_means(page_refs, refs[2], pages_per_block)
        refs = refs[1:]
    s_ref, pr_ref, pk_ref, pv_ref, pg_ref, pwa_ref = refs[2:]
    t = pl.program_id(1)
    nseq, tr = zr_ref.shape[0], zr_ref.shape[1]

    @pl.when(t == 0)
    def _():
        for ref in (s_ref, pr_ref, pk_ref, pv_ref, pg_ref, pwa_ref):
            ref[...] = jnp.zeros_like(ref)

    def shifted(z_ref, prev_ref, mu_ref, i):
        z = z_ref[i]
        row = lax.broadcasted_iota(jnp.int32, z.shape, 0)
        prev = jnp.where(row == 0, prev_ref[i], pltpu.roll(z, 1, 0))
        prev_ref[i] = z[tr - 1:tr, :]
        return z + (prev - z) * mu_ref[...]

    bd_mask = _head_mask(GROUP)
    ones_bd = jnp.where(bd_mask, 1.0, 0.0).astype(BF16)
    ri = lax.broadcasted_iota(jnp.int32, (tr, tr), 0)
    ci = lax.broadcasted_iota(jnp.int32, (tr, tr), 1)
    tri = jnp.where((ri // CHUNK == ci // CHUNK) & (ci <= ri), 1.0, 0.0).astype(BF16)
    rc = lax.broadcasted_iota(jnp.int32, (CHUNK, GROUP), 0)
    lc = lax.broadcasted_iota(jnp.int32, (CHUNK, GROUP), 1)
    sc = lc % CHUNK
    consts = (sc < rc, sc <= rc, jnp.where(sc == rc, 1.0, 0.0), lc // HEAD_A, bd_mask)
    last = t == pl.num_programs(1) - 1

    n_ch = tr // CHUNK
    prepped, insts = [], []
    for i in range(nseq):
        zm_r = shifted(zr_ref, pr_ref, mur_ref, i)
        zm_k = shifted(zk_ref, pk_ref, muk_ref, i)
        zm_v = shifted(zv_ref, pv_ref, muv_ref, i)
        zm_g = shifted(zg_ref, pg_ref, mug_ref, i)
        zm_wa = shifted(zwa_ref, pwa_ref, muwa_ref, i)
        r, kf, v, kk, b, logw = _rwkv_prep(zm_r, zm_k, zm_v, zm_wa, w0_ref[...], a0_ref[...],
                                           kk_ref[...], ka_ref[...], w2_ref[...], a2_ref[...], ones_bd)
        cl = sum(jnp.dot(tri, piece, preferred_element_type=F32) for piece in _split3(logw))
        prepped.append((r, kf, v, zm_g))
        for ch in range(n_ch):
            sl = slice(ch * CHUNK, (ch + 1) * CHUNK)
            insts.append((r[sl], kf[sl], v[sl], kk[sl], b[sl], logw[sl], cl[sl]))
    pres = _chunk_precompute(insts, consts)

    states = [s_ref[i] for i in range(nseq)]
    ys = [[] for _ in range(nseq)]
    for ch in range(n_ch):
        y, states = _chunk_state_step(states, [pres[i * n_ch + ch] for i in range(nseq)], consts)
        for i in range(nseq):
            ys[i].append(y[i])
    for i in range(nseq):
        s_ref[i] = states[i]
        r, kf, v, zm_g = prepped[i]
        o_ref[i] = _rwkv_post(jnp.concatenate(ys[i], axis=0), r, kf, v, zm_g, rk_ref[...],
                              lnw_ref[...], lnb_ref[...], ones_bd).astype(o_ref.dtype)

    @pl.when(last)
    def _():
        for i in range(nseq):
            for h in range(HEADS_PER_GROUP):
                s_out_ref[i, h] = s_ref[i, h * HEAD_A:(h + 1) * HEAD_A, h * HEAD_A:(h + 1) * HEAD_A]


def rwkv_prompt(z_main, z_wa, mu_main, mu_wa, vecs, w2p, a2p, cache_k=None, page_table=None, layer=0):
    w0, a0, k_k, k_a, r_k, lnx_w, lnx_b = vecs
    batch, seq, _ = z_main.shape
    tr = min(seq, 256)
    assert seq % tr == 0 and tr % CHUNK == 0
    nt = seq // tr
    ng = D_A // GROUP

    n_pages = ppb = 0
    if cache_k is not None:
        page = cache_k.shape[2]
        ppb = MOBA_BLOCK // page
        total = page_table.size
        if total % (ng * nt) == 0 and (total // (ng * nt)) % ppb == 0:
            n_pages = total // (ng * nt)

    def zspec(col):
        return pl.BlockSpec((batch, tr, GROUP), lambda g, t, *_, c=col // GROUP: (0, t, c + g))

    def vspec(col=0):
        return pl.BlockSpec((1, GROUP), lambda g, t, *_, c=col // GROUP: (0, c + g))

    in_specs = [
        zspec(COL_R), zspec(COL_K), zspec(COL_V), zspec(COL_G),
        pl.BlockSpec((batch, tr, 2 * LORA), lambda g, t, *_: (0, t, 0)),
        vspec(COL_R), vspec(COL_K), vspec(COL_V), vspec(COL_G),
        pl.BlockSpec((1, 2 * LORA), lambda g, t, *_: (0, 0)),
        vspec(), vspec(), vspec(), vspec(), vspec(), vspec(), vspec(),
        pl.BlockSpec((2 * LORA, GROUP), lambda g, t, *_: (0, g)),
        pl.BlockSpec((2 * LORA, GROUP), lambda g, t, *_: (0, g)),
    ]
    out_specs = [
        pl.BlockSpec((batch, tr, GROUP), lambda g, t, *_: (0, t, g)),
        pl.BlockSpec((batch, HEADS_PER_GROUP, HEAD_A, HEAD_A), lambda g, t, *_: (0, g, 0, 0)),
    ]
    out_shape = [
        jax.ShapeDtypeStruct((batch, seq, D_A), BF16),
        jax.ShapeDtypeStruct((batch, N_HEADS_A, HEAD_A, HEAD_A), F32),
    ]
    operands = [z_main, z_main, z_main, z_main, z_wa,
                mu_main, mu_main, mu_main, mu_main, mu_wa,
                w0, a0, k_k, k_a, r_k, lnx_w, lnx_b, w2p, a2p]
    if n_pages:
        sq = pl.Squeezed()
        for i in range(n_pages):
            in_specs.append(pl.BlockSpec(
                (sq, sq, page, N_HEADS_B, HEAD_B),
                lambda g, t, pt, i=i: (layer, pt[(g * nt + t) * n_pages + i], 0, 0, 0)))
        operands += [cache_k] * n_pages
        out_specs.append(pl.BlockSpec((n_pages // ppb, N_HEADS_B, HEAD_B),
                                      lambda g, t, pt: (g * nt + t, 0, 0)))
        out_shape.append(jax.ShapeDtypeStruct((page_table.size // ppb, N_HEADS_B, HEAD_B), F32))
    row = lambda n: pltpu.VMEM((batch, 1, n), F32)
    scratch = [pltpu.VMEM((batch, GROUP, GROUP), F32),
               row(GROUP), row(GROUP), row(GROUP), row(GROUP), row(2 * LORA)]
    kern = functools.partial(_rwkv_prompt_kernel, n_pages=n_pages, pages_per_block=ppb)
    params = _cparams(("arbitrary", "arbitrary"))
    if not n_pages:
        o_a, wkv = pl.pallas_call(kern, grid=(ng, nt), in_specs=in_specs, out_specs=out_specs,
                                  out_shape=out_shape, scratch_shapes=scratch,
                                  compiler_params=params, name="rwkv_prompt")(*operands)
        return o_a, wkv, None
    o_a, wkv, kmean = pl.pallas_call(
        kern,
        grid_spec=pltpu.PrefetchScalarGridSpec(
            num_scalar_prefetch=1, grid=(ng, nt), in_specs=in_specs, out_specs=out_specs,
            scratch_shapes=scratch),
        out_shape=out_shape, compiler_params=params, name="rwkv_prompt",
    )(page_table.reshape(-1), *operands)
    n = page_table.shape[0]
    return o_a, wkv, kmean.reshape(n, -1, N_HEADS_B, HEAD_B)


def _rwkv_rows_kernel(z_ref, zp_ref, zwa_ref, zpwa_ref, mu_ref, muwa_ref,
                      w0_ref, a0_ref, kk_ref, ka_ref, w2_ref, a2_ref,
                      r_ref, kf_ref, v_ref, kkn_ref, b_ref, w_ref, g_ref):
    def shifted(z, zp, mu):
        return z + (zp - z) * mu

    mu = mu_ref[...]
    z = z_ref[...]
    zp = zp_ref[...]
    zm = [shifted(z[:, c:c + D_A], zp[:, c:c + D_A], mu[:, c:c + D_A])
          for c in (COL_R, COL_K, COL_V, COL_G)]
    zm_wa = shifted(zwa_ref[...], zpwa_ref[...], muwa_ref[...])
    ones_bd = jnp.where(_head_mask(D_A), 1.0, 0.0).astype(BF16)
    r, kf, v, kk, b, logw = _rwkv_prep(zm[0], zm[1], zm[2], zm_wa, w0_ref[...], a0_ref[...],
                                       kk_ref[...], ka_ref[...], w2_ref[...], a2_ref[...], ones_bd)
    r_ref[...] = r
    kf_ref[...] = kf
    v_ref[...] = v
    kkn_ref[...] = kk
    b_ref[...] = b
    w_ref[...] = jnp.exp(logw)
    g_ref[...] = zm[3]


def rwkv_rows(z_rw, zp_rw, z_wa, zp_wa, mu_rw, mu_wa, w0, a0, k_k, k_a, w2p, a2p):
    n = z_rw.shape[0]
    out = jax.ShapeDtypeStruct((n, D_A), F32)
    return pl.pallas_call(
        _rwkv_rows_kernel,
        out_shape=[out] * 7,
        compiler_params=pltpu.CompilerParams(vmem_limit_bytes=VMEM_LIMIT),
        name="rwkv_rows",
    )(z_rw, zp_rw, z_wa, zp_wa, mu_rw, mu_wa, w0, a0, k_k, k_a, w2p, a2p)


def _rwkv_step_kernel(s_ref, w_ref, kk_ref, b_ref, kf_ref, r_ref, v_ref, g_ref,
                      rk_ref, lnw_ref, lnb_ref, s_out_ref, o_ref):
    S = s_ref[...]
    w = w_ref[...]
    kk = kk_ref[...]
    b = b_ref[...]
    kf = kf_ref[...]
    r = r_ref[...]
    v = v_ref[...]
    sa = -jnp.sum(S * kk, axis=-1, keepdims=True)
    S = S * w + sa * b + v * kf
    s_out_ref[...] = S
    y = jnp.sum(S * r, axis=-1, keepdims=True)
    mean = jnp.mean(y, axis=1, keepdims=True)
    d = y - mean
    var = jnp.mean(d * d, axis=1, keepdims=True)
    yn = d * lax.rsqrt(var + GN_EPS) * lnw_ref[...] + lnb_ref[...]
    bonus = jnp.sum(r * kf * rk_ref[...], axis=-1, keepdims=True) * v
    o_ref[...] = (yn + bonus) * _silu(g_ref[...])


def rwkv_step(state, w, kk, b, kf, r, v, g, r_k, lnx_w, lnx_b):
    n = state.shape[0]
    h = N_HEADS_A
    sq = pl.Squeezed()
    lane_vec = pl.BlockSpec((sq, h, 1, HEAD_A), lambda i: (i, 0, 0, 0))
    col_vec = pl.BlockSpec((sq, h, HEAD_A, 1), lambda i: (i, 0, 0, 0))
    return pl.pallas_call(
        _rwkv_step_kernel,
        grid=(n,),
        in_specs=[pl.BlockSpec((sq, h, HEAD_A, HEAD_A), lambda i: (i, 0, 0, 0)),
                  lane_vec, lane_vec, lane_vec, lane_vec, lane_vec, col_vec, col_vec,
                  pl.BlockSpec((h, 1, HEAD_A), lambda i: (0, 0, 0)),
                  pl.BlockSpec((h, HEAD_A, 1), lambda i: (0, 0, 0)),
                  pl.BlockSpec((h, HEAD_A, 1), lambda i: (0, 0, 0))],
        out_specs=[pl.BlockSpec((sq, h, HEAD_A, HEAD_A), lambda i: (i, 0, 0, 0)), col_vec],
        out_shape=[jax.ShapeDtypeStruct((n, h, HEAD_A, HEAD_A), F32),
                   jax.ShapeDtypeStruct((n, h, HEAD_A, 1), F32)],
        compiler_params=_cparams(("arbitrary",)),
        name="rwkv_step",
    )(state, w, kk, b, kf, r, v, g, r_k, lnx_w, lnx_b)


def _rope(x, cos_t, sin_t, lane):
    partner = jnp.where(lane < ROT_DIM // 2, pltpu.roll(x, HEAD_B - ROT_DIM // 2, 1),
                        pltpu.roll(x, ROT_DIM // 2, 1))
    return x * cos_t + partner * sin_t


def _attn_prep_kernel(zq_ref, zk_ref, zv_ref, cos_ref, sin_ref,
                      q_ref, kh_ref, vt_ref, ko_ref, vo_ref, km_ref):
    cos_t = cos_ref[...]
    sin_t = sin_ref[...]
    lane = lax.broadcasted_iota(jnp.int32, cos_t.shape, 1)
    vo_ref[...] = zv_ref[...]
    for h in range(N_HEADS_B):
        sl = slice(h * HEAD_B, (h + 1) * HEAD_B)
        q = _rope(zq_ref[:, sl], cos_t, sin_t, lane)
        k = _rope(zk_ref[:, sl], cos_t, sin_t, lane)
        q_ref[h] = q.T.astype(q_ref.dtype)
        kh_ref[h] = k.astype(kh_ref.dtype)
        vt_ref[h] = zv_ref[:, sl].T.astype(vt_ref.dtype)
        ko_ref[:, sl] = k
        km_ref[:, sl] = jnp.mean(k, axis=0, keepdims=True)


def attn_prep(z_main, cos_t, sin_t, batch, seq):
    tr = min(seq, MOBA_BLOCK)
    nt = seq // tr
    m = batch * seq
    sq = pl.Squeezed()
    zspec = lambda col: pl.BlockSpec((tr, D_B), lambda b, t, c=col // D_B: (b * nt + t, c))
    tab = pl.BlockSpec((tr, HEAD_B), lambda b, t: (t, 0))
    hm = pl.BlockSpec((sq, N_HEADS_B, tr, HEAD_B), lambda b, t: (b, 0, t, 0))
    hm_t = pl.BlockSpec((sq, N_HEADS_B, HEAD_B, tr), lambda b, t: (b, 0, 0, t))
    rows = pl.BlockSpec((tr, D_B), lambda b, t: (b * nt + t, 0))
    hm_shape = jax.ShapeDtypeStruct((batch, N_HEADS_B, seq, HEAD_B), BF16)
    return pl.pallas_call(
        _attn_prep_kernel,
        grid=(batch, nt),
        in_specs=[zspec(COL_Q), zspec(COL_KB), zspec(COL_VB), tab, tab],
        out_specs=[hm_t, hm, hm_t, rows, rows,
                   pl.BlockSpec((sq, 1, D_B), lambda b, t: (b * nt + t, 0, 0))],
        out_shape=[jax.ShapeDtypeStruct((batch, N_HEADS_B, HEAD_B, seq), BF16), hm_shape,
                   jax.ShapeDtypeStruct((batch, N_HEADS_B, HEAD_B, seq), BF16),
                   jax.ShapeDtypeStruct((m, D_B), F32), jax.ShapeDtypeStruct((m, D_B), F32),
                   jax.ShapeDtypeStruct((batch * nt, 1, D_B), F32)],
        compiler_params=_cparams(("arbitrary", "arbitrary")),
        name="attn_prep",
    )(z_main, z_main, z_main, cos_t, sin_t)


def _rope_rows_kernel(zq_ref, zk_ref, cos_ref, sin_ref, q_ref, k_ref):
    cos_t = cos_ref[...]
    sin_t = sin_ref[...]
    lane = lax.broadcasted_iota(jnp.int32, cos_t.shape, 1)
    for h in range(N_HEADS_B):
        sl = slice(h * HEAD_B, (h + 1) * HEAD_B)
        q_ref[:, sl] = _rope(zq_ref[:, sl], cos_t, sin_t, lane).astype(BF16).astype(F32)
        k_ref[:, sl] = _rope(zk_ref[:, sl], cos_t, sin_t, lane)


def rope_rows(zq, zk, cos_t, sin_t):
    out = jax.ShapeDtypeStruct(zq.shape, F32)
    return pl.pallas_call(_rope_rows_kernel, out_shape=[out, out], name="rope_rows")(zq, zk, cos_t, sin_t)


MOBA_HEADS_PER_STEP = 2


def _moba_prompt_kernel(q_ref, k_ref, vt_ref, km_ref, zg_ref, o_ref, sel_ref):
    qb = pl.program_id(1)
    nbatch, nhead, tq = q_ref.shape[0], q_ref.shape[1], q_ref.shape[3]
    nb = km_ref.shape[1]
    seqs = [(b, h) for b in range(nbatch) for h in range(nhead)]
    exp2_scale = HEAD_B ** -0.5 * LOG2_E
    qs = [q_ref[b, h] for b, h in seqs]

    blk = lax.broadcasted_iota(jnp.int32, (nb, tq), 0)
    past = blk < qb
    for i, (b, h) in enumerate(seqs):
        km = km_ref[b, :, h * HEAD_B:(h + 1) * HEAD_B]
        gate = sum(jnp.dot(piece, qs[i], preferred_element_type=F32)
                   for piece in _split3(km))
        for n in range(nb):
            g_n = gate[n:n + 1, :]
            beats = (gate > g_n) | ((gate == g_n) & (blk < n))
            cnt = jnp.sum(jnp.where(beats & past, 1.0, 0.0), axis=0, keepdims=True)
            sel_ref[i, n:n + 1, :] = jnp.where(cnt < MOBA_TOPK, 1.0, 0.0)

    def scores(i, n):
        b, h = seqs[i]
        start = pl.multiple_of(n * tq, tq)
        return jnp.dot(k_ref[b, h, pl.ds(start, tq), :], qs[i], preferred_element_type=F32)

    def softmax_step(s, m, l):
        m_new = jnp.maximum(m, jnp.max(s, axis=0, keepdims=True))
        alpha = jnp.exp2((m - m_new) * exp2_scale)
        p = jnp.exp2((s - m_new) * exp2_scale)
        return m_new, alpha, alpha * l + jnp.sum(p, axis=0, keepdims=True), p.astype(BF16)

    def pv(i, n, p):
        b, h = seqs[i]
        start = pl.multiple_of(n * tq, tq)
        return jnp.dot(vt_ref[b, h, :, pl.ds(start, tq)], p, preferred_element_type=F32)

    def step(n, masked, carry):
        sm = [softmax_step(masked[i], carry[i][0], carry[i][1]) for i in range(len(seqs))]
        acc = [sm[i][1] * carry[i][2] + pv(i, n, sm[i][3]) for i in range(len(seqs))]
        return [(sm[i][0], sm[i][2], acc[i]) for i in range(len(seqs))]

    def body(n, carry):
        s = [scores(i, n) for i in range(len(seqs))]
        masked = [jnp.where(sel_ref[i, pl.ds(n, 1), :] > 0.0, s[i], NEG)
                  for i in range(len(seqs))]
        return tuple(step(n, masked, carry))

    init = tuple((jnp.full((1, tq), NEG, F32), jnp.zeros((1, tq), F32),
                  jnp.zeros((HEAD_B, tq), F32)) for i in range(len(seqs)))
    carry = lax.fori_loop(0, qb, body, init)
    ki = lax.broadcasted_iota(jnp.int32, (tq, tq), 0)
    qi = lax.broadcasted_iota(jnp.int32, (tq, tq), 1)
    masked = [jnp.where(ki <= qi, scores(i, qb), NEG) for i in range(len(seqs))]
    final = step(qb, masked, list(carry))
    for i, (b, h) in enumerate(seqs):
        _, l, acc = final[i]
        sl = slice(h * HEAD_B, (h + 1) * HEAD_B)
        o_ref[b, :, sl] = ((acc / l).T * _silu(zg_ref[b, :, sl])).astype(o_ref.dtype)


def moba_prompt(qt_hm, k_hm, vt_hm, kmean, z_main):
    batch, _, seq, _ = k_hm.shape
    tq = min(seq, MOBA_BLOCK)
    nb = seq // tq
    hps = MOBA_HEADS_PER_STEP
    w = hps * HEAD_B
    return pl.pallas_call(
        _moba_prompt_kernel,
        grid=(N_HEADS_B // hps, nb),
        in_specs=[
            pl.BlockSpec((batch, hps, HEAD_B, tq), lambda h, i: (0, h, 0, i)),
            pl.BlockSpec((batch, hps, seq, HEAD_B), lambda h, i: (0, h, 0, 0)),
            pl.BlockSpec((batch, hps, HEAD_B, seq), lambda h, i: (0, h, 0, 0)),
            pl.BlockSpec((batch, nb, w), lambda h, i: (0, 0, h)),
            pl.BlockSpec((batch, tq, w), lambda h, i: (0, i, COL_GB // w + h)),
        ],
        out_specs=pl.BlockSpec((batch, tq, w), lambda h, i: (0, i, h)),
        out_shape=jax.ShapeDtypeStruct((batch, seq, D_B), BF16),
        scratch_shapes=[pltpu.VMEM((batch * hps, nb, tq), F32)],
        compiler_params=_cparams(("arbitrary", "arbitrary")),
        name="moba_prompt",
    )(qt_hm, k_hm, vt_hm, kmean, z_main)


KMEAN_PAGES = 8


def _cache_kmean_kernel(pt_ref, *refs, pages_per_block):
    del pt_ref
    _page_block_means(refs[:-1], refs[-1], pages_per_block)


def cache_kmean(cache_k, page_table, layer):
    n, n_pages = page_table.shape
    page = cache_k.shape[2]
    ppb = MOBA_BLOCK // page
    pps = min(KMEAN_PAGES, n_pages)
    assert n_pages % pps == 0 and pps % ppb == 0
    sq = pl.Squeezed()
    page_spec = lambda i: pl.BlockSpec(
        (sq, sq, page, N_HEADS_B, HEAD_B), lambda b, s, pt, i=i: (layer, pt[b, s * pps + i], 0, 0, 0))
    return pl.pallas_call(
        functools.partial(_cache_kmean_kernel, pages_per_block=ppb),
        grid_spec=pltpu.PrefetchScalarGridSpec(
            num_scalar_prefetch=1,
            grid=(n, n_pages // pps),
            in_specs=[page_spec(i) for i in range(pps)],
            out_specs=pl.BlockSpec((sq, pps // ppb, N_HEADS_B, HEAD_B), lambda b, s, pt: (b, s, 0, 0)),
        ),
        out_shape=jax.ShapeDtypeStruct((n, n_pages // ppb, N_HEADS_B, HEAD_B), F32),
        compiler_params=_cparams(("arbitrary", "arbitrary")),
        name="cache_kmean",
    )(page_table, *([cache_k] * pps))


def _sample_select_kernel(q_ref, km_ref, sel_ref):
    nb = km_ref.shape[0]
    ri = lax.broadcasted_iota(jnp.int32, (nb, nb), 0)
    ci = lax.broadcasted_iota(jnp.int32, (nb, nb), 1)
    lane = lax.broadcasted_iota(jnp.int32, (1, 128), 1)
    blk_row = lax.broadcasted_iota(jnp.int32, (1, nb), 1).astype(F32)
    for h in range(N_HEADS_B):
        km = km_ref[:, h, :]
        g_col = jnp.sum(km * q_ref[h:h + 1, :], axis=1, keepdims=True)
        g_row = jnp.sum(jnp.where(ri == ci, g_col, 0.0), axis=0, keepdims=True)
        beats = (g_col > g_row) | ((g_col == g_row) & (ri < ci))
        rank = jnp.sum(jnp.where(beats, 1.0, 0.0), axis=0, keepdims=True)
        out = jnp.zeros((1, 128), F32)
        for r in range(MOBA_TOPK):
            idx = jnp.sum(jnp.where(rank == float(r), blk_row, 0.0), axis=1, keepdims=True)
            out = jnp.where(lane == r, idx, out)
        sel_ref[h:h + 1, :] = out.astype(jnp.int32)


def sample_select(q_s, kmean_s):
    n, nb = kmean_s.shape[:2]
    sq = pl.Squeezed()
    return pl.pallas_call(
        _sample_select_kernel,
        grid=(n,),
        in_specs=[pl.BlockSpec((sq, N_HEADS_B, HEAD_B), lambda b: (b, 0, 0)),
                  pl.BlockSpec((sq, nb, N_HEADS_B, HEAD_B), lambda b: (b, 0, 0, 0))],
        out_specs=pl.BlockSpec((sq, N_HEADS_B, 128), lambda b: (b, 0, 0)),
        out_shape=jax.ShapeDtypeStruct((n, N_HEADS_B, 128), jnp.int32),
        compiler_params=_cparams(("arbitrary",)),
        name="sample_select",
    )(q_s, kmean_s)


def _sample_attn_kernel(sel_ref, pt_ref, q_ref, kn_ref, vn_ref, zg_ref, ck_ref, cv_ref, o_ref,
                        kbuf, vbuf, sem, *, layer, page, ppb):
    b = pl.program_id(0)
    nb_steps = pl.num_programs(0)
    n_sel = MOBA_TOPK * ppb
    scale = HEAD_B ** -0.5

    def copies(bb, slot):
        out = []
        for h in range(N_HEADS_B):
            for j in range(n_sel):
                pg = pt_ref[bb, sel_ref[bb, h, j // ppb] * ppb + j % ppb]
                dst = pl.ds(j * page, page)
                out.append(pltpu.make_async_copy(ck_ref.at[layer, pg, :, h, :],
                                                 kbuf.at[slot, h, dst, :], sem.at[slot, 0]))
                out.append(pltpu.make_async_copy(cv_ref.at[layer, pg, :, h, :],
                                                 vbuf.at[slot, h, dst, :], sem.at[slot, 1]))
        return out

    slot = b % 2

    @pl.when(b == 0)
    def _():
        for c in copies(0, 0):
            c.start()

    @pl.when(b + 1 < nb_steps)
    def _():
        for c in copies(b + 1, 1 - slot):
            c.start()

    for c in copies(b, slot):
        c.wait()

    for h in range(N_HEADS_B):
        q = q_ref[h]
        k = kbuf[slot, h]
        v = vbuf[slot, h]
        s = jnp.sum(k * q, axis=1, keepdims=True) * scale
        s_own = jnp.sum(kn_ref[h] * q, axis=1, keepdims=True) * scale
        m = jnp.maximum(jnp.max(s, axis=0, keepdims=True), s_own)
        p = jnp.exp(s - m)
        p_own = jnp.exp(s_own - m)
        l = jnp.sum(p, axis=0, keepdims=True) + p_own
        acc = jnp.sum(p * v, axis=0, keepdims=True) + p_own * vn_ref[h]
        o_ref[h] = (acc / l) * _silu(zg_ref[h])


def sample_attn(sel, page_table, q_s, k_new, v_new, zgb, cache_k, cache_v, layer):
    n = q_s.shape[0]
    page = cache_k.shape[2]
    ppb = MOBA_BLOCK // page
    rows = MOBA_TOPK * ppb * page
    sq = pl.Squeezed()
    vec = pl.BlockSpec((sq, N_HEADS_B, 1, HEAD_B), lambda b, sel, pt: (b, 0, 0, 0))
    hbm = pl.BlockSpec(memory_space=pl.ANY)
    return pl.pallas_call(
        functools.partial(_sample_attn_kernel, layer=layer, page=page, ppb=ppb),
        grid_spec=pltpu.PrefetchScalarGridSpec(
            num_scalar_prefetch=2,
            grid=(n,),
            in_specs=[vec, vec, vec, vec, hbm, hbm],
            out_specs=vec,
            scratch_shapes=[pltpu.VMEM((2, N_HEADS_B, rows, HEAD_B), F32),
                            pltpu.VMEM((2, N_HEADS_B, rows, HEAD_B), F32),
                            pltpu.SemaphoreType.DMA((2, 2))],
        ),
        out_shape=jax.ShapeDtypeStruct((n, N_HEADS_B, 1, HEAD_B), F32),
        compiler_params=_cparams(("arbitrary",)),
        name="sample_attn",
    )(sel, page_table, q_s, k_new, v_new, zgb, cache_k, cache_v)


def _merge_kernel(oa_ref, ob_ref, pa_ref, pb_ref, zga_ref, zgb_ref, o_ref):
    ya = jnp.dot(oa_ref[...], pa_ref[...], preferred_element_type=F32)
    yb = jnp.dot(ob_ref[...], pb_ref[...], preferred_element_type=F32)
    o_ref[...] = (_sigmoid(zga_ref[...]) * ya + _sigmoid(zgb_ref[...]) * yb).astype(o_ref.dtype)


def merge(o_a, o_b, p_a, p_b, z_main):
    m = o_a.shape[0]
    tm = min(m, 1024)
    tn = 512
    return pl.pallas_call(
        _merge_kernel,
        grid=(m // tm, D_MODEL // tn),
        in_specs=[
            pl.BlockSpec((tm, D_A), lambda i, j: (i, 0)),
            pl.BlockSpec((tm, D_B), lambda i, j: (i, 0)),
            pl.BlockSpec((D_A, tn), lambda i, j: (0, j)),
            pl.BlockSpec((D_B, tn), lambda i, j: (0, j)),
            pl.BlockSpec((tm, tn), lambda i, j: (i, COL_GA // tn + j)),
            pl.BlockSpec((tm, tn), lambda i, j: (i, COL_GM // tn + j)),
        ],
        out_specs=pl.BlockSpec((tm, tn), lambda i, j: (i, j)),
        out_shape=jax.ShapeDtypeStruct((m, D_MODEL), BF16),
        compiler_params=_cparams(("arbitrary", "arbitrary")),
        name="merge",
    )(o_a, o_b, p_a, p_b, z_main, z_main)


def _out_kernel(mg_ref, wo_ref, x_ref, fw_ref, y_ref):
    h = x_ref[...] + jnp.dot(mg_ref[...], wo_ref[...], preferred_element_type=F32)
    ms = jnp.mean(h * h, axis=-1, keepdims=True)
    y_ref[...] = (h * lax.rsqrt(ms + NORM_EPS)) * fw_ref[...]


def output_projection(merged, w_o, x, final_w):
    m = x.shape[0]
    tm = min(m, 512)
    return pl.pallas_call(
        _out_kernel,
        grid=(m // tm,),
        in_specs=[
            pl.BlockSpec((tm, D_MODEL), lambda i: (i, 0)),
            pl.BlockSpec((D_MODEL, D_MODEL), lambda i: (0, 0)),
            pl.BlockSpec((tm, D_MODEL), lambda i: (i, 0)),
            pl.BlockSpec((1, D_MODEL), lambda i: (0, 0)),
        ],
        out_specs=pl.BlockSpec((tm, D_MODEL), lambda i: (i, 0)),
        out_shape=jax.ShapeDtypeStruct((m, D_MODEL), F32),
        compiler_params=_cparams(("arbitrary",)),
        name="output_projection",
    )(merged, w_o, x, final_w.reshape(1, -1))


def _rope_tables(pos):
    half = ROT_DIM // 2
    inv = jnp.power(jnp.float32(ROPE_THETA), -jnp.arange(half, dtype=F32) * (2.0 / ROT_DIM))
    ang = pos.astype(F32)[:, None] * inv[None, :]
    cos, sin = jnp.cos(ang), jnp.sin(ang)
    n = pos.shape[0]
    rest = HEAD_B - ROT_DIM
    cos_t = jnp.concatenate([cos, cos, jnp.ones((n, rest), F32)], axis=1)
    sin_t = jnp.concatenate([-sin, sin, jnp.zeros((n, rest), F32)], axis=1)
    return cos_t, sin_t


def _reorder_rw(a):
    r, w_lo, k, v, a_lo, g = jnp.split(
        a, [D_A, D_A + LORA, 2 * D_A + LORA, 3 * D_A + LORA, 3 * D_A + 2 * LORA], axis=-1)
    return jnp.concatenate([r, k, v, g], axis=-1), jnp.concatenate([w_lo, a_lo], axis=-1)


def kernel(x_prompt, x_sample, state_shift, state_wkv, cache_k, cache_v, page_table, ln_w, w_in, mu,
           w0, w2, a0, a2, k_k, k_a, r_k, lnx_w, lnx_b, p_a, p_b, w_o, final_w):
    depth = ln_w.shape[0]
    assert depth == 1, "single-layer trunk"
    B, T, _ = x_prompt.shape
    DB, TS, _ = x_sample.shape
    assert TS == 1
    n_pages = page_table.shape[1]
    page = cache_k.shape[2]
    past = n_pages * page
    assert past % MOBA_BLOCK == 0 and MOBA_BLOCK % page == 0
    l = 0
    rw_cols = 4 * D_A + 2 * LORA

    w_rw, w_wa = _reorder_rw(w_in[l][:, :rw_cols])
    w_main = jnp.concatenate([w_rw, w_in[l][:, rw_cols:]], axis=1).astype(BF16)
    w_wa = w_wa.astype(BF16)
    mu_rw, mu_wa = _reorder_rw(mu[l][None, :])
    zeros = jnp.zeros((LORA, D_A), F32)
    w2p = jnp.concatenate([w2[l], zeros], axis=0).astype(BF16)
    a2p = jnp.concatenate([zeros, a2[l]], axis=0).astype(BF16)
    row = lambda a: a.reshape(1, -1)
    vecs = (row(w0[l]), row(a0[l]), row(k_k[l]), row(k_a[l]), row(r_k[l]), row(lnx_w[l]), row(lnx_b[l]))
    pa_bf, pb_bf, wo_bf = p_a[l].astype(BF16), p_b[l].astype(BF16), w_o[l].astype(BF16)

    rows = jnp.concatenate([x_prompt[:, -1, :], x_sample[:, 0, :]], axis=0)
    xn_rows = rmsnorm_rows(rows, ln_w[l])
    shift_prompt = xn_rows[:B]
    xn_s = xn_rows[B:]
    xp = x_prompt.reshape(B * T, D_MODEL)
    z_p, zwa_p = input_projection(xp, ln_w[l], w_main, w_wa, normalize=True)
    z_s2, zwa_s2 = input_projection(jnp.concatenate([xn_s, state_shift[l]], axis=0), ln_w[l],
                                    w_main, w_wa, normalize=False)

    z_p3 = z_p.reshape(B, T, N_MAIN)
    o_a_p, wkv_p, kmean_s = rwkv_prompt(z_p3, zwa_p.reshape(B, T, 2 * LORA), mu_rw, mu_wa, vecs, w2p, a2p,
                                        cache_k, page_table, l)
    cos_p, sin_p = _rope_tables(jnp.arange(T))
    qt_hm, k_hm, vt_hm, k_rows_p, v_rows_p, kmean_p = attn_prep(z_p, cos_p, sin_p, B, T)
    nb_p = T // min(T, MOBA_BLOCK)
    o_b_p = moba_prompt(qt_hm, k_hm, vt_hm, kmean_p.reshape(B, nb_p, D_B), z_p3)
    merged_p = merge(o_a_p.reshape(B * T, D_A), o_b_p.reshape(B * T, D_B), pa_bf, pb_bf, z_p)
    y_prompt = output_projection(merged_p, wo_bf, xp, final_w).reshape(B, T, D_MODEL)

    z_s, zprev_s = z_s2[:DB], z_s2[DB:]
    r_s, kf_s, v_s, kk_s, b_s, w_s, g_s = rwkv_rows(
        z_s[:, :4 * D_A], zprev_s[:, :4 * D_A], zwa_s2[:DB], zwa_s2[DB:], mu_rw, mu_wa,
        vecs[0], vecs[1], vecs[2], vecs[3], w2p, a2p)
    hs = lambda a: a.reshape(-1, N_HEADS_A, 1, HEAD_A)
    col = lambda a: a.reshape(-1, N_HEADS_A, HEAD_A, 1)
    wkv_s, o_a_s = rwkv_step(state_wkv[l], hs(w_s), hs(kk_s), hs(b_s), hs(kf_s), hs(r_s),
                             col(v_s), col(g_s), hs(r_k[l])[0], col(lnx_w[l])[0], col(lnx_b[l])[0])
    o_a_s = o_a_s.reshape(DB, D_A).astype(BF16)

    cos_s, sin_s = _rope_tables(past + jnp.arange(TS))
    cos_s = jnp.broadcast_to(cos_s, (DB, HEAD_B))
    sin_s = jnp.broadcast_to(sin_s, (DB, HEAD_B))
    q_rows_s, k_rows_s = rope_rows(z_s[:, COL_Q:COL_Q + D_B], z_s[:, COL_KB:COL_KB + D_B], cos_s, sin_s)
    v_rows_s = z_s[:, COL_VB:COL_VB + D_B]
    if kmean_s is None:
        kmean_s = cache_kmean(cache_k, page_table, l)
    sel = sample_select(q_rows_s.reshape(DB, N_HEADS_B, HEAD_B), kmean_s)[:, :, :MOBA_TOPK]
    hv = lambda a: a.reshape(DB, N_HEADS_B, 1, HEAD_B)
    o_b_s = sample_attn(sel, page_table, hv(q_rows_s), hv(k_rows_s), hv(v_rows_s),
                        hv(z_s[:, COL_GB:COL_GB + D_B]), cache_k, cache_v, l)
    o_b_s = o_b_s.reshape(DB, D_B).astype(BF16)
    merged_s = merge(o_a_s, o_b_s, pa_bf, pb_bf, z_s)
    y_sample = output_projection(merged_s, wo_bf, x_sample.reshape(DB, D_MODEL), final_w)

    return (y_prompt,
            y_sample.reshape(DB, TS, D_MODEL),
            shift_prompt[None],
            wkv_p.reshape(1, B, N_HEADS_A, HEAD_A, HEAD_A),
            k_rows_p.reshape(1, B, T, N_HEADS_B, HEAD_B),
            v_rows_p.reshape(1, B, T, N_HEADS_B, HEAD_B),
            xn_s[None],
            wkv_s[None],
            k_rows_s.reshape(1, DB, TS, N_HEADS_B, HEAD_B),
            v_rows_s.reshape(1, DB, TS, N_HEADS_B, HEAD_B))
```

```python
import functools

import jax
import jax.numpy as jnp
from jax import lax
from jax.experimental import pallas as pl
from jax.experimental.pallas import tpu as pltpu

F32 = jnp.float32
BF16 = jnp.bfloat16

D_MODEL = 2048
D_A = D_MODEL // 2
HEAD_A = 64
N_HEADS_A = D_A // HEAD_A
LORA = 64
D_B = D_MODEL // 2
HEAD_B = 128
N_HEADS_B = D_B // HEAD_B
ROT_DIM = HEAD_B // 4
ROPE_THETA = 500000.0
MOBA_BLOCK = 256
MOBA_TOPK = 3
NORM_EPS = 1e-6
GN_EPS = 64e-5
NEG = -1e30
LOG2_E = 1.4426950408889634

COL_R, COL_K, COL_V, COL_G = 0, D_A, 2 * D_A, 3 * D_A
COL_Q, COL_KB, COL_VB, COL_GB = 4 * D_A, 4 * D_A + D_B, 4 * D_A + 2 * D_B, 4 * D_A + 3 * D_B
COL_GA = 4 * D_A + 4 * D_B
COL_GM = COL_GA + D_MODEL
N_MAIN = COL_GM + D_MODEL

GROUP = 256
HEADS_PER_GROUP = GROUP // HEAD_A
CHUNK = 64
assert CHUNK == HEAD_A
VMEM_LIMIT = 56 * 1024 * 1024


def _cparams(sem):
    return pltpu.CompilerParams(dimension_semantics=sem, vmem_limit_bytes=VMEM_LIMIT)


def _bdot(a, b):
    return jnp.dot(a.astype(BF16), b.astype(BF16), preferred_element_type=F32)


def _bdot_nt(a, b):
    return lax.dot_general(a.astype(BF16), b.astype(BF16), (((1,), (1,)), ((), ())),
                           preferred_element_type=F32)


def _bdot_tn(a, b):
    return lax.dot_general(a.astype(BF16), b.astype(BF16), (((0,), (0,)), ((), ())),
                           preferred_element_type=F32)


def _split3(x):
    hi = x.astype(BF16)
    r1 = x - hi.astype(F32)
    mid = r1.astype(BF16)
    lo = (r1 - mid.astype(F32)).astype(BF16)
    return hi, mid, lo


def _sigmoid(x):
    return 1.0 / (1.0 + jnp.exp(-x))


def _silu(x):
    return x * _sigmoid(x)


def _softplus(x):
    return jnp.maximum(x, 0.0) + jnp.log(1.0 + jnp.exp(-jnp.abs(x)))


def _rmsnorm_rows_kernel(x_ref, w_ref, o_ref):
    x = x_ref[...]
    ms = jnp.mean(x * x, axis=-1, keepdims=True)
    o_ref[...] = (x * lax.rsqrt(ms + NORM_EPS)) * w_ref[...]


def rmsnorm_rows(x, w):
    return pl.pallas_call(
        _rmsnorm_rows_kernel,
        out_shape=jax.ShapeDtypeStruct(x.shape, F32),
        name="rmsnorm_rows",
    )(x, w.reshape(1, -1))


W_TILE = 1024
LANES = 128


def _w_prep_kernel(w_hbm, o_hbm, in_buf, out_buf, in_sem, out_sem, *, layer, starts):
    n = len(starts)
    rows = in_buf.shape[1]

    def in_copy(j, slot):
        off = starts[j] % LANES
        width = W_TILE + (LANES if off else 0)
        return pltpu.make_async_copy(w_hbm.at[layer, :, pl.ds(starts[j] - off, width)],
                                     in_buf.at[slot, :, pl.ds(0, width)], in_sem.at[slot])

    def out_copy(j, slot):
        return pltpu.make_async_copy(out_buf.at[slot], o_hbm.at[:, pl.ds(j * W_TILE, W_TILE)],
                                     out_sem.at[slot])

    in_copy(0, 0).start()
    for j in range(n):
        slot = j % 2
        if j + 1 < n:
            in_copy(j + 1, 1 - slot).start()
        in_copy(j, slot).wait()
        if j >= 2:
            out_copy(j - 2, slot).wait()
        off = starts[j] % LANES
        rc = 256

        def chunk(c, _, slot=slot, off=off):
            r0 = pl.multiple_of(c * rc, rc)
            out_buf[slot, pl.ds(r0, rc), :] = in_buf[slot, pl.ds(r0, rc), off:off + W_TILE].astype(BF16)
            return 0

        lax.fori_loop(0, rows // rc, chunk, 0)
        out_copy(j, slot).start()
    for j in range(max(n - 2, 0), n):
        out_copy(j, j % 2).wait()


def prepare_projection_weight(w_in, layer, starts):
    rows = w_in.shape[1]
    assert rows % 256 == 0
    hbm = pl.BlockSpec(memory_space=pl.ANY)
    return pl.pallas_call(
        functools.partial(_w_prep_kernel, layer=layer, starts=tuple(starts)),
        in_specs=[hbm],
        out_specs=hbm,
        out_shape=jax.ShapeDtypeStruct((rows, len(starts) * W_TILE), BF16),
        scratch_shapes=[pltpu.VMEM((2, rows, W_TILE + LANES), F32),
                        pltpu.VMEM((2, rows, W_TILE), BF16),
                        pltpu.SemaphoreType.DMA((2,)), pltpu.SemaphoreType.DMA((2,))],
        compiler_params=pltpu.CompilerParams(vmem_limit_bytes=VMEM_LIMIT),
        name="prepare_projection_weight",
    )(w_in)


def _proj_kernel(x_ref, lnw_ref, w_ref, wwa_ref, z_ref, zwa_ref, xn_ref, *, normalize):
    @pl.when(pl.program_id(1) == 0)
    def _():
        x = x_ref[...]
        if normalize:
            ms = jnp.mean(x * x, axis=-1, keepdims=True)
            x = (x * lax.rsqrt(ms + NORM_EPS)) * lnw_ref[...]
        xn_ref[...] = x.astype(BF16)
        zwa_ref[...] = jnp.dot(xn_ref[...], wwa_ref[...], preferred_element_type=F32)

    z_ref[...] = jnp.dot(xn_ref[...], w_ref[...], preferred_element_type=F32)


def input_projection(x, ln_w, w_main, w_wa, *, normalize):
    m = x.shape[0]
    tm = min(m, 1024)
    tn = 1024 if m >= 1024 else 2048
    assert m % tm == 0 and N_MAIN % tn == 0
    return pl.pallas_call(
        functools.partial(_proj_kernel, normalize=normalize),
        grid=(m // tm, N_MAIN // tn),
        in_specs=[
            pl.BlockSpec((tm, D_MODEL), lambda i, j: (i, 0)),
            pl.BlockSpec((1, D_MODEL), lambda i, j: (0, 0)),
            pl.BlockSpec((D_MODEL, tn), lambda i, j: (0, j)),
            pl.BlockSpec((D_MODEL, 2 * LORA), lambda i, j: (0, 0)),
        ],
        out_specs=[
            pl.BlockSpec((tm, tn), lambda i, j: (i, j)),
            pl.BlockSpec((tm, 2 * LORA), lambda i, j: (i, 0)),
        ],
        out_shape=[
            jax.ShapeDtypeStruct((m, N_MAIN), F32),
            jax.ShapeDtypeStruct((m, 2 * LORA), F32),
        ],
        scratch_shapes=[pltpu.VMEM((tm, D_MODEL), BF16)],
        compiler_params=_cparams(("arbitrary", "arbitrary")),
        name="input_projection",
    )(x, ln_w.reshape(1, -1), w_main, w_wa)


def _head_mask(n):
    r = lax.broadcasted_iota(jnp.int32, (n, n), 0) // HEAD_A
    c = lax.broadcasted_iota(jnp.int32, (n, n), 1) // HEAD_A
    return r == c


def _segsum(x, ones_bd):
    return jnp.dot(x.astype(BF16), ones_bd, preferred_element_type=F32)


def _rwkv_prep(zm_r, zm_k, zm_v, zm_wa, w0, a0, k_k, k_a, w2p, a2p, ones_bd):
    lane = lax.broadcasted_iota(jnp.int32, zm_wa.shape, 1)
    lora_in = jnp.where(lane < LORA, jnp.tanh(zm_wa), zm_wa)
    ww = _bdot(lora_in, w2p)
    aa = _bdot(lora_in, a2p)
    w_log = -_softplus(-(w0 + ww)) - 0.5
    logw = -jnp.exp(w_log)
    a = _sigmoid(a0 + aa)
    kk = zm_k * k_k
    ss = _segsum(kk * kk, ones_bd)
    kk = kk / jnp.maximum(jnp.sqrt(ss), 1e-12)
    kf = zm_k * (1.0 + (a - 1.0) * k_a)
    return zm_r, kf, zm_v, kk, kk * a, logw


def _rwkv_post(y, r, kf, v, zm_g, r_k, lnx_w, lnx_b, ones_bd):
    inv_n = 1.0 / HEAD_A
    mean = _segsum(y, ones_bd) * inv_n
    d = y - mean
    var = _segsum(d * d, ones_bd) * inv_n
    yn = d * lax.rsqrt(var + GN_EPS) * lnx_w + lnx_b
    bonus = _segsum(r * kf * r_k, ones_bd) * v
    return (yn + bonus) * _silu(zm_g)


def _stack_heads(x, lane_head):
    return jnp.concatenate(
        [jnp.where(lane_head == h, x, 0.0) for h in range(HEADS_PER_GROUP)], axis=0)


def _chunk_precompute(insts, consts, side_tasks=()):
    strict, incl, eye, lane_head, bd_mask = consts
    c = CHUNK
    n = len(insts)
    side_tasks = list(side_tasks)

    def run_side(k=1):
        for _ in range(k):
            if side_tasks:
                side_tasks.pop(0)()

    def bd(p):
        return jnp.where(bd_mask, jnp.concatenate([p] * HEADS_PER_GROUP, axis=0), 0.0).astype(BF16)

    lhs, wt, v_bd, bdkd, p_end = [], [], [], [], []
    for r, kf, v, kk, b, logw, cl in insts:
        cl_last = cl[c - 1:c, :]
        e_neg = jnp.exp(-cl)
        e_end = jnp.exp(cl_last - cl)
        lhs.append(jnp.concatenate([kk * jnp.exp(cl - logw), r * jnp.exp(cl)], axis=0).astype(BF16))
        wt.append(jnp.concatenate([_stack_heads(b * e_neg, lane_head),
                                   _stack_heads(kf * e_neg, lane_head)], axis=0).astype(BF16))
        v_bd.append(_stack_heads(v, lane_head).astype(BF16))
        bdkd.append(jnp.concatenate([b * e_end, kf * e_end], axis=0).astype(BF16))
        p_end.append(jnp.exp(cl_last))
    att = [_bdot_nt(lhs[i], wt[i]) for i in range(n)]
    run_side()
    a_ab = [jnp.where(strict, att[i][:c, :4 * c], 0.0) for i in range(n)]
    a_ak = [jnp.where(strict, att[i][:c, 4 * c:], 0.0) for i in range(n)]
    m_cat = [jnp.concatenate([jnp.where(incl, att[i][c:, :4 * c], 0.0),
                              jnp.where(incl, att[i][c:, 4 * c:], 0.0)], axis=1).astype(BF16)
             for i in range(n)]
    av = [_bdot(a_ak[i], v_bd[i]) for i in range(n)]
    run_side()

    x = [eye - a_ab[i] for i in range(n)]
    p = [_bdot(a_ab[i], bd(a_ab[i])) for i in range(n)]
    run_side()
    for _ in range(4):
        px = [_bdot(jnp.concatenate([p[i], x[i]], axis=0), bd(p[i])) for i in range(n)]
        p = [px[i][:c] for i in range(n)]
        x = [x[i] + px[i][c:] for i in range(n)]
        run_side()
    x = [(x[i] + _bdot(x[i], bd(p[i]))).astype(BF16) for i in range(n)]
    run_side(len(side_tasks))
    return [dict(lhs=lhs[i], av=av[i], t=x[i], m_cat=m_cat[i], v_bd=v_bd[i], v=insts[i][2],
                 bdkd=bdkd[i], p_end=p_end[i]) for i in range(n)]


def _chunk_state_step(states, pres, consts):
    lane_head, bd_mask = consts[3], consts[4]
    c = CHUNK
    n = len(states)
    sh = [_bdot_nt(pres[i]["lhs"], states[i]) for i in range(n)]
    rhs = [sh[i][:c] + pres[i]["av"] for i in range(n)]
    u = [-_bdot(pres[i]["t"], _stack_heads(rhs[i], lane_head)) for i in range(n)]
    y = [sh[i][c:] + _bdot(pres[i]["m_cat"],
                           jnp.concatenate([_stack_heads(u[i], lane_head).astype(BF16),
                                            pres[i]["v_bd"]], axis=0)) for i in range(n)]
    upd = [_bdot_tn(jnp.concatenate([u[i], pres[i]["v"]], axis=0), pres[i]["bdkd"]) for i in range(n)]
    new = [states[i] * pres[i]["p_end"] + jnp.where(bd_mask, upd[i], 0.0) for i in range(n)]
    return y, new


def _page_block_mean_tasks(page_refs, km_ref, pages_per_block):
    def task(j):
        def run():
            blk = jnp.sum(page_refs[j * pages_per_block][...], axis=0)
            for i in range(1, pages_per_block):
                blk = blk + jnp.sum(page_refs[j * pages_per_block + i][...], axis=0)
            km_ref[j] = blk * (1.0 / MOBA_BLOCK)
        return run
    return [task(j) for j in range(len(page_refs) // pages_per_block)]


def _rwkv_prompt_kernel(*refs, n_pages, pages_per_block):
    if n_pages:
        refs = refs[1:]
    (zr_ref, zk_ref, zv_ref, zg_ref, zwa_ref, mur_ref, muk_ref, muv_ref, mug_ref, muwa_ref,
     w0_ref, a0_ref, kk_ref, ka_ref, rk_ref, lnw_ref, lnb_ref, w2_ref, a2_ref) = refs[:19]
    page_refs = refs[19:19 + n_pages]
    refs = refs[19 + n_pages:]
    o_ref, s_out_ref = refs[:2]
    side_tasks = []
    if n_pages:
        side_tasks = _page_block_mean_tasks(page_refs, refs[2], pages_per_block)
        refs = refs[1:]
    s_ref, pr_ref, pk_ref, pv_ref, pg_ref, pwa_ref = refs[2:]
    t = pl.program_id(1)
    nseq, tr = zr_ref.shape[0], zr_ref.shape[1]

    @pl.when(t == 0)
    def _():
        for ref in (s_ref, pr_ref, pk_ref, pv_ref, pg_ref, pwa_ref):
            ref[...] = jnp.zeros_like(ref)

    def shifted(z_ref, prev_ref, mu_ref, i):
        z = z_ref[i]
        row = lax.broadcasted_iota(jnp.int32, z.shape, 0)
        prev = jnp.where(row == 0, prev_ref[i], pltpu.roll(z, 1, 0))
        prev_ref[i] = z[tr - 1:tr, :]
        return z + (prev - z) * mu_ref[...]

    bd_mask = _head_mask(GROUP)
    ones_bd = jnp.where(bd_mask, 1.0, 0.0).astype(BF16)
    ri = lax.broadcasted_iota(jnp.int32, (tr, tr), 0)
    ci = lax.broadcasted_iota(jnp.int32, (tr, tr), 1)
    tri = jnp.where((ri // CHUNK == ci // CHUNK) & (ci <= ri), 1.0, 0.0).astype(BF16)
    rc = lax.broadcasted_iota(jnp.int32, (CHUNK, GROUP), 0)
    lc = lax.broadcasted_iota(jnp.int32, (CHUNK, GROUP), 1)
    sc = lc % CHUNK
    consts = (sc < rc, sc <= rc, jnp.where(sc == rc, 1.0, 0.0), lc // HEAD_A, bd_mask)
    last = t == pl.num_programs(1) - 1

    n_ch = tr // CHUNK
    prepped, insts = [], []
    for i in range(nseq):
        zm_r = shifted(zr_ref, pr_ref, mur_ref, i)
        zm_k = shifted(zk_ref, pk_ref, muk_ref, i)
        zm_v = shifted(zv_ref, pv_ref, muv_ref, i)
        zm_g = shifted(zg_ref, pg_ref, mug_ref, i)
        zm_wa = shifted(zwa_ref, pwa_ref, muwa_ref, i)
        r, kf, v, kk, b, logw = _rwkv_prep(zm_r, zm_k, zm_v, zm_wa, w0_ref[...], a0_ref[...],
                                           kk_ref[...], ka_ref[...], w2_ref[...], a2_ref[...], ones_bd)
        cl = sum(jnp.dot(tri, piece, preferred_element_type=F32) for piece in _split3(logw))
        prepped.append((r, kf, v, zm_g))
        for ch in range(n_ch):
            sl = slice(ch * CHUNK, (ch + 1) * CHUNK)
            insts.append((r[sl], kf[sl], v[sl], kk[sl], b[sl], logw[sl], cl[sl]))
    pres = _chunk_precompute(insts, consts, side_tasks)

    states = [s_ref[i] for i in range(nseq)]
    ys = [[] for _ in range(nseq)]
    for ch in range(n_ch):
        y, states = _chunk_state_step(states, [pres[i * n_ch + ch] for i in range(nseq)], consts)
        for i in range(nseq):
            ys[i].append(y[i])
    for i in range(nseq):
        s_ref[i] = states[i]
        r, kf, v, zm_g = prepped[i]
        o_ref[i] = _rwkv_post(jnp.concatenate(ys[i], axis=0), r, kf, v, zm_g, rk_ref[...],
                              lnw_ref[...], lnb_ref[...], ones_bd).astype(o_ref.dtype)

    @pl.when(last)
    def _():
        for i in range(nseq):
            for h in range(HEADS_PER_GROUP):
                s_out_ref[i, h] = s_ref[i, h * HEAD_A:(h + 1) * HEAD_A, h * HEAD_A:(h + 1) * HEAD_A]


def rwkv_prompt(z_main, z_wa, mu_main, mu_wa, vecs, w2p, a2p, cache_k=None, page_table=None, layer=0):
    w0, a0, k_k, k_a, r_k, lnx_w, lnx_b = vecs
    batch, seq, _ = z_main.shape
    tr = min(seq, 256)
    assert seq % tr == 0 and tr % CHUNK == 0
    nt = seq // tr
    ng = D_A // GROUP

    n_pages = ppb = 0
    if cache_k is not None:
        page = cache_k.shape[2]
        ppb = MOBA_BLOCK // page
        total = page_table.size
        if total % (ng * nt) == 0 and (total // (ng * nt)) % ppb == 0:
            n_pages = total // (ng * nt)

    def zspec(col):
        return pl.BlockSpec((batch, tr, GROUP), lambda g, t, *_, c=col // GROUP: (0, t, c + g))

    def vspec(col=0):
        return pl.BlockSpec((1, GROUP), lambda g, t, *_, c=col // GROUP: (0, c + g))

    in_specs = [
        zspec(COL_R), zspec(COL_K), zspec(COL_V), zspec(COL_G),
        pl.BlockSpec((batch, tr, 2 * LORA), lambda g, t, *_: (0, t, 0)),
        vspec(COL_R), vspec(COL_K), vspec(COL_V), vspec(COL_G),
        pl.BlockSpec((1, 2 * LORA), lambda g, t, *_: (0, 0)),
        vspec(), vspec(), vspec(), vspec(), vspec(), vspec(), vspec(),
        pl.BlockSpec((2 * LORA, GROUP), lambda g, t, *_: (0, g)),
        pl.BlockSpec((2 * LORA, GROUP), lambda g, t, *_: (0, g)),
    ]
    out_specs = [
        pl.BlockSpec((batch, tr, GROUP), lambda g, t, *_: (0, t, g)),
        pl.BlockSpec((batch, HEADS_PER_GROUP, HEAD_A, HEAD_A), lambda g, t, *_: (0, g, 0, 0)),
    ]
    out_shape = [
        jax.ShapeDtypeStruct((batch, seq, D_A), BF16),
        jax.ShapeDtypeStruct((batch, N_HEADS_A, HEAD_A, HEAD_A), F32),
    ]
    operands = [z_main, z_main, z_main, z_main, z_wa,
                mu_main, mu_main, mu_main, mu_main, mu_wa,
                w0, a0, k_k, k_a, r_k, lnx_w, lnx_b, w2p, a2p]
    if n_pages:
        sq = pl.Squeezed()
        for i in range(n_pages):
            in_specs.append(pl.BlockSpec(
                (sq, sq, page, N_HEADS_B, HEAD_B),
                lambda g, t, pt, i=i: (layer, pt[(g * nt + t) * n_pages + i], 0, 0, 0)))
        operands += [cache_k] * n_pages
        out_specs.append(pl.BlockSpec((n_pages // ppb, N_HEADS_B, HEAD_B),
                                      lambda g, t, pt: (g * nt + t, 0, 0)))
        out_shape.append(jax.ShapeDtypeStruct((page_table.size // ppb, N_HEADS_B, HEAD_B), F32))
    row = lambda n: pltpu.VMEM((batch, 1, n), F32)
    scratch = [pltpu.VMEM((batch, GROUP, GROUP), F32),
               row(GROUP), row(GROUP), row(GROUP), row(GROUP), row(2 * LORA)]
    kern = functools.partial(_rwkv_prompt_kernel, n_pages=n_pages, pages_per_block=ppb)
    params = _cparams(("arbitrary", "arbitrary"))
    if not n_pages:
        o_a, wkv = pl.pallas_call(kern, grid=(ng, nt), in_specs=in_specs, out_specs=out_specs,
                                  out_shape=out_shape, scratch_shapes=scratch,
                                  compiler_params=params, name="rwkv_prompt")(*operands)
        return o_a, wkv, None
    o_a, wkv, kmean = pl.pallas_call(
        kern,
        grid_spec=pltpu.PrefetchScalarGridSpec(
            num_scalar_prefetch=1, grid=(ng, nt), in_specs=in_specs, out_specs=out_specs,
            scratch_shapes=scratch),
        out_shape=out_shape, compiler_params=params, name="rwkv_prompt",
    )(page_table.reshape(-1), *operands)
    n = page_table.shape[0]
    return o_a, wkv, kmean.reshape(n, -1, N_HEADS_B, HEAD_B)


def _rwkv_rows_kernel(z_ref, zp_ref, zwa_ref, zpwa_ref, mu_ref, muwa_ref,
                      w0_ref, a0_ref, kk_ref, ka_ref, w2_ref, a2_ref,
                      r_ref, kf_ref, v_ref, kkn_ref, b_ref, w_ref, g_ref):
    def shifted(z, zp, mu):
        return z + (zp - z) * mu

    mu = mu_ref[...]
    z = z_ref[...]
    zp = zp_ref[...]
    zm = [shifted(z[:, c:c + D_A], zp[:, c:c + D_A], mu[:, c:c + D_A])
          for c in (COL_R, COL_K, COL_V, COL_G)]
    zm_wa = shifted(zwa_ref[...], zpwa_ref[...], muwa_ref[...])
    ones_bd = jnp.where(_head_mask(D_A), 1.0, 0.0).astype(BF16)
    r, kf, v, kk, b, logw = _rwkv_prep(zm[0], zm[1], zm[2], zm_wa, w0_ref[...], a0_ref[...],
                                       kk_ref[...], ka_ref[...], w2_ref[...], a2_ref[...], ones_bd)
    r_ref[...] = r
    kf_ref[...] = kf
    v_ref[...] = v
    kkn_ref[...] = kk
    b_ref[...] = b
    w_ref[...] = jnp.exp(logw)
    g_ref[...] = zm[3]


def rwkv_rows(z_rw, zp_rw, z_wa, zp_wa, mu_rw, mu_wa, w0, a0, k_k, k_a, w2p, a2p):
    n = z_rw.shape[0]
    out = jax.ShapeDtypeStruct((n, D_A), F32)
    return pl.pallas_call(
        _rwkv_rows_kernel,
        out_shape=[out] * 7,
        compiler_params=pltpu.CompilerParams(vmem_limit_bytes=VMEM_LIMIT),
        name="rwkv_rows",
    )(z_rw, zp_rw, z_wa, zp_wa, mu_rw, mu_wa, w0, a0, k_k, k_a, w2p, a2p)


def _rwkv_step_kernel(s_ref, w_ref, kk_ref, b_ref, kf_ref, r_ref, v_ref, g_ref,
                      rk_ref, lnw_ref, lnb_ref, s_out_ref, o_ref):
    S = s_ref[...]
    w = w_ref[...]
    kk = kk_ref[...]
    b = b_ref[...]
    kf = kf_ref[...]
    r = r_ref[...]
    v = v_ref[...]
    sa = -jnp.sum(S * kk, axis=-1, keepdims=True)
    S = S * w + sa * b + v * kf
    s_out_ref[...] = S
    y = jnp.sum(S * r, axis=-1, keepdims=True)
    mean = jnp.mean(y, axis=1, keepdims=True)
    d = y - mean
    var = jnp.mean(d * d, axis=1, keepdims=True)
    yn = d * lax.rsqrt(var + GN_EPS) * lnw_ref[...] + lnb_ref[...]
    bonus = jnp.sum(r * kf * rk_ref[...], axis=-1, keepdims=True) * v
    o_ref[...] = (yn + bonus) * _silu(g_ref[...])


def rwkv_step(state, w, kk, b, kf, r, v, g, r_k, lnx_w, lnx_b):
    n = state.shape[0]
    h = N_HEADS_A
    sq = pl.Squeezed()
    lane_vec = pl.BlockSpec((sq, h, 1, HEAD_A), lambda i: (i, 0, 0, 0))
    col_vec = pl.BlockSpec((sq, h, HEAD_A, 1), lambda i: (i, 0, 0, 0))
    return pl.pallas_call(
        _rwkv_step_kernel,
        grid=(n,),
        in_specs=[pl.BlockSpec((sq, h, HEAD_A, HEAD_A), lambda i: (i, 0, 0, 0)),
                  lane_vec, lane_vec, lane_vec, lane_vec, lane_vec, col_vec, col_vec,
                  pl.BlockSpec((h, 1, HEAD_A), lambda i: (0, 0, 0)),
                  pl.BlockSpec((h, HEAD_A, 1), lambda i: (0, 0, 0)),
                  pl.BlockSpec((h, HEAD_A, 1), lambda i: (0, 0, 0))],
        out_specs=[pl.BlockSpec((sq, h, HEAD_A, HEAD_A), lambda i: (i, 0, 0, 0)), col_vec],
        out_shape=[jax.ShapeDtypeStruct((n, h, HEAD_A, HEAD_A), F32),
                   jax.ShapeDtypeStruct((n, h, HEAD_A, 1), F32)],
        compiler_params=_cparams(("arbitrary",)),
        name="rwkv_step",
    )(state, w, kk, b, kf, r, v, g, r_k, lnx_w, lnx_b)


def _rope(x, cos_t, sin_t, lane):
    partner = jnp.where(lane < ROT_DIM // 2, pltpu.roll(x, HEAD_B - ROT_DIM // 2, 1),
                        pltpu.roll(x, ROT_DIM // 2, 1))
    return x * cos_t + partner * sin_t


def _attn_prep_kernel(zq_ref, zk_ref, zv_ref, cos_ref, sin_ref,
                      q_ref, kh_ref, vt_ref, ko_ref, vo_ref, km_ref):
    cos_t = cos_ref[...]
    sin_t = sin_ref[...]
    lane = lax.broadcasted_iota(jnp.int32, cos_t.shape, 1)
    vo_ref[...] = zv_ref[...]
    for h in range(N_HEADS_B):
        sl = slice(h * HEAD_B, (h + 1) * HEAD_B)
        q = _rope(zq_ref[:, sl], cos_t, sin_t, lane)
        k = _rope(zk_ref[:, sl], cos_t, sin_t, lane)
        q_ref[h] = q.T.astype(q_ref.dtype)
        kh_ref[h] = k.astype(kh_ref.dtype)
        vt_ref[h] = zv_ref[:, sl].T.astype(vt_ref.dtype)
        ko_ref[:, sl] = k
        km_ref[:, sl] = jnp.mean(k, axis=0, keepdims=True)


def attn_prep(z_main, cos_t, sin_t, batch, seq):
    tr = min(seq, MOBA_BLOCK)
    nt = seq // tr
    m = batch * seq
    sq = pl.Squeezed()
    zspec = lambda col: pl.BlockSpec((tr, D_B), lambda b, t, c=col // D_B: (b * nt + t, c))
    tab = pl.BlockSpec((tr, HEAD_B), lambda b, t: (t, 0))
    hm = pl.BlockSpec((sq, N_HEADS_B, tr, HEAD_B), lambda b, t: (b, 0, t, 0))
    hm_t = pl.BlockSpec((sq, N_HEADS_B, HEAD_B, tr), lambda b, t: (b, 0, 0, t))
    rows = pl.BlockSpec((tr, D_B), lambda b, t: (b * nt + t, 0))
    hm_shape = jax.ShapeDtypeStruct((batch, N_HEADS_B, seq, HEAD_B), BF16)
    return pl.pallas_call(
        _attn_prep_kernel,
        grid=(batch, nt),
        in_specs=[zspec(COL_Q), zspec(COL_KB), zspec(COL_VB), tab, tab],
        out_specs=[hm_t, hm, hm_t, rows, rows,
                   pl.BlockSpec((sq, 1, D_B), lambda b, t: (b * nt + t, 0, 0))],
        out_shape=[jax.ShapeDtypeStruct((batch, N_HEADS_B, HEAD_B, seq), BF16), hm_shape,
                   jax.ShapeDtypeStruct((batch, N_HEADS_B, HEAD_B, seq), BF16),
                   jax.ShapeDtypeStruct((m, D_B), F32), jax.ShapeDtypeStruct((m, D_B), F32),
                   jax.ShapeDtypeStruct((batch * nt, 1, D_B), F32)],
        compiler_params=_cparams(("arbitrary", "arbitrary")),
        name="attn_prep",
    )(z_main, z_main, z_main, cos_t, sin_t)


def _rope_rows_kernel(zq_ref, zk_ref, cos_ref, sin_ref, q_ref, k_ref):
    cos_t = cos_ref[...]
    sin_t = sin_ref[...]
    lane = lax.broadcasted_iota(jnp.int32, cos_t.shape, 1)
    for h in range(N_HEADS_B):
        sl = slice(h * HEAD_B, (h + 1) * HEAD_B)
        q_ref[:, sl] = _rope(zq_ref[:, sl], cos_t, sin_t, lane).astype(BF16).astype(F32)
        k_ref[:, sl] = _rope(zk_ref[:, sl], cos_t, sin_t, lane)


def rope_rows(zq, zk, cos_t, sin_t):
    out = jax.ShapeDtypeStruct(zq.shape, F32)
    return pl.pallas_call(_rope_rows_kernel, out_shape=[out, out], name="rope_rows")(zq, zk, cos_t, sin_t)


MOBA_HEADS_PER_STEP = 2


def _moba_prompt_kernel(q_ref, k_ref, vt_ref, km_ref, zg_ref, o_ref, sel_ref):
    qb = pl.program_id(1)
    nbatch, nhead, tq = q_ref.shape[0], q_ref.shape[1], q_ref.shape[3]
    nb = km_ref.shape[1]
    seqs = [(b, h) for b in range(nbatch) for h in range(nhead)]
    exp2_scale = HEAD_B ** -0.5 * LOG2_E
    qs = [q_ref[b, h] for b, h in seqs]

    blk = lax.broadcasted_iota(jnp.int32, (nb, tq), 0)
    past = blk < qb
    for i, (b, h) in enumerate(seqs):
        km = km_ref[b, :, h * HEAD_B:(h + 1) * HEAD_B]
        gate = sum(jnp.dot(piece, qs[i], preferred_element_type=F32)
                   for piece in _split3(km))
        for n in range(nb):
            g_n = gate[n:n + 1, :]
            beats = (gate > g_n) | ((gate == g_n) & (blk < n))
            cnt = jnp.sum(jnp.where(beats & past, 1.0, 0.0), axis=0, keepdims=True)
            sel_ref[i, n:n + 1, :] = jnp.where(cnt < MOBA_TOPK, 1.0, 0.0)

    def scores(i, n):
        b, h = seqs[i]
        start = pl.multiple_of(n * tq, tq)
        return jnp.dot(k_ref[b, h, pl.ds(start, tq), :], qs[i], preferred_element_type=F32)

    def softmax_step(s, m, l):
        m_new = jnp.maximum(m, jnp.max(s, axis=0, keepdims=True))
        alpha = jnp.exp2((m - m_new) * exp2_scale)
        p = jnp.exp2((s - m_new) * exp2_scale)
        return m_new, alpha, alpha * l + jnp.sum(p, axis=0, keepdims=True), p.astype(BF16)

    def pv(i, n, p):
        b, h = seqs[i]
        start = pl.multiple_of(n * tq, tq)
        return jnp.dot(vt_ref[b, h, :, pl.ds(start, tq)], p, preferred_element_type=F32)

    def step(n, masked, carry):
        sm = [softmax_step(masked[i], carry[i][0], carry[i][1]) for i in range(len(seqs))]
        acc = [sm[i][1] * carry[i][2] + pv(i, n, sm[i][3]) for i in range(len(seqs))]
        return [(sm[i][0], sm[i][2], acc[i]) for i in range(len(seqs))]

    def body(n, carry):
        s = [scores(i, n) for i in range(len(seqs))]
        masked = [jnp.where(sel_ref[i, pl.ds(n, 1), :] > 0.0, s[i], NEG)
                  for i in range(len(seqs))]
        return tuple(step(n, masked, carry))

    init = tuple((jnp.full((1, tq), NEG, F32), jnp.zeros((1, tq), F32),
                  jnp.zeros((HEAD_B, tq), F32)) for i in range(len(seqs)))
    carry = lax.fori_loop(0, qb, body, init)
    ki = lax.broadcasted_iota(jnp.int32, (tq, tq), 0)
    qi = lax.broadcasted_iota(jnp.int32, (tq, tq), 1)
    masked = [jnp.where(ki <= qi, scores(i, qb), NEG) for i in range(len(seqs))]
    final = step(qb, masked, list(carry))
    for i, (b, h) in enumerate(seqs):
        _, l, acc = final[i]
        sl = slice(h * HEAD_B, (h + 1) * HEAD_B)
        o_ref[b, :, sl] = ((acc / l).T * _silu(zg_ref[b, :, sl])).astype(o_ref.dtype)


def moba_prompt(qt_hm, k_hm, vt_hm, kmean, z_main):
    batch, _, seq, _ = k_hm.shape
    tq = min(seq, MOBA_BLOCK)
    nb = seq // tq
    hps = MOBA_HEADS_PER_STEP
    w = hps * HEAD_B
    return pl.pallas_call(
        _moba_prompt_kernel,
        grid=(N_HEADS_B // hps, nb),
        in_specs=[
            pl.BlockSpec((batch, hps, HEAD_B, tq), lambda h, i: (0, h, 0, i)),
            pl.BlockSpec((batch, hps, seq, HEAD_B), lambda h, i: (0, h, 0, 0)),
            pl.BlockSpec((batch, hps, HEAD_B, seq), lambda h, i: (0, h, 0, 0)),
            pl.BlockSpec((batch, nb, w), lambda h, i: (0, 0, h)),
            pl.BlockSpec((batch, tq, w), lambda h, i: (0, i, COL_GB // w + h)),
        ],
        out_specs=pl.BlockSpec((batch, tq, w), lambda h, i: (0, i, h)),
        out_shape=jax.ShapeDtypeStruct((batch, seq, D_B), BF16),
        scratch_shapes=[pltpu.VMEM((batch * hps, nb, tq), F32)],
        compiler_params=_cparams(("arbitrary", "arbitrary")),
        name="moba_prompt",
    )(qt_hm, k_hm, vt_hm, kmean, z_main)


KMEAN_PAGES = 8


def _cache_kmean_kernel(pt_ref, *refs, pages_per_block):
    del pt_ref
    for task in _page_block_mean_tasks(refs[:-1], refs[-1], pages_per_block):
        task()


def cache_kmean(cache_k, page_table, layer):
    n, n_pages = page_table.shape
    page = cache_k.shape[2]
    ppb = MOBA_BLOCK // page
    pps = min(KMEAN_PAGES, n_pages)
    assert n_pages % pps == 0 and pps % ppb == 0
    sq = pl.Squeezed()
    page_spec = lambda i: pl.BlockSpec(
        (sq, sq, page, N_HEADS_B, HEAD_B), lambda b, s, pt, i=i: (layer, pt[b, s * pps + i], 0, 0, 0))
    return pl.pallas_call(
        functools.partial(_cache_kmean_kernel, pages_per_block=ppb),
        grid_spec=pltpu.PrefetchScalarGridSpec(
            num_scalar_prefetch=1,
            grid=(n, n_pages // pps),
            in_specs=[page_spec(i) for i in range(pps)],
            out_specs=pl.BlockSpec((sq, pps // ppb, N_HEADS_B, HEAD_B), lambda b, s, pt: (b, s, 0, 0)),
        ),
        out_shape=jax.ShapeDtypeStruct((n, n_pages // ppb, N_HEADS_B, HEAD_B), F32),
        compiler_params=_cparams(("arbitrary", "arbitrary")),
        name="cache_kmean",
    )(page_table, *([cache_k] * pps))


def _sample_select_kernel(q_ref, km_ref, sel_ref):
    nb = km_ref.shape[0]
    ri = lax.broadcasted_iota(jnp.int32, (nb, nb), 0)
    ci = lax.broadcasted_iota(jnp.int32, (nb, nb), 1)
    lane = lax.broadcasted_iota(jnp.int32, (1, 128), 1)
    blk_row = lax.broadcasted_iota(jnp.int32, (1, nb), 1).astype(F32)
    for h in range(N_HEADS_B):
        km = km_ref[:, h, :]
        g_col = jnp.sum(km * q_ref[h:h + 1, :], axis=1, keepdims=True)
        g_row = jnp.sum(jnp.where(ri == ci, g_col, 0.0), axis=0, keepdims=True)
        beats = (g_col > g_row) | ((g_col == g_row) & (ri < ci))
        rank = jnp.sum(jnp.where(beats, 1.0, 0.0), axis=0, keepdims=True)
        out = jnp.zeros((1, 128), F32)
        for r in range(MOBA_TOPK):
            idx = jnp.sum(jnp.where(rank == float(r), blk_row, 0.0), axis=1, keepdims=True)
            out = jnp.where(lane == r, idx, out)
        sel_ref[h:h + 1, :] = out.astype(jnp.int32)


def sample_select(q_s, kmean_s):
    n, nb = kmean_s.shape[:2]
    sq = pl.Squeezed()
    return pl.pallas_call(
        _sample_select_kernel,
        grid=(n,),
        in_specs=[pl.BlockSpec((sq, N_HEADS_B, HEAD_B), lambda b: (b, 0, 0)),
                  pl.BlockSpec((sq, nb, N_HEADS_B, HEAD_B), lambda b: (b, 0, 0, 0))],
        out_specs=pl.BlockSpec((sq, N_HEADS_B, 128), lambda b: (b, 0, 0)),
        out_shape=jax.ShapeDtypeStruct((n, N_HEADS_B, 128), jnp.int32),
        compiler_params=_cparams(("arbitrary",)),
        name="sample_select",
    )(q_s, kmean_s)


def _sample_attn_kernel(sel_ref, pt_ref, q_ref, kn_ref, vn_ref, zg_ref, ck_ref, cv_ref, o_ref,
                        kbuf, vbuf, sem, *, layer, page, ppb):
    b = pl.program_id(0)
    nb_steps = pl.num_programs(0)
    n_sel = MOBA_TOPK * ppb
    scale = HEAD_B ** -0.5

    def copies(bb, slot):
        out = []
        for h in range(N_HEADS_B):
            for j in range(n_sel):
                pg = pt_ref[bb, sel_ref[bb, h, j // ppb] * ppb + j % ppb]
                dst = pl.ds(j * page, page)
                out.append(pltpu.make_async_copy(ck_ref.at[layer, pg, :, h, :],
                                                 kbuf.at[slot, h, dst, :], sem.at[slot, 0]))
                out.append(pltpu.make_async_copy(cv_ref.at[layer, pg, :, h, :],
                                                 vbuf.at[slot, h, dst, :], sem.at[slot, 1]))
        return out

    slot = b % 2

    @pl.when(b == 0)
    def _():
        for c in copies(0, 0):
            c.start()

    @pl.when(b + 1 < nb_steps)
    def _():
        for c in copies(b + 1, 1 - slot):
            c.start()

    for c in copies(b, slot):
        c.wait()

    for h in range(N_HEADS_B):
        q = q_ref[h]
        k = kbuf[slot, h]
        v = vbuf[slot, h]
        s = jnp.sum(k * q, axis=1, keepdims=True) * scale
        s_own = jnp.sum(kn_ref[h] * q, axis=1, keepdims=True) * scale
        m = jnp.maximum(jnp.max(s, axis=0, keepdims=True), s_own)
        p = jnp.exp(s - m)
        p_own = jnp.exp(s_own - m)
        l = jnp.sum(p, axis=0, keepdims=True) + p_own
        acc = jnp.sum(p * v, axis=0, keepdims=True) + p_own * vn_ref[h]
        o_ref[h] = (acc / l) * _silu(zg_ref[h])


def sample_attn(sel, page_table, q_s, k_new, v_new, zgb, cache_k, cache_v, layer):
    n = q_s.shape[0]
    page = cache_k.shape[2]
    ppb = MOBA_BLOCK // page
    rows = MOBA_TOPK * ppb * page
    sq = pl.Squeezed()
    vec = pl.BlockSpec((sq, N_HEADS_B, 1, HEAD_B), lambda b, sel, pt: (b, 0, 0, 0))
    hbm = pl.BlockSpec(memory_space=pl.ANY)
    return pl.pallas_call(
        functools.partial(_sample_attn_kernel, layer=layer, page=page, ppb=ppb),
        grid_spec=pltpu.PrefetchScalarGridSpec(
            num_scalar_prefetch=2,
            grid=(n,),
            in_specs=[vec, vec, vec, vec, hbm, hbm],
            out_specs=vec,
            scratch_shapes=[pltpu.VMEM((2, N_HEADS_B, rows, HEAD_B), F32),
                            pltpu.VMEM((2, N_HEADS_B, rows, HEAD_B), F32),
                            pltpu.SemaphoreType.DMA((2, 2))],
        ),
        out_shape=jax.ShapeDtypeStruct((n, N_HEADS_B, 1, HEAD_B), F32),
        compiler_params=_cparams(("arbitrary",)),
        name="sample_attn",
    )(sel, page_table, q_s, k_new, v_new, zgb, cache_k, cache_v)


def _merge_kernel(oa_ref, ob_ref, pa_ref, pb_ref, zga_ref, zgb_ref, o_ref):
    ya = jnp.dot(oa_ref[...], pa_ref[...], preferred_element_type=F32)
    yb = jnp.dot(ob_ref[...], pb_ref[...], preferred_element_type=F32)
    o_ref[...] = (_sigmoid(zga_ref[...]) * ya + _sigmoid(zgb_ref[...]) * yb).astype(o_ref.dtype)


def merge(o_a, o_b, p_a, p_b, z_main):
    m = o_a.shape[0]
    tm = min(m, 1024)
    tn = 512
    return pl.pallas_call(
        _merge_kernel,
        grid=(m // tm, D_MODEL // tn),
        in_specs=[
            pl.BlockSpec((tm, D_A), lambda i, j: (i, 0)),
            pl.BlockSpec((tm, D_B), lambda i, j: (i, 0)),
            pl.BlockSpec((D_A, tn), lambda i, j: (0, j)),
            pl.BlockSpec((D_B, tn), lambda i, j: (0, j)),
            pl.BlockSpec((tm, tn), lambda i, j: (i, COL_GA // tn + j)),
            pl.BlockSpec((tm, tn), lambda i, j: (i, COL_GM // tn + j)),
        ],
        out_specs=pl.BlockSpec((tm, tn), lambda i, j: (i, j)),
        out_shape=jax.ShapeDtypeStruct((m, D_MODEL), BF16),
        compiler_params=_cparams(("arbitrary", "arbitrary")),
        name="merge",
    )(o_a, o_b, p_a, p_b, z_main, z_main)


def _out_kernel(mg_ref, wo_ref, x_ref, fw_ref, y_ref):
    h = x_ref[...] + jnp.dot(mg_ref[...], wo_ref[...], preferred_element_type=F32)
    ms = jnp.mean(h * h, axis=-1, keepdims=True)
    y_ref[...] = (h * lax.rsqrt(ms + NORM_EPS)) * fw_ref[...]


def output_projection(merged, w_o, x, final_w):
    m = x.shape[0]
    tm = min(m, 512)
    return pl.pallas_call(
        _out_kernel,
        grid=(m // tm,),
        in_specs=[
            pl.BlockSpec((tm, D_MODEL), lambda i: (i, 0)),
            pl.BlockSpec((D_MODEL, D_MODEL), lambda i: (0, 0)),
            pl.BlockSpec((tm, D_MODEL), lambda i: (i, 0)),
            pl.BlockSpec((1, D_MODEL), lambda i: (0, 0)),
        ],
        out_specs=pl.BlockSpec((tm, D_MODEL), lambda i: (i, 0)),
        out_shape=jax.ShapeDtypeStruct((m, D_MODEL), F32),
        compiler_params=_cparams(("arbitrary",)),
        name="output_projection",
    )(merged, w_o, x, final_w.reshape(1, -1))


def _rope_tables(pos):
    half = ROT_DIM // 2
    inv = jnp.power(jnp.float32(ROPE_THETA), -jnp.arange(half, dtype=F32) * (2.0 / ROT_DIM))
    ang = pos.astype(F32)[:, None] * inv[None, :]
    cos, sin = jnp.cos(ang), jnp.sin(ang)
    n = pos.shape[0]
    rest = HEAD_B - ROT_DIM
    cos_t = jnp.concatenate([cos, cos, jnp.ones((n, rest), F32)], axis=1)
    sin_t = jnp.concatenate([-sin, sin, jnp.zeros((n, rest), F32)], axis=1)
    return cos_t, sin_t


def _reorder_rw(a):
    r, w_lo, k, v, a_lo, g = jnp.split(
        a, [D_A, D_A + LORA, 2 * D_A + LORA, 3 * D_A + LORA, 3 * D_A + 2 * LORA], axis=-1)
    return jnp.concatenate([r, k, v, g], axis=-1), jnp.concatenate([w_lo, a_lo], axis=-1)


def kernel(x_prompt, x_sample, state_shift, state_wkv, cache_k, cache_v, page_table, ln_w, w_in, mu,
           w0, w2, a0, a2, k_k, k_a, r_k, lnx_w, lnx_b, p_a, p_b, w_o, final_w):
    depth = ln_w.shape[0]
    assert depth == 1, "single-layer trunk"
    B, T, _ = x_prompt.shape
    DB, TS, _ = x_sample.shape
    assert TS == 1
    n_pages = page_table.shape[1]
    page = cache_k.shape[2]
    past = n_pages * page
    assert past % MOBA_BLOCK == 0 and MOBA_BLOCK % page == 0
    l = 0
    rw_cols = 4 * D_A + 2 * LORA

    src_cols = [0, D_A + LORA, 2 * D_A + LORA, 3 * D_A + 2 * LORA]
    src_cols += [rw_cols + i * W_TILE for i in range((N_MAIN - 4 * D_A) // W_TILE)]
    w_main = prepare_projection_weight(w_in, l, src_cols)
    w_wa = jnp.concatenate([w_in[l][:, D_A:D_A + LORA],
                            w_in[l][:, 3 * D_A + LORA:3 * D_A + 2 * LORA]], axis=1).astype(BF16)
    mu_rw, mu_wa = _reorder_rw(mu[l][None, :])
    zeros = jnp.zeros((LORA, D_A), F32)
    w2p = jnp.concatenate([w2[l], zeros], axis=0).astype(BF16)
    a2p = jnp.concatenate([zeros, a2[l]], axis=0).astype(BF16)
    row = lambda a: a.reshape(1, -1)
    vecs = (row(w0[l]), row(a0[l]), row(k_k[l]), row(k_a[l]), row(r_k[l]), row(lnx_w[l]), row(lnx_b[l]))
    pa_bf, pb_bf, wo_bf = p_a[l].astype(BF16), p_b[l].astype(BF16), w_o[l].astype(BF16)

    rows = jnp.concatenate([x_prompt[:, -1, :], x_sample[:, 0, :]], axis=0)
    xn_rows = rmsnorm_rows(rows, ln_w[l])
    shift_prompt = xn_rows[:B]
    xn_s = xn_rows[B:]
    xp = x_prompt.reshape(B * T, D_MODEL)
    z_p, zwa_p = input_projection(xp, ln_w[l], w_main, w_wa, normalize=True)
    z_s2, zwa_s2 = input_projection(jnp.concatenate([xn_s, state_shift[l]], axis=0), ln_w[l],
                                    w_main, w_wa, normalize=False)

    z_p3 = z_p.reshape(B, T, N_MAIN)
    o_a_p, wkv_p, kmean_s = rwkv_prompt(z_p3, zwa_p.reshape(B, T, 2 * LORA), mu_rw, mu_wa, vecs, w2p, a2p,
                                        cache_k, page_table, l)
    cos_p, sin_p = _rope_tables(jnp.arange(T))
    qt_hm, k_hm, vt_hm, k_rows_p, v_rows_p, kmean_p = attn_prep(z_p, cos_p, sin_p, B, T)
    nb_p = T // min(T, MOBA_BLOCK)
    o_b_p = moba_prompt(qt_hm, k_hm, vt_hm, kmean_p.reshape(B, nb_p, D_B), z_p3)
    merged_p = merge(o_a_p.reshape(B * T, D_A), o_b_p.reshape(B * T, D_B), pa_bf, pb_bf, z_p)
    y_prompt = output_projection(merged_p, wo_bf, xp, final_w).reshape(B, T, D_MODEL)

    z_s, zprev_s = z_s2[:DB], z_s2[DB:]
    r_s, kf_s, v_s, kk_s, b_s, w_s, g_s = rwkv_rows(
        z_s[:, :4 * D_A], zprev_s[:, :4 * D_A], zwa_s2[:DB], zwa_s2[DB:], mu_rw, mu_wa,
        vecs[0], vecs[1], vecs[2], vecs[3], w2p, a2p)
    hs = lambda a: a.reshape(-1, N_HEADS_A, 1, HEAD_A)
    col = lambda a: a.reshape(-1, N_HEADS_A, HEAD_A, 1)
    wkv_s, o_a_s = rwkv_step(state_wkv[l], hs(w_s), hs(kk_s), hs(b_s), hs(kf_s), hs(r_s),
                             col(v_s), col(g_s), hs(r_k[l])[0], col(lnx_w[l])[0], col(lnx_b[l])[0])
    o_a_s = o_a_s.reshape(DB, D_A).astype(BF16)

    cos_s, sin_s = _rope_tables(past + jnp.arange(TS))
    cos_s = jnp.broadcast_to(cos_s, (DB, HEAD_B))
    sin_s = jnp.broadcast_to(sin_s, (DB, HEAD_B))
    q_rows_s, k_rows_s = rope_rows(z_s[:, COL_Q:COL_Q + D_B], z_s[:, COL_KB:COL_KB + D_B], cos_s, sin_s)
    v_rows_s = z_s[:, COL_VB:COL_VB + D_B]
    if kmean_s is None:
        kmean_s = cache_kmean(cache_k, page_table, l)
    sel = sample_select(q_rows_s.reshape(DB, N_HEADS_B, HEAD_B), kmean_s)[:, :, :MOBA_TOPK]
    hv = lambda a: a.reshape(DB, N_HEADS_B, 1, HEAD_B)
    o_b_s = sample_attn(sel, page_table, hv(q_rows_s), hv(k_rows_s), hv(v_rows_s),
                        hv(z_s[:, COL_GB:COL_GB + D_B]), cache_k, cache_v, l)
    o_b_s = o_b_s.reshape(DB, D_B).astype(BF16)
    merged_s = merge(o_a_s, o_b_s, pa_bf, pb_bf, z_s)
    y_sample = output_projection(merged_s, wo_bf, x_sample.reshape(DB, D_MODEL), final_w)

    return (y_prompt,
            y_sample.reshape(DB, TS, D_MODEL),
            shift_prompt[None],
            wkv_p.reshape(1, B, N_HEADS_A, HEAD_A, HEAD_A),
            k_rows_p.reshape(1, B, T, N_HEADS_B, HEAD_B),
            v_rows_p.reshape(1, B, T, N_HEADS_B, HEAD_B),
            xn_s[None],
            wkv_s[None],
            k_rows_s.reshape(1, DB, TS, N_HEADS_B, HEAD_B),
            v_rows_s.reshape(1, DB, TS, N_HEADS_B, HEAD_B))
```

```python
import functools

import jax
import jax.numpy as jnp
from jax import lax
from jax.experimental import pallas as pl
from jax.experimental.pallas import tpu as pltpu

F32 = jnp.float32
BF16 = jnp.bfloat16

D_MODEL = 2048
D_A = D_MODEL // 2
HEAD_A = 64
N_HEADS_A = D_A // HEAD_A
LORA = 64
D_B = D_MODEL // 2
HEAD_B = 128
N_HEADS_B = D_B // HEAD_B
ROT_DIM = HEAD_B // 4
ROPE_THETA = 500000.0
MOBA_BLOCK = 256
MOBA_TOPK = 3
NORM_EPS = 1e-6
GN_EPS = 64e-5
NEG = -1e30
LOG2_E = 1.4426950408889634

COL_R, COL_K, COL_V, COL_G = 0, D_A, 2 * D_A, 3 * D_A
COL_Q, COL_KB, COL_VB, COL_GB = 4 * D_A, 4 * D_A + D_B, 4 * D_A + 2 * D_B, 4 * D_A + 3 * D_B
COL_GA = 4 * D_A + 4 * D_B
COL_GM = COL_GA + D_MODEL
N_MAIN = COL_GM + D_MODEL

GROUP = 256
HEADS_PER_GROUP = GROUP // HEAD_A
CHUNK = 64
assert CHUNK == HEAD_A
VMEM_LIMIT = 56 * 1024 * 1024


def _cparams(sem):
    return pltpu.CompilerParams(dimension_semantics=sem, vmem_limit_bytes=VMEM_LIMIT)


def _bdot(a, b):
    return jnp.dot(a.astype(BF16), b.astype(BF16), preferred_element_type=F32)


def _bdot_nt(a, b):
    return lax.dot_general(a.astype(BF16), b.astype(BF16), (((1,), (1,)), ((), ())),
                           preferred_element_type=F32)


def _bdot_tn(a, b):
    return lax.dot_general(a.astype(BF16), b.astype(BF16), (((0,), (0,)), ((), ())),
                           preferred_element_type=F32)


def _split3(x):
    hi = x.astype(BF16)
    r1 = x - hi.astype(F32)
    mid = r1.astype(BF16)
    lo = (r1 - mid.astype(F32)).astype(BF16)
    return hi, mid, lo


def _sigmoid(x):
    return 1.0 / (1.0 + jnp.exp(-x))


def _silu(x):
    return x * _sigmoid(x)


def _softplus(x):
    return jnp.maximum(x, 0.0) + jnp.log(1.0 + jnp.exp(-jnp.abs(x)))


def _rmsnorm_rows_kernel(x_ref, w_ref, o_ref):
    x = x_ref[...]
    ms = jnp.mean(x * x, axis=-1, keepdims=True)
    o_ref[...] = (x * lax.rsqrt(ms + NORM_EPS)) * w_ref[...]


def rmsnorm_rows(x, w):
    return pl.pallas_call(
        _rmsnorm_rows_kernel,
        out_shape=jax.ShapeDtypeStruct(x.shape, F32),
        name="rmsnorm_rows",
    )(x, w.reshape(1, -1))


W_TILE = 1024
LANES = 128


def _w_prep_kernel(w_hbm, o_hbm, in_buf, out_buf, in_sem, out_sem, *, layer, starts):
    n = len(starts)
    rows = in_buf.shape[1]

    def in_copy(j, slot):
        off = starts[j] % LANES
        width = W_TILE + (LANES if off else 0)
        return pltpu.make_async_copy(w_hbm.at[layer, :, pl.ds(starts[j] - off, width)],
                                     in_buf.at[slot, :, pl.ds(0, width)], in_sem.at[slot])

    def out_copy(j, slot):
        return pltpu.make_async_copy(out_buf.at[slot], o_hbm.at[:, pl.ds(j * W_TILE, W_TILE)],
                                     out_sem.at[slot])

    in_copy(0, 0).start()
    for j in range(n):
        slot = j % 2
        if j + 1 < n:
            in_copy(j + 1, 1 - slot).start()
        in_copy(j, slot).wait()
        if j >= 2:
            out_copy(j - 2, slot).wait()
        off = starts[j] % LANES
        rc = 256

        def chunk(c, _, slot=slot, off=off):
            r0 = pl.multiple_of(c * rc, rc)
            out_buf[slot, pl.ds(r0, rc), :] = in_buf[slot, pl.ds(r0, rc), off:off + W_TILE].astype(BF16)
            return 0

        lax.fori_loop(0, rows // rc, chunk, 0)
        out_copy(j, slot).start()
    for j in range(max(n - 2, 0), n):
        out_copy(j, j % 2).wait()


def prepare_projection_weight(w_in, layer, starts):
    rows = w_in.shape[1]
    assert rows % 256 == 0
    hbm = pl.BlockSpec(memory_space=pl.ANY)
    return pl.pallas_call(
        functools.partial(_w_prep_kernel, layer=layer, starts=tuple(starts)),
        in_specs=[hbm],
        out_specs=hbm,
        out_shape=jax.ShapeDtypeStruct((rows, len(starts) * W_TILE), BF16),
        scratch_shapes=[pltpu.VMEM((2, rows, W_TILE + LANES), F32),
                        pltpu.VMEM((2, rows, W_TILE), BF16),
                        pltpu.SemaphoreType.DMA((2,)), pltpu.SemaphoreType.DMA((2,))],
        compiler_params=pltpu.CompilerParams(vmem_limit_bytes=VMEM_LIMIT),
        name="prepare_projection_weight",
    )(w_in)


def _proj_kernel(x_ref, lnw_ref, w_ref, wwa_ref, z_ref, zwa_ref, xn_ref, *, normalize):
    @pl.when(pl.program_id(1) == 0)
    def _():
        x = x_ref[...]
        if normalize:
            ms = jnp.mean(x * x, axis=-1, keepdims=True)
            x = (x * lax.rsqrt(ms + NORM_EPS)) * lnw_ref[...]
        xn_ref[...] = x.astype(BF16)
        zwa_ref[...] = jnp.dot(xn_ref[...], wwa_ref[...].astype(BF16), preferred_element_type=F32)

    z_ref[...] = jnp.dot(xn_ref[...], w_ref[...], preferred_element_type=F32)


def input_projection(x, ln_w, w_main, w_wa, *, normalize):
    m = x.shape[0]
    tm = min(m, 1024)
    tn = 1024 if m >= 1024 else 2048
    assert m % tm == 0 and N_MAIN % tn == 0
    return pl.pallas_call(
        functools.partial(_proj_kernel, normalize=normalize),
        grid=(m // tm, N_MAIN // tn),
        in_specs=[
            pl.BlockSpec((tm, D_MODEL), lambda i, j: (i, 0)),
            pl.BlockSpec((1, D_MODEL), lambda i, j: (0, 0)),
            pl.BlockSpec((D_MODEL, tn), lambda i, j: (0, j)),
            pl.BlockSpec((D_MODEL, 2 * LORA), lambda i, j: (0, 0)),
        ],
        out_specs=[
            pl.BlockSpec((tm, tn), lambda i, j: (i, j)),
            pl.BlockSpec((tm, 2 * LORA), lambda i, j: (i, 0)),
        ],
        out_shape=[
            jax.ShapeDtypeStruct((m, N_MAIN), F32),
            jax.ShapeDtypeStruct((m, 2 * LORA), F32),
        ],
        scratch_shapes=[pltpu.VMEM((tm, D_MODEL), BF16)],
        compiler_params=_cparams(("arbitrary", "arbitrary")),
        name="input_projection",
    )(x, ln_w.reshape(1, -1), w_main, w_wa)


def _head_mask(n):
    r = lax.broadcasted_iota(jnp.int32, (n, n), 0) // HEAD_A
    c = lax.broadcasted_iota(jnp.int32, (n, n), 1) // HEAD_A
    return r == c


def _segsum(x, ones_bd):
    return jnp.dot(x.astype(BF16), ones_bd, preferred_element_type=F32)


def _rwkv_prep(zm_r, zm_k, zm_v, zm_wa, w0, a0, k_k, k_a, w2p, a2p, ones_bd):
    lane = lax.broadcasted_iota(jnp.int32, zm_wa.shape, 1)
    lora_in = jnp.where(lane < LORA, jnp.tanh(zm_wa), zm_wa)
    ww = _bdot(lora_in, w2p)
    aa = _bdot(lora_in, a2p)
    w_log = -_softplus(-(w0 + ww)) - 0.5
    logw = -jnp.exp(w_log)
    a = _sigmoid(a0 + aa)
    kk = zm_k * k_k
    ss = _segsum(kk * kk, ones_bd)
    kk = kk / jnp.maximum(jnp.sqrt(ss), 1e-12)
    kf = zm_k * (1.0 + (a - 1.0) * k_a)
    return zm_r, kf, zm_v, kk, kk * a, logw


def _rwkv_post(y, r, kf, v, zm_g, r_k, lnx_w, lnx_b, ones_bd):
    inv_n = 1.0 / HEAD_A
    mean = _segsum(y, ones_bd) * inv_n
    d = y - mean
    var = _segsum(d * d, ones_bd) * inv_n
    yn = d * lax.rsqrt(var + GN_EPS) * lnx_w + lnx_b
    bonus = _segsum(r * kf * r_k, ones_bd) * v
    return (yn + bonus) * _silu(zm_g)


def _stack_heads(x, lane_head):
    return jnp.concatenate(
        [jnp.where(lane_head == h, x, 0.0) for h in range(HEADS_PER_GROUP)], axis=0)


def _chunk_precompute(insts, consts, side_tasks=()):
    strict, incl, eye, lane_head, bd_mask = consts
    c = CHUNK
    n = len(insts)
    side_tasks = list(side_tasks)

    def run_side(k=1):
        for _ in range(k):
            if side_tasks:
                side_tasks.pop(0)()

    def bd(p):
        return jnp.where(bd_mask, jnp.concatenate([p] * HEADS_PER_GROUP, axis=0), 0.0).astype(BF16)

    lhs, wt, v_bd, bdkd, p_end = [], [], [], [], []
    for r, kf, v, kk, b, logw, cl in insts:
        cl_last = cl[c - 1:c, :]
        e_neg = jnp.exp(-cl)
        e_end = jnp.exp(cl_last - cl)
        lhs.append(jnp.concatenate([kk * jnp.exp(cl - logw), r * jnp.exp(cl)], axis=0).astype(BF16))
        wt.append(jnp.concatenate([_stack_heads(b * e_neg, lane_head),
                                   _stack_heads(kf * e_neg, lane_head)], axis=0).astype(BF16))
        v_bd.append(_stack_heads(v, lane_head).astype(BF16))
        bdkd.append(jnp.concatenate([b * e_end, kf * e_end], axis=0).astype(BF16))
        p_end.append(jnp.exp(cl_last))
    att = [_bdot_nt(lhs[i], wt[i]) for i in range(n)]
    run_side()
    a_ab = [jnp.where(strict, att[i][:c, :4 * c], 0.0) for i in range(n)]
    a_ak = [jnp.where(strict, att[i][:c, 4 * c:], 0.0) for i in range(n)]
    m_cat = [jnp.concatenate([jnp.where(incl, att[i][c:, :4 * c], 0.0),
                              jnp.where(incl, att[i][c:, 4 * c:], 0.0)], axis=1).astype(BF16)
             for i in range(n)]
    av = [_bdot(a_ak[i], v_bd[i]) for i in range(n)]
    run_side()

    x = [eye - a_ab[i] for i in range(n)]
    p = [_bdot(a_ab[i], bd(a_ab[i])) for i in range(n)]
    run_side()
    for _ in range(4):
        px = [_bdot(jnp.concatenate([p[i], x[i]], axis=0), bd(p[i])) for i in range(n)]
        p = [px[i][:c] for i in range(n)]
        x = [x[i] + px[i][c:] for i in range(n)]
        run_side()
    x = [(x[i] + _bdot(x[i], bd(p[i]))).astype(BF16) for i in range(n)]
    run_side(len(side_tasks))
    return [dict(lhs=lhs[i], av=av[i], t=x[i], m_cat=m_cat[i], v_bd=v_bd[i], v=insts[i][2],
                 bdkd=bdkd[i], p_end=p_end[i]) for i in range(n)]


def _chunk_state_step(states, pres, consts):
    lane_head, bd_mask = consts[3], consts[4]
    c = CHUNK
    n = len(states)
    sh = [_bdot_nt(pres[i]["lhs"], states[i]) for i in range(n)]
    rhs = [sh[i][:c] + pres[i]["av"] for i in range(n)]
    u = [-_bdot(pres[i]["t"], _stack_heads(rhs[i], lane_head)) for i in range(n)]
    y = [sh[i][c:] + _bdot(pres[i]["m_cat"],
                           jnp.concatenate([_stack_heads(u[i], lane_head).astype(BF16),
                                            pres[i]["v_bd"]], axis=0)) for i in range(n)]
    upd = [_bdot_tn(jnp.concatenate([u[i], pres[i]["v"]], axis=0), pres[i]["bdkd"]) for i in range(n)]
    new = [states[i] * pres[i]["p_end"] + jnp.where(bd_mask, upd[i], 0.0) for i in range(n)]
    return y, new


def _page_block_mean_tasks(page_refs, km_ref, pages_per_block):
    def task(j):
        def run():
            blk = jnp.sum(page_refs[j * pages_per_block][...], axis=0)
            for i in range(1, pages_per_block):
                blk = blk + jnp.sum(page_refs[j * pages_per_block + i][...], axis=0)
            km_ref[j] = blk * (1.0 / MOBA_BLOCK)
        return run
    return [task(j) for j in range(len(page_refs) // pages_per_block)]


def _rwkv_prompt_kernel(*refs, n_pages, pages_per_block):
    if n_pages:
        refs = refs[1:]
    (zr_ref, zk_ref, zv_ref, zg_ref, zwa_ref, mur_ref, muk_ref, muv_ref, mug_ref, muwa_ref,
     w0_ref, a0_ref, kk_ref, ka_ref, rk_ref, lnw_ref, lnb_ref, w2_ref, a2_ref) = refs[:19]
    page_refs = refs[19:19 + n_pages]
    refs = refs[19 + n_pages:]
    o_ref, s_out_ref = refs[:2]
    side_tasks = []
    if n_pages:
        side_tasks = _page_block_mean_tasks(page_refs, refs[2], pages_per_block)
        refs = refs[1:]
    s_ref, pr_ref, pk_ref, pv_ref, pg_ref, pwa_ref = refs[2:]
    t = pl.program_id(1)
    nseq, tr = zr_ref.shape[0], zr_ref.shape[1]

    @pl.when(t == 0)
    def _():
        for ref in (s_ref, pr_ref, pk_ref, pv_ref, pg_ref, pwa_ref):
            ref[...] = jnp.zeros_like(ref)

    def shifted(z_ref, prev_ref, mu_ref, i):
        z = z_ref[i]
        row = lax.broadcasted_iota(jnp.int32, z.shape, 0)
        prev = jnp.where(row == 0, prev_ref[i], pltpu.roll(z, 1, 0))
        prev_ref[i] = z[tr - 1:tr, :]
        return z + (prev - z) * mu_ref[...]

    bd_mask = _head_mask(GROUP)
    ones_bd = jnp.where(bd_mask, 1.0, 0.0).astype(BF16)
    ri = lax.broadcasted_iota(jnp.int32, (tr, tr), 0)
    ci = lax.broadcasted_iota(jnp.int32, (tr, tr), 1)
    tri = jnp.where((ri // CHUNK == ci // CHUNK) & (ci <= ri), 1.0, 0.0).astype(BF16)
    rc = lax.broadcasted_iota(jnp.int32, (CHUNK, GROUP), 0)
    lc = lax.broadcasted_iota(jnp.int32, (CHUNK, GROUP), 1)
    sc = lc % CHUNK
    consts = (sc < rc, sc <= rc, jnp.where(sc == rc, 1.0, 0.0), lc // HEAD_A, bd_mask)
    last = t == pl.num_programs(1) - 1

    n_ch = tr // CHUNK
    prepped, insts = [], []
    for i in range(nseq):
        zm_r = shifted(zr_ref, pr_ref, mur_ref, i)
        zm_k = shifted(zk_ref, pk_ref, muk_ref, i)
        zm_v = shifted(zv_ref, pv_ref, muv_ref, i)
        zm_g = shifted(zg_ref, pg_ref, mug_ref, i)
        zm_wa = shifted(zwa_ref, pwa_ref, muwa_ref, i)
        r, kf, v, kk, b, logw = _rwkv_prep(zm_r, zm_k, zm_v, zm_wa, w0_ref[...], a0_ref[...],
                                           kk_ref[...], ka_ref[...], w2_ref[...], a2_ref[...], ones_bd)
        cl = sum(jnp.dot(tri, piece, preferred_element_type=F32) for piece in _split3(logw))
        prepped.append((r, kf, v, zm_g))
        for ch in range(n_ch):
            sl = slice(ch * CHUNK, (ch + 1) * CHUNK)
            insts.append((r[sl], kf[sl], v[sl], kk[sl], b[sl], logw[sl], cl[sl]))
    pres = _chunk_precompute(insts, consts, side_tasks)

    states = [s_ref[i] for i in range(nseq)]
    ys = [[] for _ in range(nseq)]
    for ch in range(n_ch):
        y, states = _chunk_state_step(states, [pres[i * n_ch + ch] for i in range(nseq)], consts)
        for i in range(nseq):
            ys[i].append(y[i])
    for i in range(nseq):
        s_ref[i] = states[i]
        r, kf, v, zm_g = prepped[i]
        o_ref[i] = _rwkv_post(jnp.concatenate(ys[i], axis=0), r, kf, v, zm_g, rk_ref[...],
                              lnw_ref[...], lnb_ref[...], ones_bd).astype(o_ref.dtype)

    @pl.when(last)
    def _():
        for i in range(nseq):
            for h in range(HEADS_PER_GROUP):
                s_out_ref[i, h] = s_ref[i, h * HEAD_A:(h + 1) * HEAD_A, h * HEAD_A:(h + 1) * HEAD_A]


def rwkv_prompt(z_main, z_wa, mu_main, mu_wa, vecs, w2p, a2p, cache_k=None, page_table=None, layer=0):
    w0, a0, k_k, k_a, r_k, lnx_w, lnx_b = vecs
    batch, seq, _ = z_main.shape
    tr = min(seq, 256)
    assert seq % tr == 0 and tr % CHUNK == 0
    nt = seq // tr
    ng = D_A // GROUP

    n_pages = ppb = 0
    if cache_k is not None:
        page = cache_k.shape[2]
        ppb = MOBA_BLOCK // page
        total = page_table.size
        if total % (ng * nt) == 0 and (total // (ng * nt)) % ppb == 0:
            n_pages = total // (ng * nt)

    def zspec(col):
        return pl.BlockSpec((batch, tr, GROUP), lambda g, t, *_, c=col // GROUP: (0, t, c + g))

    def vspec(col=0):
        return pl.BlockSpec((1, GROUP), lambda g, t, *_, c=col // GROUP: (0, c + g))

    in_specs = [
        zspec(COL_R), zspec(COL_K), zspec(COL_V), zspec(COL_G),
        pl.BlockSpec((batch, tr, 2 * LORA), lambda g, t, *_: (0, t, 0)),
        vspec(COL_R), vspec(COL_K), vspec(COL_V), vspec(COL_G),
        pl.BlockSpec((1, 2 * LORA), lambda g, t, *_: (0, 0)),
        vspec(), vspec(), vspec(), vspec(), vspec(), vspec(), vspec(),
        pl.BlockSpec((2 * LORA, GROUP), lambda g, t, *_: (0, g)),
        pl.BlockSpec((2 * LORA, GROUP), lambda g, t, *_: (0, g)),
    ]
    out_specs = [
        pl.BlockSpec((batch, tr, GROUP), lambda g, t, *_: (0, t, g)),
        pl.BlockSpec((batch, HEADS_PER_GROUP, HEAD_A, HEAD_A), lambda g, t, *_: (0, g, 0, 0)),
    ]
    out_shape = [
        jax.ShapeDtypeStruct((batch, seq, D_A), BF16),
        jax.ShapeDtypeStruct((batch, N_HEADS_A, HEAD_A, HEAD_A), F32),
    ]
    operands = [z_main, z_main, z_main, z_main, z_wa,
                mu_main, mu_main, mu_main, mu_main, mu_wa,
                w0, a0, k_k, k_a, r_k, lnx_w, lnx_b, w2p, a2p]
    if n_pages:
        sq = pl.Squeezed()
        for i in range(n_pages):
            in_specs.append(pl.BlockSpec(
                (sq, sq, page, N_HEADS_B, HEAD_B),
                lambda g, t, pt, i=i: (layer, pt[(g * nt + t) * n_pages + i], 0, 0, 0)))
        operands += [cache_k] * n_pages
        out_specs.append(pl.BlockSpec((n_pages // ppb, N_HEADS_B, HEAD_B),
                                      lambda g, t, pt: (g * nt + t, 0, 0)))
        out_shape.append(jax.ShapeDtypeStruct((page_table.size // ppb, N_HEADS_B, HEAD_B), F32))
    row = lambda n: pltpu.VMEM((batch, 1, n), F32)
    scratch = [pltpu.VMEM((batch, GROUP, GROUP), F32),
               row(GROUP), row(GROUP), row(GROUP), row(GROUP), row(2 * LORA)]
    kern = functools.partial(_rwkv_prompt_kernel, n_pages=n_pages, pages_per_block=ppb)
    params = _cparams(("arbitrary", "arbitrary"))
    if not n_pages:
        o_a, wkv = pl.pallas_call(kern, grid=(ng, nt), in_specs=in_specs, out_specs=out_specs,
                                  out_shape=out_shape, scratch_shapes=scratch,
                                  compiler_params=params, name="rwkv_prompt")(*operands)
        return o_a, wkv, None
    o_a, wkv, kmean = pl.pallas_call(
        kern,
        grid_spec=pltpu.PrefetchScalarGridSpec(
            num_scalar_prefetch=1, grid=(ng, nt), in_specs=in_specs, out_specs=out_specs,
            scratch_shapes=scratch),
        out_shape=out_shape, compiler_params=params, name="rwkv_prompt",
    )(page_table.reshape(-1), *operands)
    n = page_table.shape[0]
    return o_a, wkv, kmean.reshape(n, -1, N_HEADS_B, HEAD_B)


def _rwkv_rows_kernel(z_ref, zp_ref, zwa_ref, zpwa_ref, mu_ref, muwa_ref,
                      w0_ref, a0_ref, kk_ref, ka_ref, w2_ref, a2_ref,
                      r_ref, kf_ref, v_ref, kkn_ref, b_ref, w_ref, g_ref):
    def shifted(z, zp, mu):
        return z + (zp - z) * mu

    mu = mu_ref[...]
    z = z_ref[...]
    zp = zp_ref[...]
    zm = [shifted(z[:, c:c + D_A], zp[:, c:c + D_A], mu[:, c:c + D_A])
          for c in (COL_R, COL_K, COL_V, COL_G)]
    zm_wa = shifted(zwa_ref[...], zpwa_ref[...], muwa_ref[...])
    ones_bd = jnp.where(_head_mask(D_A), 1.0, 0.0).astype(BF16)
    r, kf, v, kk, b, logw = _rwkv_prep(zm[0], zm[1], zm[2], zm_wa, w0_ref[...], a0_ref[...],
                                       kk_ref[...], ka_ref[...], w2_ref[...], a2_ref[...], ones_bd)
    r_ref[...] = r
    kf_ref[...] = kf
    v_ref[...] = v
    kkn_ref[...] = kk
    b_ref[...] = b
    w_ref[...] = jnp.exp(logw)
    g_ref[...] = zm[3]


def rwkv_rows(z_rw, zp_rw, z_wa, zp_wa, mu_rw, mu_wa, w0, a0, k_k, k_a, w2p, a2p):
    n = z_rw.shape[0]
    out = jax.ShapeDtypeStruct((n, D_A), F32)
    return pl.pallas_call(
        _rwkv_rows_kernel,
        out_shape=[out] * 7,
        compiler_params=pltpu.CompilerParams(vmem_limit_bytes=VMEM_LIMIT),
        name="rwkv_rows",
    )(z_rw, zp_rw, z_wa, zp_wa, mu_rw, mu_wa, w0, a0, k_k, k_a, w2p, a2p)


def _rwkv_step_kernel(s_ref, w_ref, kk_ref, b_ref, kf_ref, r_ref, v_ref, g_ref,
                      rk_ref, lnw_ref, lnb_ref, s_out_ref, o_ref):
    S = s_ref[...]
    w = w_ref[...]
    kk = kk_ref[...]
    b = b_ref[...]
    kf = kf_ref[...]
    r = r_ref[...]
    v = v_ref[...]
    sa = -jnp.sum(S * kk, axis=-1, keepdims=True)
    S = S * w + sa * b + v * kf
    s_out_ref[...] = S
    y = jnp.sum(S * r, axis=-1, keepdims=True)
    mean = jnp.mean(y, axis=1, keepdims=True)
    d = y - mean
    var = jnp.mean(d * d, axis=1, keepdims=True)
    yn = d * lax.rsqrt(var + GN_EPS) * lnw_ref[...] + lnb_ref[...]
    bonus = jnp.sum(r * kf * rk_ref[...], axis=-1, keepdims=True) * v
    o_ref[...] = (yn + bonus) * _silu(g_ref[...])


def rwkv_step(state, w, kk, b, kf, r, v, g, r_k, lnx_w, lnx_b):
    n = state.shape[0]
    h = N_HEADS_A
    sq = pl.Squeezed()
    lane_vec = pl.BlockSpec((sq, h, 1, HEAD_A), lambda i: (i, 0, 0, 0))
    col_vec = pl.BlockSpec((sq, h, HEAD_A, 1), lambda i: (i, 0, 0, 0))
    return pl.pallas_call(
        _rwkv_step_kernel,
        grid=(n,),
        in_specs=[pl.BlockSpec((sq, h, HEAD_A, HEAD_A), lambda i: (i, 0, 0, 0)),
                  lane_vec, lane_vec, lane_vec, lane_vec, lane_vec, col_vec, col_vec,
                  pl.BlockSpec((h, 1, HEAD_A), lambda i: (0, 0, 0)),
                  pl.BlockSpec((h, HEAD_A, 1), lambda i: (0, 0, 0)),
                  pl.BlockSpec((h, HEAD_A, 1), lambda i: (0, 0, 0))],
        out_specs=[pl.BlockSpec((sq, h, HEAD_A, HEAD_A), lambda i: (i, 0, 0, 0)), col_vec],
        out_shape=[jax.ShapeDtypeStruct((n, h, HEAD_A, HEAD_A), F32),
                   jax.ShapeDtypeStruct((n, h, HEAD_A, 1), F32)],
        compiler_params=_cparams(("arbitrary",)),
        name="rwkv_step",
    )(state, w, kk, b, kf, r, v, g, r_k, lnx_w, lnx_b)


def _rope(x, cos_t, sin_t, lane):
    partner = jnp.where(lane < ROT_DIM // 2, pltpu.roll(x, HEAD_B - ROT_DIM // 2, 1),
                        pltpu.roll(x, ROT_DIM // 2, 1))
    return x * cos_t + partner * sin_t


def _attn_prep_kernel(zq_ref, zk_ref, zv_ref, cos_ref, sin_ref,
                      q_ref, kh_ref, vt_ref, ko_ref, vo_ref, km_ref):
    cos_t = cos_ref[...]
    sin_t = sin_ref[...]
    lane = lax.broadcasted_iota(jnp.int32, cos_t.shape, 1)
    vo_ref[...] = zv_ref[...]
    for h in range(N_HEADS_B):
        sl = slice(h * HEAD_B, (h + 1) * HEAD_B)
        q = _rope(zq_ref[:, sl], cos_t, sin_t, lane)
        k = _rope(zk_ref[:, sl], cos_t, sin_t, lane)
        q_ref[h] = q.T.astype(q_ref.dtype)
        kh_ref[h] = k.astype(kh_ref.dtype)
        vt_ref[h] = zv_ref[:, sl].T.astype(vt_ref.dtype)
        ko_ref[:, sl] = k
        km_ref[:, sl] = jnp.mean(k, axis=0, keepdims=True)


def attn_prep(z_main, cos_t, sin_t, batch, seq):
    tr = min(seq, MOBA_BLOCK)
    nt = seq // tr
    m = batch * seq
    sq = pl.Squeezed()
    zspec = lambda col: pl.BlockSpec((tr, D_B), lambda b, t, c=col // D_B: (b * nt + t, c))
    tab = pl.BlockSpec((tr, HEAD_B), lambda b, t: (t, 0))
    hm = pl.BlockSpec((sq, N_HEADS_B, tr, HEAD_B), lambda b, t: (b, 0, t, 0))
    hm_t = pl.BlockSpec((sq, N_HEADS_B, HEAD_B, tr), lambda b, t: (b, 0, 0, t))
    rows = pl.BlockSpec((tr, D_B), lambda b, t: (b * nt + t, 0))
    hm_shape = jax.ShapeDtypeStruct((batch, N_HEADS_B, seq, HEAD_B), BF16)
    return pl.pallas_call(
        _attn_prep_kernel,
        grid=(batch, nt),
        in_specs=[zspec(COL_Q), zspec(COL_KB), zspec(COL_VB), tab, tab],
        out_specs=[hm_t, hm, hm_t, rows, rows,
                   pl.BlockSpec((sq, 1, D_B), lambda b, t: (b * nt + t, 0, 0))],
        out_shape=[jax.ShapeDtypeStruct((batch, N_HEADS_B, HEAD_B, seq), BF16), hm_shape,
                   jax.ShapeDtypeStruct((batch, N_HEADS_B, HEAD_B, seq), BF16),
                   jax.ShapeDtypeStruct((m, D_B), F32), jax.ShapeDtypeStruct((m, D_B), F32),
                   jax.ShapeDtypeStruct((batch * nt, 1, D_B), F32)],
        compiler_params=_cparams(("arbitrary", "arbitrary")),
        name="attn_prep",
    )(z_main, z_main, z_main, cos_t, sin_t)


def _rope_rows_kernel(zq_ref, zk_ref, cos_ref, sin_ref, q_ref, k_ref):
    cos_t = cos_ref[...]
    sin_t = sin_ref[...]
    lane = lax.broadcasted_iota(jnp.int32, cos_t.shape, 1)
    for h in range(N_HEADS_B):
        sl = slice(h * HEAD_B, (h + 1) * HEAD_B)
        q_ref[:, sl] = _rope(zq_ref[:, sl], cos_t, sin_t, lane).astype(BF16).astype(F32)
        k_ref[:, sl] = _rope(zk_ref[:, sl], cos_t, sin_t, lane)


def rope_rows(zq, zk, cos_t, sin_t):
    out = jax.ShapeDtypeStruct(zq.shape, F32)
    return pl.pallas_call(_rope_rows_kernel, out_shape=[out, out], name="rope_rows")(zq, zk, cos_t, sin_t)


MOBA_HEADS_PER_STEP = 2


def _moba_prompt_kernel(q_ref, k_ref, vt_ref, km_ref, zg_ref, o_ref, sel_ref):
    qb = pl.program_id(1)
    nbatch, nhead, tq = q_ref.shape[0], q_ref.shape[1], q_ref.shape[3]
    nb = km_ref.shape[1]
    seqs = [(b, h) for b in range(nbatch) for h in range(nhead)]
    exp2_scale = HEAD_B ** -0.5 * LOG2_E
    qs = [q_ref[b, h] for b, h in seqs]

    blk = lax.broadcasted_iota(jnp.int32, (nb, tq), 0)
    past = blk < qb
    for i, (b, h) in enumerate(seqs):
        km = km_ref[b, :, h * HEAD_B:(h + 1) * HEAD_B]
        gate = sum(jnp.dot(piece, qs[i], preferred_element_type=F32)
                   for piece in _split3(km))
        for n in range(nb):
            g_n = gate[n:n + 1, :]
            beats = (gate > g_n) | ((gate == g_n) & (blk < n))
            cnt = jnp.sum(jnp.where(beats & past, 1.0, 0.0), axis=0, keepdims=True)
            sel_ref[i, n:n + 1, :] = jnp.where(cnt < MOBA_TOPK, 1.0, 0.0)

    def scores(i, n):
        b, h = seqs[i]
        start = pl.multiple_of(n * tq, tq)
        return jnp.dot(k_ref[b, h, pl.ds(start, tq), :], qs[i], preferred_element_type=F32)

    def softmax_step(s, m, l):
        m_new = jnp.maximum(m, jnp.max(s, axis=0, keepdims=True))
        alpha = jnp.exp2((m - m_new) * exp2_scale)
        p = jnp.exp2((s - m_new) * exp2_scale)
        return m_new, alpha, alpha * l + jnp.sum(p, axis=0, keepdims=True), p.astype(BF16)

    def pv(i, n, p):
        b, h = seqs[i]
        start = pl.multiple_of(n * tq, tq)
        return jnp.dot(vt_ref[b, h, :, pl.ds(start, tq)], p, preferred_element_type=F32)

    def step(n, masked, carry):
        sm = [softmax_step(masked[i], carry[i][0], carry[i][1]) for i in range(len(seqs))]
        acc = [sm[i][1] * carry[i][2] + pv(i, n, sm[i][3]) for i in range(len(seqs))]
        return [(sm[i][0], sm[i][2], acc[i]) for i in range(len(seqs))]

    def body(n, carry):
        s = [scores(i, n) for i in range(len(seqs))]
        masked = [jnp.where(sel_ref[i, pl.ds(n, 1), :] > 0.0, s[i], NEG)
                  for i in range(len(seqs))]
        return tuple(step(n, masked, carry))

    init = tuple((jnp.full((1, tq), NEG, F32), jnp.zeros((1, tq), F32),
                  jnp.zeros((HEAD_B, tq), F32)) for i in range(len(seqs)))
    carry = lax.fori_loop(0, qb, body, init)
    ki = lax.broadcasted_iota(jnp.int32, (tq, tq), 0)
    qi = lax.broadcasted_iota(jnp.int32, (tq, tq), 1)
    masked = [jnp.where(ki <= qi, scores(i, qb), NEG) for i in range(len(seqs))]
    final = step(qb, masked, list(carry))
    for i, (b, h) in enumerate(seqs):
        _, l, acc = final[i]
        sl = slice(h * HEAD_B, (h + 1) * HEAD_B)
        o_ref[b, :, sl] = ((acc / l).T * _silu(zg_ref[b, :, sl])).astype(o_ref.dtype)


def moba_prompt(qt_hm, k_hm, vt_hm, kmean, z_main):
    batch, _, seq, _ = k_hm.shape
    tq = min(seq, MOBA_BLOCK)
    nb = seq // tq
    hps = MOBA_HEADS_PER_STEP
    w = hps * HEAD_B
    return pl.pallas_call(
        _moba_prompt_kernel,
        grid=(N_HEADS_B // hps, nb),
        in_specs=[
            pl.BlockSpec((batch, hps, HEAD_B, tq), lambda h, i: (0, h, 0, i)),
            pl.BlockSpec((batch, hps, seq, HEAD_B), lambda h, i: (0, h, 0, 0)),
            pl.BlockSpec((batch, hps, HEAD_B, seq), lambda h, i: (0, h, 0, 0)),
            pl.BlockSpec((batch, nb, w), lambda h, i: (0, 0, h)),
            pl.BlockSpec((batch, tq, w), lambda h, i: (0, i, COL_GB // w + h)),
        ],
        out_specs=pl.BlockSpec((batch, tq, w), lambda h, i: (0, i, h)),
        out_shape=jax.ShapeDtypeStruct((batch, seq, D_B), BF16),
        scratch_shapes=[pltpu.VMEM((batch * hps, nb, tq), F32)],
        compiler_params=_cparams(("arbitrary", "arbitrary")),
        name="moba_prompt",
    )(qt_hm, k_hm, vt_hm, kmean, z_main)


KMEAN_PAGES = 8


def _cache_kmean_kernel(pt_ref, *refs, pages_per_block):
    del pt_ref
    for task in _page_block_mean_tasks(refs[:-1], refs[-1], pages_per_block):
        task()


def cache_kmean(cache_k, page_table, layer):
    n, n_pages = page_table.shape
    page = cache_k.shape[2]
    ppb = MOBA_BLOCK // page
    pps = min(KMEAN_PAGES, n_pages)
    assert n_pages % pps == 0 and pps % ppb == 0
    sq = pl.Squeezed()
    page_spec = lambda i: pl.BlockSpec(
        (sq, sq, page, N_HEADS_B, HEAD_B), lambda b, s, pt, i=i: (layer, pt[b, s * pps + i], 0, 0, 0))
    return pl.pallas_call(
        functools.partial(_cache_kmean_kernel, pages_per_block=ppb),
        grid_spec=pltpu.PrefetchScalarGridSpec(
            num_scalar_prefetch=1,
            grid=(n, n_pages // pps),
            in_specs=[page_spec(i) for i in range(pps)],
            out_specs=pl.BlockSpec((sq, pps // ppb, N_HEADS_B, HEAD_B), lambda b, s, pt: (b, s, 0, 0)),
        ),
        out_shape=jax.ShapeDtypeStruct((n, n_pages // ppb, N_HEADS_B, HEAD_B), F32),
        compiler_params=_cparams(("arbitrary", "arbitrary")),
        name="cache_kmean",
    )(page_table, *([cache_k] * pps))


def _sample_select_kernel(q_ref, km_ref, sel_ref):
    nb = km_ref.shape[0]
    ri = lax.broadcasted_iota(jnp.int32, (nb, nb), 0)
    ci = lax.broadcasted_iota(jnp.int32, (nb, nb), 1)
    lane = lax.broadcasted_iota(jnp.int32, (1, 128), 1)
    blk_row = lax.broadcasted_iota(jnp.int32, (1, nb), 1).astype(F32)
    for h in range(N_HEADS_B):
        km = km_ref[:, h, :]
        g_col = jnp.sum(km * q_ref[h:h + 1, :], axis=1, keepdims=True)
        g_row = jnp.sum(jnp.where(ri == ci, g_col, 0.0), axis=0, keepdims=True)
        beats = (g_col > g_row) | ((g_col == g_row) & (ri < ci))
        rank = jnp.sum(jnp.where(beats, 1.0, 0.0), axis=0, keepdims=True)
        out = jnp.zeros((1, 128), F32)
        for r in range(MOBA_TOPK):
            idx = jnp.sum(jnp.where(rank == float(r), blk_row, 0.0), axis=1, keepdims=True)
            out = jnp.where(lane == r, idx, out)
        sel_ref[h:h + 1, :] = out.astype(jnp.int32)


def sample_select(q_s, kmean_s):
    n, nb = kmean_s.shape[:2]
    sq = pl.Squeezed()
    return pl.pallas_call(
        _sample_select_kernel,
        grid=(n,),
        in_specs=[pl.BlockSpec((sq, N_HEADS_B, HEAD_B), lambda b: (b, 0, 0)),
                  pl.BlockSpec((sq, nb, N_HEADS_B, HEAD_B), lambda b: (b, 0, 0, 0))],
        out_specs=pl.BlockSpec((sq, N_HEADS_B, 128), lambda b: (b, 0, 0)),
        out_shape=jax.ShapeDtypeStruct((n, N_HEADS_B, 128), jnp.int32),
        compiler_params=_cparams(("arbitrary",)),
        name="sample_select",
    )(q_s, kmean_s)


def _sample_attn_kernel(sel_ref, pt_ref, q_ref, kn_ref, vn_ref, zg_ref, ck_ref, cv_ref, o_ref,
                        kbuf, vbuf, sem, *, layer, page, ppb):
    b = pl.program_id(0)
    nb_steps = pl.num_programs(0)
    n_sel = MOBA_TOPK * ppb
    scale = HEAD_B ** -0.5

    def copies(bb, slot):
        out = []
        for h in range(N_HEADS_B):
            for j in range(n_sel):
                pg = pt_ref[bb, sel_ref[bb, h, j // ppb] * ppb + j % ppb]
                dst = pl.ds(j * page, page)
                out.append(pltpu.make_async_copy(ck_ref.at[layer, pg, :, h, :],
                                                 kbuf.at[slot, h, dst, :], sem.at[slot, 0]))
                out.append(pltpu.make_async_copy(cv_ref.at[layer, pg, :, h, :],
                                                 vbuf.at[slot, h, dst, :], sem.at[slot, 1]))
        return out

    slot = b % 2

    @pl.when(b == 0)
    def _():
        for c in copies(0, 0):
            c.start()

    @pl.when(b + 1 < nb_steps)
    def _():
        for c in copies(b + 1, 1 - slot):
            c.start()

    for c in copies(b, slot):
        c.wait()

    for h in range(N_HEADS_B):
        q = q_ref[h]
        k = kbuf[slot, h]
        v = vbuf[slot, h]
        s = jnp.sum(k * q, axis=1, keepdims=True) * scale
        s_own = jnp.sum(kn_ref[h] * q, axis=1, keepdims=True) * scale
        m = jnp.maximum(jnp.max(s, axis=0, keepdims=True), s_own)
        p = jnp.exp(s - m)
        p_own = jnp.exp(s_own - m)
        l = jnp.sum(p, axis=0, keepdims=True) + p_own
        acc = jnp.sum(p * v, axis=0, keepdims=True) + p_own * vn_ref[h]
        o_ref[h] = (acc / l) * _silu(zg_ref[h])


def sample_attn(sel, page_table, q_s, k_new, v_new, zgb, cache_k, cache_v, layer):
    n = q_s.shape[0]
    page = cache_k.shape[2]
    ppb = MOBA_BLOCK // page
    rows = MOBA_TOPK * ppb * page
    sq = pl.Squeezed()
    vec = pl.BlockSpec((sq, N_HEADS_B, 1, HEAD_B), lambda b, sel, pt: (b, 0, 0, 0))
    hbm = pl.BlockSpec(memory_space=pl.ANY)
    return pl.pallas_call(
        functools.partial(_sample_attn_kernel, layer=layer, page=page, ppb=ppb),
        grid_spec=pltpu.PrefetchScalarGridSpec(
            num_scalar_prefetch=2,
            grid=(n,),
            in_specs=[vec, vec, vec, vec, hbm, hbm],
            out_specs=vec,
            scratch_shapes=[pltpu.VMEM((2, N_HEADS_B, rows, HEAD_B), F32),
                            pltpu.VMEM((2, N_HEADS_B, rows, HEAD_B), F32),
                            pltpu.SemaphoreType.DMA((2, 2))],
        ),
        out_shape=jax.ShapeDtypeStruct((n, N_HEADS_B, 1, HEAD_B), F32),
        compiler_params=_cparams(("arbitrary",)),
        name="sample_attn",
    )(sel, page_table, q_s, k_new, v_new, zgb, cache_k, cache_v)


def _merge_out_kernel(oa_ref, ob_ref, pa_ref, pb_ref, zga_ref, zgm_ref, wo_ref, x_ref, fw_ref, y_ref):
    ya = jnp.dot(oa_ref[...], pa_ref[...], preferred_element_type=F32)
    yb = jnp.dot(ob_ref[...], pb_ref[...], preferred_element_type=F32)
    merged = (_sigmoid(zga_ref[...]) * ya + _sigmoid(zgm_ref[...]) * yb).astype(BF16)
    h = x_ref[...] + jnp.dot(merged, wo_ref[...], preferred_element_type=F32)
    ms = jnp.mean(h * h, axis=-1, keepdims=True)
    y_ref[...] = (h * lax.rsqrt(ms + NORM_EPS)) * fw_ref[...]


def merge_and_output(o_a, o_b, p_a, p_b, z_main, w_o, x, final_w):
    m = x.shape[0]
    tm = min(m, 256)
    resident = pl.Buffered(1)
    return pl.pallas_call(
        _merge_out_kernel,
        grid=(m // tm,),
        in_specs=[
            pl.BlockSpec((tm, D_A), lambda i: (i, 0)),
            pl.BlockSpec((tm, D_B), lambda i: (i, 0)),
            pl.BlockSpec((D_A, D_MODEL), lambda i: (0, 0), pipeline_mode=resident),
            pl.BlockSpec((D_B, D_MODEL), lambda i: (0, 0), pipeline_mode=resident),
            pl.BlockSpec((tm, D_MODEL), lambda i: (i, COL_GA // D_MODEL)),
            pl.BlockSpec((tm, D_MODEL), lambda i: (i, COL_GM // D_MODEL)),
            pl.BlockSpec((D_MODEL, D_MODEL), lambda i: (0, 0), pipeline_mode=resident),
            pl.BlockSpec((tm, D_MODEL), lambda i: (i, 0)),
            pl.BlockSpec((1, D_MODEL), lambda i: (0, 0)),
        ],
        out_specs=pl.BlockSpec((tm, D_MODEL), lambda i: (i, 0)),
        out_shape=jax.ShapeDtypeStruct((m, D_MODEL), F32),
        compiler_params=_cparams(("arbitrary",)),
        name="merge_and_output",
    )(o_a, o_b, p_a, p_b, z_main, z_main, w_o, x, final_w.reshape(1, -1))


def _rope_tables(pos):
    half = ROT_DIM // 2
    inv = jnp.power(jnp.float32(ROPE_THETA), -jnp.arange(half, dtype=F32) * (2.0 / ROT_DIM))
    ang = pos.astype(F32)[:, None] * inv[None, :]
    cos, sin = jnp.cos(ang), jnp.sin(ang)
    n = pos.shape[0]
    rest = HEAD_B - ROT_DIM
    cos_t = jnp.concatenate([cos, cos, jnp.ones((n, rest), F32)], axis=1)
    sin_t = jnp.concatenate([-sin, sin, jnp.zeros((n, rest), F32)], axis=1)
    return cos_t, sin_t


def _reorder_rw(a):
    r, w_lo, k, v, a_lo, g = jnp.split(
        a, [D_A, D_A + LORA, 2 * D_A + LORA, 3 * D_A + LORA, 3 * D_A + 2 * LORA], axis=-1)
    return jnp.concatenate([r, k, v, g], axis=-1), jnp.concatenate([w_lo, a_lo], axis=-1)


def kernel(x_prompt, x_sample, state_shift, state_wkv, cache_k, cache_v, page_table, ln_w, w_in, mu,
           w0, w2, a0, a2, k_k, k_a, r_k, lnx_w, lnx_b, p_a, p_b, w_o, final_w):
    depth = ln_w.shape[0]
    assert depth == 1, "single-layer trunk"
    B, T, _ = x_prompt.shape
    DB, TS, _ = x_sample.shape
    assert TS == 1
    n_pages = page_table.shape[1]
    page = cache_k.shape[2]
    past = n_pages * page
    assert past % MOBA_BLOCK == 0 and MOBA_BLOCK % page == 0
    l = 0
    rw_cols = 4 * D_A + 2 * LORA

    src_cols = [0, D_A + LORA, 2 * D_A + LORA, 3 * D_A + 2 * LORA]
    src_cols += [rw_cols + i * W_TILE for i in range((N_MAIN - 4 * D_A) // W_TILE)]
    w_main = prepare_projection_weight(w_in, l, src_cols)
    w_wa = jnp.concatenate([w_in[l, :, D_A:D_A + LORA],
                            w_in[l, :, 3 * D_A + LORA:3 * D_A + 2 * LORA]], axis=1)
    mu_rw, mu_wa = _reorder_rw(mu[l][None, :])
    zeros = jnp.zeros((LORA, D_A), F32)
    w2p = jnp.concatenate([w2[l], zeros], axis=0).astype(BF16)
    a2p = jnp.concatenate([zeros, a2[l]], axis=0).astype(BF16)
    row = lambda a: a.reshape(1, -1)
    vecs = (row(w0[l]), row(a0[l]), row(k_k[l]), row(k_a[l]), row(r_k[l]), row(lnx_w[l]), row(lnx_b[l]))
    pa_bf, pb_bf, wo_bf = p_a[l].astype(BF16), p_b[l].astype(BF16), w_o[l].astype(BF16)

    rows = jnp.concatenate([x_prompt[:, -1, :], x_sample[:, 0, :]], axis=0)
    xn_rows = rmsnorm_rows(rows, ln_w[l])
    shift_prompt = xn_rows[:B]
    xn_s = xn_rows[B:]
    xp = x_prompt.reshape(B * T, D_MODEL)
    z_p, zwa_p = input_projection(xp, ln_w[l], w_main, w_wa, normalize=True)
    z_s2, zwa_s2 = input_projection(jnp.concatenate([xn_s, state_shift[l]], axis=0), ln_w[l],
                                    w_main, w_wa, normalize=False)

    z_p3 = z_p.reshape(B, T, N_MAIN)
    o_a_p, wkv_p, kmean_s = rwkv_prompt(z_p3, zwa_p.reshape(B, T, 2 * LORA), mu_rw, mu_wa, vecs, w2p, a2p,
                                        cache_k, page_table, l)
    cos_p, sin_p = _rope_tables(jnp.arange(T))
    qt_hm, k_hm, vt_hm, k_rows_p, v_rows_p, kmean_p = attn_prep(z_p, cos_p, sin_p, B, T)
    nb_p = T // min(T, MOBA_BLOCK)
    o_b_p = moba_prompt(qt_hm, k_hm, vt_hm, kmean_p.reshape(B, nb_p, D_B), z_p3)
    y_prompt = merge_and_output(o_a_p.reshape(B * T, D_A), o_b_p.reshape(B * T, D_B), pa_bf, pb_bf, z_p,
                                wo_bf, xp, final_w).reshape(B, T, D_MODEL)

    z_s, zprev_s = z_s2[:DB], z_s2[DB:]
    r_s, kf_s, v_s, kk_s, b_s, w_s, g_s = rwkv_rows(
        z_s[:, :4 * D_A], zprev_s[:, :4 * D_A], zwa_s2[:DB], zwa_s2[DB:], mu_rw, mu_wa,
        vecs[0], vecs[1], vecs[2], vecs[3], w2p, a2p)
    hs = lambda a: a.reshape(-1, N_HEADS_A, 1, HEAD_A)
    col = lambda a: a.reshape(-1, N_HEADS_A, HEAD_A, 1)
    wkv_s, o_a_s = rwkv_step(state_wkv[l], hs(w_s), hs(kk_s), hs(b_s), hs(kf_s), hs(r_s),
                             col(v_s), col(g_s), hs(r_k[l])[0], col(lnx_w[l])[0], col(lnx_b[l])[0])
    o_a_s = o_a_s.reshape(DB, D_A).astype(BF16)

    cos_s, sin_s = _rope_tables(past + jnp.arange(TS))
    cos_s = jnp.broadcast_to(cos_s, (DB, HEAD_B))
    sin_s = jnp.broadcast_to(sin_s, (DB, HEAD_B))
    q_rows_s, k_rows_s = rope_rows(z_s[:, COL_Q:COL_Q + D_B], z_s[:, COL_KB:COL_KB + D_B], cos_s, sin_s)
    v_rows_s = z_s[:, COL_VB:COL_VB + D_B]
    if kmean_s is None:
        kmean_s = cache_kmean(cache_k, page_table, l)
    sel = sample_select(q_rows_s.reshape(DB, N_HEADS_B, HEAD_B), kmean_s)[:, :, :MOBA_TOPK]
    hv = lambda a: a.reshape(DB, N_HEADS_B, 1, HEAD_B)
    o_b_s = sample_attn(sel, page_table, hv(q_rows_s), hv(k_rows_s), hv(v_rows_s),
                        hv(z_s[:, COL_GB:COL_GB + D_B]), cache_k, cache_v, l)
    o_b_s = o_b_s.reshape(DB, D_B).astype(BF16)
    y_sample = merge_and_output(o_a_s, o_b_s, pa_bf, pb_bf, z_s, wo_bf, x_sample.reshape(DB, D_MODEL),
                                final_w)

    return (y_prompt,
            y_sample.reshape(DB, TS, D_MODEL),
            shift_prompt[None],
            wkv_p.reshape(1, B, N_HEADS_A, HEAD_A, HEAD_A),
            k_rows_p.reshape(1, B, T, N_HEADS_B, HEAD_B),
            v_rows_p.reshape(1, B, T, N_HEADS_B, HEAD_B),
            xn_s[None],
            wkv_s[None],
            k_rows_s.reshape(1, DB, TS, N_HEADS_B, HEAD_B),
            v_rows_s.reshape(1, DB, TS, N_HEADS_B, HEAD_B))
```

```python
import functools

import jax
import jax.numpy as jnp
from jax import lax
from jax.experimental import pallas as pl
from jax.experimental.pallas import tpu as pltpu

F32 = jnp.float32
BF16 = jnp.bfloat16

D_MODEL = 2048
D_A = D_MODEL // 2
HEAD_A = 64
N_HEADS_A = D_A // HEAD_A
LORA = 64
D_B = D_MODEL // 2
HEAD_B = 128
N_HEADS_B = D_B // HEAD_B
ROT_DIM = HEAD_B // 4
ROPE_THETA = 500000.0
MOBA_BLOCK = 256
MOBA_TOPK = 3
NORM_EPS = 1e-6
GN_EPS = 64e-5
NEG = -1e30
LOG2_E = 1.4426950408889634

COL_R, COL_K, COL_V, COL_G = 0, D_A, 2 * D_A, 3 * D_A
COL_GA = 4 * D_A
COL_GM = COL_GA + D_MODEL
COL_GB = COL_GM + D_MODEL
N_Z = COL_GB + D_B
COL_Q, COL_KB, COL_VB = N_Z, N_Z + D_B, N_Z + 2 * D_B
N_MAIN = COL_VB + D_B

GROUP = 256
HEADS_PER_GROUP = GROUP // HEAD_A
CHUNK = 64
assert CHUNK == HEAD_A
VMEM_LIMIT = 56 * 1024 * 1024


def _cparams(sem):
    return pltpu.CompilerParams(dimension_semantics=sem, vmem_limit_bytes=VMEM_LIMIT)


def _bdot(a, b):
    return jnp.dot(a.astype(BF16), b.astype(BF16), preferred_element_type=F32)


def _bdot_nt(a, b):
    return lax.dot_general(a.astype(BF16), b.astype(BF16), (((1,), (1,)), ((), ())),
                           preferred_element_type=F32)


def _bdot_tn(a, b):
    return lax.dot_general(a.astype(BF16), b.astype(BF16), (((0,), (0,)), ((), ())),
                           preferred_element_type=F32)


def _split3(x):
    hi = x.astype(BF16)
    r1 = x - hi.astype(F32)
    mid = r1.astype(BF16)
    lo = (r1 - mid.astype(F32)).astype(BF16)
    return hi, mid, lo


def _sigmoid(x):
    return 1.0 / (1.0 + jnp.exp(-x))


def _silu(x):
    return x * _sigmoid(x)


def _softplus(x):
    return jnp.maximum(x, 0.0) + jnp.log(1.0 + jnp.exp(-jnp.abs(x)))


def _rmsnorm_rows_kernel(x_ref, w_ref, o_ref):
    x = x_ref[...]
    ms = jnp.mean(x * x, axis=-1, keepdims=True)
    o_ref[...] = (x * lax.rsqrt(ms + NORM_EPS)) * w_ref[...]


def rmsnorm_rows(x, w):
    return pl.pallas_call(
        _rmsnorm_rows_kernel,
        out_shape=jax.ShapeDtypeStruct(x.shape, F32),
        name="rmsnorm_rows",
    )(x, w.reshape(1, -1))


W_TILE = 1024
LANES = 128


def _w_prep_kernel(w_hbm, o_hbm, in_buf, out_buf, in_sem, out_sem, *, layer, starts):
    n = len(starts)
    rows = in_buf.shape[1]

    def in_copy(j, slot):
        off = starts[j] % LANES
        width = W_TILE + (LANES if off else 0)
        return pltpu.make_async_copy(w_hbm.at[layer, :, pl.ds(starts[j] - off, width)],
                                     in_buf.at[slot, :, pl.ds(0, width)], in_sem.at[slot])

    def out_copy(j, slot):
        return pltpu.make_async_copy(out_buf.at[slot], o_hbm.at[:, pl.ds(j * W_TILE, W_TILE)],
                                     out_sem.at[slot])

    in_copy(0, 0).start()
    for j in range(n):
        slot = j % 2
        if j + 1 < n:
            in_copy(j + 1, 1 - slot).start()
        in_copy(j, slot).wait()
        if j >= 2:
            out_copy(j - 2, slot).wait()
        off = starts[j] % LANES
        rc = 256

        def chunk(c, _, slot=slot, off=off):
            r0 = pl.multiple_of(c * rc, rc)
            out_buf[slot, pl.ds(r0, rc), :] = in_buf[slot, pl.ds(r0, rc), off:off + W_TILE].astype(BF16)
            return 0

        lax.fori_loop(0, rows // rc, chunk, 0)
        out_copy(j, slot).start()
    for j in range(max(n - 2, 0), n):
        out_copy(j, j % 2).wait()


def prepare_projection_weight(w_in, layer, starts):
    rows = w_in.shape[1]
    assert rows % 256 == 0
    hbm = pl.BlockSpec(memory_space=pl.ANY)
    return pl.pallas_call(
        functools.partial(_w_prep_kernel, layer=layer, starts=tuple(starts)),
        in_specs=[hbm],
        out_specs=hbm,
        out_shape=jax.ShapeDtypeStruct((rows, len(starts) * W_TILE), BF16),
        scratch_shapes=[pltpu.VMEM((2, rows, W_TILE + LANES), F32),
                        pltpu.VMEM((2, rows, W_TILE), BF16),
                        pltpu.SemaphoreType.DMA((2,)), pltpu.SemaphoreType.DMA((2,))],
        compiler_params=pltpu.CompilerParams(vmem_limit_bytes=VMEM_LIMIT),
        name="prepare_projection_weight",
    )(w_in)


def _proj_kernel(x_ref, lnw_ref, w_ref, wwa_ref, z_ref, zwa_ref, xn_ref, *, normalize):
    @pl.when(pl.program_id(1) == 0)
    def _():
        x = x_ref[...]
        if normalize:
            ms = jnp.mean(x * x, axis=-1, keepdims=True)
            x = (x * lax.rsqrt(ms + NORM_EPS)) * lnw_ref[...]
        xn_ref[...] = x.astype(BF16)
        zwa_ref[...] = jnp.dot(xn_ref[...], wwa_ref[...].astype(BF16), preferred_element_type=F32)

    z_ref[...] = jnp.dot(xn_ref[...], w_ref[...], preferred_element_type=F32)


def input_projection(x, ln_w, w_main, w_wa, *, normalize, n_cols):
    m = x.shape[0]
    tm = min(m, 1024)
    tn = 1024 if m >= 1024 else 2048
    assert m % tm == 0 and n_cols % tn == 0
    return pl.pallas_call(
        functools.partial(_proj_kernel, normalize=normalize),
        grid=(m // tm, n_cols // tn),
        in_specs=[
            pl.BlockSpec((tm, D_MODEL), lambda i, j: (i, 0)),
            pl.BlockSpec((1, D_MODEL), lambda i, j: (0, 0)),
            pl.BlockSpec((D_MODEL, tn), lambda i, j: (0, j)),
            pl.BlockSpec((D_MODEL, 2 * LORA), lambda i, j: (0, 0)),
        ],
        out_specs=[
            pl.BlockSpec((tm, tn), lambda i, j: (i, j)),
            pl.BlockSpec((tm, 2 * LORA), lambda i, j: (i, 0)),
        ],
        out_shape=[
            jax.ShapeDtypeStruct((m, n_cols), F32),
            jax.ShapeDtypeStruct((m, 2 * LORA), F32),
        ],
        scratch_shapes=[pltpu.VMEM((tm, D_MODEL), BF16)],
        compiler_params=_cparams(("arbitrary", "arbitrary")),
        name="input_projection",
    )(x, ln_w.reshape(1, -1), w_main, w_wa)


def _head_mask(n):
    r = lax.broadcasted_iota(jnp.int32, (n, n), 0) // HEAD_A
    c = lax.broadcasted_iota(jnp.int32, (n, n), 1) // HEAD_A
    return r == c


def _segsum(x, ones_bd):
    return jnp.dot(x.astype(BF16), ones_bd, preferred_element_type=F32)


def _rwkv_prep(zm_r, zm_k, zm_v, zm_wa, w0, a0, k_k, k_a, w2p, a2p, ones_bd):
    lane = lax.broadcasted_iota(jnp.int32, zm_wa.shape, 1)
    lora_in = jnp.where(lane < LORA, jnp.tanh(zm_wa), zm_wa)
    ww = _bdot(lora_in, w2p)
    aa = _bdot(lora_in, a2p)
    w_log = -_softplus(-(w0 + ww)) - 0.5
    logw = -jnp.exp(w_log)
    a = _sigmoid(a0 + aa)
    kk = zm_k * k_k
    ss = _segsum(kk * kk, ones_bd)
    kk = kk / jnp.maximum(jnp.sqrt(ss), 1e-12)
    kf = zm_k * (1.0 + (a - 1.0) * k_a)
    return zm_r, kf, zm_v, kk, kk * a, logw


def _rwkv_post(y, r, kf, v, zm_g, r_k, lnx_w, lnx_b, ones_bd):
    inv_n = 1.0 / HEAD_A
    mean = _segsum(y, ones_bd) * inv_n
    d = y - mean
    var = _segsum(d * d, ones_bd) * inv_n
    yn = d * lax.rsqrt(var + GN_EPS) * lnx_w + lnx_b
    bonus = _segsum(r * kf * r_k, ones_bd) * v
    return (yn + bonus) * _silu(zm_g)


def _stack_heads(x, lane_head):
    return jnp.concatenate(
        [jnp.where(lane_head == h, x, 0.0) for h in range(HEADS_PER_GROUP)], axis=0)


def _chunk_precompute(insts, consts, side_tasks=()):
    strict, incl, eye, lane_head, bd_mask = consts
    c = CHUNK
    n = len(insts)
    side_tasks = list(side_tasks)

    def run_side(k=1):
        for _ in range(k):
            if side_tasks:
                side_tasks.pop(0)()

    def bd(p):
        return jnp.where(bd_mask, jnp.concatenate([p] * HEADS_PER_GROUP, axis=0), 0.0).astype(BF16)

    lhs, wt, v_bd, bdkd, p_end = [], [], [], [], []
    for r, kf, v, kk, b, logw, cl in insts:
        cl_last = cl[c - 1:c, :]
        e_neg = jnp.exp(-cl)
        e_end = jnp.exp(cl_last - cl)
        lhs.append(jnp.concatenate([kk * jnp.exp(cl - logw), r * jnp.exp(cl)], axis=0).astype(BF16))
        wt.append(jnp.concatenate([_stack_heads(b * e_neg, lane_head),
                                   _stack_heads(kf * e_neg, lane_head)], axis=0).astype(BF16))
        v_bd.append(_stack_heads(v, lane_head).astype(BF16))
        bdkd.append(jnp.concatenate([b * e_end, kf * e_end], axis=0).astype(BF16))
        p_end.append(jnp.exp(cl_last))
    att = [_bdot_nt(lhs[i], wt[i]) for i in range(n)]
    run_side()
    a_ab = [jnp.where(strict, att[i][:c, :4 * c], 0.0) for i in range(n)]
    a_ak = [jnp.where(strict, att[i][:c, 4 * c:], 0.0) for i in range(n)]
    m_cat = [jnp.concatenate([jnp.where(incl, att[i][c:, :4 * c], 0.0),
                              jnp.where(incl, att[i][c:, 4 * c:], 0.0)], axis=1).astype(BF16)
             for i in range(n)]
    av = [_bdot(a_ak[i], v_bd[i]) for i in range(n)]
    run_side()

    x = [eye - a_ab[i] for i in range(n)]
    p = [_bdot(a_ab[i], bd(a_ab[i])) for i in range(n)]
    run_side()
    for _ in range(4):
        px = [_bdot(jnp.concatenate([p[i], x[i]], axis=0), bd(p[i])) for i in range(n)]
        p = [px[i][:c] for i in range(n)]
        x = [x[i] + px[i][c:] for i in range(n)]
        run_side()
    x = [(x[i] + _bdot(x[i], bd(p[i]))).astype(BF16) for i in range(n)]
    run_side(len(side_tasks))
    return [dict(lhs=lhs[i], av=av[i], t=x[i], m_cat=m_cat[i], v_bd=v_bd[i], v=insts[i][2],
                 bdkd=bdkd[i], p_end=p_end[i]) for i in range(n)]


def _chunk_state_step(states, pres, consts):
    lane_head, bd_mask = consts[3], consts[4]
    c = CHUNK
    n = len(states)
    sh = [_bdot_nt(pres[i]["lhs"], states[i]) for i in range(n)]
    rhs = [sh[i][:c] + pres[i]["av"] for i in range(n)]
    u = [-_bdot(pres[i]["t"], _stack_heads(rhs[i], lane_head)) for i in range(n)]
    y = [sh[i][c:] + _bdot(pres[i]["m_cat"],
                           jnp.concatenate([_stack_heads(u[i], lane_head).astype(BF16),
                                            pres[i]["v_bd"]], axis=0)) for i in range(n)]
    upd = [_bdot_tn(jnp.concatenate([u[i], pres[i]["v"]], axis=0), pres[i]["bdkd"]) for i in range(n)]
    new = [states[i] * pres[i]["p_end"] + jnp.where(bd_mask, upd[i], 0.0) for i in range(n)]
    return y, new


def _page_block_mean_tasks(page_refs, km_ref, pages_per_block):
    def task(j):
        def run():
            blk = jnp.sum(page_refs[j * pages_per_block][...], axis=0)
            for i in range(1, pages_per_block):
                blk = blk + jnp.sum(page_refs[j * pages_per_block + i][...], axis=0)
            km_ref[j] = blk * (1.0 / MOBA_BLOCK)
        return run
    return [task(j) for j in range(len(page_refs) // pages_per_block)]


def _rwkv_prompt_kernel(*refs, n_pages, pages_per_block):
    if n_pages:
        refs = refs[1:]
    (zr_ref, zk_ref, zv_ref, zg_ref, zwa_ref, mur_ref, muk_ref, muv_ref, mug_ref, muwa_ref,
     w0_ref, a0_ref, kk_ref, ka_ref, rk_ref, lnw_ref, lnb_ref, w2_ref, a2_ref) = refs[:19]
    page_refs = refs[19:19 + n_pages]
    refs = refs[19 + n_pages:]
    o_ref, s_out_ref = refs[:2]
    side_tasks = []
    if n_pages:
        side_tasks = _page_block_mean_tasks(page_refs, refs[2], pages_per_block)
        refs = refs[1:]
    s_ref, pr_ref, pk_ref, pv_ref, pg_ref, pwa_ref = refs[2:]
    t = pl.program_id(1)
    nseq, tr = zr_ref.shape[0], zr_ref.shape[1]

    @pl.when(t == 0)
    def _():
        for ref in (s_ref, pr_ref, pk_ref, pv_ref, pg_ref, pwa_ref):
            ref[...] = jnp.zeros_like(ref)

    def shifted(z_ref, prev_ref, mu_ref, i):
        z = z_ref[i]
        row = lax.broadcasted_iota(jnp.int32, z.shape, 0)
        prev = jnp.where(row == 0, prev_ref[i], pltpu.roll(z, 1, 0))
        prev_ref[i] = z[tr - 1:tr, :]
        return z + (prev - z) * mu_ref[...]

    bd_mask = _head_mask(GROUP)
    ones_bd = jnp.where(bd_mask, 1.0, 0.0).astype(BF16)
    ri = lax.broadcasted_iota(jnp.int32, (tr, tr), 0)
    ci = lax.broadcasted_iota(jnp.int32, (tr, tr), 1)
    tri = jnp.where((ri // CHUNK == ci // CHUNK) & (ci <= ri), 1.0, 0.0).astype(BF16)
    rc = lax.broadcasted_iota(jnp.int32, (CHUNK, GROUP), 0)
    lc = lax.broadcasted_iota(jnp.int32, (CHUNK, GROUP), 1)
    sc = lc % CHUNK
    consts = (sc < rc, sc <= rc, jnp.where(sc == rc, 1.0, 0.0), lc // HEAD_A, bd_mask)
    last = t == pl.num_programs(1) - 1

    n_ch = tr // CHUNK
    prepped, insts = [], []
    for i in range(nseq):
        zm_r = shifted(zr_ref, pr_ref, mur_ref, i)
        zm_k = shifted(zk_ref, pk_ref, muk_ref, i)
        zm_v = shifted(zv_ref, pv_ref, muv_ref, i)
        zm_g = shifted(zg_ref, pg_ref, mug_ref, i)
        zm_wa = shifted(zwa_ref, pwa_ref, muwa_ref, i)
        r, kf, v, kk, b, logw = _rwkv_prep(zm_r, zm_k, zm_v, zm_wa, w0_ref[...], a0_ref[...],
                                           kk_ref[...], ka_ref[...], w2_ref[...], a2_ref[...], ones_bd)
        cl = sum(jnp.dot(tri, piece, preferred_element_type=F32) for piece in _split3(logw))
        prepped.append((r, kf, v, zm_g))
        for ch in range(n_ch):
            sl = slice(ch * CHUNK, (ch + 1) * CHUNK)
            insts.append((r[sl], kf[sl], v[sl], kk[sl], b[sl], logw[sl], cl[sl]))
    pres = _chunk_precompute(insts, consts, side_tasks)

    states = [s_ref[i] for i in range(nseq)]
    ys = [[] for _ in range(nseq)]
    for ch in range(n_ch):
        y, states = _chunk_state_step(states, [pres[i * n_ch + ch] for i in range(nseq)], consts)
        for i in range(nseq):
            ys[i].append(y[i])
    for i in range(nseq):
        s_ref[i] = states[i]
        r, kf, v, zm_g = prepped[i]
        o_ref[i] = _rwkv_post(jnp.concatenate(ys[i], axis=0), r, kf, v, zm_g, rk_ref[...],
                              lnw_ref[...], lnb_ref[...], ones_bd).astype(o_ref.dtype)

    @pl.when(last)
    def _():
        for i in range(nseq):
            for h in range(HEADS_PER_GROUP):
                s_out_ref[i, h] = s_ref[i, h * HEAD_A:(h + 1) * HEAD_A, h * HEAD_A:(h + 1) * HEAD_A]


def rwkv_prompt(z_main, z_wa, mu_main, mu_wa, vecs, w2p, a2p, cache_k=None, page_table=None, layer=0):
    w0, a0, k_k, k_a, r_k, lnx_w, lnx_b = vecs
    batch, seq, _ = z_main.shape
    tr = min(seq, 256)
    assert seq % tr == 0 and tr % CHUNK == 0
    nt = seq // tr
    ng = D_A // GROUP

    n_pages = ppb = 0
    if cache_k is not None:
        page = cache_k.shape[2]
        ppb = MOBA_BLOCK // page
        total = page_table.size
        if total % (ng * nt) == 0 and (total // (ng * nt)) % ppb == 0:
            n_pages = total // (ng * nt)

    def zspec(col):
        return pl.BlockSpec((batch, tr, GROUP), lambda g, t, *_, c=col // GROUP: (0, t, c + g))

    def vspec(col=0):
        return pl.BlockSpec((1, GROUP), lambda g, t, *_, c=col // GROUP: (0, c + g))

    in_specs = [
        zspec(COL_R), zspec(COL_K), zspec(COL_V), zspec(COL_G),
        pl.BlockSpec((batch, tr, 2 * LORA), lambda g, t, *_: (0, t, 0)),
        vspec(COL_R), vspec(COL_K), vspec(COL_V), vspec(COL_G),
        pl.BlockSpec((1, 2 * LORA), lambda g, t, *_: (0, 0)),
        vspec(), vspec(), vspec(), vspec(), vspec(), vspec(), vspec(),
        pl.BlockSpec((2 * LORA, GROUP), lambda g, t, *_: (0, g)),
        pl.BlockSpec((2 * LORA, GROUP), lambda g, t, *_: (0, g)),
    ]
    out_specs = [
        pl.BlockSpec((batch, tr, GROUP), lambda g, t, *_: (0, t, g)),
        pl.BlockSpec((batch, HEADS_PER_GROUP, HEAD_A, HEAD_A), lambda g, t, *_: (0, g, 0, 0)),
    ]
    out_shape = [
        jax.ShapeDtypeStruct((batch, seq, D_A), BF16),
        jax.ShapeDtypeStruct((batch, N_HEADS_A, HEAD_A, HEAD_A), F32),
    ]
    operands = [z_main, z_main, z_main, z_main, z_wa,
                mu_main, mu_main, mu_main, mu_main, mu_wa,
                w0, a0, k_k, k_a, r_k, lnx_w, lnx_b, w2p, a2p]
    if n_pages:
        sq = pl.Squeezed()
        for i in range(n_pages):
            in_specs.append(pl.BlockSpec(
                (sq, sq, page, N_HEADS_B, HEAD_B),
                lambda g, t, pt, i=i: (layer, pt[(g * nt + t) * n_pages + i], 0, 0, 0)))
        operands += [cache_k] * n_pages
        out_specs.append(pl.BlockSpec((n_pages // ppb, N_HEADS_B, HEAD_B),
                                      lambda g, t, pt: (g * nt + t, 0, 0)))
        out_shape.append(jax.ShapeDtypeStruct((page_table.size // ppb, N_HEADS_B, HEAD_B), F32))
    row = lambda n: pltpu.VMEM((batch, 1, n), F32)
    scratch = [pltpu.VMEM((batch, GROUP, GROUP), F32),
               row(GROUP), row(GROUP), row(GROUP), row(GROUP), row(2 * LORA)]
    kern = functools.partial(_rwkv_prompt_kernel, n_pages=n_pages, pages_per_block=ppb)
    params = _cparams(("arbitrary", "arbitrary"))
    if not n_pages:
        o_a, wkv = pl.pallas_call(kern, grid=(ng, nt), in_specs=in_specs, out_specs=out_specs,
                                  out_shape=out_shape, scratch_shapes=scratch,
                                  compiler_params=params, name="rwkv_prompt")(*operands)
        return o_a, wkv, None
    o_a, wkv, kmean = pl.pallas_call(
        kern,
        grid_spec=pltpu.PrefetchScalarGridSpec(
            num_scalar_prefetch=1, grid=(ng, nt), in_specs=in_specs, out_specs=out_specs,
            scratch_shapes=scratch),
        out_shape=out_shape, compiler_params=params, name="rwkv_prompt",
    )(page_table.reshape(-1), *operands)
    n = page_table.shape[0]
    return o_a, wkv, kmean.reshape(n, -1, N_HEADS_B, HEAD_B)


def _rwkv_rows_kernel(z_ref, zp_ref, zwa_ref, zpwa_ref, mu_ref, muwa_ref,
                      w0_ref, a0_ref, kk_ref, ka_ref, w2_ref, a2_ref,
                      r_ref, kf_ref, v_ref, kkn_ref, b_ref, w_ref, g_ref):
    def shifted(z, zp, mu):
        return z + (zp - z) * mu

    mu = mu_ref[...]
    z = z_ref[...]
    zp = zp_ref[...]
    zm = [shifted(z[:, c:c + D_A], zp[:, c:c + D_A], mu[:, c:c + D_A])
          for c in (COL_R, COL_K, COL_V, COL_G)]
    zm_wa = shifted(zwa_ref[...], zpwa_ref[...], muwa_ref[...])
    ones_bd = jnp.where(_head_mask(D_A), 1.0, 0.0).astype(BF16)
    r, kf, v, kk, b, logw = _rwkv_prep(zm[0], zm[1], zm[2], zm_wa, w0_ref[...], a0_ref[...],
                                       kk_ref[...], ka_ref[...], w2_ref[...], a2_ref[...], ones_bd)
    r_ref[...] = r
    kf_ref[...] = kf
    v_ref[...] = v
    kkn_ref[...] = kk
    b_ref[...] = b
    w_ref[...] = jnp.exp(logw)
    g_ref[...] = zm[3]


def rwkv_rows(z_rw, zp_rw, z_wa, zp_wa, mu_rw, mu_wa, w0, a0, k_k, k_a, w2p, a2p):
    n = z_rw.shape[0]
    out = jax.ShapeDtypeStruct((n, D_A), F32)
    return pl.pallas_call(
        _rwkv_rows_kernel,
        out_shape=[out] * 7,
        compiler_params=pltpu.CompilerParams(vmem_limit_bytes=VMEM_LIMIT),
        name="rwkv_rows",
    )(z_rw, zp_rw, z_wa, zp_wa, mu_rw, mu_wa, w0, a0, k_k, k_a, w2p, a2p)


def _rwkv_step_kernel(s_ref, w_ref, kk_ref, b_ref, kf_ref, r_ref, v_ref, g_ref,
                      rk_ref, lnw_ref, lnb_ref, s_out_ref, o_ref):
    S = s_ref[...]
    w = w_ref[...]
    kk = kk_ref[...]
    b = b_ref[...]
    kf = kf_ref[...]
    r = r_ref[...]
    v = v_ref[...]
    sa = -jnp.sum(S * kk, axis=-1, keepdims=True)
    S = S * w + sa * b + v * kf
    s_out_ref[...] = S
    y = jnp.sum(S * r, axis=-1, keepdims=True)
    mean = jnp.mean(y, axis=1, keepdims=True)
    d = y - mean
    var = jnp.mean(d * d, axis=1, keepdims=True)
    yn = d * lax.rsqrt(var + GN_EPS) * lnw_ref[...] + lnb_ref[...]
    bonus = jnp.sum(r * kf * rk_ref[...], axis=-1, keepdims=True) * v
    o_ref[...] = (yn + bonus) * _silu(g_ref[...])


def rwkv_step(state, w, kk, b, kf, r, v, g, r_k, lnx_w, lnx_b):
    n = state.shape[0]
    h = N_HEADS_A
    sq = pl.Squeezed()
    lane_vec = pl.BlockSpec((sq, h, 1, HEAD_A), lambda i: (i, 0, 0, 0))
    col_vec = pl.BlockSpec((sq, h, HEAD_A, 1), lambda i: (i, 0, 0, 0))
    return pl.pallas_call(
        _rwkv_step_kernel,
        grid=(n,),
        in_specs=[pl.BlockSpec((sq, h, HEAD_A, HEAD_A), lambda i: (i, 0, 0, 0)),
                  lane_vec, lane_vec, lane_vec, lane_vec, lane_vec, col_vec, col_vec,
                  pl.BlockSpec((h, 1, HEAD_A), lambda i: (0, 0, 0)),
                  pl.BlockSpec((h, HEAD_A, 1), lambda i: (0, 0, 0)),
                  pl.BlockSpec((h, HEAD_A, 1), lambda i: (0, 0, 0))],
        out_specs=[pl.BlockSpec((sq, h, HEAD_A, HEAD_A), lambda i: (i, 0, 0, 0)), col_vec],
        out_shape=[jax.ShapeDtypeStruct((n, h, HEAD_A, HEAD_A), F32),
                   jax.ShapeDtypeStruct((n, h, HEAD_A, 1), F32)],
        compiler_params=_cparams(("arbitrary",)),
        name="rwkv_step",
    )(state, w, kk, b, kf, r, v, g, r_k, lnx_w, lnx_b)


def _rope(x, cos_t, sin_t, lane):
    partner = jnp.where(lane < ROT_DIM // 2, pltpu.roll(x, HEAD_B - ROT_DIM // 2, 1),
                        pltpu.roll(x, ROT_DIM // 2, 1))
    return x * cos_t + partner * sin_t


def _attn_proj_kernel(x_ref, lnw_ref, w_ref, cos_ref, sin_ref,
                      qt_ref, kh_ref, vt_ref, ko_ref, vo_ref, km_ref, xn_ref):
    j = pl.program_id(1)
    tm = x_ref.shape[0]
    slab = 2 * HEAD_B

    @pl.when(j == 0)
    def _():
        x = x_ref[...]
        ms = jnp.mean(x * x, axis=-1, keepdims=True)
        xn_ref[...] = ((x * lax.rsqrt(ms + NORM_EPS)) * lnw_ref[...]).astype(BF16)

    cos_t = cos_ref[...]
    sin_t = sin_ref[...]
    lane = lax.broadcasted_iota(jnp.int32, cos_t.shape, 1)

    def slabs():
        for sidx in range(D_B // slab):
            z = jnp.dot(xn_ref[...], w_ref[:, sidx * slab:(sidx + 1) * slab],
                        preferred_element_type=F32)
            for hh in range(slab // HEAD_B):
                h = sidx * (slab // HEAD_B) + hh
                yield h, z[:, hh * HEAD_B:(hh + 1) * HEAD_B]

    @pl.when(j == 0)
    def _():
        for h, z in slabs():
            qt_ref[h] = _rope(z, cos_t, sin_t, lane).T.astype(qt_ref.dtype)

    @pl.when(j == 1)
    def _():
        for h, z in slabs():
            sl = slice(h * HEAD_B, (h + 1) * HEAD_B)
            k = _rope(z, cos_t, sin_t, lane)
            kh_ref[h] = k.astype(kh_ref.dtype)
            ko_ref[:, sl] = k
            for blk in range(km_ref.shape[0]):
                rows = tm // km_ref.shape[0]
                km_ref[blk, :, sl] = jnp.mean(k[blk * rows:(blk + 1) * rows], axis=0, keepdims=True)

    @pl.when(j == 2)
    def _():
        for h, z in slabs():
            vo_ref[:, h * HEAD_B:(h + 1) * HEAD_B] = z
            vt_ref[h] = z.T.astype(vt_ref.dtype)


def attn_projection(x, ln_w, w_main, cos_t, sin_t, batch, seq):
    m = batch * seq
    blk = min(seq, MOBA_BLOCK)
    tm = min(seq, 512)
    assert seq % tm == 0 and tm % blk == 0
    nt = seq // tm
    sq = pl.Squeezed()
    tab = pl.BlockSpec((tm, HEAD_B), lambda i, j: (i % nt, 0))
    hm = pl.BlockSpec((sq, N_HEADS_B, tm, HEAD_B), lambda i, j: (i // nt, 0, i % nt, 0))
    hm_t = pl.BlockSpec((sq, N_HEADS_B, HEAD_B, tm), lambda i, j: (i // nt, 0, 0, i % nt))
    rows = pl.BlockSpec((tm, D_B), lambda i, j: (i, 0))
    t_shape = jax.ShapeDtypeStruct((batch, N_HEADS_B, HEAD_B, seq), BF16)
    return pl.pallas_call(
        _attn_proj_kernel,
        grid=(m // tm, 3),
        in_specs=[
            pl.BlockSpec((tm, D_MODEL), lambda i, j: (i, 0)),
            pl.BlockSpec((1, D_MODEL), lambda i, j: (0, 0)),
            pl.BlockSpec((D_MODEL, D_B), lambda i, j: (0, COL_Q // D_B + j)),
            tab, tab,
        ],
        out_specs=[hm_t, hm, hm_t, rows, rows,
                   pl.BlockSpec((tm // blk, 1, D_B), lambda i, j: (i, 0, 0))],
        out_shape=[t_shape, jax.ShapeDtypeStruct((batch, N_HEADS_B, seq, HEAD_B), BF16), t_shape,
                   jax.ShapeDtypeStruct((m, D_B), F32), jax.ShapeDtypeStruct((m, D_B), F32),
                   jax.ShapeDtypeStruct((m // blk, 1, D_B), F32)],
        scratch_shapes=[pltpu.VMEM((tm, D_MODEL), BF16)],
        compiler_params=_cparams(("arbitrary", "arbitrary")),
        name="attn_projection",
    )(x, ln_w.reshape(1, -1), w_main, cos_t, sin_t)


def _rope_rows_kernel(zq_ref, zk_ref, cos_ref, sin_ref, q_ref, k_ref):
    cos_t = cos_ref[...]
    sin_t = sin_ref[...]
    lane = lax.broadcasted_iota(jnp.int32, cos_t.shape, 1)
    for h in range(N_HEADS_B):
        sl = slice(h * HEAD_B, (h + 1) * HEAD_B)
        q_ref[:, sl] = _rope(zq_ref[:, sl], cos_t, sin_t, lane).astype(BF16).astype(F32)
        k_ref[:, sl] = _rope(zk_ref[:, sl], cos_t, sin_t, lane)


def rope_rows(zq, zk, cos_t, sin_t):
    out = jax.ShapeDtypeStruct(zq.shape, F32)
    return pl.pallas_call(_rope_rows_kernel, out_shape=[out, out], name="rope_rows")(zq, zk, cos_t, sin_t)


MOBA_HEADS_PER_STEP = 2


def _moba_prompt_kernel(q_ref, k_ref, vt_ref, km_ref, zg_ref, o_ref, sel_ref):
    qb = pl.program_id(1)
    nbatch, nhead, tq = q_ref.shape[0], q_ref.shape[1], q_ref.shape[3]
    nb = km_ref.shape[1]
    seqs = [(b, h) for b in range(nbatch) for h in range(nhead)]
    exp2_scale = HEAD_B ** -0.5 * LOG2_E
    qs = [q_ref[b, h] for b, h in seqs]

    blk = lax.broadcasted_iota(jnp.int32, (nb, tq), 0)
    for i, (b, h) in enumerate(seqs):
        km = km_ref[b, :, h * HEAD_B:(h + 1) * HEAD_B]
        gate = sum(jnp.dot(piece, qs[i], preferred_element_type=F32)
                   for piece in _split3(km))
        gate = jnp.where(blk < qb, gate, -jnp.inf)
        for n in range(nb):
            g_n = gate[n:n + 1, :]
            tie = jnp.where(blk < n, 1.0, 0.0)
            beats = jnp.where(gate > g_n, 1.0, jnp.where(gate == g_n, tie, 0.0))
            cnt = jnp.sum(beats, axis=0, keepdims=True)
            sel_ref[i, n:n + 1, :] = jnp.where(cnt < MOBA_TOPK, 1.0, 0.0)

    def scores(i, n):
        b, h = seqs[i]
        start = pl.multiple_of(n * tq, tq)
        return jnp.dot(k_ref[b, h, pl.ds(start, tq), :], qs[i], preferred_element_type=F32)

    def softmax_step(s, m, l):
        m_new = jnp.maximum(m, jnp.max(s, axis=0, keepdims=True))
        alpha = jnp.exp2((m - m_new) * exp2_scale)
        p = jnp.exp2((s - m_new) * exp2_scale)
        return m_new, alpha, alpha * l + jnp.sum(p, axis=0, keepdims=True), p.astype(BF16)

    def pv(i, n, p):
        b, h = seqs[i]
        start = pl.multiple_of(n * tq, tq)
        return jnp.dot(vt_ref[b, h, :, pl.ds(start, tq)], p, preferred_element_type=F32)

    def step(n, masked, carry):
        sm = [softmax_step(masked[i], carry[i][0], carry[i][1]) for i in range(len(seqs))]
        acc = [sm[i][1] * carry[i][2] + pv(i, n, sm[i][3]) for i in range(len(seqs))]
        return [(sm[i][0], sm[i][2], acc[i]) for i in range(len(seqs))]

    def body(n, carry):
        s = [scores(i, n) for i in range(len(seqs))]
        masked = [jnp.where(sel_ref[i, pl.ds(n, 1), :] > 0.0, s[i], NEG)
                  for i in range(len(seqs))]
        return tuple(step(n, masked, carry))

    init = tuple((jnp.full((1, tq), NEG, F32), jnp.zeros((1, tq), F32),
                  jnp.zeros((HEAD_B, tq), F32)) for i in range(len(seqs)))
    carry = lax.fori_loop(0, qb, body, init)
    ki = lax.broadcasted_iota(jnp.int32, (tq, tq), 0)
    qi = lax.broadcasted_iota(jnp.int32, (tq, tq), 1)
    masked = [jnp.where(ki <= qi, scores(i, qb), NEG) for i in range(len(seqs))]
    final = step(qb, masked, list(carry))
    for i, (b, h) in enumerate(seqs):
        _, l, acc = final[i]
        sl = slice(h * HEAD_B, (h + 1) * HEAD_B)
        o_ref[b, :, sl] = ((acc / l).T * _silu(zg_ref[b, :, sl])).astype(o_ref.dtype)


def moba_prompt(qt_hm, k_hm, vt_hm, kmean, z_main):
    batch, _, seq, _ = k_hm.shape
    tq = min(seq, MOBA_BLOCK)
    nb = seq // tq
    hps = MOBA_HEADS_PER_STEP
    w = hps * HEAD_B
    return pl.pallas_call(
        _moba_prompt_kernel,
        grid=(N_HEADS_B // hps, nb),
        in_specs=[
            pl.BlockSpec((batch, hps, HEAD_B, tq), lambda h, i: (0, h, 0, i)),
            pl.BlockSpec((batch, hps, seq, HEAD_B), lambda h, i: (0, h, 0, 0)),
            pl.BlockSpec((batch, hps, HEAD_B, seq), lambda h, i: (0, h, 0, 0)),
            pl.BlockSpec((batch, nb, w), lambda h, i: (0, 0, h)),
            pl.BlockSpec((batch, tq, w), lambda h, i: (0, i, COL_GB // w + h)),
        ],
        out_specs=pl.BlockSpec((batch, tq, w), lambda h, i: (0, i, h)),
        out_shape=jax.ShapeDtypeStruct((batch, seq, D_B), BF16),
        scratch_shapes=[pltpu.VMEM((batch * hps, nb, tq), F32)],
        compiler_params=_cparams(("arbitrary", "arbitrary")),
        name="moba_prompt",
    )(qt_hm, k_hm, vt_hm, kmean, z_main)


KMEAN_PAGES = 8


def _cache_kmean_kernel(pt_ref, *refs, pages_per_block):
    del pt_ref
    for task in _page_block_mean_tasks(refs[:-1], refs[-1], pages_per_block):
        task()


def cache_kmean(cache_k, page_table, layer):
    n, n_pages = page_table.shape
    page = cache_k.shape[2]
    ppb = MOBA_BLOCK // page
    pps = min(KMEAN_PAGES, n_pages)
    assert n_pages % pps == 0 and pps % ppb == 0
    sq = pl.Squeezed()
    page_spec = lambda i: pl.BlockSpec(
        (sq, sq, page, N_HEADS_B, HEAD_B), lambda b, s, pt, i=i: (layer, pt[b, s * pps + i], 0, 0, 0))
    return pl.pallas_call(
        functools.partial(_cache_kmean_kernel, pages_per_block=ppb),
        grid_spec=pltpu.PrefetchScalarGridSpec(
            num_scalar_prefetch=1,
            grid=(n, n_pages // pps),
            in_specs=[page_spec(i) for i in range(pps)],
            out_specs=pl.BlockSpec((sq, pps // ppb, N_HEADS_B, HEAD_B), lambda b, s, pt: (b, s, 0, 0)),
        ),
        out_shape=jax.ShapeDtypeStruct((n, n_pages // ppb, N_HEADS_B, HEAD_B), F32),
        compiler_params=_cparams(("arbitrary", "arbitrary")),
        name="cache_kmean",
    )(page_table, *([cache_k] * pps))


def _sample_select_kernel(q_ref, km_ref, sel_ref):
    nb = km_ref.shape[0]
    ri = lax.broadcasted_iota(jnp.int32, (nb, nb), 0)
    ci = lax.broadcasted_iota(jnp.int32, (nb, nb), 1)
    lane = lax.broadcasted_iota(jnp.int32, (1, 128), 1)
    blk_row = lax.broadcasted_iota(jnp.int32, (1, nb), 1).astype(F32)
    for h in range(N_HEADS_B):
        km = km_ref[:, h, :]
        g_col = jnp.sum(km * q_ref[h:h + 1, :], axis=1, keepdims=True)
        g_row = jnp.sum(jnp.where(ri == ci, g_col, 0.0), axis=0, keepdims=True)
        beats = (g_col > g_row) | ((g_col == g_row) & (ri < ci))
        rank = jnp.sum(jnp.where(beats, 1.0, 0.0), axis=0, keepdims=True)
        out = jnp.zeros((1, 128), F32)
        for r in range(MOBA_TOPK):
            idx = jnp.sum(jnp.where(rank == float(r), blk_row, 0.0), axis=1, keepdims=True)
            out = jnp.where(lane == r, idx, out)
        sel_ref[h:h + 1, :] = out.astype(jnp.int32)


def sample_select(q_s, kmean_s):
    n, nb = kmean_s.shape[:2]
    sq = pl.Squeezed()
    return pl.pallas_call(
        _sample_select_kernel,
        grid=(n,),
        in_specs=[pl.BlockSpec((sq, N_HEADS_B, HEAD_B), lambda b: (b, 0, 0)),
                  pl.BlockSpec((sq, nb, N_HEADS_B, HEAD_B), lambda b: (b, 0, 0, 0))],
        out_specs=pl.BlockSpec((sq, N_HEADS_B, 128), lambda b: (b, 0, 0)),
        out_shape=jax.ShapeDtypeStruct((n, N_HEADS_B, 128), jnp.int32),
        compiler_params=_cparams(("arbitrary",)),
        name="sample_select",
    )(q_s, kmean_s)


def _sample_attn_kernel(sel_ref, pt_ref, q_ref, kn_ref, vn_ref, zg_ref, ck_ref, cv_ref, o_ref,
                        kbuf, vbuf, sem, *, layer, page, ppb):
    b = pl.program_id(0)
    nb_steps = pl.num_programs(0)
    n_sel = MOBA_TOPK * ppb
    scale = HEAD_B ** -0.5

    def copies(bb, slot):
        out = []
        for h in range(N_HEADS_B):
            for j in range(n_sel):
                pg = pt_ref[bb, sel_ref[bb, h, j // ppb] * ppb + j % ppb]
                dst = pl.ds(j * page, page)
                out.append(pltpu.make_async_copy(ck_ref.at[layer, pg, :, h, :],
                                                 kbuf.at[slot, h, dst, :], sem.at[slot, 0]))
                out.append(pltpu.make_async_copy(cv_ref.at[layer, pg, :, h, :],
                                                 vbuf.at[slot, h, dst, :], sem.at[slot, 1]))
        return out

    slot = b % 2

    @pl.when(b == 0)
    def _():
        for c in copies(0, 0):
            c.start()

    @pl.when(b + 1 < nb_steps)
    def _():
        for c in copies(b + 1, 1 - slot):
            c.start()

    for c in copies(b, slot):
        c.wait()

    for h in range(N_HEADS_B):
        q = q_ref[h]
        k = kbuf[slot, h]
        v = vbuf[slot, h]
        s = jnp.sum(k * q, axis=1, keepdims=True) * scale
        s_own = jnp.sum(kn_ref[h] * q, axis=1, keepdims=True) * scale
        m = jnp.maximum(jnp.max(s, axis=0, keepdims=True), s_own)
        p = jnp.exp(s - m)
        p_own = jnp.exp(s_own - m)
        l = jnp.sum(p, axis=0, keepdims=True) + p_own
        acc = jnp.sum(p * v, axis=0, keepdims=True) + p_own * vn_ref[h]
        o_ref[h] = (acc / l) * _silu(zg_ref[h])


def sample_attn(sel, page_table, q_s, k_new, v_new, zgb, cache_k, cache_v, layer):
    n = q_s.shape[0]
    page = cache_k.shape[2]
    ppb = MOBA_BLOCK // page
    rows = MOBA_TOPK * ppb * page
    sq = pl.Squeezed()
    vec = pl.BlockSpec((sq, N_HEADS_B, 1, HEAD_B), lambda b, sel, pt: (b, 0, 0, 0))
    hbm = pl.BlockSpec(memory_space=pl.ANY)
    return pl.pallas_call(
        functools.partial(_sample_attn_kernel, layer=layer, page=page, ppb=ppb),
        grid_spec=pltpu.PrefetchScalarGridSpec(
            num_scalar_prefetch=2,
            grid=(n,),
            in_specs=[vec, vec, vec, vec, hbm, hbm],
            out_specs=vec,
            scratch_shapes=[pltpu.VMEM((2, N_HEADS_B, rows, HEAD_B), F32),
                            pltpu.VMEM((2, N_HEADS_B, rows, HEAD_B), F32),
                            pltpu.SemaphoreType.DMA((2, 2))],
        ),
        out_shape=jax.ShapeDtypeStruct((n, N_HEADS_B, 1, HEAD_B), F32),
        compiler_params=_cparams(("arbitrary",)),
        name="sample_attn",
    )(sel, page_table, q_s, k_new, v_new, zgb, cache_k, cache_v)


def _merge_out_kernel(oa_ref, ob_ref, pa_ref, pb_ref, zga_ref, zgm_ref, wo_ref, x_ref, fw_ref, y_ref):
    ya = jnp.dot(oa_ref[...], pa_ref[...], preferred_element_type=F32)
    yb = jnp.dot(ob_ref[...], pb_ref[...], preferred_element_type=F32)
    merged = (_sigmoid(zga_ref[...]) * ya + _sigmoid(zgm_ref[...]) * yb).astype(BF16)
    h = x_ref[...] + jnp.dot(merged, wo_ref[...], preferred_element_type=F32)
    ms = jnp.mean(h * h, axis=-1, keepdims=True)
    y_ref[...] = (h * lax.rsqrt(ms + NORM_EPS)) * fw_ref[...]


def merge_and_output(o_a, o_b, p_a, p_b, z_main, w_o, x, final_w):
    m = x.shape[0]
    tm = min(m, 256)
    resident = pl.Buffered(1)
    return pl.pallas_call(
        _merge_out_kernel,
        grid=(m // tm,),
        in_specs=[
            pl.BlockSpec((tm, D_A), lambda i: (i, 0)),
            pl.BlockSpec((tm, D_B), lambda i: (i, 0)),
            pl.BlockSpec((D_A, D_MODEL), lambda i: (0, 0), pipeline_mode=resident),
            pl.BlockSpec((D_B, D_MODEL), lambda i: (0, 0), pipeline_mode=resident),
            pl.BlockSpec((tm, D_MODEL), lambda i: (i, COL_GA // D_MODEL)),
            pl.BlockSpec((tm, D_MODEL), lambda i: (i, COL_GM // D_MODEL)),
            pl.BlockSpec((D_MODEL, D_MODEL), lambda i: (0, 0), pipeline_mode=resident),
            pl.BlockSpec((tm, D_MODEL), lambda i: (i, 0)),
            pl.BlockSpec((1, D_MODEL), lambda i: (0, 0)),
        ],
        out_specs=pl.BlockSpec((tm, D_MODEL), lambda i: (i, 0)),
        out_shape=jax.ShapeDtypeStruct((m, D_MODEL), F32),
        compiler_params=_cparams(("arbitrary",)),
        name="merge_and_output",
    )(o_a, o_b, p_a, p_b, z_main, z_main, w_o, x, final_w.reshape(1, -1))


def _rope_tables(pos):
    half = ROT_DIM // 2
    inv = jnp.power(jnp.float32(ROPE_THETA), -jnp.arange(half, dtype=F32) * (2.0 / ROT_DIM))
    ang = pos.astype(F32)[:, None] * inv[None, :]
    cos, sin = jnp.cos(ang), jnp.sin(ang)
    n = pos.shape[0]
    rest = HEAD_B - ROT_DIM
    cos_t = jnp.concatenate([cos, cos, jnp.ones((n, rest), F32)], axis=1)
    sin_t = jnp.concatenate([-sin, sin, jnp.zeros((n, rest), F32)], axis=1)
    return cos_t, sin_t


def _reorder_rw(a):
    r, w_lo, k, v, a_lo, g = jnp.split(
        a, [D_A, D_A + LORA, 2 * D_A + LORA, 3 * D_A + LORA, 3 * D_A + 2 * LORA], axis=-1)
    return jnp.concatenate([r, k, v, g], axis=-1), jnp.concatenate([w_lo, a_lo], axis=-1)


def kernel(x_prompt, x_sample, state_shift, state_wkv, cache_k, cache_v, page_table, ln_w, w_in, mu,
           w0, w2, a0, a2, k_k, k_a, r_k, lnx_w, lnx_b, p_a, p_b, w_o, final_w):
    depth = ln_w.shape[0]
    assert depth == 1, "single-layer trunk"
    B, T, _ = x_prompt.shape
    DB, TS, _ = x_sample.shape
    assert TS == 1
    n_pages = page_table.shape[1]
    page = cache_k.shape[2]
    past = n_pages * page
    assert past % MOBA_BLOCK == 0 and MOBA_BLOCK % page == 0
    l = 0
    rw_cols = 4 * D_A + 2 * LORA

    att0 = rw_cols
    src_cols = [0, D_A + LORA, 2 * D_A + LORA, 3 * D_A + 2 * LORA]
    src_cols += [att0 + 4 * D_B + i * W_TILE for i in range(2 * D_MODEL // W_TILE)]
    src_cols += [att0 + 3 * D_B, att0, att0 + D_B, att0 + 2 * D_B]
    w_main = prepare_projection_weight(w_in, l, src_cols)
    w_wa = jnp.concatenate([w_in[l, :, D_A:D_A + LORA],
                            w_in[l, :, 3 * D_A + LORA:3 * D_A + 2 * LORA]], axis=1)
    mu_rw, mu_wa = _reorder_rw(mu[l][None, :])
    zeros = jnp.zeros((LORA, D_A), F32)
    w2p = jnp.concatenate([w2[l], zeros], axis=0).astype(BF16)
    a2p = jnp.concatenate([zeros, a2[l]], axis=0).astype(BF16)
    row = lambda a: a.reshape(1, -1)
    vecs = (row(w0[l]), row(a0[l]), row(k_k[l]), row(k_a[l]), row(r_k[l]), row(lnx_w[l]), row(lnx_b[l]))
    pa_bf, pb_bf, wo_bf = p_a[l].astype(BF16), p_b[l].astype(BF16), w_o[l].astype(BF16)

    rows = jnp.concatenate([x_prompt[:, -1, :], x_sample[:, 0, :]], axis=0)
    xn_rows = rmsnorm_rows(rows, ln_w[l])
    shift_prompt = xn_rows[:B]
    xn_s = xn_rows[B:]
    xp = x_prompt.reshape(B * T, D_MODEL)
    z_p, zwa_p = input_projection(xp, ln_w[l], w_main, w_wa, normalize=True, n_cols=N_Z)
    z_s2, zwa_s2 = input_projection(jnp.concatenate([xn_s, state_shift[l]], axis=0), ln_w[l],
                                    w_main, w_wa, normalize=False, n_cols=N_MAIN)

    z_p3 = z_p.reshape(B, T, N_Z)
    o_a_p, wkv_p, kmean_s = rwkv_prompt(z_p3, zwa_p.reshape(B, T, 2 * LORA), mu_rw, mu_wa, vecs, w2p, a2p,
                                        cache_k, page_table, l)
    cos_p, sin_p = _rope_tables(jnp.arange(T))
    qt_hm, k_hm, vt_hm, k_rows_p, v_rows_p, kmean_p = attn_projection(xp, ln_w[l], w_main, cos_p, sin_p, B, T)
    nb_p = T // min(T, MOBA_BLOCK)
    o_b_p = moba_prompt(qt_hm, k_hm, vt_hm, kmean_p.reshape(B, nb_p, D_B), z_p3)
    y_prompt = merge_and_output(o_a_p.reshape(B * T, D_A), o_b_p.reshape(B * T, D_B), pa_bf, pb_bf, z_p,
                                wo_bf, xp, final_w).reshape(B, T, D_MODEL)

    z_s, zprev_s = z_s2[:DB], z_s2[DB:]
    r_s, kf_s, v_s, kk_s, b_s, w_s, g_s = rwkv_rows(
        z_s[:, :4 * D_A], zprev_s[:, :4 * D_A], zwa_s2[:DB], zwa_s2[DB:], mu_rw, mu_wa,
        vecs[0], vecs[1], vecs[2], vecs[3], w2p, a2p)
    hs = lambda a: a.reshape(-1, N_HEADS_A, 1, HEAD_A)
    col = lambda a: a.reshape(-1, N_HEADS_A, HEAD_A, 1)
    wkv_s, o_a_s = rwkv_step(state_wkv[l], hs(w_s), hs(kk_s), hs(b_s), hs(kf_s), hs(r_s),
                             col(v_s), col(g_s), hs(r_k[l])[0], col(lnx_w[l])[0], col(lnx_b[l])[0])
    o_a_s = o_a_s.reshape(DB, D_A).astype(BF16)

    cos_s, sin_s = _rope_tables(past + jnp.arange(TS))
    cos_s = jnp.broadcast_to(cos_s, (DB, HEAD_B))
    sin_s = jnp.broadcast_to(sin_s, (DB, HEAD_B))
    q_rows_s, k_rows_s = rope_rows(z_s[:, COL_Q:COL_Q + D_B], z_s[:, COL_KB:COL_KB + D_B], cos_s, sin_s)
    v_rows_s = z_s[:, COL_VB:COL_VB + D_B]
    if kmean_s is None:
        kmean_s = cache_kmean(cache_k, page_table, l)
    sel = sample_select(q_rows_s.reshape(DB, N_HEADS_B, HEAD_B), kmean_s)[:, :, :MOBA_TOPK]
    hv = lambda a: a.reshape(DB, N_HEADS_B, 1, HEAD_B)
    o_b_s = sample_attn(sel, page_table, hv(q_rows_s), hv(k_rows_s), hv(v_rows_s),
                        hv(z_s[:, COL_GB:COL_GB + D_B]), cache_k, cache_v, l)
    o_b_s = o_b_s.reshape(DB, D_B).astype(BF16)
    y_sample = merge_and_output(o_a_s, o_b_s, pa_bf, pb_bf, z_s, wo_bf, x_sample.reshape(DB, D_MODEL),
                                final_w)

    return (y_prompt,
            y_sample.reshape(DB, TS, D_MODEL),
            shift_prompt[None],
            wkv_p.reshape(1, B, N_HEADS_A, HEAD_A, HEAD_A),
            k_rows_p.reshape(1, B, T, N_HEADS_B, HEAD_B),
            v_rows_p.reshape(1, B, T, N_HEADS_B, HEAD_B),
            xn_s[None],
            wkv_s[None],
            k_rows_s.reshape(1, DB, TS, N_HEADS_B, HEAD_B),
            v_rows_s.reshape(1, DB, TS, N_HEADS_B, HEAD_B))
```

```python
import functools

import jax
import jax.numpy as jnp
from jax import lax
from jax.experimental import pallas as pl
from jax.experimental.pallas import tpu as pltpu

F32 = jnp.float32
BF16 = jnp.bfloat16

D_MODEL = 2048
D_A = D_MODEL // 2
HEAD_A = 64
N_HEADS_A = D_A // HEAD_A
LORA = 64
D_B = D_MODEL // 2
HEAD_B = 128
N_HEADS_B = D_B // HEAD_B
ROT_DIM = HEAD_B // 4
ROPE_THETA = 500000.0
MOBA_BLOCK = 256
MOBA_TOPK = 3
NORM_EPS = 1e-6
GN_EPS = 64e-5
NEG = -1e30
LOG2_E = 1.4426950408889634

COL_R, COL_K, COL_V, COL_G = 0, D_A, 2 * D_A, 3 * D_A
COL_GA = 4 * D_A
COL_GM = COL_GA + D_MODEL
COL_GB = COL_GM + D_MODEL
N_Z = COL_GB + D_B
COL_Q, COL_KB, COL_VB = N_Z, N_Z + D_B, N_Z + 2 * D_B
N_MAIN = COL_VB + D_B

GROUP = 256
HEADS_PER_GROUP = GROUP // HEAD_A
CHUNK = 64
assert CHUNK == HEAD_A
VMEM_LIMIT = 56 * 1024 * 1024


def _cparams(sem):
    return pltpu.CompilerParams(dimension_semantics=sem, vmem_limit_bytes=VMEM_LIMIT)


def _bdot(a, b):
    return jnp.dot(a.astype(BF16), b.astype(BF16), preferred_element_type=F32)


def _bdot_nt(a, b):
    return lax.dot_general(a.astype(BF16), b.astype(BF16), (((1,), (1,)), ((), ())),
                           preferred_element_type=F32)


def _bdot_tn(a, b):
    return lax.dot_general(a.astype(BF16), b.astype(BF16), (((0,), (0,)), ((), ())),
                           preferred_element_type=F32)


def _split3(x):
    hi = x.astype(BF16)
    r1 = x - hi.astype(F32)
    mid = r1.astype(BF16)
    lo = (r1 - mid.astype(F32)).astype(BF16)
    return hi, mid, lo


def _sigmoid(x):
    return 1.0 / (1.0 + jnp.exp(-x))


def _silu(x):
    return x * _sigmoid(x)


def _softplus(x):
    return jnp.maximum(x, 0.0) + jnp.log(1.0 + jnp.exp(-jnp.abs(x)))


def _rmsnorm_rows_kernel(x_ref, w_ref, o_ref):
    x = x_ref[...]
    ms = jnp.mean(x * x, axis=-1, keepdims=True)
    o_ref[...] = (x * lax.rsqrt(ms + NORM_EPS)) * w_ref[...]


def rmsnorm_rows(x, w):
    return pl.pallas_call(
        _rmsnorm_rows_kernel,
        out_shape=jax.ShapeDtypeStruct(x.shape, F32),
        name="rmsnorm_rows",
    )(x, w.reshape(1, -1))


W_TILE = 1024
LANES = 128


def _w_prep_kernel(w_hbm, o_hbm, in_buf, out_buf, in_sem, out_sem, *, layer, starts):
    n = len(starts)
    rows = in_buf.shape[1]

    def in_copy(j, slot):
        off = starts[j] % LANES
        width = W_TILE + (LANES if off else 0)
        return pltpu.make_async_copy(w_hbm.at[layer, :, pl.ds(starts[j] - off, width)],
                                     in_buf.at[slot, :, pl.ds(0, width)], in_sem.at[slot])

    def out_copy(j, slot):
        return pltpu.make_async_copy(out_buf.at[slot], o_hbm.at[:, pl.ds(j * W_TILE, W_TILE)],
                                     out_sem.at[slot])

    in_copy(0, 0).start()
    for j in range(n):
        slot = j % 2
        if j + 1 < n:
            in_copy(j + 1, 1 - slot).start()
        in_copy(j, slot).wait()
        if j >= 2:
            out_copy(j - 2, slot).wait()
        off = starts[j] % LANES
        rc = 256

        def chunk(c, _, slot=slot, off=off):
            r0 = pl.multiple_of(c * rc, rc)
            out_buf[slot, pl.ds(r0, rc), :] = in_buf[slot, pl.ds(r0, rc), off:off + W_TILE].astype(BF16)
            return 0

        lax.fori_loop(0, rows // rc, chunk, 0)
        out_copy(j, slot).start()
    for j in range(max(n - 2, 0), n):
        out_copy(j, j % 2).wait()


def prepare_projection_weight(w_in, layer, starts):
    rows = w_in.shape[1]
    assert rows % 256 == 0
    hbm = pl.BlockSpec(memory_space=pl.ANY)
    return pl.pallas_call(
        functools.partial(_w_prep_kernel, layer=layer, starts=tuple(starts)),
        in_specs=[hbm],
        out_specs=hbm,
        out_shape=jax.ShapeDtypeStruct((rows, len(starts) * W_TILE), BF16),
        scratch_shapes=[pltpu.VMEM((2, rows, W_TILE + LANES), F32),
                        pltpu.VMEM((2, rows, W_TILE), BF16),
                        pltpu.SemaphoreType.DMA((2,)), pltpu.SemaphoreType.DMA((2,))],
        compiler_params=pltpu.CompilerParams(vmem_limit_bytes=VMEM_LIMIT),
        name="prepare_projection_weight",
    )(w_in)


def _proj_kernel(x_ref, lnw_ref, w_ref, wwa_ref, z_ref, zwa_ref, xn_ref, *, normalize):
    @pl.when(pl.program_id(1) == 0)
    def _():
        x = x_ref[...]
        if normalize:
            ms = jnp.mean(x * x, axis=-1, keepdims=True)
            x = (x * lax.rsqrt(ms + NORM_EPS)) * lnw_ref[...]
        xn_ref[...] = x.astype(BF16)
        zwa_ref[...] = jnp.dot(xn_ref[...], wwa_ref[...].astype(BF16), preferred_element_type=F32)

    z_ref[...] = jnp.dot(xn_ref[...], w_ref[...], preferred_element_type=F32)


def input_projection(x, ln_w, w_main, w_wa, *, normalize, n_cols):
    m = x.shape[0]
    tm = min(m, 1024)
    if m < 1024:
        tn = 2048
    else:
        tn = 1536 if n_cols % 1536 == 0 else 1024
    assert m % tm == 0 and n_cols % tn == 0
    return pl.pallas_call(
        functools.partial(_proj_kernel, normalize=normalize),
        grid=(m // tm, n_cols // tn),
        in_specs=[
            pl.BlockSpec((tm, D_MODEL), lambda i, j: (i, 0)),
            pl.BlockSpec((1, D_MODEL), lambda i, j: (0, 0)),
            pl.BlockSpec((D_MODEL, tn), lambda i, j: (0, j)),
            pl.BlockSpec((D_MODEL, 2 * LORA), lambda i, j: (0, 0)),
        ],
        out_specs=[
            pl.BlockSpec((tm, tn), lambda i, j: (i, j)),
            pl.BlockSpec((tm, 2 * LORA), lambda i, j: (i, 0)),
        ],
        out_shape=[
            jax.ShapeDtypeStruct((m, n_cols), F32),
            jax.ShapeDtypeStruct((m, 2 * LORA), F32),
        ],
        scratch_shapes=[pltpu.VMEM((tm, D_MODEL), BF16)],
        compiler_params=_cparams(("arbitrary", "arbitrary")),
        name="input_projection",
    )(x, ln_w.reshape(1, -1), w_main, w_wa)


def _head_mask(n):
    r = lax.broadcasted_iota(jnp.int32, (n, n), 0) // HEAD_A
    c = lax.broadcasted_iota(jnp.int32, (n, n), 1) // HEAD_A
    return r == c


def _segsum(x, ones_bd):
    return jnp.dot(x.astype(BF16), ones_bd, preferred_element_type=F32)


def _rwkv_prep(zm_r, zm_k, zm_v, zm_wa, w0, a0, k_k, k_a, w2p, a2p, ones_bd):
    lane = lax.broadcasted_iota(jnp.int32, zm_wa.shape, 1)
    lora_in = jnp.where(lane < LORA, jnp.tanh(zm_wa), zm_wa)
    ww = _bdot(lora_in, w2p)
    aa = _bdot(lora_in, a2p)
    w_log = -_softplus(-(w0 + ww)) - 0.5
    logw = -jnp.exp(w_log)
    a = _sigmoid(a0 + aa)
    kk = zm_k * k_k
    ss = _segsum(kk * kk, ones_bd)
    kk = kk / jnp.maximum(jnp.sqrt(ss), 1e-12)
    kf = zm_k * (1.0 + (a - 1.0) * k_a)
    return zm_r, kf, zm_v, kk, kk * a, logw


def _rwkv_post(y, r, kf, v, zm_g, r_k, lnx_w, lnx_b, ones_bd):
    inv_n = 1.0 / HEAD_A
    mean = _segsum(y, ones_bd) * inv_n
    d = y - mean
    var = _segsum(d * d, ones_bd) * inv_n
    yn = d * lax.rsqrt(var + GN_EPS) * lnx_w + lnx_b
    bonus = _segsum(r * kf * r_k, ones_bd) * v
    return (yn + bonus) * _silu(zm_g)


def _stack_heads(x, lane_head):
    return jnp.concatenate(
        [jnp.where(lane_head == h, x, 0.0) for h in range(HEADS_PER_GROUP)], axis=0)


def _chunk_precompute(insts, consts, side_tasks=()):
    strict, incl, eye, lane_head, bd_mask = consts
    c = CHUNK
    n = len(insts)
    side_tasks = list(side_tasks)

    def run_side(k=1):
        for _ in range(k):
            if side_tasks:
                side_tasks.pop(0)()

    def bd(p):
        return jnp.where(bd_mask, jnp.concatenate([p] * HEADS_PER_GROUP, axis=0), 0.0).astype(BF16)

    lhs, wt, v_bd, bdkd, p_end = [], [], [], [], []
    for r, kf, v, kk, b, logw, cl in insts:
        cl_last = cl[c - 1:c, :]
        e_neg = jnp.exp(-cl)
        e_end = jnp.exp(cl_last - cl)
        lhs.append(jnp.concatenate([kk * jnp.exp(cl - logw), r * jnp.exp(cl)], axis=0).astype(BF16))
        wt.append(jnp.concatenate([_stack_heads(b * e_neg, lane_head),
                                   _stack_heads(kf * e_neg, lane_head)], axis=0).astype(BF16))
        v_bd.append(_stack_heads(v, lane_head).astype(BF16))
        bdkd.append(jnp.concatenate([b * e_end, kf * e_end], axis=0).astype(BF16))
        p_end.append(jnp.exp(cl_last))
    att = [_bdot_nt(lhs[i], wt[i]) for i in range(n)]
    run_side()
    a_ab = [jnp.where(strict, att[i][:c, :4 * c], 0.0) for i in range(n)]
    a_ak = [jnp.where(strict, att[i][:c, 4 * c:], 0.0) for i in range(n)]
    m_cat = [jnp.concatenate([jnp.where(incl, att[i][c:, :4 * c], 0.0),
                              jnp.where(incl, att[i][c:, 4 * c:], 0.0)], axis=1).astype(BF16)
             for i in range(n)]
    av = [_bdot(a_ak[i], v_bd[i]) for i in range(n)]
    run_side()

    x = [eye - a_ab[i] for i in range(n)]
    p = [_bdot(a_ab[i], bd(a_ab[i])) for i in range(n)]
    run_side()
    for _ in range(4):
        px = [_bdot(jnp.concatenate([p[i], x[i]], axis=0), bd(p[i])) for i in range(n)]
        p = [px[i][:c] for i in range(n)]
        x = [x[i] + px[i][c:] for i in range(n)]
        run_side()
    x = [(x[i] + _bdot(x[i], bd(p[i]))).astype(BF16) for i in range(n)]
    run_side(len(side_tasks))
    return [dict(lhs=lhs[i], av=av[i], t=x[i], m_cat=m_cat[i], v_bd=v_bd[i], v=insts[i][2],
                 bdkd=bdkd[i], p_end=p_end[i]) for i in range(n)]


def _chunk_state_step(states, pres, consts):
    lane_head, bd_mask = consts[3], consts[4]
    c = CHUNK
    n = len(states)
    sh = [_bdot_nt(pres[i]["lhs"], states[i]) for i in range(n)]
    rhs = [sh[i][:c] + pres[i]["av"] for i in range(n)]
    u = [-_bdot(pres[i]["t"], _stack_heads(rhs[i], lane_head)) for i in range(n)]
    y = [sh[i][c:] + _bdot(pres[i]["m_cat"],
                           jnp.concatenate([_stack_heads(u[i], lane_head).astype(BF16),
                                            pres[i]["v_bd"]], axis=0)) for i in range(n)]
    upd = [_bdot_tn(jnp.concatenate([u[i], pres[i]["v"]], axis=0), pres[i]["bdkd"]) for i in range(n)]
    new = [states[i] * pres[i]["p_end"] + jnp.where(bd_mask, upd[i], 0.0) for i in range(n)]
    return y, new


def _page_block_mean_tasks(page_refs, km_ref, pages_per_block):
    def task(j):
        def run():
            blk = jnp.sum(page_refs[j * pages_per_block][...], axis=0)
            for i in range(1, pages_per_block):
                blk = blk + jnp.sum(page_refs[j * pages_per_block + i][...], axis=0)
            km_ref[j] = blk * (1.0 / MOBA_BLOCK)
        return run
    return [task(j) for j in range(len(page_refs) // pages_per_block)]


def _rwkv_prompt_kernel(*refs, n_pages, pages_per_block):
    if n_pages:
        refs = refs[1:]
    (zr_ref, zk_ref, zv_ref, zg_ref, zwa_ref, mur_ref, muk_ref, muv_ref, mug_ref, muwa_ref,
     w0_ref, a0_ref, kk_ref, ka_ref, rk_ref, lnw_ref, lnb_ref, w2_ref, a2_ref) = refs[:19]
    page_refs = refs[19:19 + n_pages]
    refs = refs[19 + n_pages:]
    o_ref, s_out_ref = refs[:2]
    side_tasks = []
    if n_pages:
        side_tasks = _page_block_mean_tasks(page_refs, refs[2], pages_per_block)
        refs = refs[1:]
    s_ref, pr_ref, pk_ref, pv_ref, pg_ref, pwa_ref = refs[2:]
    t = pl.program_id(1)
    nseq, tr = zr_ref.shape[0], zr_ref.shape[1]

    @pl.when(t == 0)
    def _():
        for ref in (s_ref, pr_ref, pk_ref, pv_ref, pg_ref, pwa_ref):
            ref[...] = jnp.zeros_like(ref)

    def shifted(z_ref, prev_ref, mu_ref, i):
        z = z_ref[i]
        row = lax.broadcasted_iota(jnp.int32, z.shape, 0)
        prev = jnp.where(row == 0, prev_ref[i], pltpu.roll(z, 1, 0))
        prev_ref[i] = z[tr - 1:tr, :]
        return z + (prev - z) * mu_ref[...]

    bd_mask = _head_mask(GROUP)
    ones_bd = jnp.where(bd_mask, 1.0, 0.0).astype(BF16)
    ri = lax.broadcasted_iota(jnp.int32, (tr, tr), 0)
    ci = lax.broadcasted_iota(jnp.int32, (tr, tr), 1)
    tri = jnp.where((ri // CHUNK == ci // CHUNK) & (ci <= ri), 1.0, 0.0).astype(BF16)
    rc = lax.broadcasted_iota(jnp.int32, (CHUNK, GROUP), 0)
    lc = lax.broadcasted_iota(jnp.int32, (CHUNK, GROUP), 1)
    sc = lc % CHUNK
    consts = (sc < rc, sc <= rc, jnp.where(sc == rc, 1.0, 0.0), lc // HEAD_A, bd_mask)
    last = t == pl.num_programs(1) - 1

    n_ch = tr // CHUNK
    prepped, insts = [], []
    for i in range(nseq):
        zm_r = shifted(zr_ref, pr_ref, mur_ref, i)
        zm_k = shifted(zk_ref, pk_ref, muk_ref, i)
        zm_v = shifted(zv_ref, pv_ref, muv_ref, i)
        zm_g = shifted(zg_ref, pg_ref, mug_ref, i)
        zm_wa = shifted(zwa_ref, pwa_ref, muwa_ref, i)
        r, kf, v, kk, b, logw = _rwkv_prep(zm_r, zm_k, zm_v, zm_wa, w0_ref[...], a0_ref[...],
                                           kk_ref[...], ka_ref[...], w2_ref[...], a2_ref[...], ones_bd)
        cl = sum(jnp.dot(tri, piece, preferred_element_type=F32) for piece in _split3(logw))
        prepped.append((r, kf, v, zm_g))
        for ch in range(n_ch):
            sl = slice(ch * CHUNK, (ch + 1) * CHUNK)
            insts.append((r[sl], kf[sl], v[sl], kk[sl], b[sl], logw[sl], cl[sl]))
    pres = _chunk_precompute(insts, consts, side_tasks)

    states = [s_ref[i] for i in range(nseq)]
    ys = [[] for _ in range(nseq)]
    for ch in range(n_ch):
        y, states = _chunk_state_step(states, [pres[i * n_ch + ch] for i in range(nseq)], consts)
        for i in range(nseq):
            ys[i].append(y[i])
    for i in range(nseq):
        s_ref[i] = states[i]
        r, kf, v, zm_g = prepped[i]
        o_ref[i] = _rwkv_post(jnp.concatenate(ys[i], axis=0), r, kf, v, zm_g, rk_ref[...],
                              lnw_ref[...], lnb_ref[...], ones_bd).astype(o_ref.dtype)

    @pl.when(last)
    def _():
        for i in range(nseq):
            for h in range(HEADS_PER_GROUP):
                s_out_ref[i, h] = s_ref[i, h * HEAD_A:(h + 1) * HEAD_A, h * HEAD_A:(h + 1) * HEAD_A]


def rwkv_prompt(z_main, z_wa, mu_main, mu_wa, vecs, w2p, a2p, cache_k=None, page_table=None, layer=0):
    w0, a0, k_k, k_a, r_k, lnx_w, lnx_b = vecs
    batch, seq, _ = z_main.shape
    tr = min(seq, 256)
    assert seq % tr == 0 and tr % CHUNK == 0
    nt = seq // tr
    ng = D_A // GROUP

    n_pages = ppb = 0
    if cache_k is not None:
        page = cache_k.shape[2]
        ppb = MOBA_BLOCK // page
        total = page_table.size
        if total % (ng * nt) == 0 and (total // (ng * nt)) % ppb == 0:
            n_pages = total // (ng * nt)

    def zspec(col):
        return pl.BlockSpec((batch, tr, GROUP), lambda g, t, *_, c=col // GROUP: (0, t, c + g))

    def vspec(col=0):
        return pl.BlockSpec((1, GROUP), lambda g, t, *_, c=col // GROUP: (0, c + g))

    in_specs = [
        zspec(COL_R), zspec(COL_K), zspec(COL_V), zspec(COL_G),
        pl.BlockSpec((batch, tr, 2 * LORA), lambda g, t, *_: (0, t, 0)),
        vspec(COL_R), vspec(COL_K), vspec(COL_V), vspec(COL_G),
        pl.BlockSpec((1, 2 * LORA), lambda g, t, *_: (0, 0)),
        vspec(), vspec(), vspec(), vspec(), vspec(), vspec(), vspec(),
        pl.BlockSpec((2 * LORA, GROUP), lambda g, t, *_: (0, g)),
        pl.BlockSpec((2 * LORA, GROUP), lambda g, t, *_: (0, g)),
    ]
    out_specs = [
        pl.BlockSpec((batch, tr, GROUP), lambda g, t, *_: (0, t, g)),
        pl.BlockSpec((batch, HEADS_PER_GROUP, HEAD_A, HEAD_A), lambda g, t, *_: (0, g, 0, 0)),
    ]
    out_shape = [
        jax.ShapeDtypeStruct((batch, seq, D_A), BF16),
        jax.ShapeDtypeStruct((batch, N_HEADS_A, HEAD_A, HEAD_A), F32),
    ]
    operands = [z_main, z_main, z_main, z_main, z_wa,
                mu_main, mu_main, mu_main, mu_main, mu_wa,
                w0, a0, k_k, k_a, r_k, lnx_w, lnx_b, w2p, a2p]
    if n_pages:
        sq = pl.Squeezed()
        for i in range(n_pages):
            in_specs.append(pl.BlockSpec(
                (sq, sq, page, N_HEADS_B, HEAD_B),
                lambda g, t, pt, i=i: (layer, pt[(g * nt + t) * n_pages + i], 0, 0, 0)))
        operands += [cache_k] * n_pages
        out_specs.append(pl.BlockSpec((n_pages // ppb, N_HEADS_B, HEAD_B),
                                      lambda g, t, pt: (g * nt + t, 0, 0)))
        out_shape.append(jax.ShapeDtypeStruct((page_table.size // ppb, N_HEADS_B, HEAD_B), F32))
    row = lambda n: pltpu.VMEM((batch, 1, n), F32)
    scratch = [pltpu.VMEM((batch, GROUP, GROUP), F32),
               row(GROUP), row(GROUP), row(GROUP), row(GROUP), row(2 * LORA)]
    kern = functools.partial(_rwkv_prompt_kernel, n_pages=n_pages, pages_per_block=ppb)
    params = _cparams(("arbitrary", "arbitrary"))
    if not n_pages:
        o_a, wkv = pl.pallas_call(kern, grid=(ng, nt), in_specs=in_specs, out_specs=out_specs,
                                  out_shape=out_shape, scratch_shapes=scratch,
                                  compiler_params=params, name="rwkv_prompt")(*operands)
        return o_a, wkv, None
    o_a, wkv, kmean = pl.pallas_call(
        kern,
        grid_spec=pltpu.PrefetchScalarGridSpec(
            num_scalar_prefetch=1, grid=(ng, nt), in_specs=in_specs, out_specs=out_specs,
            scratch_shapes=scratch),
        out_shape=out_shape, compiler_params=params, name="rwkv_prompt",
    )(page_table.reshape(-1), *operands)
    n = page_table.shape[0]
    return o_a, wkv, kmean.reshape(n, -1, N_HEADS_B, HEAD_B)


def _rwkv_rows_kernel(z_ref, zp_ref, zwa_ref, zpwa_ref, mu_ref, muwa_ref,
                      w0_ref, a0_ref, kk_ref, ka_ref, w2_ref, a2_ref,
                      r_ref, kf_ref, v_ref, kkn_ref, b_ref, w_ref, g_ref):
    def shifted(z, zp, mu):
        return z + (zp - z) * mu

    mu = mu_ref[...]
    z = z_ref[...]
    zp = zp_ref[...]
    zm = [shifted(z[:, c:c + D_A], zp[:, c:c + D_A], mu[:, c:c + D_A])
          for c in (COL_R, COL_K, COL_V, COL_G)]
    zm_wa = shifted(zwa_ref[...], zpwa_ref[...], muwa_ref[...])
    ones_bd = jnp.where(_head_mask(D_A), 1.0, 0.0).astype(BF16)
    r, kf, v, kk, b, logw = _rwkv_prep(zm[0], zm[1], zm[2], zm_wa, w0_ref[...], a0_ref[...],
                                       kk_ref[...], ka_ref[...], w2_ref[...], a2_ref[...], ones_bd)
    r_ref[...] = r
    kf_ref[...] = kf
    v_ref[...] = v
    kkn_ref[...] = kk
    b_ref[...] = b
    w_ref[...] = jnp.exp(logw)
    g_ref[...] = zm[3]


def rwkv_rows(z_rw, zp_rw, z_wa, zp_wa, mu_rw, mu_wa, w0, a0, k_k, k_a, w2p, a2p):
    n = z_rw.shape[0]
    out = jax.ShapeDtypeStruct((n, D_A), F32)
    return pl.pallas_call(
        _rwkv_rows_kernel,
        out_shape=[out] * 7,
        compiler_params=pltpu.CompilerParams(vmem_limit_bytes=VMEM_LIMIT),
        name="rwkv_rows",
    )(z_rw, zp_rw, z_wa, zp_wa, mu_rw, mu_wa, w0, a0, k_k, k_a, w2p, a2p)


def _rwkv_step_kernel(s_ref, w_ref, kk_ref, b_ref, kf_ref, r_ref, v_ref, g_ref,
                      rk_ref, lnw_ref, lnb_ref, s_out_ref, o_ref):
    S = s_ref[...]
    w = w_ref[...]
    kk = kk_ref[...]
    b = b_ref[...]
    kf = kf_ref[...]
    r = r_ref[...]
    v = v_ref[...]
    sa = -jnp.sum(S * kk, axis=-1, keepdims=True)
    S = S * w + sa * b + v * kf
    s_out_ref[...] = S
    y = jnp.sum(S * r, axis=-1, keepdims=True)
    mean = jnp.mean(y, axis=1, keepdims=True)
    d = y - mean
    var = jnp.mean(d * d, axis=1, keepdims=True)
    yn = d * lax.rsqrt(var + GN_EPS) * lnw_ref[...] + lnb_ref[...]
    bonus = jnp.sum(r * kf * rk_ref[...], axis=-1, keepdims=True) * v
    o_ref[...] = (yn + bonus) * _silu(g_ref[...])


def rwkv_step(state, w, kk, b, kf, r, v, g, r_k, lnx_w, lnx_b):
    n = state.shape[0]
    per = 4 if n % 4 == 0 else 1
    h = per * N_HEADS_A
    flat = lambda a: a.reshape((n * N_HEADS_A,) + a.shape[2:])
    tile = lambda a: jnp.tile(a, (per, 1, 1))
    lane_vec = pl.BlockSpec((h, 1, HEAD_A), lambda i: (i, 0, 0))
    col_vec = pl.BlockSpec((h, HEAD_A, 1), lambda i: (i, 0, 0))
    s_new, o = pl.pallas_call(
        _rwkv_step_kernel,
        grid=(n // per,),
        in_specs=[pl.BlockSpec((h, HEAD_A, HEAD_A), lambda i: (i, 0, 0)),
                  lane_vec, lane_vec, lane_vec, lane_vec, lane_vec, col_vec, col_vec,
                  pl.BlockSpec((h, 1, HEAD_A), lambda i: (0, 0, 0)),
                  pl.BlockSpec((h, HEAD_A, 1), lambda i: (0, 0, 0)),
                  pl.BlockSpec((h, HEAD_A, 1), lambda i: (0, 0, 0))],
        out_specs=[pl.BlockSpec((h, HEAD_A, HEAD_A), lambda i: (i, 0, 0)), col_vec],
        out_shape=[jax.ShapeDtypeStruct((n * N_HEADS_A, HEAD_A, HEAD_A), F32),
                   jax.ShapeDtypeStruct((n * N_HEADS_A, HEAD_A, 1), F32)],
        compiler_params=_cparams(("arbitrary",)),
        name="rwkv_step",
    )(flat(state), flat(w), flat(kk), flat(b), flat(kf), flat(r), flat(v), flat(g),
      tile(r_k), tile(lnx_w), tile(lnx_b))
    return s_new.reshape(state.shape), o.reshape(n, N_HEADS_A, HEAD_A, 1)


def _rope(x, cos_t, sin_t, lane):
    partner = jnp.where(lane < ROT_DIM // 2, pltpu.roll(x, HEAD_B - ROT_DIM // 2, 1),
                        pltpu.roll(x, ROT_DIM // 2, 1))
    return x * cos_t + partner * sin_t


def _attn_proj_kernel(x_ref, lnw_ref, w_ref, cos_ref, sin_ref,
                      qt_ref, kh_ref, vt_ref, ko_ref, vo_ref, km_ref, xn_ref):
    j = pl.program_id(1)
    tm = x_ref.shape[0]
    slab = 2 * HEAD_B

    @pl.when(j == 0)
    def _():
        x = x_ref[...]
        ms = jnp.mean(x * x, axis=-1, keepdims=True)
        xn_ref[...] = ((x * lax.rsqrt(ms + NORM_EPS)) * lnw_ref[...]).astype(BF16)

    cos_t = cos_ref[...]
    sin_t = sin_ref[...]
    lane = lax.broadcasted_iota(jnp.int32, cos_t.shape, 1)

    def slabs():
        for sidx in range(D_B // slab):
            z = jnp.dot(xn_ref[...], w_ref[:, sidx * slab:(sidx + 1) * slab],
                        preferred_element_type=F32)
            for hh in range(slab // HEAD_B):
                h = sidx * (slab // HEAD_B) + hh
                yield h, z[:, hh * HEAD_B:(hh + 1) * HEAD_B]

    @pl.when(j == 0)
    def _():
        for h, z in slabs():
            qt_ref[h] = _rope(z, cos_t, sin_t, lane).astype(qt_ref.dtype).T

    @pl.when(j == 1)
    def _():
        for h, z in slabs():
            sl = slice(h * HEAD_B, (h + 1) * HEAD_B)
            k = _rope(z, cos_t, sin_t, lane)
            kh_ref[h] = k.astype(kh_ref.dtype)
            ko_ref[:, sl] = k
            for blk in range(km_ref.shape[0]):
                rows = tm // km_ref.shape[0]
                km_ref[blk, :, sl] = jnp.mean(k[blk * rows:(blk + 1) * rows], axis=0, keepdims=True)

    @pl.when(j == 2)
    def _():
        for h, z in slabs():
            vo_ref[:, h * HEAD_B:(h + 1) * HEAD_B] = z
            vt_ref[h] = z.astype(vt_ref.dtype).T


def attn_projection(x, ln_w, w_main, cos_t, sin_t, batch, seq):
    m = batch * seq
    blk = min(seq, MOBA_BLOCK)
    tm = min(seq, 512)
    assert seq % tm == 0 and tm % blk == 0
    nt = seq // tm
    sq = pl.Squeezed()
    tab = pl.BlockSpec((tm, HEAD_B), lambda i, j: (i % nt, 0))
    hm = pl.BlockSpec((sq, N_HEADS_B, tm, HEAD_B), lambda i, j: (i // nt, 0, i % nt, 0))
    hm_t = pl.BlockSpec((sq, N_HEADS_B, HEAD_B, tm), lambda i, j: (i // nt, 0, 0, i % nt))
    rows = pl.BlockSpec((tm, D_B), lambda i, j: (i, 0))
    t_shape = jax.ShapeDtypeStruct((batch, N_HEADS_B, HEAD_B, seq), BF16)
    return pl.pallas_call(
        _attn_proj_kernel,
        grid=(m // tm, 3),
        in_specs=[
            pl.BlockSpec((tm, D_MODEL), lambda i, j: (i, 0)),
            pl.BlockSpec((1, D_MODEL), lambda i, j: (0, 0)),
            pl.BlockSpec((D_MODEL, D_B), lambda i, j: (0, COL_Q // D_B + j)),
            tab, tab,
        ],
        out_specs=[hm_t, hm, hm_t, rows, rows,
                   pl.BlockSpec((tm // blk, 1, D_B), lambda i, j: (i, 0, 0))],
        out_shape=[t_shape, jax.ShapeDtypeStruct((batch, N_HEADS_B, seq, HEAD_B), BF16), t_shape,
                   jax.ShapeDtypeStruct((m, D_B), F32), jax.ShapeDtypeStruct((m, D_B), F32),
                   jax.ShapeDtypeStruct((m // blk, 1, D_B), F32)],
        scratch_shapes=[pltpu.VMEM((tm, D_MODEL), BF16)],
        compiler_params=_cparams(("arbitrary", "arbitrary")),
        name="attn_projection",
    )(x, ln_w.reshape(1, -1), w_main, cos_t, sin_t)


def _rope_rows_kernel(zq_ref, zk_ref, cos_ref, sin_ref, q_ref, k_ref):
    cos_t = cos_ref[...]
    sin_t = sin_ref[...]
    lane = lax.broadcasted_iota(jnp.int32, cos_t.shape, 1)
    for h in range(N_HEADS_B):
        sl = slice(h * HEAD_B, (h + 1) * HEAD_B)
        q_ref[:, sl] = _rope(zq_ref[:, sl], cos_t, sin_t, lane).astype(BF16).astype(F32)
        k_ref[:, sl] = _rope(zk_ref[:, sl], cos_t, sin_t, lane)


def rope_rows(zq, zk, cos_t, sin_t):
    out = jax.ShapeDtypeStruct(zq.shape, F32)
    return pl.pallas_call(_rope_rows_kernel, out_shape=[out, out], name="rope_rows")(zq, zk, cos_t, sin_t)


MOBA_HEADS_PER_STEP = 4


def _moba_prompt_kernel(q_ref, k_ref, vt_ref, km_ref, zg_ref, o_ref, sel_ref):
    qb = pl.program_id(1)
    nbatch, nhead, tq = q_ref.shape[0], q_ref.shape[1], q_ref.shape[3]
    nb = km_ref.shape[1]
    seqs = [(b, h) for b in range(nbatch) for h in range(nhead)]
    exp2_scale = HEAD_B ** -0.5 * LOG2_E
    qs = [q_ref[b, h] for b, h in seqs]

    blk = lax.broadcasted_iota(jnp.int32, (nb, tq), 0)
    for i, (b, h) in enumerate(seqs):
        km = km_ref[b, :, h * HEAD_B:(h + 1) * HEAD_B]
        gate = sum(jnp.dot(piece, qs[i], preferred_element_type=F32)
                   for piece in _split3(km))
        gate = jnp.where(blk < qb, gate, -jnp.inf)
        for n in range(nb):
            g_n = gate[n:n + 1, :]
            tie = jnp.where(blk < n, 1.0, 0.0)
            beats = jnp.where(gate > g_n, 1.0, jnp.where(gate == g_n, tie, 0.0))
            cnt = jnp.sum(beats, axis=0, keepdims=True)
            sel_ref[i, n:n + 1, :] = jnp.where(cnt < MOBA_TOPK, 1.0, 0.0)

    def scores(i, n):
        b, h = seqs[i]
        start = pl.multiple_of(n * tq, tq)
        return jnp.dot(k_ref[b, h, pl.ds(start, tq), :], qs[i], preferred_element_type=F32)

    def softmax_step(s, m, l):
        m_new = jnp.maximum(m, jnp.max(s, axis=0, keepdims=True))
        alpha = jnp.exp2((m - m_new) * exp2_scale)
        p = jnp.exp2((s - m_new) * exp2_scale)
        return m_new, alpha, alpha * l + jnp.sum(p, axis=0, keepdims=True), p.astype(BF16)

    def pv(i, n, p):
        b, h = seqs[i]
        start = pl.multiple_of(n * tq, tq)
        return jnp.dot(vt_ref[b, h, :, pl.ds(start, tq)], p, preferred_element_type=F32)

    def step(n, masked, carry):
        sm = [softmax_step(masked[i], carry[i][0], carry[i][1]) for i in range(len(seqs))]
        acc = [sm[i][1] * carry[i][2] + pv(i, n, sm[i][3]) for i in range(len(seqs))]
        return [(sm[i][0], sm[i][2], acc[i]) for i in range(len(seqs))]

    def body(n, carry):
        s = [scores(i, n) for i in range(len(seqs))]
        masked = [jnp.where(sel_ref[i, pl.ds(n, 1), :] > 0.0, s[i], NEG)
                  for i in range(len(seqs))]
        return tuple(step(n, masked, carry))

    init = tuple((jnp.full((1, tq), NEG, F32), jnp.zeros((1, tq), F32),
                  jnp.zeros((HEAD_B, tq), F32)) for i in range(len(seqs)))
    carry = lax.fori_loop(0, qb, body, init)
    ki = lax.broadcasted_iota(jnp.int32, (tq, tq), 0)
    qi = lax.broadcasted_iota(jnp.int32, (tq, tq), 1)
    masked = [jnp.where(ki <= qi, scores(i, qb), NEG) for i in range(len(seqs))]
    final = step(qb, masked, list(carry))
    for i, (b, h) in enumerate(seqs):
        _, l, acc = final[i]
        sl = slice(h * HEAD_B, (h + 1) * HEAD_B)
        o_ref[b, :, sl] = ((acc / l).T * _silu(zg_ref[b, :, sl])).astype(o_ref.dtype)


def moba_prompt(qt_hm, k_hm, vt_hm, kmean, z_main):
    batch, _, seq, _ = k_hm.shape
    tq = min(seq, MOBA_BLOCK)
    nb = seq // tq
    hps = MOBA_HEADS_PER_STEP
    w = hps * HEAD_B
    return pl.pallas_call(
        _moba_prompt_kernel,
        grid=(N_HEADS_B // hps, nb),
        in_specs=[
            pl.BlockSpec((batch, hps, HEAD_B, tq), lambda h, i: (0, h, 0, i)),
            pl.BlockSpec((batch, hps, seq, HEAD_B), lambda h, i: (0, h, 0, 0)),
            pl.BlockSpec((batch, hps, HEAD_B, seq), lambda h, i: (0, h, 0, 0)),
            pl.BlockSpec((batch, nb, w), lambda h, i: (0, 0, h)),
            pl.BlockSpec((batch, tq, w), lambda h, i: (0, i, COL_GB // w + h)),
        ],
        out_specs=pl.BlockSpec((batch, tq, w), lambda h, i: (0, i, h)),
        out_shape=jax.ShapeDtypeStruct((batch, seq, D_B), BF16),
        scratch_shapes=[pltpu.VMEM((batch * hps, nb, tq), F32)],
        compiler_params=_cparams(("arbitrary", "arbitrary")),
        name="moba_prompt",
    )(qt_hm, k_hm, vt_hm, kmean, z_main)


KMEAN_PAGES = 8


def _cache_kmean_kernel(pt_ref, *refs, pages_per_block):
    del pt_ref
    for task in _page_block_mean_tasks(refs[:-1], refs[-1], pages_per_block):
        task()


def cache_kmean(cache_k, page_table, layer):
    n, n_pages = page_table.shape
    page = cache_k.shape[2]
    ppb = MOBA_BLOCK // page
    pps = min(KMEAN_PAGES, n_pages)
    assert n_pages % pps == 0 and pps % ppb == 0
    sq = pl.Squeezed()
    page_spec = lambda i: pl.BlockSpec(
        (sq, sq, page, N_HEADS_B, HEAD_B), lambda b, s, pt, i=i: (layer, pt[b, s * pps + i], 0, 0, 0))
    return pl.pallas_call(
        functools.partial(_cache_kmean_kernel, pages_per_block=ppb),
        grid_spec=pltpu.PrefetchScalarGridSpec(
            num_scalar_prefetch=1,
            grid=(n, n_pages // pps),
            in_specs=[page_spec(i) for i in range(pps)],
            out_specs=pl.BlockSpec((sq, pps // ppb, N_HEADS_B, HEAD_B), lambda b, s, pt: (b, s, 0, 0)),
        ),
        out_shape=jax.ShapeDtypeStruct((n, n_pages // ppb, N_HEADS_B, HEAD_B), F32),
        compiler_params=_cparams(("arbitrary", "arbitrary")),
        name="cache_kmean",
    )(page_table, *([cache_k] * pps))


def _sample_select_kernel(q_ref, km_ref, sel_ref):
    nb = km_ref.shape[0]
    ri = lax.broadcasted_iota(jnp.int32, (nb, nb), 0)
    ci = lax.broadcasted_iota(jnp.int32, (nb, nb), 1)
    lane = lax.broadcasted_iota(jnp.int32, (1, 128), 1)
    blk_row = lax.broadcasted_iota(jnp.int32, (1, nb), 1).astype(F32)
    for h in range(N_HEADS_B):
        km = km_ref[:, h, :]
        g_col = jnp.sum(km * q_ref[h:h + 1, :], axis=1, keepdims=True)
        g_row = jnp.sum(jnp.where(ri == ci, g_col, 0.0), axis=0, keepdims=True)
        beats = (g_col > g_row) | ((g_col == g_row) & (ri < ci))
        rank = jnp.sum(jnp.where(beats, 1.0, 0.0), axis=0, keepdims=True)
        out = jnp.zeros((1, 128), F32)
        for r in range(MOBA_TOPK):
            idx = jnp.sum(jnp.where(rank == float(r), blk_row, 0.0), axis=1, keepdims=True)
            out = jnp.where(lane == r, idx, out)
        sel_ref[h:h + 1, :] = out.astype(jnp.int32)


def sample_select(q_s, kmean_s):
    n, nb = kmean_s.shape[:2]
    sq = pl.Squeezed()
    return pl.pallas_call(
        _sample_select_kernel,
        grid=(n,),
        in_specs=[pl.BlockSpec((sq, N_HEADS_B, HEAD_B), lambda b: (b, 0, 0)),
                  pl.BlockSpec((sq, nb, N_HEADS_B, HEAD_B), lambda b: (b, 0, 0, 0))],
        out_specs=pl.BlockSpec((sq, N_HEADS_B, 128), lambda b: (b, 0, 0)),
        out_shape=jax.ShapeDtypeStruct((n, N_HEADS_B, 128), jnp.int32),
        compiler_params=_cparams(("arbitrary",)),
        name="sample_select",
    )(q_s, kmean_s)


def _sample_attn_kernel(sel_ref, pt_ref, q_ref, kn_ref, vn_ref, zg_ref, ck_ref, cv_ref, o_ref,
                        kbuf, vbuf, sem, *, layer, page, ppb):
    b = pl.program_id(0)
    nb_steps = pl.num_programs(0)
    n_sel = MOBA_TOPK * ppb
    scale = HEAD_B ** -0.5

    def copies(bb, slot):
        out = []
        for h in range(N_HEADS_B):
            for j in range(n_sel):
                pg = pt_ref[bb, sel_ref[bb, h, j // ppb] * ppb + j % ppb]
                dst = pl.ds(j * page, page)
                out.append(pltpu.make_async_copy(ck_ref.at[layer, pg, :, h, :],
                                                 kbuf.at[slot, h, dst, :], sem.at[slot, 0]))
                out.append(pltpu.make_async_copy(cv_ref.at[layer, pg, :, h, :],
                                                 vbuf.at[slot, h, dst, :], sem.at[slot, 1]))
        return out

    slot = b % 2

    @pl.when(b == 0)
    def _():
        for c in copies(0, 0):
            c.start()

    @pl.when(b + 1 < nb_steps)
    def _():
        for c in copies(b + 1, 1 - slot):
            c.start()

    for c in copies(b, slot):
        c.wait()

    for h in range(N_HEADS_B):
        q = q_ref[h]
        k = kbuf[slot, h]
        v = vbuf[slot, h]
        s = jnp.sum(k * q, axis=1, keepdims=True) * scale
        s_own = jnp.sum(kn_ref[h] * q, axis=1, keepdims=True) * scale
        m = jnp.maximum(jnp.max(s, axis=0, keepdims=True), s_own)
        p = jnp.exp(s - m)
        p_own = jnp.exp(s_own - m)
        l = jnp.sum(p, axis=0, keepdims=True) + p_own
        acc = jnp.sum(p * v, axis=0, keepdims=True) + p_own * vn_ref[h]
        o_ref[h] = (acc / l) * _silu(zg_ref[h])


def sample_attn(sel, page_table, q_s, k_new, v_new, zgb, cache_k, cache_v, layer):
    n = q_s.shape[0]
    page = cache_k.shape[2]
    ppb = MOBA_BLOCK // page
    rows = MOBA_TOPK * ppb * page
    sq = pl.Squeezed()
    vec = pl.BlockSpec((sq, N_HEADS_B, 1, HEAD_B), lambda b, sel, pt: (b, 0, 0, 0))
    hbm = pl.BlockSpec(memory_space=pl.ANY)
    return pl.pallas_call(
        functools.partial(_sample_attn_kernel, layer=layer, page=page, ppb=ppb),
        grid_spec=pltpu.PrefetchScalarGridSpec(
            num_scalar_prefetch=2,
            grid=(n,),
            in_specs=[vec, vec, vec, vec, hbm, hbm],
            out_specs=vec,
            scratch_shapes=[pltpu.VMEM((2, N_HEADS_B, rows, HEAD_B), F32),
                            pltpu.VMEM((2, N_HEADS_B, rows, HEAD_B), F32),
                            pltpu.SemaphoreType.DMA((2, 2))],
        ),
        out_shape=jax.ShapeDtypeStruct((n, N_HEADS_B, 1, HEAD_B), F32),
        compiler_params=_cparams(("arbitrary",)),
        name="sample_attn",
    )(sel, page_table, q_s, k_new, v_new, zgb, cache_k, cache_v)


def _merge_out_kernel(oa_ref, ob_ref, pa_ref, pb_ref, zga_ref, zgm_ref, wo_ref, x_ref, fw_ref, y_ref):
    ya = jnp.dot(oa_ref[...], pa_ref[...], preferred_element_type=F32)
    yb = jnp.dot(ob_ref[...], pb_ref[...], preferred_element_type=F32)
    merged = (_sigmoid(zga_ref[...]) * ya + _sigmoid(zgm_ref[...]) * yb).astype(BF16)
    h = x_ref[...] + jnp.dot(merged, wo_ref[...], preferred_element_type=F32)
    ms = jnp.mean(h * h, axis=-1, keepdims=True)
    y_ref[...] = (h * lax.rsqrt(ms + NORM_EPS)) * fw_ref[...]


def merge_and_output(o_a, o_b, p_a, p_b, z_main, w_o, x, final_w):
    m = x.shape[0]
    tm = min(m, 256)
    resident = pl.Buffered(1)
    return pl.pallas_call(
        _merge_out_kernel,
        grid=(m // tm,),
        in_specs=[
            pl.BlockSpec((tm, D_A), lambda i: (i, 0)),
            pl.BlockSpec((tm, D_B), lambda i: (i, 0)),
            pl.BlockSpec((D_A, D_MODEL), lambda i: (0, 0), pipeline_mode=resident),
            pl.BlockSpec((D_B, D_MODEL), lambda i: (0, 0), pipeline_mode=resident),
            pl.BlockSpec((tm, D_MODEL), lambda i: (i, COL_GA // D_MODEL)),
            pl.BlockSpec((tm, D_MODEL), lambda i: (i, COL_GM // D_MODEL)),
            pl.BlockSpec((D_MODEL, D_MODEL), lambda i: (0, 0), pipeline_mode=resident),
            pl.BlockSpec((tm, D_MODEL), lambda i: (i, 0)),
            pl.BlockSpec((1, D_MODEL), lambda i: (0, 0)),
        ],
        out_specs=pl.BlockSpec((tm, D_MODEL), lambda i: (i, 0)),
        out_shape=jax.ShapeDtypeStruct((m, D_MODEL), F32),
        compiler_params=_cparams(("arbitrary",)),
        name="merge_and_output",
    )(o_a, o_b, p_a, p_b, z_main, z_main, w_o, x, final_w.reshape(1, -1))


def _rope_tables(pos):
    half = ROT_DIM // 2
    inv = jnp.power(jnp.float32(ROPE_THETA), -jnp.arange(half, dtype=F32) * (2.0 / ROT_DIM))
    ang = pos.astype(F32)[:, None] * inv[None, :]
    cos, sin = jnp.cos(ang), jnp.sin(ang)
    n = pos.shape[0]
    rest = HEAD_B - ROT_DIM
    cos_t = jnp.concatenate([cos, cos, jnp.ones((n, rest), F32)], axis=1)
    sin_t = jnp.concatenate([-sin, sin, jnp.zeros((n, rest), F32)], axis=1)
    return cos_t, sin_t


def _reorder_rw(a):
    r, w_lo, k, v, a_lo, g = jnp.split(
        a, [D_A, D_A + LORA, 2 * D_A + LORA, 3 * D_A + LORA, 3 * D_A + 2 * LORA], axis=-1)
    return jnp.concatenate([r, k, v, g], axis=-1), jnp.concatenate([w_lo, a_lo], axis=-1)


def kernel(x_prompt, x_sample, state_shift, state_wkv, cache_k, cache_v, page_table, ln_w, w_in, mu,
           w0, w2, a0, a2, k_k, k_a, r_k, lnx_w, lnx_b, p_a, p_b, w_o, final_w):
    depth = ln_w.shape[0]
    assert depth == 1, "single-layer trunk"
    B, T, _ = x_prompt.shape
    DB, TS, _ = x_sample.shape
    assert TS == 1
    n_pages = page_table.shape[1]
    page = cache_k.shape[2]
    past = n_pages * page
    assert past % MOBA_BLOCK == 0 and MOBA_BLOCK % page == 0
    l = 0
    rw_cols = 4 * D_A + 2 * LORA

    att0 = rw_cols
    src_cols = [0, D_A + LORA, 2 * D_A + LORA, 3 * D_A + 2 * LORA]
    src_cols += [att0 + 4 * D_B + i * W_TILE for i in range(2 * D_MODEL // W_TILE)]
    src_cols += [att0 + 3 * D_B, att0, att0 + D_B, att0 + 2 * D_B]
    w_main = prepare_projection_weight(w_in, l, src_cols)
    w_wa = jnp.concatenate([w_in[l, :, D_A:D_A + LORA],
                            w_in[l, :, 3 * D_A + LORA:3 * D_A + 2 * LORA]], axis=1)
    mu_rw, mu_wa = _reorder_rw(mu[l][None, :])
    zeros = jnp.zeros((LORA, D_A), F32)
    w2p = jnp.concatenate([w2[l], zeros], axis=0).astype(BF16)
    a2p = jnp.concatenate([zeros, a2[l]], axis=0).astype(BF16)
    row = lambda a: a.reshape(1, -1)
    vecs = (row(w0[l]), row(a0[l]), row(k_k[l]), row(k_a[l]), row(r_k[l]), row(lnx_w[l]), row(lnx_b[l]))
    pa_bf, pb_bf, wo_bf = p_a[l].astype(BF16), p_b[l].astype(BF16), w_o[l].astype(BF16)

    rows = jnp.concatenate([x_prompt[:, -1, :], x_sample[:, 0, :]], axis=0)
    xn_rows = rmsnorm_rows(rows, ln_w[l])
    shift_prompt = xn_rows[:B]
    xn_s = xn_rows[B:]
    xp = x_prompt.reshape(B * T, D_MODEL)
    z_p, zwa_p = input_projection(xp, ln_w[l], w_main, w_wa, normalize=True, n_cols=N_Z)
    z_s2, zwa_s2 = input_projection(jnp.concatenate([xn_s, state_shift[l]], axis=0), ln_w[l],
                                    w_main, w_wa, normalize=False, n_cols=N_MAIN)

    z_p3 = z_p.reshape(B, T, N_Z)
    o_a_p, wkv_p, kmean_s = rwkv_prompt(z_p3, zwa_p.reshape(B, T, 2 * LORA), mu_rw, mu_wa, vecs, w2p, a2p,
                                        cache_k, page_table, l)
    cos_p, sin_p = _rope_tables(jnp.arange(T))
    qt_hm, k_hm, vt_hm, k_rows_p, v_rows_p, kmean_p = attn_projection(xp, ln_w[l], w_main, cos_p, sin_p, B, T)
    nb_p = T // min(T, MOBA_BLOCK)
    o_b_p = moba_prompt(qt_hm, k_hm, vt_hm, kmean_p.reshape(B, nb_p, D_B), z_p3)
    y_prompt = merge_and_output(o_a_p.reshape(B * T, D_A), o_b_p.reshape(B * T, D_B), pa_bf, pb_bf, z_p,
                                wo_bf, xp, final_w).reshape(B, T, D_MODEL)

    z_s, zprev_s = z_s2[:DB], z_s2[DB:]
    r_s, kf_s, v_s, kk_s, b_s, w_s, g_s = rwkv_rows(
        z_s[:, :4 * D_A], zprev_s[:, :4 * D_A], zwa_s2[:DB], zwa_s2[DB:], mu_rw, mu_wa,
        vecs[0], vecs[1], vecs[2], vecs[3], w2p, a2p)
    hs = lambda a: a.reshape(-1, N_HEADS_A, 1, HEAD_A)
    col = lambda a: a.reshape(-1, N_HEADS_A, HEAD_A, 1)
    wkv_s, o_a_s = rwkv_step(state_wkv[l], hs(w_s), hs(kk_s), hs(b_s), hs(kf_s), hs(r_s),
                             col(v_s), col(g_s), hs(r_k[l])[0], col(lnx_w[l])[0], col(lnx_b[l])[0])
    o_a_s = o_a_s.reshape(DB, D_A).astype(BF16)

    cos_s, sin_s = _rope_tables(past + jnp.arange(TS))
    cos_s = jnp.broadcast_to(cos_s, (DB, HEAD_B))
    sin_s = jnp.broadcast_to(sin_s, (DB, HEAD_B))
    q_rows_s, k_rows_s = rope_rows(z_s[:, COL_Q:COL_Q + D_B], z_s[:, COL_KB:COL_KB + D_B], cos_s, sin_s)
    v_rows_s = z_s[:, COL_VB:COL_VB + D_B]
    if kmean_s is None:
        kmean_s = cache_kmean(cache_k, page_table, l)
    sel = sample_select(q_rows_s.reshape(DB, N_HEADS_B, HEAD_B), kmean_s)[:, :, :MOBA_TOPK]
    hv = lambda a: a.reshape(DB, N_HEADS_B, 1, HEAD_B)
    o_b_s = sample_attn(sel, page_table, hv(q_rows_s), hv(k_rows_s), hv(v_rows_s),
                        hv(z_s[:, COL_GB:COL_GB + D_B]), cache_k, cache_v, l)
    o_b_s = o_b_s.reshape(DB, D_B).astype(BF16)
    y_sample = merge_and_output(o_a_s, o_b_s, pa_bf, pb_bf, z_s, wo_bf, x_sample.reshape(DB, D_MODEL),
                                final_w)

    return (y_prompt,
            y_sample.reshape(DB, TS, D_MODEL),
            shift_prompt[None],
            wkv_p.reshape(1, B, N_HEADS_A, HEAD_A, HEAD_A),
            k_rows_p.reshape(1, B, T, N_HEADS_B, HEAD_B),
            v_rows_p.reshape(1, B, T, N_HEADS_B, HEAD_B),
            xn_s[None],
            wkv_s[None],
            k_rows_s.reshape(1, DB, TS, N_HEADS_B, HEAD_B),
            v_rows_s.reshape(1, DB, TS, N_HEADS_B, HEAD_B))
```

```python
import functools

import jax
import jax.numpy as jnp
from jax import lax
from jax.experimental import pallas as pl
from jax.experimental.pallas import tpu as pltpu

F32 = jnp.float32
BF16 = jnp.bfloat16

D_MODEL = 2048
D_A = D_MODEL // 2
HEAD_A = 64
N_HEADS_A = D_A // HEAD_A
LORA = 64
D_B = D_MODEL // 2
HEAD_B = 128
N_HEADS_B = D_B // HEAD_B
ROT_DIM = HEAD_B // 4
ROPE_THETA = 500000.0
MOBA_BLOCK = 256
MOBA_TOPK = 3
NORM_EPS = 1e-6
GN_EPS = 64e-5
NEG = -1e30
LOG2_E = 1.4426950408889634

COL_R, COL_K, COL_V, COL_G = 0, D_A, 2 * D_A, 3 * D_A
COL_GA = 4 * D_A
COL_GM = COL_GA + D_MODEL
COL_GB = COL_GM + D_MODEL
N_Z = COL_GB + D_B
COL_Q, COL_KB, COL_VB = N_Z, N_Z + D_B, N_Z + 2 * D_B
N_MAIN = COL_VB + D_B

GROUP = 256
HEADS_PER_GROUP = GROUP // HEAD_A
CHUNK = 64
assert CHUNK == HEAD_A
VMEM_LIMIT = 56 * 1024 * 1024


def _cparams(sem):
    return pltpu.CompilerParams(dimension_semantics=sem, vmem_limit_bytes=VMEM_LIMIT)


def _bdot(a, b):
    return jnp.dot(a.astype(BF16), b.astype(BF16), preferred_element_type=F32)


def _bdot_nt(a, b):
    return lax.dot_general(a.astype(BF16), b.astype(BF16), (((1,), (1,)), ((), ())),
                           preferred_element_type=F32)


def _bdot_tn(a, b):
    return lax.dot_general(a.astype(BF16), b.astype(BF16), (((0,), (0,)), ((), ())),
                           preferred_element_type=F32)


def _split3(x):
    hi = x.astype(BF16)
    r1 = x - hi.astype(F32)
    mid = r1.astype(BF16)
    lo = (r1 - mid.astype(F32)).astype(BF16)
    return hi, mid, lo


def _sigmoid(x):
    return 1.0 / (1.0 + jnp.exp(-x))


def _silu(x):
    return x * _sigmoid(x)


def _softplus(x):
    return jnp.maximum(x, 0.0) + jnp.log(1.0 + jnp.exp(-jnp.abs(x)))


def _rmsnorm_rows_kernel(x_ref, w_ref, o_ref):
    x = x_ref[...]
    ms = jnp.mean(x * x, axis=-1, keepdims=True)
    o_ref[...] = (x * lax.rsqrt(ms + NORM_EPS)) * w_ref[...]


def rmsnorm_rows(x, w):
    return pl.pallas_call(
        _rmsnorm_rows_kernel,
        out_shape=jax.ShapeDtypeStruct(x.shape, F32),
        name="rmsnorm_rows",
    )(x, w.reshape(1, -1))


W_TILE = 1024
LANES = 128


def _w_prep_kernel(w_hbm, o_hbm, in_buf, out_buf, in_sem, out_sem, *, layer, starts):
    n = len(starts)
    rows = in_buf.shape[1]

    def in_copy(j, slot):
        off = starts[j] % LANES
        width = W_TILE + (LANES if off else 0)
        return pltpu.make_async_copy(w_hbm.at[layer, :, pl.ds(starts[j] - off, width)],
                                     in_buf.at[slot, :, pl.ds(0, width)], in_sem.at[slot])

    def out_copy(j, slot):
        return pltpu.make_async_copy(out_buf.at[slot], o_hbm.at[:, pl.ds(j * W_TILE, W_TILE)],
                                     out_sem.at[slot])

    in_copy(0, 0).start()
    for j in range(n):
        slot = j % 2
        if j + 1 < n:
            in_copy(j + 1, 1 - slot).start()
        in_copy(j, slot).wait()
        if j >= 2:
            out_copy(j - 2, slot).wait()
        off = starts[j] % LANES
        rc = 256

        def chunk(c, _, slot=slot, off=off):
            r0 = pl.multiple_of(c * rc, rc)
            out_buf[slot, pl.ds(r0, rc), :] = in_buf[slot, pl.ds(r0, rc), off:off + W_TILE].astype(BF16)
            return 0

        lax.fori_loop(0, rows // rc, chunk, 0)
        out_copy(j, slot).start()
    for j in range(max(n - 2, 0), n):
        out_copy(j, j % 2).wait()


def prepare_projection_weight(w_in, layer, starts):
    rows = w_in.shape[1]
    assert rows % 256 == 0
    hbm = pl.BlockSpec(memory_space=pl.ANY)
    return pl.pallas_call(
        functools.partial(_w_prep_kernel, layer=layer, starts=tuple(starts)),
        in_specs=[hbm],
        out_specs=hbm,
        out_shape=jax.ShapeDtypeStruct((rows, len(starts) * W_TILE), BF16),
        scratch_shapes=[pltpu.VMEM((2, rows, W_TILE + LANES), F32),
                        pltpu.VMEM((2, rows, W_TILE), BF16),
                        pltpu.SemaphoreType.DMA((2,)), pltpu.SemaphoreType.DMA((2,))],
        compiler_params=pltpu.CompilerParams(vmem_limit_bytes=VMEM_LIMIT),
        name="prepare_projection_weight",
    )(w_in)


def _proj_kernel(x_ref, lnw_ref, w_ref, wwa_ref, z_ref, zwa_ref, xn_ref, *, normalize):
    @pl.when(pl.program_id(1) == 0)
    def _():
        x = x_ref[...]
        if normalize:
            ms = jnp.mean(x * x, axis=-1, keepdims=True)
            x = (x * lax.rsqrt(ms + NORM_EPS)) * lnw_ref[...]
        xn_ref[...] = x.astype(BF16)
        zwa_ref[...] = jnp.dot(xn_ref[...], wwa_ref[...].astype(BF16), preferred_element_type=F32)

    z_ref[...] = jnp.dot(xn_ref[...], w_ref[...], preferred_element_type=F32)


def input_projection(x, ln_w, w_main, w_wa, *, normalize, n_cols):
    m = x.shape[0]
    tm = min(m, 1024)
    if m < 1024:
        tn = 2048
    else:
        tn = 1536 if n_cols % 1536 == 0 else 1024
    assert m % tm == 0 and n_cols % tn == 0
    return pl.pallas_call(
        functools.partial(_proj_kernel, normalize=normalize),
        grid=(m // tm, n_cols // tn),
        in_specs=[
            pl.BlockSpec((tm, D_MODEL), lambda i, j: (i, 0)),
            pl.BlockSpec((1, D_MODEL), lambda i, j: (0, 0)),
            pl.BlockSpec((D_MODEL, tn), lambda i, j: (0, j)),
            pl.BlockSpec((D_MODEL, 2 * LORA), lambda i, j: (0, 0)),
        ],
        out_specs=[
            pl.BlockSpec((tm, tn), lambda i, j: (i, j)),
            pl.BlockSpec((tm, 2 * LORA), lambda i, j: (i, 0)),
        ],
        out_shape=[
            jax.ShapeDtypeStruct((m, n_cols), F32),
            jax.ShapeDtypeStruct((m, 2 * LORA), F32),
        ],
        scratch_shapes=[pltpu.VMEM((tm, D_MODEL), BF16)],
        compiler_params=_cparams(("arbitrary", "arbitrary")),
        name="input_projection",
    )(x, ln_w.reshape(1, -1), w_main, w_wa)


def _head_mask(n):
    r = lax.broadcasted_iota(jnp.int32, (n, n), 0) // HEAD_A
    c = lax.broadcasted_iota(jnp.int32, (n, n), 1) // HEAD_A
    return r == c


def _segsum(x, ones_bd):
    return jnp.dot(x.astype(BF16), ones_bd, preferred_element_type=F32)


def _rwkv_prep(zm_r, zm_k, zm_v, zm_wa, w0, a0, k_k, k_a, w2p, a2p, ones_bd):
    lane = lax.broadcasted_iota(jnp.int32, zm_wa.shape, 1)
    lora_in = jnp.where(lane < LORA, jnp.tanh(zm_wa), zm_wa)
    ww = _bdot(lora_in, w2p)
    aa = _bdot(lora_in, a2p)
    w_log = -_softplus(-(w0 + ww)) - 0.5
    logw = -jnp.exp(w_log)
    a = _sigmoid(a0 + aa)
    kk = zm_k * k_k
    ss = _segsum(kk * kk, ones_bd)
    kk = kk / jnp.maximum(jnp.sqrt(ss), 1e-12)
    kf = zm_k * (1.0 + (a - 1.0) * k_a)
    return zm_r, kf, zm_v, kk, kk * a, logw


def _rwkv_post(y, r, kf, v, zm_g, r_k, lnx_w, lnx_b, ones_bd):
    inv_n = 1.0 / HEAD_A
    mean = _segsum(y, ones_bd) * inv_n
    d = y - mean
    var = _segsum(d * d, ones_bd) * inv_n
    yn = d * lax.rsqrt(var + GN_EPS) * lnx_w + lnx_b
    bonus = _segsum(r * kf * r_k, ones_bd) * v
    return (yn + bonus) * _silu(zm_g)


def _stack_heads(x, lane_head):
    return jnp.concatenate(
        [jnp.where(lane_head == h, x, 0.0) for h in range(HEADS_PER_GROUP)], axis=0)


def _chunk_precompute(insts, consts, side_tasks=()):
    strict, incl, eye, lane_head, bd_mask = consts
    c = CHUNK
    n = len(insts)
    side_tasks = list(side_tasks)

    def run_side(k=1):
        for _ in range(k):
            if side_tasks:
                side_tasks.pop(0)()

    def bd(p):
        return jnp.where(bd_mask, jnp.concatenate([p] * HEADS_PER_GROUP, axis=0), 0.0).astype(BF16)

    lhs, wt, v_bd, bdkd, p_end = [], [], [], [], []
    for r, kf, v, kk, b, logw, cl in insts:
        cl_last = cl[c - 1:c, :]
        e_neg = jnp.exp(-cl)
        e_end = jnp.exp(cl_last - cl)
        lhs.append(jnp.concatenate([kk * jnp.exp(cl - logw), r * jnp.exp(cl)], axis=0).astype(BF16))
        wt.append(jnp.concatenate([_stack_heads(b * e_neg, lane_head),
                                   _stack_heads(kf * e_neg, lane_head)], axis=0).astype(BF16))
        v_bd.append(_stack_heads(v, lane_head).astype(BF16))
        bdkd.append(jnp.concatenate([b * e_end, kf * e_end], axis=0).astype(BF16))
        p_end.append(jnp.exp(cl_last))
    att = [_bdot_nt(lhs[i], wt[i]) for i in range(n)]
    run_side()
    a_ab = [jnp.where(strict, att[i][:c, :4 * c], 0.0) for i in range(n)]
    a_ak = [jnp.where(strict, att[i][:c, 4 * c:], 0.0) for i in range(n)]
    m_cat = [jnp.concatenate([jnp.where(incl, att[i][c:, :4 * c], 0.0),
                              jnp.where(incl, att[i][c:, 4 * c:], 0.0)], axis=1).astype(BF16)
             for i in range(n)]
    av = [_bdot(a_ak[i], v_bd[i]) for i in range(n)]
    run_side()

    x = [eye - a_ab[i] for i in range(n)]
    p = [_bdot(a_ab[i], bd(a_ab[i])) for i in range(n)]
    run_side()
    for _ in range(4):
        px = [_bdot(jnp.concatenate([p[i], x[i]], axis=0), bd(p[i])) for i in range(n)]
        p = [px[i][:c] for i in range(n)]
        x = [x[i] + px[i][c:] for i in range(n)]
        run_side()
    x = [(x[i] + _bdot(x[i], bd(p[i]))).astype(BF16) for i in range(n)]
    run_side(len(side_tasks))
    return [dict(lhs=lhs[i], av=av[i], t=x[i], m_cat=m_cat[i], v_bd=v_bd[i], v=insts[i][2],
                 bdkd=bdkd[i], p_end=p_end[i]) for i in range(n)]


def _chunk_state_step(states, pres, consts):
    lane_head, bd_mask = consts[3], consts[4]
    c = CHUNK
    n = len(states)
    sh = [_bdot_nt(pres[i]["lhs"], states[i]) for i in range(n)]
    rhs = [sh[i][:c] + pres[i]["av"] for i in range(n)]
    u = [-_bdot(pres[i]["t"], _stack_heads(rhs[i], lane_head)) for i in range(n)]
    y = [sh[i][c:] + _bdot(pres[i]["m_cat"],
                           jnp.concatenate([_stack_heads(u[i], lane_head).astype(BF16),
                                            pres[i]["v_bd"]], axis=0)) for i in range(n)]
    upd = [_bdot_tn(jnp.concatenate([u[i], pres[i]["v"]], axis=0), pres[i]["bdkd"]) for i in range(n)]
    new = [states[i] * pres[i]["p_end"] + jnp.where(bd_mask, upd[i], 0.0) for i in range(n)]
    return y, new


def _page_block_mean_tasks(page_refs, km_ref, pages_per_block):
    def task(j):
        def run():
            blk = jnp.sum(page_refs[j * pages_per_block][...], axis=0)
            for i in range(1, pages_per_block):
                blk = blk + jnp.sum(page_refs[j * pages_per_block + i][...], axis=0)
            km_ref[j] = blk * (1.0 / MOBA_BLOCK)
        return run
    return [task(j) for j in range(len(page_refs) // pages_per_block)]


def _rwkv_prompt_kernel(*refs, n_pages, pages_per_block):
    if n_pages:
        refs = refs[1:]
    (zr_ref, zk_ref, zv_ref, zg_ref, zwa_ref, mur_ref, muk_ref, muv_ref, mug_ref, muwa_ref,
     w0_ref, a0_ref, kk_ref, ka_ref, rk_ref, lnw_ref, lnb_ref, w2_ref, a2_ref) = refs[:19]
    page_refs = refs[19:19 + n_pages]
    refs = refs[19 + n_pages:]
    o_ref, s_out_ref = refs[:2]
    side_tasks = []
    if n_pages:
        side_tasks = _page_block_mean_tasks(page_refs, refs[2], pages_per_block)
        refs = refs[1:]
    s_ref, pr_ref, pk_ref, pv_ref, pg_ref, pwa_ref = refs[2:]
    t = pl.program_id(1)
    nseq, tr = zr_ref.shape[0], zr_ref.shape[1]

    @pl.when(t == 0)
    def _():
        for ref in (s_ref, pr_ref, pk_ref, pv_ref, pg_ref, pwa_ref):
            ref[...] = jnp.zeros_like(ref)

    def shifted(z_ref, prev_ref, mu_ref, i):
        z = z_ref[i]
        row = lax.broadcasted_iota(jnp.int32, z.shape, 0)
        prev = jnp.where(row == 0, prev_ref[i], pltpu.roll(z, 1, 0))
        prev_ref[i] = z[tr - 1:tr, :]
        return z + (prev - z) * mu_ref[...]

    bd_mask = _head_mask(GROUP)
    ones_bd = jnp.where(bd_mask, 1.0, 0.0).astype(BF16)
    ri = lax.broadcasted_iota(jnp.int32, (tr, tr), 0)
    ci = lax.broadcasted_iota(jnp.int32, (tr, tr), 1)
    tri = jnp.where((ri // CHUNK == ci // CHUNK) & (ci <= ri), 1.0, 0.0).astype(BF16)
    rc = lax.broadcasted_iota(jnp.int32, (CHUNK, GROUP), 0)
    lc = lax.broadcasted_iota(jnp.int32, (CHUNK, GROUP), 1)
    sc = lc % CHUNK
    consts = (sc < rc, sc <= rc, jnp.where(sc == rc, 1.0, 0.0), lc // HEAD_A, bd_mask)
    last = t == pl.num_programs(1) - 1

    n_ch = tr // CHUNK
    prepped, insts = [], []
    for i in range(nseq):
        zm_r = shifted(zr_ref, pr_ref, mur_ref, i)
        zm_k = shifted(zk_ref, pk_ref, muk_ref, i)
        zm_v = shifted(zv_ref, pv_ref, muv_ref, i)
        zm_g = shifted(zg_ref, pg_ref, mug_ref, i)
        zm_wa = shifted(zwa_ref, pwa_ref, muwa_ref, i)
        r, kf, v, kk, b, logw = _rwkv_prep(zm_r, zm_k, zm_v, zm_wa, w0_ref[...], a0_ref[...],
                                           kk_ref[...], ka_ref[...], w2_ref[...], a2_ref[...], ones_bd)
        cl = sum(jnp.dot(tri, piece, preferred_element_type=F32) for piece in _split3(logw))
        prepped.append((r, kf, v, zm_g))
        for ch in range(n_ch):
            sl = slice(ch * CHUNK, (ch + 1) * CHUNK)
            insts.append((r[sl], kf[sl], v[sl], kk[sl], b[sl], logw[sl], cl[sl]))
    pres = _chunk_precompute(insts, consts, side_tasks)

    states = [s_ref[i] for i in range(nseq)]
    ys = [[] for _ in range(nseq)]
    for ch in range(n_ch):
        y, states = _chunk_state_step(states, [pres[i * n_ch + ch] for i in range(nseq)], consts)
        for i in range(nseq):
            ys[i].append(y[i])
    for i in range(nseq):
        s_ref[i] = states[i]
        r, kf, v, zm_g = prepped[i]
        o_ref[i] = _rwkv_post(jnp.concatenate(ys[i], axis=0), r, kf, v, zm_g, rk_ref[...],
                              lnw_ref[...], lnb_ref[...], ones_bd).astype(o_ref.dtype)

    @pl.when(last)
    def _():
        for i in range(nseq):
            for h in range(HEADS_PER_GROUP):
                s_out_ref[i, h] = s_ref[i, h * HEAD_A:(h + 1) * HEAD_A, h * HEAD_A:(h + 1) * HEAD_A]


def rwkv_prompt(z_main, z_wa, mu_main, mu_wa, vecs, w2p, a2p, cache_k=None, page_table=None, layer=0):
    w0, a0, k_k, k_a, r_k, lnx_w, lnx_b = vecs
    batch, seq, _ = z_main.shape
    tr = min(seq, 256)
    assert seq % tr == 0 and tr % CHUNK == 0
    nt = seq // tr
    ng = D_A // GROUP

    n_pages = ppb = 0
    if cache_k is not None:
        page = cache_k.shape[2]
        ppb = MOBA_BLOCK // page
        total = page_table.size
        if total % (ng * nt) == 0 and (total // (ng * nt)) % ppb == 0:
            n_pages = total // (ng * nt)

    def zspec(col):
        return pl.BlockSpec((batch, tr, GROUP), lambda g, t, *_, c=col // GROUP: (0, t, c + g))

    def vspec(col=0):
        return pl.BlockSpec((1, GROUP), lambda g, t, *_, c=col // GROUP: (0, c + g))

    in_specs = [
        zspec(COL_R), zspec(COL_K), zspec(COL_V), zspec(COL_G),
        pl.BlockSpec((batch, tr, 2 * LORA), lambda g, t, *_: (0, t, 0)),
        vspec(COL_R), vspec(COL_K), vspec(COL_V), vspec(COL_G),
        pl.BlockSpec((1, 2 * LORA), lambda g, t, *_: (0, 0)),
        vspec(), vspec(), vspec(), vspec(), vspec(), vspec(), vspec(),
        pl.BlockSpec((2 * LORA, GROUP), lambda g, t, *_: (0, g)),
        pl.BlockSpec((2 * LORA, GROUP), lambda g, t, *_: (0, g)),
    ]
    out_specs = [
        pl.BlockSpec((batch, tr, GROUP), lambda g, t, *_: (0, t, g)),
        pl.BlockSpec((batch, HEADS_PER_GROUP, HEAD_A, HEAD_A), lambda g, t, *_: (0, g, 0, 0)),
    ]
    out_shape = [
        jax.ShapeDtypeStruct((batch, seq, D_A), BF16),
        jax.ShapeDtypeStruct((batch, N_HEADS_A, HEAD_A, HEAD_A), F32),
    ]
    operands = [z_main, z_main, z_main, z_main, z_wa,
                mu_main, mu_main, mu_main, mu_main, mu_wa,
                w0, a0, k_k, k_a, r_k, lnx_w, lnx_b, w2p, a2p]
    if n_pages:
        sq = pl.Squeezed()
        for i in range(n_pages):
            in_specs.append(pl.BlockSpec(
                (sq, sq, page, N_HEADS_B, HEAD_B),
                lambda g, t, pt, i=i: (layer, pt[(g * nt + t) * n_pages + i], 0, 0, 0)))
        operands += [cache_k] * n_pages
        out_specs.append(pl.BlockSpec((n_pages // ppb, N_HEADS_B, HEAD_B),
                                      lambda g, t, pt: (g * nt + t, 0, 0)))
        out_shape.append(jax.ShapeDtypeStruct((page_table.size // ppb, N_HEADS_B, HEAD_B), F32))
    row = lambda n: pltpu.VMEM((batch, 1, n), F32)
    scratch = [pltpu.VMEM((batch, GROUP, GROUP), F32),
               row(GROUP), row(GROUP), row(GROUP), row(GROUP), row(2 * LORA)]
    kern = functools.partial(_rwkv_prompt_kernel, n_pages=n_pages, pages_per_block=ppb)
    params = _cparams(("arbitrary", "arbitrary"))
    if not n_pages:
        o_a, wkv = pl.pallas_call(kern, grid=(ng, nt), in_specs=in_specs, out_specs=out_specs,
                                  out_shape=out_shape, scratch_shapes=scratch,
                                  compiler_params=params, name="rwkv_prompt")(*operands)
        return o_a, wkv, None
    o_a, wkv, kmean = pl.pallas_call(
        kern,
        grid_spec=pltpu.PrefetchScalarGridSpec(
            num_scalar_prefetch=1, grid=(ng, nt), in_specs=in_specs, out_specs=out_specs,
            scratch_shapes=scratch),
        out_shape=out_shape, compiler_params=params, name="rwkv_prompt",
    )(page_table.reshape(-1), *operands)
    n = page_table.shape[0]
    return o_a, wkv, kmean.reshape(n, -1, N_HEADS_B, HEAD_B)


def _rwkv_rows_kernel(z_ref, zp_ref, zwa_ref, zpwa_ref, mu_ref, muwa_ref,
                      w0_ref, a0_ref, kk_ref, ka_ref, w2_ref, a2_ref,
                      r_ref, kf_ref, v_ref, kkn_ref, b_ref, w_ref, g_ref):
    def shifted(z, zp, mu):
        return z + (zp - z) * mu

    mu = mu_ref[...]
    z = z_ref[...]
    zp = zp_ref[...]
    zm = [shifted(z[:, c:c + D_A], zp[:, c:c + D_A], mu[:, c:c + D_A])
          for c in (COL_R, COL_K, COL_V, COL_G)]
    zm_wa = shifted(zwa_ref[...], zpwa_ref[...], muwa_ref[...])
    ones_bd = jnp.where(_head_mask(D_A), 1.0, 0.0).astype(BF16)
    r, kf, v, kk, b, logw = _rwkv_prep(zm[0], zm[1], zm[2], zm_wa, w0_ref[...], a0_ref[...],
                                       kk_ref[...], ka_ref[...], w2_ref[...], a2_ref[...], ones_bd)
    r_ref[...] = r
    kf_ref[...] = kf
    v_ref[...] = v
    kkn_ref[...] = kk
    b_ref[...] = b
    w_ref[...] = jnp.exp(logw)
    g_ref[...] = zm[3]


def rwkv_rows(z_rw, zp_rw, z_wa, zp_wa, mu_rw, mu_wa, w0, a0, k_k, k_a, w2p, a2p):
    n = z_rw.shape[0]
    out = jax.ShapeDtypeStruct((n, D_A), F32)
    return pl.pallas_call(
        _rwkv_rows_kernel,
        out_shape=[out] * 7,
        compiler_params=pltpu.CompilerParams(vmem_limit_bytes=VMEM_LIMIT),
        name="rwkv_rows",
    )(z_rw, zp_rw, z_wa, zp_wa, mu_rw, mu_wa, w0, a0, k_k, k_a, w2p, a2p)


def _rwkv_step_kernel(s_ref, w_ref, kk_ref, b_ref, kf_ref, r_ref, v_ref, g_ref,
                      rk_ref, lnw_ref, lnb_ref, s_out_ref, o_ref):
    S = s_ref[...]
    w = w_ref[...]
    kk = kk_ref[...]
    b = b_ref[...]
    kf = kf_ref[...]
    r = r_ref[...]
    v = v_ref[...]
    sa = -jnp.sum(S * kk, axis=-1, keepdims=True)
    S = S * w + sa * b + v * kf
    s_out_ref[...] = S
    y = jnp.sum(S * r, axis=-1, keepdims=True)
    mean = jnp.mean(y, axis=1, keepdims=True)
    d = y - mean
    var = jnp.mean(d * d, axis=1, keepdims=True)
    yn = d * lax.rsqrt(var + GN_EPS) * lnw_ref[...] + lnb_ref[...]
    bonus = jnp.sum(r * kf * rk_ref[...], axis=-1, keepdims=True) * v
    o_ref[...] = (yn + bonus) * _silu(g_ref[...])


def rwkv_step(state, w, kk, b, kf, r, v, g, r_k, lnx_w, lnx_b):
    n = state.shape[0]
    per = 4 if n % 4 == 0 else 1
    h = per * N_HEADS_A
    flat = lambda a: a.reshape((n * N_HEADS_A,) + a.shape[2:])
    tile = lambda a: jnp.tile(a, (per, 1, 1))
    lane_vec = pl.BlockSpec((h, 1, HEAD_A), lambda i: (i, 0, 0))
    col_vec = pl.BlockSpec((h, HEAD_A, 1), lambda i: (i, 0, 0))
    s_new, o = pl.pallas_call(
        _rwkv_step_kernel,
        grid=(n // per,),
        in_specs=[pl.BlockSpec((h, HEAD_A, HEAD_A), lambda i: (i, 0, 0)),
                  lane_vec, lane_vec, lane_vec, lane_vec, lane_vec, col_vec, col_vec,
                  pl.BlockSpec((h, 1, HEAD_A), lambda i: (0, 0, 0)),
                  pl.BlockSpec((h, HEAD_A, 1), lambda i: (0, 0, 0)),
                  pl.BlockSpec((h, HEAD_A, 1), lambda i: (0, 0, 0))],
        out_specs=[pl.BlockSpec((h, HEAD_A, HEAD_A), lambda i: (i, 0, 0)), col_vec],
        out_shape=[jax.ShapeDtypeStruct((n * N_HEADS_A, HEAD_A, HEAD_A), F32),
                   jax.ShapeDtypeStruct((n * N_HEADS_A, HEAD_A, 1), F32)],
        compiler_params=_cparams(("arbitrary",)),
        name="rwkv_step",
    )(flat(state), flat(w), flat(kk), flat(b), flat(kf), flat(r), flat(v), flat(g),
      tile(r_k), tile(lnx_w), tile(lnx_b))
    return s_new.reshape(state.shape), o.reshape(n, N_HEADS_A, HEAD_A, 1)


def _rope(x, cos_t, sin_t, lane):
    partner = jnp.where(lane < ROT_DIM // 2, pltpu.roll(x, HEAD_B - ROT_DIM // 2, 1),
                        pltpu.roll(x, ROT_DIM // 2, 1))
    return x * cos_t + partner * sin_t


def _attn_proj_kernel(x_ref, lnw_ref, w_ref, cos_ref, sin_ref,
                      qh_ref, kh_ref, vh_ref, ko_ref, vo_ref, km_ref, xn_ref):
    j = pl.program_id(1)
    tm = x_ref.shape[0]
    slab = 2 * HEAD_B

    @pl.when(j == 0)
    def _():
        x = x_ref[...]
        ms = jnp.mean(x * x, axis=-1, keepdims=True)
        xn_ref[...] = ((x * lax.rsqrt(ms + NORM_EPS)) * lnw_ref[...]).astype(BF16)

    cos_t = cos_ref[...]
    sin_t = sin_ref[...]
    lane = lax.broadcasted_iota(jnp.int32, cos_t.shape, 1)

    def slabs():
        for sidx in range(D_B // slab):
            z = jnp.dot(xn_ref[...], w_ref[:, sidx * slab:(sidx + 1) * slab],
                        preferred_element_type=F32)
            for hh in range(slab // HEAD_B):
                h = sidx * (slab // HEAD_B) + hh
                yield h, z[:, hh * HEAD_B:(hh + 1) * HEAD_B]

    @pl.when(j == 0)
    def _():
        for h, z in slabs():
            qh_ref[h] = _rope(z, cos_t, sin_t, lane).astype(qh_ref.dtype)

    @pl.when(j == 1)
    def _():
        for h, z in slabs():
            sl = slice(h * HEAD_B, (h + 1) * HEAD_B)
            k = _rope(z, cos_t, sin_t, lane)
            kh_ref[h] = k.astype(kh_ref.dtype)
            ko_ref[:, sl] = k
            for blk in range(km_ref.shape[0]):
                rows = tm // km_ref.shape[0]
                km_ref[blk, :, sl] = jnp.mean(k[blk * rows:(blk + 1) * rows], axis=0, keepdims=True)

    @pl.when(j == 2)
    def _():
        for h, z in slabs():
            vo_ref[:, h * HEAD_B:(h + 1) * HEAD_B] = z
            vh_ref[h] = z.astype(vh_ref.dtype)


def attn_projection(x, ln_w, w_main, cos_t, sin_t, batch, seq):
    m = batch * seq
    blk = min(seq, MOBA_BLOCK)
    tm = min(seq, 512)
    assert seq % tm == 0 and tm % blk == 0
    nt = seq // tm
    sq = pl.Squeezed()
    tab = pl.BlockSpec((tm, HEAD_B), lambda i, j: (i % nt, 0))
    hm = pl.BlockSpec((sq, N_HEADS_B, tm, HEAD_B), lambda i, j: (i // nt, 0, i % nt, 0))
    rows = pl.BlockSpec((tm, D_B), lambda i, j: (i, 0))
    hm_shape = jax.ShapeDtypeStruct((batch, N_HEADS_B, seq, HEAD_B), BF16)
    return pl.pallas_call(
        _attn_proj_kernel,
        grid=(m // tm, 3),
        in_specs=[
            pl.BlockSpec((tm, D_MODEL), lambda i, j: (i, 0)),
            pl.BlockSpec((1, D_MODEL), lambda i, j: (0, 0)),
            pl.BlockSpec((D_MODEL, D_B), lambda i, j: (0, COL_Q // D_B + j)),
            tab, tab,
        ],
        out_specs=[hm, hm, hm, rows, rows,
                   pl.BlockSpec((tm // blk, 1, D_B), lambda i, j: (i, 0, 0))],
        out_shape=[hm_shape, hm_shape, hm_shape,
                   jax.ShapeDtypeStruct((m, D_B), F32), jax.ShapeDtypeStruct((m, D_B), F32),
                   jax.ShapeDtypeStruct((m // blk, 1, D_B), F32)],
        scratch_shapes=[pltpu.VMEM((tm, D_MODEL), BF16)],
        compiler_params=_cparams(("arbitrary", "arbitrary")),
        name="attn_projection",
    )(x, ln_w.reshape(1, -1), w_main, cos_t, sin_t)


def _rope_rows_kernel(zq_ref, zk_ref, cos_ref, sin_ref, q_ref, k_ref):
    cos_t = cos_ref[...]
    sin_t = sin_ref[...]
    lane = lax.broadcasted_iota(jnp.int32, cos_t.shape, 1)
    for h in range(N_HEADS_B):
        sl = slice(h * HEAD_B, (h + 1) * HEAD_B)
        q_ref[:, sl] = _rope(zq_ref[:, sl], cos_t, sin_t, lane).astype(BF16).astype(F32)
        k_ref[:, sl] = _rope(zk_ref[:, sl], cos_t, sin_t, lane)


def rope_rows(zq, zk, cos_t, sin_t):
    out = jax.ShapeDtypeStruct(zq.shape, F32)
    return pl.pallas_call(_rope_rows_kernel, out_shape=[out, out], name="rope_rows")(zq, zk, cos_t, sin_t)


MOBA_HEADS_PER_STEP = 4


def _moba_prompt_kernel(q_ref, k_ref, v_ref, km_ref, zg_ref, o_ref, sel_ref):
    qb = pl.program_id(1)
    nbatch, nhead, tq = q_ref.shape[0], q_ref.shape[1], q_ref.shape[2]
    nb = km_ref.shape[1]
    seqs = [(b, h) for b in range(nbatch) for h in range(nhead)]
    exp2_scale = HEAD_B ** -0.5 * LOG2_E
    qs = [q_ref[b, h].T for b, h in seqs]

    blk = lax.broadcasted_iota(jnp.int32, (nb, tq), 0)
    for i, (b, h) in enumerate(seqs):
        km = km_ref[b, :, h * HEAD_B:(h + 1) * HEAD_B]
        gate = sum(jnp.dot(piece, qs[i], preferred_element_type=F32)
                   for piece in _split3(km))
        gate = jnp.where(blk < qb, gate, -jnp.inf)
        for n in range(nb):
            g_n = gate[n:n + 1, :]
            tie = jnp.where(blk < n, 1.0, 0.0)
            beats = jnp.where(gate > g_n, 1.0, jnp.where(gate == g_n, tie, 0.0))
            cnt = jnp.sum(beats, axis=0, keepdims=True)
            sel_ref[i, n:n + 1, :] = jnp.where(cnt < MOBA_TOPK, 1.0, 0.0)

    def scores(i, n):
        b, h = seqs[i]
        start = pl.multiple_of(n * tq, tq)
        return jnp.dot(k_ref[b, h, pl.ds(start, tq), :], qs[i], preferred_element_type=F32)

    def softmax_step(s, m, l):
        m_new = jnp.maximum(m, jnp.max(s, axis=0, keepdims=True))
        alpha = jnp.exp2((m - m_new) * exp2_scale)
        p = jnp.exp2((s - m_new) * exp2_scale)
        return m_new, alpha, alpha * l + jnp.sum(p, axis=0, keepdims=True), p.astype(BF16)

    def pv(i, n, p):
        b, h = seqs[i]
        start = pl.multiple_of(n * tq, tq)
        return _bdot_tn(v_ref[b, h, pl.ds(start, tq), :], p)

    def step(n, masked, carry):
        sm = [softmax_step(masked[i], carry[i][0], carry[i][1]) for i in range(len(seqs))]
        acc = [sm[i][1] * carry[i][2] + pv(i, n, sm[i][3]) for i in range(len(seqs))]
        return [(sm[i][0], sm[i][2], acc[i]) for i in range(len(seqs))]

    def body(n, carry):
        s = [scores(i, n) for i in range(len(seqs))]
        masked = [jnp.where(sel_ref[i, pl.ds(n, 1), :] > 0.0, s[i], NEG)
                  for i in range(len(seqs))]
        return tuple(step(n, masked, carry))

    init = tuple((jnp.full((1, tq), NEG, F32), jnp.zeros((1, tq), F32),
                  jnp.zeros((HEAD_B, tq), F32)) for i in range(len(seqs)))
    carry = lax.fori_loop(0, qb, body, init)
    ki = lax.broadcasted_iota(jnp.int32, (tq, tq), 0)
    qi = lax.broadcasted_iota(jnp.int32, (tq, tq), 1)
    masked = [jnp.where(ki <= qi, scores(i, qb), NEG) for i in range(len(seqs))]
    final = step(qb, masked, list(carry))
    for i, (b, h) in enumerate(seqs):
        _, l, acc = final[i]
        sl = slice(h * HEAD_B, (h + 1) * HEAD_B)
        o_ref[b, :, sl] = ((acc / l).T * _silu(zg_ref[b, :, sl])).astype(o_ref.dtype)


def moba_prompt(q_hm, k_hm, v_hm, kmean, z_main):
    batch, _, seq, _ = k_hm.shape
    tq = min(seq, MOBA_BLOCK)
    nb = seq // tq
    hps = MOBA_HEADS_PER_STEP
    w = hps * HEAD_B
    return pl.pallas_call(
        _moba_prompt_kernel,
        grid=(N_HEADS_B // hps, nb),
        in_specs=[
            pl.BlockSpec((batch, hps, tq, HEAD_B), lambda h, i: (0, h, i, 0)),
            pl.BlockSpec((batch, hps, seq, HEAD_B), lambda h, i: (0, h, 0, 0)),
            pl.BlockSpec((batch, hps, seq, HEAD_B), lambda h, i: (0, h, 0, 0)),
            pl.BlockSpec((batch, nb, w), lambda h, i: (0, 0, h)),
            pl.BlockSpec((batch, tq, w), lambda h, i: (0, i, COL_GB // w + h)),
        ],
        out_specs=pl.BlockSpec((batch, tq, w), lambda h, i: (0, i, h)),
        out_shape=jax.ShapeDtypeStruct((batch, seq, D_B), BF16),
        scratch_shapes=[pltpu.VMEM((batch * hps, nb, tq), F32)],
        compiler_params=_cparams(("arbitrary", "arbitrary")),
        name="moba_prompt",
    )(q_hm, k_hm, v_hm, kmean, z_main)


KMEAN_PAGES = 8


def _cache_kmean_kernel(pt_ref, *refs, pages_per_block):
    del pt_ref
    for task in _page_block_mean_tasks(refs[:-1], refs[-1], pages_per_block):
        task()


def cache_kmean(cache_k, page_table, layer):
    n, n_pages = page_table.shape
    page = cache_k.shape[2]
    ppb = MOBA_BLOCK // page
    pps = min(KMEAN_PAGES, n_pages)
    assert n_pages % pps == 0 and pps % ppb == 0
    sq = pl.Squeezed()
    page_spec = lambda i: pl.BlockSpec(
        (sq, sq, page, N_HEADS_B, HEAD_B), lambda b, s, pt, i=i: (layer, pt[b, s * pps + i], 0, 0, 0))
    return pl.pallas_call(
        functools.partial(_cache_kmean_kernel, pages_per_block=ppb),
        grid_spec=pltpu.PrefetchScalarGridSpec(
            num_scalar_prefetch=1,
            grid=(n, n_pages // pps),
            in_specs=[page_spec(i) for i in range(pps)],
            out_specs=pl.BlockSpec((sq, pps // ppb, N_HEADS_B, HEAD_B), lambda b, s, pt: (b, s, 0, 0)),
        ),
        out_shape=jax.ShapeDtypeStruct((n, n_pages // ppb, N_HEADS_B, HEAD_B), F32),
        compiler_params=_cparams(("arbitrary", "arbitrary")),
        name="cache_kmean",
    )(page_table, *([cache_k] * pps))


def _sample_select_kernel(q_ref, km_ref, sel_ref):
    nb = km_ref.shape[0]
    ri = lax.broadcasted_iota(jnp.int32, (nb, nb), 0)
    ci = lax.broadcasted_iota(jnp.int32, (nb, nb), 1)
    lane = lax.broadcasted_iota(jnp.int32, (1, 128), 1)
    blk_row = lax.broadcasted_iota(jnp.int32, (1, nb), 1).astype(F32)
    for h in range(N_HEADS_B):
        km = km_ref[:, h, :]
        g_col = jnp.sum(km * q_ref[h:h + 1, :], axis=1, keepdims=True)
        g_row = jnp.sum(jnp.where(ri == ci, g_col, 0.0), axis=0, keepdims=True)
        beats = (g_col > g_row) | ((g_col == g_row) & (ri < ci))
        rank = jnp.sum(jnp.where(beats, 1.0, 0.0), axis=0, keepdims=True)
        out = jnp.zeros((1, 128), F32)
        for r in range(MOBA_TOPK):
            idx = jnp.sum(jnp.where(rank == float(r), blk_row, 0.0), axis=1, keepdims=True)
            out = jnp.where(lane == r, idx, out)
        sel_ref[h:h + 1, :] = out.astype(jnp.int32)


def sample_select(q_s, kmean_s):
    n, nb = kmean_s.shape[:2]
    sq = pl.Squeezed()
    return pl.pallas_call(
        _sample_select_kernel,
        grid=(n,),
        in_specs=[pl.BlockSpec((sq, N_HEADS_B, HEAD_B), lambda b: (b, 0, 0)),
                  pl.BlockSpec((sq, nb, N_HEADS_B, HEAD_B), lambda b: (b, 0, 0, 0))],
        out_specs=pl.BlockSpec((sq, N_HEADS_B, 128), lambda b: (b, 0, 0)),
        out_shape=jax.ShapeDtypeStruct((n, N_HEADS_B, 128), jnp.int32),
        compiler_params=_cparams(("arbitrary",)),
        name="sample_select",
    )(q_s, kmean_s)


def _sample_attn_kernel(sel_ref, pt_ref, q_ref, kn_ref, vn_ref, zg_ref, ck_ref, cv_ref, o_ref,
                        kbuf, vbuf, sem, *, layer, page, ppb):
    b = pl.program_id(0)
    nb_steps = pl.num_programs(0)
    n_sel = MOBA_TOPK * ppb
    scale = HEAD_B ** -0.5

    def copies(bb, slot):
        out = []
        for h in range(N_HEADS_B):
            for j in range(n_sel):
                pg = pt_ref[bb, sel_ref[bb, h, j // ppb] * ppb + j % ppb]
                dst = pl.ds(j * page, page)
                out.append(pltpu.make_async_copy(ck_ref.at[layer, pg, :, h, :],
                                                 kbuf.at[slot, h, dst, :], sem.at[slot, 0]))
                out.append(pltpu.make_async_copy(cv_ref.at[layer, pg, :, h, :],
                                                 vbuf.at[slot, h, dst, :], sem.at[slot, 1]))
        return out

    slot = b % 2

    @pl.when(b == 0)
    def _():
        for c in copies(0, 0):
            c.start()

    @pl.when(b + 1 < nb_steps)
    def _():
        for c in copies(b + 1, 1 - slot):
            c.start()

    for c in copies(b, slot):
        c.wait()

    for h in range(N_HEADS_B):
        q = q_ref[h]
        k = kbuf[slot, h]
        v = vbuf[slot, h]
        s = jnp.sum(k * q, axis=1, keepdims=True) * scale
        s_own = jnp.sum(kn_ref[h] * q, axis=1, keepdims=True) * scale
        m = jnp.maximum(jnp.max(s, axis=0, keepdims=True), s_own)
        p = jnp.exp(s - m)
        p_own = jnp.exp(s_own - m)
        l = jnp.sum(p, axis=0, keepdims=True) + p_own
        acc = jnp.sum(p * v, axis=0, keepdims=True) + p_own * vn_ref[h]
        o_ref[h] = (acc / l) * _silu(zg_ref[h])


def sample_attn(sel, page_table, q_s, k_new, v_new, zgb, cache_k, cache_v, layer):
    n = q_s.shape[0]
    page = cache_k.shape[2]
    ppb = MOBA_BLOCK // page
    rows = MOBA_TOPK * ppb * page
    sq = pl.Squeezed()
    vec = pl.BlockSpec((sq, N_HEADS_B, 1, HEAD_B), lambda b, sel, pt: (b, 0, 0, 0))
    hbm = pl.BlockSpec(memory_space=pl.ANY)
    return pl.pallas_call(
        functools.partial(_sample_attn_kernel, layer=layer, page=page, ppb=ppb),
        grid_spec=pltpu.PrefetchScalarGridSpec(
            num_scalar_prefetch=2,
            grid=(n,),
            in_specs=[vec, vec, vec, vec, hbm, hbm],
            out_specs=vec,
            scratch_shapes=[pltpu.VMEM((2, N_HEADS_B, rows, HEAD_B), F32),
                            pltpu.VMEM((2, N_HEADS_B, rows, HEAD_B), F32),
                            pltpu.SemaphoreType.DMA((2, 2))],
        ),
        out_shape=jax.ShapeDtypeStruct((n, N_HEADS_B, 1, HEAD_B), F32),
        compiler_params=_cparams(("arbitrary",)),
        name="sample_attn",
    )(sel, page_table, q_s, k_new, v_new, zgb, cache_k, cache_v)


def _merge_out_kernel(oa_ref, ob_ref, pa_ref, pb_ref, zga_ref, zgm_ref, wo_ref, x_ref, fw_ref, y_ref):
    ya = jnp.dot(oa_ref[...], pa_ref[...], preferred_element_type=F32)
    yb = jnp.dot(ob_ref[...], pb_ref[...], preferred_element_type=F32)
    merged = (_sigmoid(zga_ref[...]) * ya + _sigmoid(zgm_ref[...]) * yb).astype(BF16)
    h = x_ref[...] + jnp.dot(merged, wo_ref[...], preferred_element_type=F32)
    ms = jnp.mean(h * h, axis=-1, keepdims=True)
    y_ref[...] = (h * lax.rsqrt(ms + NORM_EPS)) * fw_ref[...]


def merge_and_output(o_a, o_b, p_a, p_b, z_main, w_o, x, final_w):
    m = x.shape[0]
    tm = min(m, 256)
    resident = pl.Buffered(1)
    return pl.pallas_call(
        _merge_out_kernel,
        grid=(m // tm,),
        in_specs=[
            pl.BlockSpec((tm, D_A), lambda i: (i, 0)),
            pl.BlockSpec((tm, D_B), lambda i: (i, 0)),
            pl.BlockSpec((D_A, D_MODEL), lambda i: (0, 0), pipeline_mode=resident),
            pl.BlockSpec((D_B, D_MODEL), lambda i: (0, 0), pipeline_mode=resident),
            pl.BlockSpec((tm, D_MODEL), lambda i: (i, COL_GA // D_MODEL)),
            pl.BlockSpec((tm, D_MODEL), lambda i: (i, COL_GM // D_MODEL)),
            pl.BlockSpec((D_MODEL, D_MODEL), lambda i: (0, 0), pipeline_mode=resident),
            pl.BlockSpec((tm, D_MODEL), lambda i: (i, 0)),
            pl.BlockSpec((1, D_MODEL), lambda i: (0, 0)),
        ],
        out_specs=pl.BlockSpec((tm, D_MODEL), lambda i: (i, 0)),
        out_shape=jax.ShapeDtypeStruct((m, D_MODEL), F32),
        compiler_params=_cparams(("arbitrary",)),
        name="merge_and_output",
    )(o_a, o_b, p_a, p_b, z_main, z_main, w_o, x, final_w.reshape(1, -1))


def _rope_tables(pos):
    half = ROT_DIM // 2
    inv = jnp.power(jnp.float32(ROPE_THETA), -jnp.arange(half, dtype=F32) * (2.0 / ROT_DIM))
    ang = pos.astype(F32)[:, None] * inv[None, :]
    cos, sin = jnp.cos(ang), jnp.sin(ang)
    n = pos.shape[0]
    rest = HEAD_B - ROT_DIM
    cos_t = jnp.concatenate([cos, cos, jnp.ones((n, rest), F32)], axis=1)
    sin_t = jnp.concatenate([-sin, sin, jnp.zeros((n, rest), F32)], axis=1)
    return cos_t, sin_t


def _reorder_rw(a):
    r, w_lo, k, v, a_lo, g = jnp.split(
        a, [D_A, D_A + LORA, 2 * D_A + LORA, 3 * D_A + LORA, 3 * D_A + 2 * LORA], axis=-1)
    return jnp.concatenate([r, k, v, g], axis=-1), jnp.concatenate([w_lo, a_lo], axis=-1)


def kernel(x_prompt, x_sample, state_shift, state_wkv, cache_k, cache_v, page_table, ln_w, w_in, mu,
           w0, w2, a0, a2, k_k, k_a, r_k, lnx_w, lnx_b, p_a, p_b, w_o, final_w):
    depth = ln_w.shape[0]
    assert depth == 1, "single-layer trunk"
    B, T, _ = x_prompt.shape
    DB, TS, _ = x_sample.shape
    assert TS == 1
    n_pages = page_table.shape[1]
    page = cache_k.shape[2]
    past = n_pages * page
    assert past % MOBA_BLOCK == 0 and MOBA_BLOCK % page == 0
    l = 0
    rw_cols = 4 * D_A + 2 * LORA

    att0 = rw_cols
    src_cols = [0, D_A + LORA, 2 * D_A + LORA, 3 * D_A + 2 * LORA]
    src_cols += [att0 + 4 * D_B + i * W_TILE for i in range(2 * D_MODEL // W_TILE)]
    src_cols += [att0 + 3 * D_B, att0, att0 + D_B, att0 + 2 * D_B]
    w_main = prepare_projection_weight(w_in, l, src_cols)
    w_wa = jnp.concatenate([w_in[l, :, D_A:D_A + LORA],
                            w_in[l, :, 3 * D_A + LORA:3 * D_A + 2 * LORA]], axis=1)
    mu_rw, mu_wa = _reorder_rw(mu[l][None, :])
    zeros = jnp.zeros((LORA, D_A), F32)
    w2p = jnp.concatenate([w2[l], zeros], axis=0).astype(BF16)
    a2p = jnp.concatenate([zeros, a2[l]], axis=0).astype(BF16)
    row = lambda a: a.reshape(1, -1)
    vecs = (row(w0[l]), row(a0[l]), row(k_k[l]), row(k_a[l]), row(r_k[l]), row(lnx_w[l]), row(lnx_b[l]))
    pa_bf, pb_bf, wo_bf = p_a[l].astype(BF16), p_b[l].astype(BF16), w_o[l].astype(BF16)

    rows = jnp.concatenate([x_prompt[:, -1, :], x_sample[:, 0, :]], axis=0)
    xn_rows = rmsnorm_rows(rows, ln_w[l])
    shift_prompt = xn_rows[:B]
    xn_s = xn_rows[B:]
    xp = x_prompt.reshape(B * T, D_MODEL)
    z_p, zwa_p = input_projection(xp, ln_w[l], w_main, w_wa, normalize=True, n_cols=N_Z)
    z_s2, zwa_s2 = input_projection(jnp.concatenate([xn_s, state_shift[l]], axis=0), ln_w[l],
                                    w_main, w_wa, normalize=False, n_cols=N_MAIN)

    z_p3 = z_p.reshape(B, T, N_Z)
    o_a_p, wkv_p, kmean_s = rwkv_prompt(z_p3, zwa_p.reshape(B, T, 2 * LORA), mu_rw, mu_wa, vecs, w2p, a2p,
                                        cache_k, page_table, l)
    cos_p, sin_p = _rope_tables(jnp.arange(T))
    q_hm, k_hm, v_hm, k_rows_p, v_rows_p, kmean_p = attn_projection(xp, ln_w[l], w_main, cos_p, sin_p, B, T)
    nb_p = T // min(T, MOBA_BLOCK)
    o_b_p = moba_prompt(q_hm, k_hm, v_hm, kmean_p.reshape(B, nb_p, D_B), z_p3)
    y_prompt = merge_and_output(o_a_p.reshape(B * T, D_A), o_b_p.reshape(B * T, D_B), pa_bf, pb_bf, z_p,
                                wo_bf, xp, final_w).reshape(B, T, D_MODEL)

    z_s, zprev_s = z_s2[:DB], z_s2[DB:]
    r_s, kf_s, v_s, kk_s, b_s, w_s, g_s = rwkv_rows(
        z_s[:, :4 * D_A], zprev_s[:, :4 * D_A], zwa_s2[:DB], zwa_s2[DB:], mu_rw, mu_wa,
        vecs[0], vecs[1], vecs[2], vecs[3], w2p, a2p)
    hs = lambda a: a.reshape(-1, N_HEADS_A, 1, HEAD_A)
    col = lambda a: a.reshape(-1, N_HEADS_A, HEAD_A, 1)
    wkv_s, o_a_s = rwkv_step(state_wkv[l], hs(w_s), hs(kk_s), hs(b_s), hs(kf_s), hs(r_s),
                             col(v_s), col(g_s), hs(r_k[l])[0], col(lnx_w[l])[0], col(lnx_b[l])[0])
    o_a_s = o_a_s.reshape(DB, D_A).astype(BF16)

    cos_s, sin_s = _rope_tables(past + jnp.arange(TS))
    cos_s = jnp.broadcast_to(cos_s, (DB, HEAD_B))
    sin_s = jnp.broadcast_to(sin_s, (DB, HEAD_B))
    q_rows_s, k_rows_s = rope_rows(z_s[:, COL_Q:COL_Q + D_B], z_s[:, COL_KB:COL_KB + D_B], cos_s, sin_s)
    v_rows_s = z_s[:, COL_VB:COL_VB + D_B]
    if kmean_s is None:
        kmean_s = cache_kmean(cache_k, page_table, l)
    sel = sample_select(q_rows_s.reshape(DB, N_HEADS_B, HEAD_B), kmean_s)[:, :, :MOBA_TOPK]
    hv = lambda a: a.reshape(DB, N_HEADS_B, 1, HEAD_B)
    o_b_s = sample_attn(sel, page_table, hv(q_rows_s), hv(k_rows_s), hv(v_rows_s),
                        hv(z_s[:, COL_GB:COL_GB + D_B]), cache_k, cache_v, l)
    o_b_s = o_b_s.reshape(DB, D_B).astype(BF16)
    y_sample = merge_and_output(o_a_s, o_b_s, pa_bf, pb_bf, z_s, wo_bf, x_sample.reshape(DB, D_MODEL),
                                final_w)

    return (y_prompt,
            y_sample.reshape(DB, TS, D_MODEL),
            shift_prompt[None],
            wkv_p.reshape(1, B, N_HEADS_A, HEAD_A, HEAD_A),
            k_rows_p.reshape(1, B, T, N_HEADS_B, HEAD_B),
            v_rows_p.reshape(1, B, T, N_HEADS_B, HEAD_B),
            xn_s[None],
            wkv_s[None],
            k_rows_s.reshape(1, DB, TS, N_HEADS_B, HEAD_B),
            v_rows_s.reshape(1, DB, TS, N_HEADS_B, HEAD_B))
```

```python
import functools

import jax
import jax.numpy as jnp
from jax import lax
from jax.experimental import pallas as pl
from jax.experimental.pallas import tpu as pltpu

F32 = jnp.float32
BF16 = jnp.bfloat16

D_MODEL = 2048
D_A = D_MODEL // 2
HEAD_A = 64
N_HEADS_A = D_A // HEAD_A
LORA = 64
D_B = D_MODEL // 2
HEAD_B = 128
N_HEADS_B = D_B // HEAD_B
ROT_DIM = HEAD_B // 4
ROPE_THETA = 500000.0
MOBA_BLOCK = 256
MOBA_TOPK = 3
NORM_EPS = 1e-6
GN_EPS = 64e-5
NEG = -1e30
LOG2_E = 1.4426950408889634

COL_R, COL_K, COL_V, COL_G = 0, D_A, 2 * D_A, 3 * D_A
COL_GA = 4 * D_A
COL_GM = COL_GA + D_MODEL
COL_GB = COL_GM + D_MODEL
N_Z = COL_GB + D_B
COL_Q, COL_KB, COL_VB = N_Z, N_Z + D_B, N_Z + 2 * D_B
N_MAIN = COL_VB + D_B

GROUP = 256
HEADS_PER_GROUP = GROUP // HEAD_A
CHUNK = 64
assert CHUNK == HEAD_A
VMEM_LIMIT = 56 * 1024 * 1024


def _cparams(sem):
    return pltpu.CompilerParams(dimension_semantics=sem, vmem_limit_bytes=VMEM_LIMIT)


def _bdot(a, b):
    return jnp.dot(a.astype(BF16), b.astype(BF16), preferred_element_type=F32)


def _bdot_nt(a, b):
    return lax.dot_general(a.astype(BF16), b.astype(BF16), (((1,), (1,)), ((), ())),
                           preferred_element_type=F32)


def _bdot_tn(a, b):
    return lax.dot_general(a.astype(BF16), b.astype(BF16), (((0,), (0,)), ((), ())),
                           preferred_element_type=F32)


def _split3(x):
    hi = x.astype(BF16)
    r1 = x - hi.astype(F32)
    mid = r1.astype(BF16)
    lo = (r1 - mid.astype(F32)).astype(BF16)
    return hi, mid, lo


def _sigmoid(x):
    return 1.0 / (1.0 + jnp.exp(-x))


def _silu(x):
    return x * _sigmoid(x)


def _softplus(x):
    return jnp.maximum(x, 0.0) + jnp.log(1.0 + jnp.exp(-jnp.abs(x)))


def _rmsnorm_rows_kernel(x_ref, w_ref, o_ref):
    x = x_ref[...]
    ms = jnp.mean(x * x, axis=-1, keepdims=True)
    o_ref[...] = (x * lax.rsqrt(ms + NORM_EPS)) * w_ref[...]


def rmsnorm_rows(x, w):
    return pl.pallas_call(
        _rmsnorm_rows_kernel,
        out_shape=jax.ShapeDtypeStruct(x.shape, F32),
        name="rmsnorm_rows",
    )(x, w.reshape(1, -1))


W_TILE = 1024
LANES = 128


def _w_prep_kernel(w_hbm, o_hbm, in_buf, out_buf, in_sem, out_sem, *, layer, starts):
    n = len(starts)
    rows = in_buf.shape[1]

    def in_copy(j, slot):
        off = starts[j] % LANES
        width = W_TILE + (LANES if off else 0)
        return pltpu.make_async_copy(w_hbm.at[layer, :, pl.ds(starts[j] - off, width)],
                                     in_buf.at[slot, :, pl.ds(0, width)], in_sem.at[slot])

    def out_copy(j, slot):
        return pltpu.make_async_copy(out_buf.at[slot], o_hbm.at[:, pl.ds(j * W_TILE, W_TILE)],
                                     out_sem.at[slot])

    in_copy(0, 0).start()
    for j in range(n):
        slot = j % 2
        if j + 1 < n:
            in_copy(j + 1, 1 - slot).start()
        in_copy(j, slot).wait()
        if j >= 2:
            out_copy(j - 2, slot).wait()
        off = starts[j] % LANES
        rc = 256

        def chunk(c, _, slot=slot, off=off):
            r0 = pl.multiple_of(c * rc, rc)
            out_buf[slot, pl.ds(r0, rc), :] = in_buf[slot, pl.ds(r0, rc), off:off + W_TILE].astype(BF16)
            return 0

        lax.fori_loop(0, rows // rc, chunk, 0)
        out_copy(j, slot).start()
    for j in range(max(n - 2, 0), n):
        out_copy(j, j % 2).wait()


def prepare_projection_weight(w_in, layer, starts):
    rows = w_in.shape[1]
    assert rows % 256 == 0
    hbm = pl.BlockSpec(memory_space=pl.ANY)
    return pl.pallas_call(
        functools.partial(_w_prep_kernel, layer=layer, starts=tuple(starts)),
        in_specs=[hbm],
        out_specs=hbm,
        out_shape=jax.ShapeDtypeStruct((rows, len(starts) * W_TILE), BF16),
        scratch_shapes=[pltpu.VMEM((2, rows, W_TILE + LANES), F32),
                        pltpu.VMEM((2, rows, W_TILE), BF16),
                        pltpu.SemaphoreType.DMA((2,)), pltpu.SemaphoreType.DMA((2,))],
        compiler_params=pltpu.CompilerParams(vmem_limit_bytes=VMEM_LIMIT),
        name="prepare_projection_weight",
    )(w_in)


def _proj_kernel(x_ref, lnw_ref, w_ref, wwa_ref, z_ref, zwa_ref, xn_ref, *, normalize):
    @pl.when(pl.program_id(1) == 0)
    def _():
        x = x_ref[...]
        if normalize:
            ms = jnp.mean(x * x, axis=-1, keepdims=True)
            x = (x * lax.rsqrt(ms + NORM_EPS)) * lnw_ref[...]
        xn_ref[...] = x.astype(BF16)
        zwa_ref[...] = jnp.dot(xn_ref[...], wwa_ref[...].astype(BF16), preferred_element_type=F32)

    z_ref[...] = jnp.dot(xn_ref[...], w_ref[...], preferred_element_type=F32)


def input_projection(x, ln_w, w_main, w_wa, *, normalize, n_cols):
    m = x.shape[0]
    tm = min(m, 1024)
    if m < 1024:
        tn = 2048
    else:
        tn = 1536 if n_cols % 1536 == 0 else 1024
    assert m % tm == 0 and n_cols % tn == 0
    return pl.pallas_call(
        functools.partial(_proj_kernel, normalize=normalize),
        grid=(m // tm, n_cols // tn),
        in_specs=[
            pl.BlockSpec((tm, D_MODEL), lambda i, j: (i, 0)),
            pl.BlockSpec((1, D_MODEL), lambda i, j: (0, 0)),
            pl.BlockSpec((D_MODEL, tn), lambda i, j: (0, j)),
            pl.BlockSpec((D_MODEL, 2 * LORA), lambda i, j: (0, 0)),
        ],
        out_specs=[
            pl.BlockSpec((tm, tn), lambda i, j: (i, j)),
            pl.BlockSpec((tm, 2 * LORA), lambda i, j: (i, 0)),
        ],
        out_shape=[
            jax.ShapeDtypeStruct((m, n_cols), F32),
            jax.ShapeDtypeStruct((m, 2 * LORA), F32),
        ],
        scratch_shapes=[pltpu.VMEM((tm, D_MODEL), BF16)],
        compiler_params=_cparams(("arbitrary", "arbitrary")),
        name="input_projection",
    )(x, ln_w.reshape(1, -1), w_main, w_wa)


def _head_mask(n):
    r = lax.broadcasted_iota(jnp.int32, (n, n), 0) // HEAD_A
    c = lax.broadcasted_iota(jnp.int32, (n, n), 1) // HEAD_A
    return r == c


def _segsum(x, ones_bd):
    return jnp.dot(x.astype(BF16), ones_bd, preferred_element_type=F32)


def _rwkv_prep(zm_r, zm_k, zm_v, zm_wa, w0, a0, k_k, k_a, w2p, a2p, ones_bd):
    lane = lax.broadcasted_iota(jnp.int32, zm_wa.shape, 1)
    lora_in = jnp.where(lane < LORA, jnp.tanh(zm_wa), zm_wa)
    ww = _bdot(lora_in, w2p)
    aa = _bdot(lora_in, a2p)
    w_log = -_softplus(-(w0 + ww)) - 0.5
    logw = -jnp.exp(w_log)
    a = _sigmoid(a0 + aa)
    kk = zm_k * k_k
    ss = _segsum(kk * kk, ones_bd)
    kk = kk / jnp.maximum(jnp.sqrt(ss), 1e-12)
    kf = zm_k * (1.0 + (a - 1.0) * k_a)
    return zm_r, kf, zm_v, kk, kk * a, logw


def _rwkv_post(y, r, kf, v, zm_g, r_k, lnx_w, lnx_b, ones_bd):
    inv_n = 1.0 / HEAD_A
    mean = _segsum(y, ones_bd) * inv_n
    d = y - mean
    var = _segsum(d * d, ones_bd) * inv_n
    yn = d * lax.rsqrt(var + GN_EPS) * lnx_w + lnx_b
    bonus = _segsum(r * kf * r_k, ones_bd) * v
    return (yn + bonus) * _silu(zm_g)


def _stack_heads(x, lane_head):
    return jnp.concatenate(
        [jnp.where(lane_head == h, x, 0.0) for h in range(HEADS_PER_GROUP)], axis=0)


def _chunk_precompute(insts, consts, side_tasks=()):
    strict, incl, eye, lane_head, bd_mask = consts
    c = CHUNK
    n = len(insts)
    side_tasks = list(side_tasks)

    def run_side(k=1):
        for _ in range(k):
            if side_tasks:
                side_tasks.pop(0)()

    def bd(p):
        return jnp.where(bd_mask, jnp.concatenate([p] * HEADS_PER_GROUP, axis=0), 0.0).astype(BF16)

    lhs, wt, v_bd, bdkd, p_end = [], [], [], [], []
    for r, kf, v, kk, b, logw, cl in insts:
        cl_last = cl[c - 1:c, :]
        e_neg = jnp.exp(-cl)
        e_end = jnp.exp(cl_last - cl)
        lhs.append(jnp.concatenate([kk * jnp.exp(cl - logw), r * jnp.exp(cl)], axis=0).astype(BF16))
        wt.append(jnp.concatenate([_stack_heads(b * e_neg, lane_head),
                                   _stack_heads(kf * e_neg, lane_head)], axis=0).astype(BF16))
        v_bd.append(_stack_heads(v, lane_head).astype(BF16))
        bdkd.append(jnp.concatenate([b * e_end, kf * e_end], axis=0).astype(BF16))
        p_end.append(jnp.exp(cl_last))
    att = [_bdot_nt(lhs[i], wt[i]) for i in range(n)]
    run_side()
    a_ab = [jnp.where(strict, att[i][:c, :4 * c], 0.0) for i in range(n)]
    a_ak = [jnp.where(strict, att[i][:c, 4 * c:], 0.0) for i in range(n)]
    m_cat = [jnp.concatenate([jnp.where(incl, att[i][c:, :4 * c], 0.0),
                              jnp.where(incl, att[i][c:, 4 * c:], 0.0)], axis=1).astype(BF16)
             for i in range(n)]
    av = [_bdot(a_ak[i], v_bd[i]) for i in range(n)]
    run_side()

    x = [eye - a_ab[i] for i in range(n)]
    p = [_bdot(a_ab[i], bd(a_ab[i])) for i in range(n)]
    run_side()
    for _ in range(4):
        px = [_bdot(jnp.concatenate([p[i], x[i]], axis=0), bd(p[i])) for i in range(n)]
        p = [px[i][:c] for i in range(n)]
        x = [x[i] + px[i][c:] for i in range(n)]
        run_side()
    x = [(x[i] + _bdot(x[i], bd(p[i]))).astype(BF16) for i in range(n)]
    run_side(len(side_tasks))
    return [dict(lhs=lhs[i], av=av[i], t=x[i], m_cat=m_cat[i], v_bd=v_bd[i], v=insts[i][2],
                 bdkd=bdkd[i], p_end=p_end[i]) for i in range(n)]


def _chunk_state_step(states, pres, consts):
    lane_head, bd_mask = consts[3], consts[4]
    c = CHUNK
    n = len(states)
    sh = [_bdot_nt(pres[i]["lhs"], states[i]) for i in range(n)]
    rhs = [sh[i][:c] + pres[i]["av"] for i in range(n)]
    u = [-_bdot(pres[i]["t"], _stack_heads(rhs[i], lane_head)) for i in range(n)]
    y = [sh[i][c:] + _bdot(pres[i]["m_cat"],
                           jnp.concatenate([_stack_heads(u[i], lane_head).astype(BF16),
                                            pres[i]["v_bd"]], axis=0)) for i in range(n)]
    upd = [_bdot_tn(jnp.concatenate([u[i], pres[i]["v"]], axis=0), pres[i]["bdkd"]) for i in range(n)]
    new = [states[i] * pres[i]["p_end"] + jnp.where(bd_mask, upd[i], 0.0) for i in range(n)]
    return y, new


def _page_block_mean_tasks(page_refs, km_ref, pages_per_block):
    def task(j):
        def run():
            blk = jnp.sum(page_refs[j * pages_per_block][...], axis=0)
            for i in range(1, pages_per_block):
                blk = blk + jnp.sum(page_refs[j * pages_per_block + i][...], axis=0)
            km_ref[j] = blk * (1.0 / MOBA_BLOCK)
        return run
    return [task(j) for j in range(len(page_refs) // pages_per_block)]


def _rwkv_prompt_kernel(*refs, n_pages, pages_per_block):
    if n_pages:
        refs = refs[1:]
    (zr_ref, zk_ref, zv_ref, zg_ref, zwa_ref, mur_ref, muk_ref, muv_ref, mug_ref, muwa_ref,
     w0_ref, a0_ref, kk_ref, ka_ref, rk_ref, lnw_ref, lnb_ref, w2_ref, a2_ref) = refs[:19]
    page_refs = refs[19:19 + n_pages]
    refs = refs[19 + n_pages:]
    o_ref, s_out_ref = refs[:2]
    side_tasks = []
    if n_pages:
        side_tasks = _page_block_mean_tasks(page_refs, refs[2], pages_per_block)
        refs = refs[1:]
    s_ref, pr_ref, pk_ref, pv_ref, pg_ref, pwa_ref = refs[2:]
    t = pl.program_id(1)
    nseq, tr = zr_ref.shape[0], zr_ref.shape[1]

    @pl.when(t == 0)
    def _():
        for ref in (s_ref, pr_ref, pk_ref, pv_ref, pg_ref, pwa_ref):
            ref[...] = jnp.zeros_like(ref)

    def shifted(z_ref, prev_ref, mu_ref, i):
        z = z_ref[i]
        row = lax.broadcasted_iota(jnp.int32, z.shape, 0)
        prev = jnp.where(row == 0, prev_ref[i], pltpu.roll(z, 1, 0))
        prev_ref[i] = z[tr - 1:tr, :]
        return z + (prev - z) * mu_ref[...]

    bd_mask = _head_mask(GROUP)
    ones_bd = jnp.where(bd_mask, 1.0, 0.0).astype(BF16)
    ri = lax.broadcasted_iota(jnp.int32, (tr, tr), 0)
    ci = lax.broadcasted_iota(jnp.int32, (tr, tr), 1)
    tri = jnp.where((ri // CHUNK == ci // CHUNK) & (ci <= ri), 1.0, 0.0).astype(BF16)
    rc = lax.broadcasted_iota(jnp.int32, (CHUNK, GROUP), 0)
    lc = lax.broadcasted_iota(jnp.int32, (CHUNK, GROUP), 1)
    sc = lc % CHUNK
    consts = (sc < rc, sc <= rc, jnp.where(sc == rc, 1.0, 0.0), lc // HEAD_A, bd_mask)
    last = t == pl.num_programs(1) - 1

    n_ch = tr // CHUNK
    prepped, insts = [], []
    for i in range(nseq):
        zm_r = shifted(zr_ref, pr_ref, mur_ref, i)
        zm_k = shifted(zk_ref, pk_ref, muk_ref, i)
        zm_v = shifted(zv_ref, pv_ref, muv_ref, i)
        zm_g = shifted(zg_ref, pg_ref, mug_ref, i)
        zm_wa = shifted(zwa_ref, pwa_ref, muwa_ref, i)
        r, kf, v, kk, b, logw = _rwkv_prep(zm_r, zm_k, zm_v, zm_wa, w0_ref[...], a0_ref[...],
                                           kk_ref[...], ka_ref[...], w2_ref[...], a2_ref[...], ones_bd)
        cl = sum(jnp.dot(tri, piece, preferred_element_type=F32) for piece in _split3(logw))
        prepped.append((r, kf, v, zm_g))
        for ch in range(n_ch):
            sl = slice(ch * CHUNK, (ch + 1) * CHUNK)
            insts.append((r[sl], kf[sl], v[sl], kk[sl], b[sl], logw[sl], cl[sl]))
    pres = _chunk_precompute(insts, consts, side_tasks)

    states = [s_ref[i] for i in range(nseq)]
    ys = [[] for _ in range(nseq)]
    for ch in range(n_ch):
        y, states = _chunk_state_step(states, [pres[i * n_ch + ch] for i in range(nseq)], consts)
        for i in range(nseq):
            ys[i].append(y[i])
    for i in range(nseq):
        s_ref[i] = states[i]
        r, kf, v, zm_g = prepped[i]
        o_ref[i] = _rwkv_post(jnp.concatenate(ys[i], axis=0), r, kf, v, zm_g, rk_ref[...],
                              lnw_ref[...], lnb_ref[...], ones_bd).astype(o_ref.dtype)

    @pl.when(last)
    def _():
        for i in range(nseq):
            for h in range(HEADS_PER_GROUP):
                s_out_ref[i, h] = s_ref[i, h * HEAD_A:(h + 1) * HEAD_A, h * HEAD_A:(h + 1) * HEAD_A]


def rwkv_prompt(z_main, z_wa, mu_main, mu_wa, vecs, w2p, a2p, cache_k=None, page_table=None, layer=0):
    w0, a0, k_k, k_a, r_k, lnx_w, lnx_b = vecs
    batch, seq, _ = z_main.shape
    tr = min(seq, 256)
    assert seq % tr == 0 and tr % CHUNK == 0
    nt = seq // tr
    ng = D_A // GROUP

    n_pages = ppb = 0
    if cache_k is not None:
        page = cache_k.shape[2]
        ppb = MOBA_BLOCK // page
        total = page_table.size
        if total % (ng * nt) == 0 and (total // (ng * nt)) % ppb == 0:
            n_pages = total // (ng * nt)

    def zspec(col):
        return pl.BlockSpec((batch, tr, GROUP), lambda g, t, *_, c=col // GROUP: (0, t, c + g))

    def vspec(col=0):
        return pl.BlockSpec((1, GROUP), lambda g, t, *_, c=col // GROUP: (0, c + g))

    in_specs = [
        zspec(COL_R), zspec(COL_K), zspec(COL_V), zspec(COL_G),
        pl.BlockSpec((batch, tr, 2 * LORA), lambda g, t, *_: (0, t, 0)),
        vspec(COL_R), vspec(COL_K), vspec(COL_V), vspec(COL_G),
        pl.BlockSpec((1, 2 * LORA), lambda g, t, *_: (0, 0)),
        vspec(), vspec(), vspec(), vspec(), vspec(), vspec(), vspec(),
        pl.BlockSpec((2 * LORA, GROUP), lambda g, t, *_: (0, g)),
        pl.BlockSpec((2 * LORA, GROUP), lambda g, t, *_: (0, g)),
    ]
    out_specs = [
        pl.BlockSpec((batch, tr, GROUP), lambda g, t, *_: (0, t, g)),
        pl.BlockSpec((batch, HEADS_PER_GROUP, HEAD_A, HEAD_A), lambda g, t, *_: (0, g, 0, 0)),
    ]
    out_shape = [
        jax.ShapeDtypeStruct((batch, seq, D_A), BF16),
        jax.ShapeDtypeStruct((batch, N_HEADS_A, HEAD_A, HEAD_A), F32),
    ]
    operands = [z_main, z_main, z_main, z_main, z_wa,
                mu_main, mu_main, mu_main, mu_main, mu_wa,
                w0, a0, k_k, k_a, r_k, lnx_w, lnx_b, w2p, a2p]
    if n_pages:
        sq = pl.Squeezed()
        for i in range(n_pages):
            in_specs.append(pl.BlockSpec(
                (sq, sq, page, N_HEADS_B, HEAD_B),
                lambda g, t, pt, i=i: (layer, pt[(g * nt + t) * n_pages + i], 0, 0, 0)))
        operands += [cache_k] * n_pages
        out_specs.append(pl.BlockSpec((n_pages // ppb, N_HEADS_B, HEAD_B),
                                      lambda g, t, pt: (g * nt + t, 0, 0)))
        out_shape.append(jax.ShapeDtypeStruct((page_table.size // ppb, N_HEADS_B, HEAD_B), F32))
    row = lambda n: pltpu.VMEM((batch, 1, n), F32)
    scratch = [pltpu.VMEM((batch, GROUP, GROUP), F32),
               row(GROUP), row(GROUP), row(GROUP), row(GROUP), row(2 * LORA)]
    kern = functools.partial(_rwkv_prompt_kernel, n_pages=n_pages, pages_per_block=ppb)
    params = _cparams(("arbitrary", "arbitrary"))
    if not n_pages:
        o_a, wkv = pl.pallas_call(kern, grid=(ng, nt), in_specs=in_specs, out_specs=out_specs,
                                  out_shape=out_shape, scratch_shapes=scratch,
                                  compiler_params=params, name="rwkv_prompt")(*operands)
        return o_a, wkv, None
    o_a, wkv, kmean = pl.pallas_call(
        kern,
        grid_spec=pltpu.PrefetchScalarGridSpec(
            num_scalar_prefetch=1, grid=(ng, nt), in_specs=in_specs, out_specs=out_specs,
            scratch_shapes=scratch),
        out_shape=out_shape, compiler_params=params, name="rwkv_prompt",
    )(page_table.reshape(-1), *operands)
    n = page_table.shape[0]
    return o_a, wkv, kmean.reshape(n, -1, N_HEADS_B, HEAD_B)


def _rwkv_rows_kernel(z_ref, zp_ref, zwa_ref, zpwa_ref, mu_ref, muwa_ref,
                      w0_ref, a0_ref, kk_ref, ka_ref, w2_ref, a2_ref,
                      r_ref, kf_ref, v_ref, kkn_ref, b_ref, w_ref, g_ref):
    def shifted(z, zp, mu):
        return z + (zp - z) * mu

    mu = mu_ref[...]
    z = z_ref[...]
    zp = zp_ref[...]
    zm = [shifted(z[:, c:c + D_A], zp[:, c:c + D_A], mu[:, c:c + D_A])
          for c in (COL_R, COL_K, COL_V, COL_G)]
    zm_wa = shifted(zwa_ref[...], zpwa_ref[...], muwa_ref[...])
    ones_bd = jnp.where(_head_mask(D_A), 1.0, 0.0).astype(BF16)
    r, kf, v, kk, b, logw = _rwkv_prep(zm[0], zm[1], zm[2], zm_wa, w0_ref[...], a0_ref[...],
                                       kk_ref[...], ka_ref[...], w2_ref[...], a2_ref[...], ones_bd)
    r_ref[...] = r
    kf_ref[...] = kf
    kkn_ref[...] = kk
    b_ref[...] = b
    w_ref[...] = jnp.exp(logw)
    v_t = v.T
    g_t = zm[3].T
    for i in range(v_ref.shape[0]):
        v_ref[i] = v_t[:, i:i + 1]
        g_ref[i] = g_t[:, i:i + 1]


def rwkv_rows(z_rw, zp_rw, z_wa, zp_wa, mu_rw, mu_wa, w0, a0, k_k, k_a, w2p, a2p):
    n = z_rw.shape[0]
    out = jax.ShapeDtypeStruct((n, D_A), F32)
    col = jax.ShapeDtypeStruct((n, D_A, 1), F32)
    return pl.pallas_call(
        _rwkv_rows_kernel,
        out_shape=[out, out, col, out, out, out, col],
        compiler_params=pltpu.CompilerParams(vmem_limit_bytes=VMEM_LIMIT),
        name="rwkv_rows",
    )(z_rw, zp_rw, z_wa, zp_wa, mu_rw, mu_wa, w0, a0, k_k, k_a, w2p, a2p)


def _rwkv_step_kernel(s_ref, w_ref, kk_ref, b_ref, kf_ref, r_ref, v_ref, g_ref,
                      rk_ref, lnw_ref, lnb_ref, s_out_ref, o_ref):
    S = s_ref[...]
    w = w_ref[...]
    kk = kk_ref[...]
    b = b_ref[...]
    kf = kf_ref[...]
    r = r_ref[...]
    v = v_ref[...]
    sa = -jnp.sum(S * kk, axis=-1, keepdims=True)
    S = S * w + sa * b + v * kf
    s_out_ref[...] = S
    y = jnp.sum(S * r, axis=-1, keepdims=True)
    mean = jnp.mean(y, axis=1, keepdims=True)
    d = y - mean
    var = jnp.mean(d * d, axis=1, keepdims=True)
    yn = d * lax.rsqrt(var + GN_EPS) * lnw_ref[...] + lnb_ref[...]
    bonus = jnp.sum(r * kf * rk_ref[...], axis=-1, keepdims=True) * v
    o_ref[...] = (yn + bonus) * _silu(g_ref[...])


def rwkv_step(state, w, kk, b, kf, r, v, g, r_k, lnx_w, lnx_b):
    n = state.shape[0]
    per = 4 if n % 4 == 0 else 1
    h = per * N_HEADS_A
    flat = lambda a: a.reshape((n * N_HEADS_A,) + a.shape[2:])
    tile = lambda a: jnp.tile(a, (per, 1, 1))
    lane_vec = pl.BlockSpec((h, 1, HEAD_A), lambda i: (i, 0, 0))
    col_vec = pl.BlockSpec((h, HEAD_A, 1), lambda i: (i, 0, 0))
    s_new, o = pl.pallas_call(
        _rwkv_step_kernel,
        grid=(n // per,),
        in_specs=[pl.BlockSpec((h, HEAD_A, HEAD_A), lambda i: (i, 0, 0)),
                  lane_vec, lane_vec, lane_vec, lane_vec, lane_vec, col_vec, col_vec,
                  pl.BlockSpec((h, 1, HEAD_A), lambda i: (0, 0, 0)),
                  pl.BlockSpec((h, HEAD_A, 1), lambda i: (0, 0, 0)),
                  pl.BlockSpec((h, HEAD_A, 1), lambda i: (0, 0, 0))],
        out_specs=[pl.BlockSpec((h, HEAD_A, HEAD_A), lambda i: (i, 0, 0)), col_vec],
        out_shape=[jax.ShapeDtypeStruct((n * N_HEADS_A, HEAD_A, HEAD_A), F32),
                   jax.ShapeDtypeStruct((n * N_HEADS_A, HEAD_A, 1), F32)],
        compiler_params=_cparams(("arbitrary",)),
        name="rwkv_step",
    )(flat(state), flat(w), flat(kk), flat(b), flat(kf), flat(r), flat(v), flat(g),
      tile(r_k), tile(lnx_w), tile(lnx_b))
    return s_new.reshape(state.shape), o.reshape(n, N_HEADS_A, HEAD_A, 1)


def _rope(x, cos_t, sin_t, lane):
    partner = jnp.where(lane < ROT_DIM // 2, pltpu.roll(x, HEAD_B - ROT_DIM // 2, 1),
                        pltpu.roll(x, ROT_DIM // 2, 1))
    return x * cos_t + partner * sin_t


def _attn_proj_kernel(x_ref, lnw_ref, w_ref, cos_ref, sin_ref,
                      qh_ref, kh_ref, vh_ref, ko_ref, vo_ref, km_ref, xn_ref):
    j = pl.program_id(1)
    tm = x_ref.shape[0]
    slab = 2 * HEAD_B

    @pl.when(j == 0)
    def _():
        x = x_ref[...]
        ms = jnp.mean(x * x, axis=-1, keepdims=True)
        xn_ref[...] = ((x * lax.rsqrt(ms + NORM_EPS)) * lnw_ref[...]).astype(BF16)

    cos_t = cos_ref[...]
    sin_t = sin_ref[...]
    lane = lax.broadcasted_iota(jnp.int32, cos_t.shape, 1)

    def slabs():
        for sidx in range(D_B // slab):
            z = jnp.dot(xn_ref[...], w_ref[:, sidx * slab:(sidx + 1) * slab],
                        preferred_element_type=F32)
            for hh in range(slab // HEAD_B):
                h = sidx * (slab // HEAD_B) + hh
                yield h, z[:, hh * HEAD_B:(hh + 1) * HEAD_B]

    @pl.when(j == 0)
    def _():
        for h, z in slabs():
            qh_ref[h] = _rope(z, cos_t, sin_t, lane).astype(qh_ref.dtype)

    @pl.when(j == 1)
    def _():
        for h, z in slabs():
            sl = slice(h * HEAD_B, (h + 1) * HEAD_B)
            k = _rope(z, cos_t, sin_t, lane)
            kh_ref[h] = k.astype(kh_ref.dtype)
            ko_ref[:, sl] = k
            for blk in range(km_ref.shape[0]):
                rows = tm // km_ref.shape[0]
                km_ref[blk, :, sl] = jnp.mean(k[blk * rows:(blk + 1) * rows], axis=0, keepdims=True)

    @pl.when(j == 2)
    def _():
        for h, z in slabs():
            vo_ref[:, h * HEAD_B:(h + 1) * HEAD_B] = z
            vh_ref[h] = z.astype(vh_ref.dtype)


def attn_projection(x, ln_w, w_main, cos_t, sin_t, batch, seq):
    m = batch * seq
    blk = min(seq, MOBA_BLOCK)
    tm = min(seq, 512)
    assert seq % tm == 0 and tm % blk == 0
    nt = seq // tm
    sq = pl.Squeezed()
    tab = pl.BlockSpec((tm, HEAD_B), lambda i, j: (i % nt, 0))
    hm = pl.BlockSpec((sq, N_HEADS_B, tm, HEAD_B), lambda i, j: (i // nt, 0, i % nt, 0))
    rows = pl.BlockSpec((tm, D_B), lambda i, j: (i, 0))
    hm_shape = jax.ShapeDtypeStruct((batch, N_HEADS_B, seq, HEAD_B), BF16)
    return pl.pallas_call(
        _attn_proj_kernel,
        grid=(m // tm, 3),
        in_specs=[
            pl.BlockSpec((tm, D_MODEL), lambda i, j: (i, 0)),
            pl.BlockSpec((1, D_MODEL), lambda i, j: (0, 0)),
            pl.BlockSpec((D_MODEL, D_B), lambda i, j: (0, COL_Q // D_B + j)),
            tab, tab,
        ],
        out_specs=[hm, hm, hm, rows, rows,
                   pl.BlockSpec((tm // blk, 1, D_B), lambda i, j: (i, 0, 0))],
        out_shape=[hm_shape, hm_shape, hm_shape,
                   jax.ShapeDtypeStruct((m, D_B), F32), jax.ShapeDtypeStruct((m, D_B), F32),
                   jax.ShapeDtypeStruct((m // blk, 1, D_B), F32)],
        scratch_shapes=[pltpu.VMEM((tm, D_MODEL), BF16)],
        compiler_params=_cparams(("arbitrary", "arbitrary")),
        name="attn_projection",
    )(x, ln_w.reshape(1, -1), w_main, cos_t, sin_t)


def _rope_rows_kernel(zq_ref, zk_ref, cos_ref, sin_ref, q_ref, k_ref):
    cos_t = cos_ref[...]
    sin_t = sin_ref[...]
    lane = lax.broadcasted_iota(jnp.int32, cos_t.shape, 1)
    for h in range(N_HEADS_B):
        sl = slice(h * HEAD_B, (h + 1) * HEAD_B)
        q_ref[:, sl] = _rope(zq_ref[:, sl], cos_t, sin_t, lane).astype(BF16).astype(F32)
        k_ref[:, sl] = _rope(zk_ref[:, sl], cos_t, sin_t, lane)


def rope_rows(zq, zk, cos_t, sin_t):
    out = jax.ShapeDtypeStruct(zq.shape, F32)
    return pl.pallas_call(_rope_rows_kernel, out_shape=[out, out], name="rope_rows")(zq, zk, cos_t, sin_t)


MOBA_HEADS_PER_STEP = 8


def _moba_prompt_kernel(q_ref, k_ref, v_ref, km_ref, zg_ref, o_ref, sel_ref):
    qb = pl.program_id(1)
    nbatch, nhead, tq = q_ref.shape[0], q_ref.shape[1], q_ref.shape[2]
    nb = km_ref.shape[1]
    seqs = [(b, h) for b in range(nbatch) for h in range(nhead)]
    exp2_scale = HEAD_B ** -0.5 * LOG2_E
    qs = [q_ref[b, h].T for b, h in seqs]

    blk = lax.broadcasted_iota(jnp.int32, (nb, tq), 0)
    for i, (b, h) in enumerate(seqs):
        km = km_ref[b, :, h * HEAD_B:(h + 1) * HEAD_B]
        gate = sum(jnp.dot(piece, qs[i], preferred_element_type=F32)
                   for piece in _split3(km))
        gate = jnp.where(blk < qb, gate, -jnp.inf)
        for n in range(nb):
            g_n = gate[n:n + 1, :]
            tie = jnp.where(blk < n, 1.0, 0.0)
            beats = jnp.where(gate > g_n, 1.0, jnp.where(gate == g_n, tie, 0.0))
            cnt = jnp.sum(beats, axis=0, keepdims=True)
            sel_ref[i, n:n + 1, :] = jnp.where(cnt < MOBA_TOPK, 1.0, 0.0)

    def scores(i, n):
        b, h = seqs[i]
        start = pl.multiple_of(n * tq, tq)
        return jnp.dot(k_ref[b, h, pl.ds(start, tq), :], qs[i], preferred_element_type=F32)

    def softmax_step(s, m, l):
        m_new = jnp.maximum(m, jnp.max(s, axis=0, keepdims=True))
        alpha = jnp.exp2((m - m_new) * exp2_scale)
        p = jnp.exp2((s - m_new) * exp2_scale)
        return m_new, alpha, alpha * l + jnp.sum(p, axis=0, keepdims=True), p.astype(BF16)

    def pv(i, n, p):
        b, h = seqs[i]
        start = pl.multiple_of(n * tq, tq)
        return _bdot_tn(v_ref[b, h, pl.ds(start, tq), :], p)

    def step(n, masked, carry):
        sm = [softmax_step(masked[i], carry[i][0], carry[i][1]) for i in range(len(seqs))]
        acc = [sm[i][1] * carry[i][2] + pv(i, n, sm[i][3]) for i in range(len(seqs))]
        return [(sm[i][0], sm[i][2], acc[i]) for i in range(len(seqs))]

    def body(n, carry):
        s = [scores(i, n) for i in range(len(seqs))]
        masked = [jnp.where(sel_ref[i, pl.ds(n, 1), :] > 0.0, s[i], NEG)
                  for i in range(len(seqs))]
        return tuple(step(n, masked, carry))

    init = tuple((jnp.full((1, tq), NEG, F32), jnp.zeros((1, tq), F32),
                  jnp.zeros((HEAD_B, tq), F32)) for i in range(len(seqs)))
    carry = lax.fori_loop(0, qb, body, init)
    ki = lax.broadcasted_iota(jnp.int32, (tq, tq), 0)
    qi = lax.broadcasted_iota(jnp.int32, (tq, tq), 1)
    masked = [jnp.where(ki <= qi, scores(i, qb), NEG) for i in range(len(seqs))]
    final = step(qb, masked, list(carry))
    for i, (b, h) in enumerate(seqs):
        _, l, acc = final[i]
        sl = slice(h * HEAD_B, (h + 1) * HEAD_B)
        o_ref[b, :, sl] = ((acc / l).T * _silu(zg_ref[b, :, sl])).astype(o_ref.dtype)


def moba_prompt(q_hm, k_hm, v_hm, kmean, z_main):
    batch, _, seq, _ = k_hm.shape
    tq = min(seq, MOBA_BLOCK)
    nb = seq // tq
    hps = MOBA_HEADS_PER_STEP
    w = hps * HEAD_B
    return pl.pallas_call(
        _moba_prompt_kernel,
        grid=(N_HEADS_B // hps, nb),
        in_specs=[
            pl.BlockSpec((batch, hps, tq, HEAD_B), lambda h, i: (0, h, i, 0)),
            pl.BlockSpec((batch, hps, seq, HEAD_B), lambda h, i: (0, h, 0, 0),
                         pipeline_mode=pl.Buffered(1)),
            pl.BlockSpec((batch, hps, seq, HEAD_B), lambda h, i: (0, h, 0, 0),
                         pipeline_mode=pl.Buffered(1)),
            pl.BlockSpec((batch, nb, w), lambda h, i: (0, 0, h)),
            pl.BlockSpec((batch, tq, w), lambda h, i: (0, i, COL_GB // w + h)),
        ],
        out_specs=pl.BlockSpec((batch, tq, w), lambda h, i: (0, i, h)),
        out_shape=jax.ShapeDtypeStruct((batch, seq, D_B), BF16),
        scratch_shapes=[pltpu.VMEM((batch * hps, nb, tq), F32)],
        compiler_params=_cparams(("arbitrary", "arbitrary")),
        name="moba_prompt",
    )(q_hm, k_hm, v_hm, kmean, z_main)


KMEAN_PAGES = 8


def _cache_kmean_kernel(pt_ref, *refs, pages_per_block):
    del pt_ref
    for task in _page_block_mean_tasks(refs[:-1], refs[-1], pages_per_block):
        task()


def cache_kmean(cache_k, page_table, layer):
    n, n_pages = page_table.shape
    page = cache_k.shape[2]
    ppb = MOBA_BLOCK // page
    pps = min(KMEAN_PAGES, n_pages)
    assert n_pages % pps == 0 and pps % ppb == 0
    sq = pl.Squeezed()
    page_spec = lambda i: pl.BlockSpec(
        (sq, sq, page, N_HEADS_B, HEAD_B), lambda b, s, pt, i=i: (layer, pt[b, s * pps + i], 0, 0, 0))
    return pl.pallas_call(
        functools.partial(_cache_kmean_kernel, pages_per_block=ppb),
        grid_spec=pltpu.PrefetchScalarGridSpec(
            num_scalar_prefetch=1,
            grid=(n, n_pages // pps),
            in_specs=[page_spec(i) for i in range(pps)],
            out_specs=pl.BlockSpec((sq, pps // ppb, N_HEADS_B, HEAD_B), lambda b, s, pt: (b, s, 0, 0)),
        ),
        out_shape=jax.ShapeDtypeStruct((n, n_pages // ppb, N_HEADS_B, HEAD_B), F32),
        compiler_params=_cparams(("arbitrary", "arbitrary")),
        name="cache_kmean",
    )(page_table, *([cache_k] * pps))


def _sample_select_kernel(q_ref, km_ref, sel_ref):
    nb = km_ref.shape[0]
    ri = lax.broadcasted_iota(jnp.int32, (nb, nb), 0)
    ci = lax.broadcasted_iota(jnp.int32, (nb, nb), 1)
    lane = lax.broadcasted_iota(jnp.int32, (1, 128), 1)
    blk_row = lax.broadcasted_iota(jnp.int32, (1, nb), 1).astype(F32)
    for h in range(N_HEADS_B):
        km = km_ref[:, h, :]
        g_col = jnp.sum(km * q_ref[h:h + 1, :], axis=1, keepdims=True)
        g_row = jnp.sum(jnp.where(ri == ci, g_col, 0.0), axis=0, keepdims=True)
        beats = (g_col > g_row) | ((g_col == g_row) & (ri < ci))
        rank = jnp.sum(jnp.where(beats, 1.0, 0.0), axis=0, keepdims=True)
        out = jnp.zeros((1, 128), F32)
        for r in range(MOBA_TOPK):
            idx = jnp.sum(jnp.where(rank == float(r), blk_row, 0.0), axis=1, keepdims=True)
            out = jnp.where(lane == r, idx, out)
        sel_ref[h:h + 1, :] = out.astype(jnp.int32)


def sample_select(q_s, kmean_s):
    n, nb = kmean_s.shape[:2]
    sq = pl.Squeezed()
    return pl.pallas_call(
        _sample_select_kernel,
        grid=(n,),
        in_specs=[pl.BlockSpec((sq, N_HEADS_B, HEAD_B), lambda b: (b, 0, 0)),
                  pl.BlockSpec((sq, nb, N_HEADS_B, HEAD_B), lambda b: (b, 0, 0, 0))],
        out_specs=pl.BlockSpec((sq, N_HEADS_B, 128), lambda b: (b, 0, 0)),
        out_shape=jax.ShapeDtypeStruct((n, N_HEADS_B, 128), jnp.int32),
        compiler_params=_cparams(("arbitrary",)),
        name="sample_select",
    )(q_s, kmean_s)


def _sample_attn_kernel(sel_ref, pt_ref, q_ref, kn_ref, vn_ref, zg_ref, ck_ref, cv_ref, o_ref,
                        kbuf, vbuf, sem, *, layer, page, ppb):
    b = pl.program_id(0)
    nb_steps = pl.num_programs(0)
    n_sel = MOBA_TOPK * ppb
    scale = HEAD_B ** -0.5

    def copies(bb, slot):
        out = []
        for h in range(N_HEADS_B):
            for j in range(n_sel):
                pg = pt_ref[bb, sel_ref[bb, h, j // ppb] * ppb + j % ppb]
                dst = pl.ds(j * page, page)
                out.append(pltpu.make_async_copy(ck_ref.at[layer, pg, :, h, :],
                                                 kbuf.at[slot, h, dst, :], sem.at[slot, 0]))
                out.append(pltpu.make_async_copy(cv_ref.at[layer, pg, :, h, :],
                                                 vbuf.at[slot, h, dst, :], sem.at[slot, 1]))
        return out

    slot = b % 2

    @pl.when(b == 0)
    def _():
        for c in copies(0, 0):
            c.start()

    @pl.when(b + 1 < nb_steps)
    def _():
        for c in copies(b + 1, 1 - slot):
            c.start()

    for c in copies(b, slot):
        c.wait()

    for h in range(N_HEADS_B):
        q = q_ref[h]
        k = kbuf[slot, h]
        v = vbuf[slot, h]
        s = jnp.sum(k * q, axis=1, keepdims=True) * scale
        s_own = jnp.sum(kn_ref[h] * q, axis=1, keepdims=True) * scale
        m = jnp.maximum(jnp.max(s, axis=0, keepdims=True), s_own)
        p = jnp.exp(s - m)
        p_own = jnp.exp(s_own - m)
        l = jnp.sum(p, axis=0, keepdims=True) + p_own
        acc = jnp.sum(p * v, axis=0, keepdims=True) + p_own * vn_ref[h]
        o_ref[h] = (acc / l) * _silu(zg_ref[h])


def sample_attn(sel, page_table, q_s, k_new, v_new, zgb, cache_k, cache_v, layer):
    n = q_s.shape[0]
    page = cache_k.shape[2]
    ppb = MOBA_BLOCK // page
    rows = MOBA_TOPK * ppb * page
    sq = pl.Squeezed()
    vec = pl.BlockSpec((sq, N_HEADS_B, 1, HEAD_B), lambda b, sel, pt: (b, 0, 0, 0))
    hbm = pl.BlockSpec(memory_space=pl.ANY)
    return pl.pallas_call(
        functools.partial(_sample_attn_kernel, layer=layer, page=page, ppb=ppb),
        grid_spec=pltpu.PrefetchScalarGridSpec(
            num_scalar_prefetch=2,
            grid=(n,),
            in_specs=[vec, vec, vec, vec, hbm, hbm],
            out_specs=vec,
            scratch_shapes=[pltpu.VMEM((2, N_HEADS_B, rows, HEAD_B), F32),
                            pltpu.VMEM((2, N_HEADS_B, rows, HEAD_B), F32),
                            pltpu.SemaphoreType.DMA((2, 2))],
        ),
        out_shape=jax.ShapeDtypeStruct((n, N_HEADS_B, 1, HEAD_B), F32),
        compiler_params=_cparams(("arbitrary",)),
        name="sample_attn",
    )(sel, page_table, q_s, k_new, v_new, zgb, cache_k, cache_v)


def _merge_out_kernel(oa_ref, ob_ref, pa_ref, pb_ref, zga_ref, zgm_ref, wo_ref, x_ref, fw_ref, y_ref):
    ya = jnp.dot(oa_ref[...], pa_ref[...], preferred_element_type=F32)
    yb = jnp.dot(ob_ref[...], pb_ref[...], preferred_element_type=F32)
    merged = (_sigmoid(zga_ref[...]) * ya + _sigmoid(zgm_ref[...]) * yb).astype(BF16)
    h = x_ref[...] + jnp.dot(merged, wo_ref[...], preferred_element_type=F32)
    ms = jnp.mean(h * h, axis=-1, keepdims=True)
    y_ref[...] = (h * lax.rsqrt(ms + NORM_EPS)) * fw_ref[...]


def merge_and_output(o_a, o_b, p_a, p_b, z_main, w_o, x, final_w):
    m = x.shape[0]
    tm = min(m, 256)
    resident = pl.Buffered(1)
    return pl.pallas_call(
        _merge_out_kernel,
        grid=(m // tm,),
        in_specs=[
            pl.BlockSpec((tm, D_A), lambda i: (i, 0)),
            pl.BlockSpec((tm, D_B), lambda i: (i, 0)),
            pl.BlockSpec((D_A, D_MODEL), lambda i: (0, 0), pipeline_mode=resident),
            pl.BlockSpec((D_B, D_MODEL), lambda i: (0, 0), pipeline_mode=resident),
            pl.BlockSpec((tm, D_MODEL), lambda i: (i, COL_GA // D_MODEL)),
            pl.BlockSpec((tm, D_MODEL), lambda i: (i, COL_GM // D_MODEL)),
            pl.BlockSpec((D_MODEL, D_MODEL), lambda i: (0, 0), pipeline_mode=resident),
            pl.BlockSpec((tm, D_MODEL), lambda i: (i, 0)),
            pl.BlockSpec((1, D_MODEL), lambda i: (0, 0)),
        ],
        out_specs=pl.BlockSpec((tm, D_MODEL), lambda i: (i, 0)),
        out_shape=jax.ShapeDtypeStruct((m, D_MODEL), F32),
        compiler_params=_cparams(("arbitrary",)),
        name="merge_and_output",
    )(o_a, o_b, p_a, p_b, z_main, z_main, w_o, x, final_w.reshape(1, -1))


def _rope_tables(pos):
    half = ROT_DIM // 2
    inv = jnp.power(jnp.float32(ROPE_THETA), -jnp.arange(half, dtype=F32) * (2.0 / ROT_DIM))
    ang = pos.astype(F32)[:, None] * inv[None, :]
    cos, sin = jnp.cos(ang), jnp.sin(ang)
    n = pos.shape[0]
    rest = HEAD_B - ROT_DIM
    cos_t = jnp.concatenate([cos, cos, jnp.ones((n, rest), F32)], axis=1)
    sin_t = jnp.concatenate([-sin, sin, jnp.zeros((n, rest), F32)], axis=1)
    return cos_t, sin_t


def _reorder_rw(a):
    r, w_lo, k, v, a_lo, g = jnp.split(
        a, [D_A, D_A + LORA, 2 * D_A + LORA, 3 * D_A + LORA, 3 * D_A + 2 * LORA], axis=-1)
    return jnp.concatenate([r, k, v, g], axis=-1), jnp.concatenate([w_lo, a_lo], axis=-1)


def kernel(x_prompt, x_sample, state_shift, state_wkv, cache_k, cache_v, page_table, ln_w, w_in, mu,
           w0, w2, a0, a2, k_k, k_a, r_k, lnx_w, lnx_b, p_a, p_b, w_o, final_w):
    depth = ln_w.shape[0]
    assert depth == 1, "single-layer trunk"
    B, T, _ = x_prompt.shape
    DB, TS, _ = x_sample.shape
    assert TS == 1
    n_pages = page_table.shape[1]
    page = cache_k.shape[2]
    past = n_pages * page
    assert past % MOBA_BLOCK == 0 and MOBA_BLOCK % page == 0
    l = 0
    rw_cols = 4 * D_A + 2 * LORA

    att0 = rw_cols
    src_cols = [0, D_A + LORA, 2 * D_A + LORA, 3 * D_A + 2 * LORA]
    src_cols += [att0 + 4 * D_B + i * W_TILE for i in range(2 * D_MODEL // W_TILE)]
    src_cols += [att0 + 3 * D_B, att0, att0 + D_B, att0 + 2 * D_B]
    w_main = prepare_projection_weight(w_in, l, src_cols)
    w_wa = jnp.concatenate([w_in[l, :, D_A:D_A + LORA],
                            w_in[l, :, 3 * D_A + LORA:3 * D_A + 2 * LORA]], axis=1)
    mu_rw, mu_wa = _reorder_rw(mu[l][None, :])
    zeros = jnp.zeros((LORA, D_A), F32)
    w2p = jnp.concatenate([w2[l], zeros], axis=0).astype(BF16)
    a2p = jnp.concatenate([zeros, a2[l]], axis=0).astype(BF16)
    row = lambda a: a.reshape(1, -1)
    vecs = (row(w0[l]), row(a0[l]), row(k_k[l]), row(k_a[l]), row(r_k[l]), row(lnx_w[l]), row(lnx_b[l]))
    pa_bf, pb_bf, wo_bf = p_a[l].astype(BF16), p_b[l].astype(BF16), w_o[l].astype(BF16)

    rows = jnp.concatenate([x_prompt[:, -1, :], x_sample[:, 0, :]], axis=0)
    xn_rows = rmsnorm_rows(rows, ln_w[l])
    shift_prompt = xn_rows[:B]
    xn_s = xn_rows[B:]
    xp = x_prompt.reshape(B * T, D_MODEL)
    z_p, zwa_p = input_projection(xp, ln_w[l], w_main, w_wa, normalize=True, n_cols=N_Z)
    z_s2, zwa_s2 = input_projection(jnp.concatenate([xn_s, state_shift[l]], axis=0), ln_w[l],
                                    w_main, w_wa, normalize=False, n_cols=N_MAIN)

    z_p3 = z_p.reshape(B, T, N_Z)
    o_a_p, wkv_p, kmean_s = rwkv_prompt(z_p3, zwa_p.reshape(B, T, 2 * LORA), mu_rw, mu_wa, vecs, w2p, a2p,
                                        cache_k, page_table, l)
    cos_p, sin_p = _rope_tables(jnp.arange(T))
    q_hm, k_hm, v_hm, k_rows_p, v_rows_p, kmean_p = attn_projection(xp, ln_w[l], w_main, cos_p, sin_p, B, T)
    nb_p = T // min(T, MOBA_BLOCK)
    o_b_p = moba_prompt(q_hm, k_hm, v_hm, kmean_p.reshape(B, nb_p, D_B), z_p3)
    y_prompt = merge_and_output(o_a_p.reshape(B * T, D_A), o_b_p.reshape(B * T, D_B), pa_bf, pb_bf, z_p,
                                wo_bf, xp, final_w).reshape(B, T, D_MODEL)

    z_s, zprev_s = z_s2[:DB], z_s2[DB:]
    r_s, kf_s, v_s, kk_s, b_s, w_s, g_s = rwkv_rows(
        z_s[:, :4 * D_A], zprev_s[:, :4 * D_A], zwa_s2[:DB], zwa_s2[DB:], mu_rw, mu_wa,
        vecs[0], vecs[1], vecs[2], vecs[3], w2p, a2p)
    hs = lambda a: a.reshape(-1, N_HEADS_A, 1, HEAD_A)
    col = lambda a: a.reshape(-1, N_HEADS_A, HEAD_A, 1)
    wkv_s, o_a_s = rwkv_step(state_wkv[l], hs(w_s), hs(kk_s), hs(b_s), hs(kf_s), hs(r_s),
                             col(v_s), col(g_s), hs(r_k[l])[0], col(lnx_w[l])[0], col(lnx_b[l])[0])
    o_a_s = o_a_s.reshape(DB, D_A).astype(BF16)

    cos_s, sin_s = _rope_tables(past + jnp.arange(TS))
    cos_s = jnp.broadcast_to(cos_s, (DB, HEAD_B))
    sin_s = jnp.broadcast_to(sin_s, (DB, HEAD_B))
    q_rows_s, k_rows_s = rope_rows(z_s[:, COL_Q:COL_Q + D_B], z_s[:, COL_KB:COL_KB + D_B], cos_s, sin_s)
    v_rows_s = z_s[:, COL_VB:COL_VB + D_B]
    if kmean_s is None:
        kmean_s = cache_kmean(cache_k, page_table, l)
    sel = sample_select(q_rows_s.reshape(DB, N_HEADS_B, HEAD_B), kmean_s)[:, :, :MOBA_TOPK]
    hv = lambda a: a.reshape(DB, N_HEADS_B, 1, HEAD_B)
    o_b_s = sample_attn(sel, page_table, hv(q_rows_s), hv(k_rows_s), hv(v_rows_s),
                        hv(z_s[:, COL_GB:COL_GB + D_B]), cache_k, cache_v, l)
    o_b_s = o_b_s.reshape(DB, D_B).astype(BF16)
    y_sample = merge_and_output(o_a_s, o_b_s, pa_bf, pb_bf, z_s, wo_bf, x_sample.reshape(DB, D_MODEL),
                                final_w)

    return (y_prompt,
            y_sample.reshape(DB, TS, D_MODEL),
            shift_prompt[None],
            wkv_p.reshape(1, B, N_HEADS_A, HEAD_A, HEAD_A),
            k_rows_p.reshape(1, B, T, N_HEADS_B, HEAD_B),
            v_rows_p.reshape(1, B, T, N_HEADS_B, HEAD_B),
            xn_s[None],
            wkv_s[None],
            k_rows_s.reshape(1, DB, TS, N_HEADS_B, HEAD_B),
            v_rows_s.reshape(1, DB, TS, N_HEADS_B, HEAD_B))
```

```python
import functools

import jax
import jax.numpy as jnp
from jax import lax
from jax.experimental import pallas as pl
from jax.experimental.pallas import tpu as pltpu

F32 = jnp.float32
BF16 = jnp.bfloat16

D_MODEL = 2048
D_A = D_MODEL // 2
HEAD_A = 64
N_HEADS_A = D_A // HEAD_A
LORA = 64
D_B = D_MODEL // 2
HEAD_B = 128
N_HEADS_B = D_B // HEAD_B
ROT_DIM = HEAD_B // 4
ROPE_THETA = 500000.0
MOBA_BLOCK = 256
MOBA_TOPK = 3
NORM_EPS = 1e-6
GN_EPS = 64e-5
NEG = -1e30
LOG2_E = 1.4426950408889634

COL_R, COL_K, COL_V, COL_G = 0, D_A, 2 * D_A, 3 * D_A
COL_GA = 4 * D_A
COL_GM = COL_GA + D_MODEL
COL_GB = COL_GM + D_MODEL
N_Z = COL_GB + D_B
COL_Q, COL_KB, COL_VB = N_Z, N_Z + D_B, N_Z + 2 * D_B
N_MAIN = COL_VB + D_B

GROUP = 256
HEADS_PER_GROUP = GROUP // HEAD_A
CHUNK = 64
assert CHUNK == HEAD_A
VMEM_LIMIT = 56 * 1024 * 1024


def _cparams(sem):
    return pltpu.CompilerParams(dimension_semantics=sem, vmem_limit_bytes=VMEM_LIMIT)


def _bdot(a, b):
    return jnp.dot(a.astype(BF16), b.astype(BF16), preferred_element_type=F32)


def _bdot_nt(a, b):
    return lax.dot_general(a.astype(BF16), b.astype(BF16), (((1,), (1,)), ((), ())),
                           preferred_element_type=F32)


def _bdot_tn(a, b):
    return lax.dot_general(a.astype(BF16), b.astype(BF16), (((0,), (0,)), ((), ())),
                           preferred_element_type=F32)


def _split3(x):
    hi = x.astype(BF16)
    r1 = x - hi.astype(F32)
    mid = r1.astype(BF16)
    lo = (r1 - mid.astype(F32)).astype(BF16)
    return hi, mid, lo


def _sigmoid(x):
    return 1.0 / (1.0 + jnp.exp(-x))


def _silu(x):
    return x * _sigmoid(x)


def _softplus(x):
    return jnp.maximum(x, 0.0) + jnp.log(1.0 + jnp.exp(-jnp.abs(x)))


def _rmsnorm_rows_kernel(x_ref, w_ref, o_ref):
    x = x_ref[...]
    ms = jnp.mean(x * x, axis=-1, keepdims=True)
    o_ref[...] = (x * lax.rsqrt(ms + NORM_EPS)) * w_ref[...]


def rmsnorm_rows(x, w):
    return pl.pallas_call(
        _rmsnorm_rows_kernel,
        out_shape=jax.ShapeDtypeStruct(x.shape, F32),
        name="rmsnorm_rows",
    )(x, w.reshape(1, -1))


W_TILE = 1024
LANES = 128


def _w_prep_kernel(w_hbm, o_hbm, in_buf, out_buf, in_sem, out_sem, *, layer, starts):
    n = len(starts)
    rows = in_buf.shape[1]

    def in_copy(j, slot):
        off = starts[j] % LANES
        width = W_TILE + (LANES if off else 0)
        return pltpu.make_async_copy(w_hbm.at[layer, :, pl.ds(starts[j] - off, width)],
                                     in_buf.at[slot, :, pl.ds(0, width)], in_sem.at[slot])

    def out_copy(j, slot):
        return pltpu.make_async_copy(out_buf.at[slot], o_hbm.at[:, pl.ds(j * W_TILE, W_TILE)],
                                     out_sem.at[slot])

    in_copy(0, 0).start()
    for j in range(n):
        slot = j % 2
        if j + 1 < n:
            in_copy(j + 1, 1 - slot).start()
        in_copy(j, slot).wait()
        if j >= 2:
            out_copy(j - 2, slot).wait()
        off = starts[j] % LANES
        rc = 256

        def chunk(c, _, slot=slot, off=off):
            r0 = pl.multiple_of(c * rc, rc)
            out_buf[slot, pl.ds(r0, rc), :] = in_buf[slot, pl.ds(r0, rc), off:off + W_TILE].astype(BF16)
            return 0

        lax.fori_loop(0, rows // rc, chunk, 0)
        out_copy(j, slot).start()
    for j in range(max(n - 2, 0), n):
        out_copy(j, j % 2).wait()


def prepare_projection_weight(w_in, layer, starts):
    rows = w_in.shape[1]
    assert rows % 256 == 0
    hbm = pl.BlockSpec(memory_space=pl.ANY)
    return pl.pallas_call(
        functools.partial(_w_prep_kernel, layer=layer, starts=tuple(starts)),
        in_specs=[hbm],
        out_specs=hbm,
        out_shape=jax.ShapeDtypeStruct((rows, len(starts) * W_TILE), BF16),
        scratch_shapes=[pltpu.VMEM((2, rows, W_TILE + LANES), F32),
                        pltpu.VMEM((2, rows, W_TILE), BF16),
                        pltpu.SemaphoreType.DMA((2,)), pltpu.SemaphoreType.DMA((2,))],
        compiler_params=pltpu.CompilerParams(vmem_limit_bytes=VMEM_LIMIT),
        name="prepare_projection_weight",
    )(w_in)


def _proj_kernel(x_ref, lnw_ref, w_ref, wwa_ref, z_ref, zwa_ref, xn_ref, *, normalize):
    @pl.when(pl.program_id(1) == 0)
    def _():
        x = x_ref[...]
        if normalize:
            ms = jnp.mean(x * x, axis=-1, keepdims=True)
            x = (x * lax.rsqrt(ms + NORM_EPS)) * lnw_ref[...]
        xn_ref[...] = x.astype(BF16)
        zwa_ref[...] = jnp.dot(xn_ref[...], wwa_ref[...].astype(BF16), preferred_element_type=F32)

    z_ref[...] = jnp.dot(xn_ref[...], w_ref[...], preferred_element_type=F32)


def input_projection(x, ln_w, w_main, w_wa, *, normalize, n_cols):
    m = x.shape[0]
    tm = min(m, 1024)
    if m < 1024:
        tn = 2048
    else:
        tn = 1536 if n_cols % 1536 == 0 else 1024
    assert m % tm == 0 and n_cols % tn == 0
    return pl.pallas_call(
        functools.partial(_proj_kernel, normalize=normalize),
        grid=(m // tm, n_cols // tn),
        in_specs=[
            pl.BlockSpec((tm, D_MODEL), lambda i, j: (i, 0)),
            pl.BlockSpec((1, D_MODEL), lambda i, j: (0, 0)),
            pl.BlockSpec((D_MODEL, tn), lambda i, j: (0, j)),
            pl.BlockSpec((D_MODEL, 2 * LORA), lambda i, j: (0, 0)),
        ],
        out_specs=[
            pl.BlockSpec((tm, tn), lambda i, j: (i, j)),
            pl.BlockSpec((tm, 2 * LORA), lambda i, j: (i, 0)),
        ],
        out_shape=[
            jax.ShapeDtypeStruct((m, n_cols), F32),
            jax.ShapeDtypeStruct((m, 2 * LORA), F32),
        ],
        scratch_shapes=[pltpu.VMEM((tm, D_MODEL), BF16)],
        compiler_params=_cparams(("arbitrary", "arbitrary")),
        name="input_projection",
    )(x, ln_w.reshape(1, -1), w_main, w_wa)


def _head_mask(n):
    r = lax.broadcasted_iota(jnp.int32, (n, n), 0) // HEAD_A
    c = lax.broadcasted_iota(jnp.int32, (n, n), 1) // HEAD_A
    return r == c


def _segsum(x, ones_bd):
    return jnp.dot(x.astype(BF16), ones_bd, preferred_element_type=F32)


def _rwkv_prep(zm_r, zm_k, zm_v, zm_wa, w0, a0, k_k, k_a, w2p, a2p, ones_bd):
    lane = lax.broadcasted_iota(jnp.int32, zm_wa.shape, 1)
    lora_in = jnp.where(lane < LORA, jnp.tanh(zm_wa), zm_wa)
    ww = _bdot(lora_in, w2p)
    aa = _bdot(lora_in, a2p)
    w_log = -_softplus(-(w0 + ww)) - 0.5
    logw = -jnp.exp(w_log)
    a = _sigmoid(a0 + aa)
    kk = zm_k * k_k
    ss = _segsum(kk * kk, ones_bd)
    kk = kk / jnp.maximum(jnp.sqrt(ss), 1e-12)
    kf = zm_k * (1.0 + (a - 1.0) * k_a)
    return zm_r, kf, zm_v, kk, kk * a, logw


def _rwkv_post(y, r, kf, v, zm_g, r_k, lnx_w, lnx_b, ones_bd):
    inv_n = 1.0 / HEAD_A
    mean = _segsum(y, ones_bd) * inv_n
    d = y - mean
    var = _segsum(d * d, ones_bd) * inv_n
    yn = d * lax.rsqrt(var + GN_EPS) * lnx_w + lnx_b
    bonus = _segsum(r * kf * r_k, ones_bd) * v
    return (yn + bonus) * _silu(zm_g)


def _stack_heads(x, lane_head):
    return jnp.concatenate(
        [jnp.where(lane_head == h, x, 0.0) for h in range(HEADS_PER_GROUP)], axis=0)


def _chunk_precompute(insts, consts, side_tasks=()):
    strict, incl, eye, lane_head, bd_mask = consts
    c = CHUNK
    n = len(insts)
    side_tasks = list(side_tasks)
    n_phases = 8
    per_phase = -(-len(side_tasks) // n_phases)

    def run_side(k=None):
        for _ in range(per_phase if k is None else k):
            if side_tasks:
                side_tasks.pop(0)()

    def bd(p):
        return jnp.where(bd_mask, jnp.concatenate([p] * HEADS_PER_GROUP, axis=0), 0.0).astype(BF16)

    lhs, wt, v_bd, bdkd, p_end = [], [], [], [], []
    for r, kf, v, kk, b, logw, cl in insts:
        cl_last = cl[c - 1:c, :]
        e_neg = jnp.exp(-cl)
        e_end = jnp.exp(cl_last - cl)
        lhs.append(jnp.concatenate([kk * jnp.exp(cl - logw), r * jnp.exp(cl)], axis=0).astype(BF16))
        wt.append(jnp.concatenate([_stack_heads(b * e_neg, lane_head),
                                   _stack_heads(kf * e_neg, lane_head)], axis=0).astype(BF16))
        v_bd.append(_stack_heads(v, lane_head).astype(BF16))
        bdkd.append(jnp.concatenate([b * e_end, kf * e_end], axis=0).astype(BF16))
        p_end.append(jnp.exp(cl_last))
    att = [_bdot_nt(lhs[i], wt[i]) for i in range(n)]
    run_side()
    a_ab = [jnp.where(strict, att[i][:c, :4 * c], 0.0) for i in range(n)]
    a_ak = [jnp.where(strict, att[i][:c, 4 * c:], 0.0) for i in range(n)]
    m_cat = [jnp.concatenate([jnp.where(incl, att[i][c:, :4 * c], 0.0),
                              jnp.where(incl, att[i][c:, 4 * c:], 0.0)], axis=1).astype(BF16)
             for i in range(n)]
    av = [_bdot(a_ak[i], v_bd[i]) for i in range(n)]
    run_side()

    x = [eye - a_ab[i] for i in range(n)]
    p = [_bdot(a_ab[i], bd(a_ab[i])) for i in range(n)]
    run_side()
    for _ in range(4):
        px = [_bdot(jnp.concatenate([p[i], x[i]], axis=0), bd(p[i])) for i in range(n)]
        p = [px[i][:c] for i in range(n)]
        x = [x[i] + px[i][c:] for i in range(n)]
        run_side()
    x = [(x[i] + _bdot(x[i], bd(p[i]))).astype(BF16) for i in range(n)]
    run_side(len(side_tasks))
    return [dict(lhs=lhs[i], av=av[i], t=x[i], m_cat=m_cat[i], v_bd=v_bd[i], v=insts[i][2],
                 bdkd=bdkd[i], p_end=p_end[i]) for i in range(n)]


def _chunk_state_step(states, pres, consts):
    lane_head, bd_mask = consts[3], consts[4]
    c = CHUNK
    n = len(states)
    sh = [_bdot_nt(pres[i]["lhs"], states[i]) for i in range(n)]
    rhs = [sh[i][:c] + pres[i]["av"] for i in range(n)]
    u = [-_bdot(pres[i]["t"], _stack_heads(rhs[i], lane_head)) for i in range(n)]
    y = [sh[i][c:] + _bdot(pres[i]["m_cat"],
                           jnp.concatenate([_stack_heads(u[i], lane_head).astype(BF16),
                                            pres[i]["v_bd"]], axis=0)) for i in range(n)]
    upd = [_bdot_tn(jnp.concatenate([u[i], pres[i]["v"]], axis=0), pres[i]["bdkd"]) for i in range(n)]
    new = [states[i] * pres[i]["p_end"] + jnp.where(bd_mask, upd[i], 0.0) for i in range(n)]
    return y, new


def _page_block_mean_tasks(page_refs, km_ref, pages_per_block):
    def task(j):
        def run():
            blk = jnp.sum(page_refs[j * pages_per_block][...], axis=0)
            for i in range(1, pages_per_block):
                blk = blk + jnp.sum(page_refs[j * pages_per_block + i][...], axis=0)
            km_ref[j] = blk * (1.0 / MOBA_BLOCK)
        return run
    return [task(j) for j in range(len(page_refs) // pages_per_block)]


def _rwkv_prompt_kernel(*refs, n_pages, pages_per_block):
    if n_pages:
        refs = refs[1:]
    (zr_ref, zk_ref, zv_ref, zg_ref, zwa_ref, mur_ref, muk_ref, muv_ref, mug_ref, muwa_ref,
     w0_ref, a0_ref, kk_ref, ka_ref, rk_ref, lnw_ref, lnb_ref, w2_ref, a2_ref) = refs[:19]
    page_refs = refs[19:19 + n_pages]
    refs = refs[19 + n_pages:]
    o_ref, s_out_ref = refs[:2]
    side_tasks = []
    if n_pages:
        side_tasks = _page_block_mean_tasks(page_refs, refs[2], pages_per_block)
        refs = refs[1:]
    s_ref, pr_ref, pk_ref, pv_ref, pg_ref, pwa_ref = refs[2:]
    t = pl.program_id(1)
    nseq, tr = zr_ref.shape[0], zr_ref.shape[1]

    @pl.when(t == 0)
    def _():
        for ref in (s_ref, pr_ref, pk_ref, pv_ref, pg_ref, pwa_ref):
            ref[...] = jnp.zeros_like(ref)

    def shifted(z_ref, prev_ref, mu_ref, i):
        z = z_ref[i]
        row = lax.broadcasted_iota(jnp.int32, z.shape, 0)
        prev = jnp.where(row == 0, prev_ref[i], pltpu.roll(z, 1, 0))
        prev_ref[i] = z[tr - 1:tr, :]
        return z + (prev - z) * mu_ref[...]

    bd_mask = _head_mask(GROUP)
    ones_bd = jnp.where(bd_mask, 1.0, 0.0).astype(BF16)
    ri = lax.broadcasted_iota(jnp.int32, (tr, tr), 0)
    ci = lax.broadcasted_iota(jnp.int32, (tr, tr), 1)
    tri = jnp.where((ri // CHUNK == ci // CHUNK) & (ci <= ri), 1.0, 0.0).astype(BF16)
    rc = lax.broadcasted_iota(jnp.int32, (CHUNK, GROUP), 0)
    lc = lax.broadcasted_iota(jnp.int32, (CHUNK, GROUP), 1)
    sc = lc % CHUNK
    consts = (sc < rc, sc <= rc, jnp.where(sc == rc, 1.0, 0.0), lc // HEAD_A, bd_mask)
    last = t == pl.num_programs(1) - 1

    n_ch = tr // CHUNK
    prepped, insts = [], []
    for i in range(nseq):
        zm_r = shifted(zr_ref, pr_ref, mur_ref, i)
        zm_k = shifted(zk_ref, pk_ref, muk_ref, i)
        zm_v = shifted(zv_ref, pv_ref, muv_ref, i)
        zm_g = shifted(zg_ref, pg_ref, mug_ref, i)
        zm_wa = shifted(zwa_ref, pwa_ref, muwa_ref, i)
        r, kf, v, kk, b, logw = _rwkv_prep(zm_r, zm_k, zm_v, zm_wa, w0_ref[...], a0_ref[...],
                                           kk_ref[...], ka_ref[...], w2_ref[...], a2_ref[...], ones_bd)
        cl = sum(jnp.dot(tri, piece, preferred_element_type=F32) for piece in _split3(logw))
        prepped.append((r, kf, v, zm_g))
        for ch in range(n_ch):
            sl = slice(ch * CHUNK, (ch + 1) * CHUNK)
            insts.append((r[sl], kf[sl], v[sl], kk[sl], b[sl], logw[sl], cl[sl]))
    pres = _chunk_precompute(insts, consts, side_tasks)

    states = [s_ref[i] for i in range(nseq)]
    ys = [[] for _ in range(nseq)]
    for ch in range(n_ch):
        y, states = _chunk_state_step(states, [pres[i * n_ch + ch] for i in range(nseq)], consts)
        for i in range(nseq):
            ys[i].append(y[i])
    for i in range(nseq):
        s_ref[i] = states[i]
        r, kf, v, zm_g = prepped[i]
        o_ref[i] = _rwkv_post(jnp.concatenate(ys[i], axis=0), r, kf, v, zm_g, rk_ref[...],
                              lnw_ref[...], lnb_ref[...], ones_bd).astype(o_ref.dtype)

    @pl.when(last)
    def _():
        for i in range(nseq):
            for h in range(HEADS_PER_GROUP):
                s_out_ref[i, h] = s_ref[i, h * HEAD_A:(h + 1) * HEAD_A, h * HEAD_A:(h + 1) * HEAD_A]


def rwkv_prompt(z_main, z_wa, mu_main, mu_wa, vecs, w2p, a2p, cache_k=None, page_table=None, layer=0):
    w0, a0, k_k, k_a, r_k, lnx_w, lnx_b = vecs
    batch, seq, _ = z_main.shape
    tr = min(seq, 512)
    assert seq % tr == 0 and tr % CHUNK == 0
    nt = seq // tr
    ng = D_A // GROUP

    n_pages = ppb = 0
    if cache_k is not None:
        page = cache_k.shape[2]
        ppb = MOBA_BLOCK // page
        total = page_table.size
        if total % (ng * nt) == 0 and (total // (ng * nt)) % ppb == 0:
            n_pages = total // (ng * nt)

    def zspec(col):
        return pl.BlockSpec((batch, tr, GROUP), lambda g, t, *_, c=col // GROUP: (0, t, c + g))

    def vspec(col=0):
        return pl.BlockSpec((1, GROUP), lambda g, t, *_, c=col // GROUP: (0, c + g))

    in_specs = [
        zspec(COL_R), zspec(COL_K), zspec(COL_V), zspec(COL_G),
        pl.BlockSpec((batch, tr, 2 * LORA), lambda g, t, *_: (0, t, 0)),
        vspec(COL_R), vspec(COL_K), vspec(COL_V), vspec(COL_G),
        pl.BlockSpec((1, 2 * LORA), lambda g, t, *_: (0, 0)),
        vspec(), vspec(), vspec(), vspec(), vspec(), vspec(), vspec(),
        pl.BlockSpec((2 * LORA, GROUP), lambda g, t, *_: (0, g)),
        pl.BlockSpec((2 * LORA, GROUP), lambda g, t, *_: (0, g)),
    ]
    out_specs = [
        pl.BlockSpec((batch, tr, GROUP), lambda g, t, *_: (0, t, g)),
        pl.BlockSpec((batch, HEADS_PER_GROUP, HEAD_A, HEAD_A), lambda g, t, *_: (0, g, 0, 0)),
    ]
    out_shape = [
        jax.ShapeDtypeStruct((batch, seq, D_A), BF16),
        jax.ShapeDtypeStruct((batch, N_HEADS_A, HEAD_A, HEAD_A), F32),
    ]
    operands = [z_main, z_main, z_main, z_main, z_wa,
                mu_main, mu_main, mu_main, mu_main, mu_wa,
                w0, a0, k_k, k_a, r_k, lnx_w, lnx_b, w2p, a2p]
    if n_pages:
        sq = pl.Squeezed()
        for i in range(n_pages):
            in_specs.append(pl.BlockSpec(
                (sq, sq, page, N_HEADS_B, HEAD_B),
                lambda g, t, pt, i=i: (layer, pt[(g * nt + t) * n_pages + i], 0, 0, 0)))
        operands += [cache_k] * n_pages
        out_specs.append(pl.BlockSpec((n_pages // ppb, N_HEADS_B, HEAD_B),
                                      lambda g, t, pt: (g * nt + t, 0, 0)))
        out_shape.append(jax.ShapeDtypeStruct((page_table.size // ppb, N_HEADS_B, HEAD_B), F32))
    row = lambda n: pltpu.VMEM((batch, 1, n), F32)
    scratch = [pltpu.VMEM((batch, GROUP, GROUP), F32),
               row(GROUP), row(GROUP), row(GROUP), row(GROUP), row(2 * LORA)]
    kern = functools.partial(_rwkv_prompt_kernel, n_pages=n_pages, pages_per_block=ppb)
    params = _cparams(("arbitrary", "arbitrary"))
    if not n_pages:
        o_a, wkv = pl.pallas_call(kern, grid=(ng, nt), in_specs=in_specs, out_specs=out_specs,
                                  out_shape=out_shape, scratch_shapes=scratch,
                                  compiler_params=params, name="rwkv_prompt")(*operands)
        return o_a, wkv, None
    o_a, wkv, kmean = pl.pallas_call(
        kern,
        grid_spec=pltpu.PrefetchScalarGridSpec(
            num_scalar_prefetch=1, grid=(ng, nt), in_specs=in_specs, out_specs=out_specs,
            scratch_shapes=scratch),
        out_shape=out_shape, compiler_params=params, name="rwkv_prompt",
    )(page_table.reshape(-1), *operands)
    n = page_table.shape[0]
    return o_a, wkv, kmean.reshape(n, -1, N_HEADS_B, HEAD_B)


def _rwkv_rows_kernel(z_ref, zp_ref, zwa_ref, zpwa_ref, mu_ref, muwa_ref,
                      w0_ref, a0_ref, kk_ref, ka_ref, w2_ref, a2_ref,
                      r_ref, kf_ref, v_ref, kkn_ref, b_ref, w_ref, g_ref):
    def shifted(z, zp, mu):
        return z + (zp - z) * mu

    mu = mu_ref[...]
    z = z_ref[...]
    zp = zp_ref[...]
    zm = [shifted(z[:, c:c + D_A], zp[:, c:c + D_A], mu[:, c:c + D_A])
          for c in (COL_R, COL_K, COL_V, COL_G)]
    zm_wa = shifted(zwa_ref[...], zpwa_ref[...], muwa_ref[...])
    ones_bd = jnp.where(_head_mask(D_A), 1.0, 0.0).astype(BF16)
    r, kf, v, kk, b, logw = _rwkv_prep(zm[0], zm[1], zm[2], zm_wa, w0_ref[...], a0_ref[...],
                                       kk_ref[...], ka_ref[...], w2_ref[...], a2_ref[...], ones_bd)
    r_ref[...] = r
    kf_ref[...] = kf
    kkn_ref[...] = kk
    b_ref[...] = b
    w_ref[...] = jnp.exp(logw)
    v_t = v.T
    g_t = zm[3].T
    for i in range(v_ref.shape[0]):
        v_ref[i] = v_t[:, i:i + 1]
        g_ref[i] = g_t[:, i:i + 1]


def rwkv_rows(z_rw, zp_rw, z_wa, zp_wa, mu_rw, mu_wa, w0, a0, k_k, k_a, w2p, a2p):
    n = z_rw.shape[0]
    out = jax.ShapeDtypeStruct((n, D_A), F32)
    col = jax.ShapeDtypeStruct((n, D_A, 1), F32)
    return pl.pallas_call(
        _rwkv_rows_kernel,
        out_shape=[out, out, col, out, out, out, col],
        compiler_params=pltpu.CompilerParams(vmem_limit_bytes=VMEM_LIMIT),
        name="rwkv_rows",
    )(z_rw, zp_rw, z_wa, zp_wa, mu_rw, mu_wa, w0, a0, k_k, k_a, w2p, a2p)


def _rwkv_step_kernel(s_ref, w_ref, kk_ref, b_ref, kf_ref, r_ref, v_ref, g_ref,
                      rk_ref, lnw_ref, lnb_ref, s_out_ref, o_ref):
    S = s_ref[...]
    w = w_ref[...]
    kk = kk_ref[...]
    b = b_ref[...]
    kf = kf_ref[...]
    r = r_ref[...]
    v = v_ref[...]
    sa = -jnp.sum(S * kk, axis=-1, keepdims=True)
    S = S * w + sa * b + v * kf
    s_out_ref[...] = S
    y = jnp.sum(S * r, axis=-1, keepdims=True)
    mean = jnp.mean(y, axis=1, keepdims=True)
    d = y - mean
    var = jnp.mean(d * d, axis=1, keepdims=True)
    yn = d * lax.rsqrt(var + GN_EPS) * lnw_ref[...] + lnb_ref[...]
    bonus = jnp.sum(r * kf * rk_ref[...], axis=-1, keepdims=True) * v
    o_ref[...] = (yn + bonus) * _silu(g_ref[...])


def rwkv_step(state, w, kk, b, kf, r, v, g, r_k, lnx_w, lnx_b):
    n = state.shape[0]
    per = 4 if n % 4 == 0 else 1
    h = per * N_HEADS_A
    flat = lambda a: a.reshape((n * N_HEADS_A,) + a.shape[2:])
    tile = lambda a: jnp.tile(a, (per, 1, 1))
    lane_vec = pl.BlockSpec((h, 1, HEAD_A), lambda i: (i, 0, 0))
    col_vec = pl.BlockSpec((h, HEAD_A, 1), lambda i: (i, 0, 0))
    s_new, o = pl.pallas_call(
        _rwkv_step_kernel,
        grid=(n // per,),
        in_specs=[pl.BlockSpec((h, HEAD_A, HEAD_A), lambda i: (i, 0, 0)),
                  lane_vec, lane_vec, lane_vec, lane_vec, lane_vec, col_vec, col_vec,
                  pl.BlockSpec((h, 1, HEAD_A), lambda i: (0, 0, 0)),
                  pl.BlockSpec((h, HEAD_A, 1), lambda i: (0, 0, 0)),
                  pl.BlockSpec((h, HEAD_A, 1), lambda i: (0, 0, 0))],
        out_specs=[pl.BlockSpec((h, HEAD_A, HEAD_A), lambda i: (i, 0, 0)), col_vec],
        out_shape=[jax.ShapeDtypeStruct((n * N_HEADS_A, HEAD_A, HEAD_A), F32),
                   jax.ShapeDtypeStruct((n * N_HEADS_A, HEAD_A, 1), F32)],
        compiler_params=_cparams(("arbitrary",)),
        name="rwkv_step",
    )(flat(state), flat(w), flat(kk), flat(b), flat(kf), flat(r), flat(v), flat(g),
      tile(r_k), tile(lnx_w), tile(lnx_b))
    return s_new.reshape(state.shape), o.reshape(n, N_HEADS_A, HEAD_A, 1)


def _rope(x, cos_t, sin_t, lane):
    partner = jnp.where(lane < ROT_DIM // 2, pltpu.roll(x, HEAD_B - ROT_DIM // 2, 1),
                        pltpu.roll(x, ROT_DIM // 2, 1))
    return x * cos_t + partner * sin_t


def _attn_proj_kernel(x_ref, lnw_ref, w_ref, cos_ref, sin_ref,
                      qh_ref, kh_ref, vh_ref, ko_ref, vo_ref, km_ref, xn_ref):
    j = pl.program_id(1)
    tm = x_ref.shape[0]
    slab = 2 * HEAD_B

    @pl.when(j == 0)
    def _():
        x = x_ref[...]
        ms = jnp.mean(x * x, axis=-1, keepdims=True)
        xn_ref[...] = ((x * lax.rsqrt(ms + NORM_EPS)) * lnw_ref[...]).astype(BF16)

    cos_t = cos_ref[...]
    sin_t = sin_ref[...]
    lane = lax.broadcasted_iota(jnp.int32, cos_t.shape, 1)

    def slabs():
        for sidx in range(D_B // slab):
            z = jnp.dot(xn_ref[...], w_ref[:, sidx * slab:(sidx + 1) * slab],
                        preferred_element_type=F32)
            for hh in range(slab // HEAD_B):
                h = sidx * (slab // HEAD_B) + hh
                yield h, z[:, hh * HEAD_B:(hh + 1) * HEAD_B]

    @pl.when(j == 0)
    def _():
        for h, z in slabs():
            qh_ref[h] = _rope(z, cos_t, sin_t, lane).astype(qh_ref.dtype)

    @pl.when(j == 1)
    def _():
        for h, z in slabs():
            sl = slice(h * HEAD_B, (h + 1) * HEAD_B)
            k = _rope(z, cos_t, sin_t, lane)
            kh_ref[h] = k.astype(kh_ref.dtype)
            ko_ref[:, sl] = k
            for blk in range(km_ref.shape[0]):
                rows = tm // km_ref.shape[0]
                km_ref[blk, :, sl] = jnp.mean(k[blk * rows:(blk + 1) * rows], axis=0, keepdims=True)

    @pl.when(j == 2)
    def _():
        for h, z in slabs():
            vo_ref[:, h * HEAD_B:(h + 1) * HEAD_B] = z
            vh_ref[h] = z.astype(vh_ref.dtype)


def attn_projection(x, ln_w, w_main, cos_t, sin_t, batch, seq):
    m = batch * seq
    blk = min(seq, MOBA_BLOCK)
    tm = min(seq, 512)
    assert seq % tm == 0 and tm % blk == 0
    nt = seq // tm
    sq = pl.Squeezed()
    tab = pl.BlockSpec((tm, HEAD_B), lambda i, j: (i % nt, 0))
    hm = pl.BlockSpec((sq, N_HEADS_B, tm, HEAD_B), lambda i, j: (i // nt, 0, i % nt, 0))
    rows = pl.BlockSpec((tm, D_B), lambda i, j: (i, 0))
    hm_shape = jax.ShapeDtypeStruct((batch, N_HEADS_B, seq, HEAD_B), BF16)
    return pl.pallas_call(
        _attn_proj_kernel,
        grid=(m // tm, 3),
        in_specs=[
            pl.BlockSpec((tm, D_MODEL), lambda i, j: (i, 0)),
            pl.BlockSpec((1, D_MODEL), lambda i, j: (0, 0)),
            pl.BlockSpec((D_MODEL, D_B), lambda i, j: (0, COL_Q // D_B + j)),
            tab, tab,
        ],
        out_specs=[hm, hm, hm, rows, rows,
                   pl.BlockSpec((tm // blk, 1, D_B), lambda i, j: (i, 0, 0))],
        out_shape=[hm_shape, hm_shape, hm_shape,
                   jax.ShapeDtypeStruct((m, D_B), F32), jax.ShapeDtypeStruct((m, D_B), F32),
                   jax.ShapeDtypeStruct((m // blk, 1, D_B), F32)],
        scratch_shapes=[pltpu.VMEM((tm, D_MODEL), BF16)],
        compiler_params=_cparams(("arbitrary", "arbitrary")),
        name="attn_projection",
    )(x, ln_w.reshape(1, -1), w_main, cos_t, sin_t)


def _rope_rows_kernel(zq_ref, zk_ref, cos_ref, sin_ref, q_ref, k_ref):
    cos_t = cos_ref[...]
    sin_t = sin_ref[...]
    lane = lax.broadcasted_iota(jnp.int32, cos_t.shape, 1)
    for h in range(N_HEADS_B):
        sl = slice(h * HEAD_B, (h + 1) * HEAD_B)
        q_ref[:, sl] = _rope(zq_ref[:, sl], cos_t, sin_t, lane).astype(BF16).astype(F32)
        k_ref[:, sl] = _rope(zk_ref[:, sl], cos_t, sin_t, lane)


def rope_rows(zq, zk, cos_t, sin_t):
    out = jax.ShapeDtypeStruct(zq.shape, F32)
    return pl.pallas_call(_rope_rows_kernel, out_shape=[out, out], name="rope_rows")(zq, zk, cos_t, sin_t)


MOBA_HEADS_PER_STEP = 8


def _moba_prompt_kernel(q_ref, k_ref, v_ref, km_ref, zg_ref, o_ref, sel_ref):
    qb = pl.program_id(1)
    nbatch, nhead, tq = q_ref.shape[0], q_ref.shape[1], q_ref.shape[2]
    nb = km_ref.shape[1]
    seqs = [(b, h) for b in range(nbatch) for h in range(nhead)]
    exp2_scale = HEAD_B ** -0.5 * LOG2_E
    qs = [q_ref[b, h].T for b, h in seqs]

    blk = lax.broadcasted_iota(jnp.int32, (nb, tq), 0)
    for i, (b, h) in enumerate(seqs):
        km = km_ref[b, :, h * HEAD_B:(h + 1) * HEAD_B]
        gate = sum(jnp.dot(piece, qs[i], preferred_element_type=F32)
                   for piece in _split3(km))
        gate = jnp.where(blk < qb, gate, -jnp.inf)
        for n in range(nb):
            g_n = gate[n:n + 1, :]
            tie = jnp.where(blk < n, 1.0, 0.0)
            beats = jnp.where(gate > g_n, 1.0, jnp.where(gate == g_n, tie, 0.0))
            cnt = jnp.sum(beats, axis=0, keepdims=True)
            sel_ref[i, n:n + 1, :] = jnp.where(cnt < MOBA_TOPK, 1.0, 0.0)

    def scores(i, n):
        b, h = seqs[i]
        start = pl.multiple_of(n * tq, tq)
        return jnp.dot(k_ref[b, h, pl.ds(start, tq), :], qs[i], preferred_element_type=F32)

    def softmax_step(s, m, l):
        m_new = jnp.maximum(m, jnp.max(s, axis=0, keepdims=True))
        alpha = jnp.exp2((m - m_new) * exp2_scale)
        p = jnp.exp2((s - m_new) * exp2_scale)
        return m_new, alpha, alpha * l + jnp.sum(p, axis=0, keepdims=True), p.astype(BF16)

    def pv(i, n, p):
        b, h = seqs[i]
        start = pl.multiple_of(n * tq, tq)
        return _bdot_tn(v_ref[b, h, pl.ds(start, tq), :], p)

    def step(n, masked, carry):
        sm = [softmax_step(masked[i], carry[i][0], carry[i][1]) for i in range(len(seqs))]
        acc = [sm[i][1] * carry[i][2] + pv(i, n, sm[i][3]) for i in range(len(seqs))]
        return [(sm[i][0], sm[i][2], acc[i]) for i in range(len(seqs))]

    def body(n, carry):
        s = [scores(i, n) for i in range(len(seqs))]
        masked = [jnp.where(sel_ref[i, pl.ds(n, 1), :] > 0.0, s[i], NEG)
                  for i in range(len(seqs))]
        return tuple(step(n, masked, carry))

    init = tuple((jnp.full((1, tq), NEG, F32), jnp.zeros((1, tq), F32),
                  jnp.zeros((HEAD_B, tq), F32)) for i in range(len(seqs)))
    carry = lax.fori_loop(0, qb, body, init)
    ki = lax.broadcasted_iota(jnp.int32, (tq, tq), 0)
    qi = lax.broadcasted_iota(jnp.int32, (tq, tq), 1)
    masked = [jnp.where(ki <= qi, scores(i, qb), NEG) for i in range(len(seqs))]
    final = step(qb, masked, list(carry))
    for i, (b, h) in enumerate(seqs):
        _, l, acc = final[i]
        sl = slice(h * HEAD_B, (h + 1) * HEAD_B)
        o_ref[b, :, sl] = ((acc / l).T * _silu(zg_ref[b, :, sl])).astype(o_ref.dtype)


def moba_prompt(q_hm, k_hm, v_hm, kmean, z_main):
    batch, _, seq, _ = k_hm.shape
    tq = min(seq, MOBA_BLOCK)
    nb = seq // tq
    hps = MOBA_HEADS_PER_STEP
    w = hps * HEAD_B
    return pl.pallas_call(
        _moba_prompt_kernel,
        grid=(N_HEADS_B // hps, nb),
        in_specs=[
            pl.BlockSpec((batch, hps, tq, HEAD_B), lambda h, i: (0, h, i, 0)),
            pl.BlockSpec((batch, hps, seq, HEAD_B), lambda h, i: (0, h, 0, 0),
                         pipeline_mode=pl.Buffered(1)),
            pl.BlockSpec((batch, hps, seq, HEAD_B), lambda h, i: (0, h, 0, 0),
                         pipeline_mode=pl.Buffered(1)),
            pl.BlockSpec((batch, nb, w), lambda h, i: (0, 0, h)),
            pl.BlockSpec((batch, tq, w), lambda h, i: (0, i, COL_GB // w + h)),
        ],
        out_specs=pl.BlockSpec((batch, tq, w), lambda h, i: (0, i, h)),
        out_shape=jax.ShapeDtypeStruct((batch, seq, D_B), BF16),
        scratch_shapes=[pltpu.VMEM((batch * hps, nb, tq), F32)],
        compiler_params=_cparams(("arbitrary", "arbitrary")),
        name="moba_prompt",
    )(q_hm, k_hm, v_hm, kmean, z_main)


KMEAN_PAGES = 8


def _cache_kmean_kernel(pt_ref, *refs, pages_per_block):
    del pt_ref
    for task in _page_block_mean_tasks(refs[:-1], refs[-1], pages_per_block):
        task()


def cache_kmean(cache_k, page_table, layer):
    n, n_pages = page_table.shape
    page = cache_k.shape[2]
    ppb = MOBA_BLOCK // page
    pps = min(KMEAN_PAGES, n_pages)
    assert n_pages % pps == 0 and pps % ppb == 0
    sq = pl.Squeezed()
    page_spec = lambda i: pl.BlockSpec(
        (sq, sq, page, N_HEADS_B, HEAD_B), lambda b, s, pt, i=i: (layer, pt[b, s * pps + i], 0, 0, 0))
    return pl.pallas_call(
        functools.partial(_cache_kmean_kernel, pages_per_block=ppb),
        grid_spec=pltpu.PrefetchScalarGridSpec(
            num_scalar_prefetch=1,
            grid=(n, n_pages // pps),
            in_specs=[page_spec(i) for i in range(pps)],
            out_specs=pl.BlockSpec((sq, pps // ppb, N_HEADS_B, HEAD_B), lambda b, s, pt: (b, s, 0, 0)),
        ),
        out_shape=jax.ShapeDtypeStruct((n, n_pages // ppb, N_HEADS_B, HEAD_B), F32),
        compiler_params=_cparams(("arbitrary", "arbitrary")),
        name="cache_kmean",
    )(page_table, *([cache_k] * pps))


def _sample_select_kernel(q_ref, km_ref, sel_ref):
    nb = km_ref.shape[0]
    ri = lax.broadcasted_iota(jnp.int32, (nb, nb), 0)
    ci = lax.broadcasted_iota(jnp.int32, (nb, nb), 1)
    lane = lax.broadcasted_iota(jnp.int32, (1, 128), 1)
    blk_row = lax.broadcasted_iota(jnp.int32, (1, nb), 1).astype(F32)
    for h in range(N_HEADS_B):
        km = km_ref[:, h, :]
        g_col = jnp.sum(km * q_ref[h:h + 1, :], axis=1, keepdims=True)
        g_row = jnp.sum(jnp.where(ri == ci, g_col, 0.0), axis=0, keepdims=True)
        beats = (g_col > g_row) | ((g_col == g_row) & (ri < ci))
        rank = jnp.sum(jnp.where(beats, 1.0, 0.0), axis=0, keepdims=True)
        out = jnp.zeros((1, 128), F32)
        for r in range(MOBA_TOPK):
            idx = jnp.sum(jnp.where(rank == float(r), blk_row, 0.0), axis=1, keepdims=True)
            out = jnp.where(lane == r, idx, out)
        sel_ref[h:h + 1, :] = out.astype(jnp.int32)


def sample_select(q_s, kmean_s):
    n, nb = kmean_s.shape[:2]
    sq = pl.Squeezed()
    return pl.pallas_call(
        _sample_select_kernel,
        grid=(n,),
        in_specs=[pl.BlockSpec((sq, N_HEADS_B, HEAD_B), lambda b: (b, 0, 0)),
                  pl.BlockSpec((sq, nb, N_HEADS_B, HEAD_B), lambda b: (b, 0, 0, 0))],
        out_specs=pl.BlockSpec((sq, N_HEADS_B, 128), lambda b: (b, 0, 0)),
        out_shape=jax.ShapeDtypeStruct((n, N_HEADS_B, 128), jnp.int32),
        compiler_params=_cparams(("arbitrary",)),
        name="sample_select",
    )(q_s, kmean_s)


def _sample_attn_kernel(sel_ref, pt_ref, q_ref, kn_ref, vn_ref, zg_ref, ck_ref, cv_ref, o_ref,
                        kbuf, vbuf, sem, *, layer, page, ppb):
    b = pl.program_id(0)
    nb_steps = pl.num_programs(0)
    n_sel = MOBA_TOPK * ppb
    scale = HEAD_B ** -0.5

    def copies(bb, slot):
        out = []
        for h in range(N_HEADS_B):
            for j in range(n_sel):
                pg = pt_ref[bb, sel_ref[bb, h, j // ppb] * ppb + j % ppb]
                dst = pl.ds(j * page, page)
                out.append(pltpu.make_async_copy(ck_ref.at[layer, pg, :, h, :],
                                                 kbuf.at[slot, h, dst, :], sem.at[slot, 0]))
                out.append(pltpu.make_async_copy(cv_ref.at[layer, pg, :, h, :],
                                                 vbuf.at[slot, h, dst, :], sem.at[slot, 1]))
        return out

    slot = b % 2

    @pl.when(b == 0)
    def _():
        for c in copies(0, 0):
            c.start()

    @pl.when(b + 1 < nb_steps)
    def _():
        for c in copies(b + 1, 1 - slot):
            c.start()

    for c in copies(b, slot):
        c.wait()

    for h in range(N_HEADS_B):
        q = q_ref[h]
        k = kbuf[slot, h]
        v = vbuf[slot, h]
        s = jnp.sum(k * q, axis=1, keepdims=True) * scale
        s_own = jnp.sum(kn_ref[h] * q, axis=1, keepdims=True) * scale
        m = jnp.maximum(jnp.max(s, axis=0, keepdims=True), s_own)
        p = jnp.exp(s - m)
        p_own = jnp.exp(s_own - m)
        l = jnp.sum(p, axis=0, keepdims=True) + p_own
        acc = jnp.sum(p * v, axis=0, keepdims=True) + p_own * vn_ref[h]
        o_ref[h] = (acc / l) * _silu(zg_ref[h])


def sample_attn(sel, page_table, q_s, k_new, v_new, zgb, cache_k, cache_v, layer):
    n = q_s.shape[0]
    page = cache_k.shape[2]
    ppb = MOBA_BLOCK // page
    rows = MOBA_TOPK * ppb * page
    sq = pl.Squeezed()
    vec = pl.BlockSpec((sq, N_HEADS_B, 1, HEAD_B), lambda b, sel, pt: (b, 0, 0, 0))
    hbm = pl.BlockSpec(memory_space=pl.ANY)
    return pl.pallas_call(
        functools.partial(_sample_attn_kernel, layer=layer, page=page, ppb=ppb),
        grid_spec=pltpu.PrefetchScalarGridSpec(
            num_scalar_prefetch=2,
            grid=(n,),
            in_specs=[vec, vec, vec, vec, hbm, hbm],
            out_specs=vec,
            scratch_shapes=[pltpu.VMEM((2, N_HEADS_B, rows, HEAD_B), F32),
                            pltpu.VMEM((2, N_HEADS_B, rows, HEAD_B), F32),
                            pltpu.SemaphoreType.DMA((2, 2))],
        ),
        out_shape=jax.ShapeDtypeStruct((n, N_HEADS_B, 1, HEAD_B), F32),
        compiler_params=_cparams(("arbitrary",)),
        name="sample_attn",
    )(sel, page_table, q_s, k_new, v_new, zgb, cache_k, cache_v)


def _merge_out_kernel(oa_ref, ob_ref, pa_ref, pb_ref, zga_ref, zgm_ref, wo_ref, x_ref, fw_ref, y_ref):
    ya = jnp.dot(oa_ref[...], pa_ref[...], preferred_element_type=F32)
    yb = jnp.dot(ob_ref[...], pb_ref[...], preferred_element_type=F32)
    merged = (_sigmoid(zga_ref[...]) * ya + _sigmoid(zgm_ref[...]) * yb).astype(BF16)
    h = x_ref[...] + jnp.dot(merged, wo_ref[...], preferred_element_type=F32)
    ms = jnp.mean(h * h, axis=-1, keepdims=True)
    y_ref[...] = (h * lax.rsqrt(ms + NORM_EPS)) * fw_ref[...]


def merge_and_output(o_a, o_b, p_a, p_b, z_main, w_o, x, final_w):
    m = x.shape[0]
    tm = min(m, 256)
    resident = pl.Buffered(1)
    return pl.pallas_call(
        _merge_out_kernel,
        grid=(m // tm,),
        in_specs=[
            pl.BlockSpec((tm, D_A), lambda i: (i, 0)),
            pl.BlockSpec((tm, D_B), lambda i: (i, 0)),
            pl.BlockSpec((D_A, D_MODEL), lambda i: (0, 0), pipeline_mode=resident),
            pl.BlockSpec((D_B, D_MODEL), lambda i: (0, 0), pipeline_mode=resident),
            pl.BlockSpec((tm, D_MODEL), lambda i: (i, COL_GA // D_MODEL)),
            pl.BlockSpec((tm, D_MODEL), lambda i: (i, COL_GM // D_MODEL)),
            pl.BlockSpec((D_MODEL, D_MODEL), lambda i: (0, 0), pipeline_mode=resident),
            pl.BlockSpec((tm, D_MODEL), lambda i: (i, 0)),
            pl.BlockSpec((1, D_MODEL), lambda i: (0, 0)),
        ],
        out_specs=pl.BlockSpec((tm, D_MODEL), lambda i: (i, 0)),
        out_shape=jax.ShapeDtypeStruct((m, D_MODEL), F32),
        compiler_params=_cparams(("arbitrary",)),
        name="merge_and_output",
    )(o_a, o_b, p_a, p_b, z_main, z_main, w_o, x, final_w.reshape(1, -1))


def _rope_tables(pos):
    half = ROT_DIM // 2
    inv = jnp.power(jnp.float32(ROPE_THETA), -jnp.arange(half, dtype=F32) * (2.0 / ROT_DIM))
    ang = pos.astype(F32)[:, None] * inv[None, :]
    cos, sin = jnp.cos(ang), jnp.sin(ang)
    n = pos.shape[0]
    rest = HEAD_B - ROT_DIM
    cos_t = jnp.concatenate([cos, cos, jnp.ones((n, rest), F32)], axis=1)
    sin_t = jnp.concatenate([-sin, sin, jnp.zeros((n, rest), F32)], axis=1)
    return cos_t, sin_t


def _reorder_rw(a):
    r, w_lo, k, v, a_lo, g = jnp.split(
        a, [D_A, D_A + LORA, 2 * D_A + LORA, 3 * D_A + LORA, 3 * D_A + 2 * LORA], axis=-1)
    return jnp.concatenate([r, k, v, g], axis=-1), jnp.concatenate([w_lo, a_lo], axis=-1)


def kernel(x_prompt, x_sample, state_shift, state_wkv, cache_k, cache_v, page_table, ln_w, w_in, mu,
           w0, w2, a0, a2, k_k, k_a, r_k, lnx_w, lnx_b, p_a, p_b, w_o, final_w):
    depth = ln_w.shape[0]
    assert depth == 1, "single-layer trunk"
    B, T, _ = x_prompt.shape
    DB, TS, _ = x_sample.shape
    assert TS == 1
    n_pages = page_table.shape[1]
    page = cache_k.shape[2]
    past = n_pages * page
    assert past % MOBA_BLOCK == 0 and MOBA_BLOCK % page == 0
    assert past // MOBA_BLOCK >= MOBA_TOPK, "sample attention expects at least top-k full past blocks"
    l = 0
    rw_cols = 4 * D_A + 2 * LORA

    att0 = rw_cols
    src_cols = [0, D_A + LORA, 2 * D_A + LORA, 3 * D_A + 2 * LORA]
    src_cols += [att0 + 4 * D_B + i * W_TILE for i in range(2 * D_MODEL // W_TILE)]
    src_cols += [att0 + 3 * D_B, att0, att0 + D_B, att0 + 2 * D_B]
    w_main = prepare_projection_weight(w_in, l, src_cols)
    w_wa = jnp.concatenate([w_in[l, :, D_A:D_A + LORA],
                            w_in[l, :, 3 * D_A + LORA:3 * D_A + 2 * LORA]], axis=1)
    mu_rw, mu_wa = _reorder_rw(mu[l][None, :])
    zeros = jnp.zeros((LORA, D_A), F32)
    w2p = jnp.concatenate([w2[l], zeros], axis=0).astype(BF16)
    a2p = jnp.concatenate([zeros, a2[l]], axis=0).astype(BF16)
    row = lambda a: a.reshape(1, -1)
    vecs = (row(w0[l]), row(a0[l]), row(k_k[l]), row(k_a[l]), row(r_k[l]), row(lnx_w[l]), row(lnx_b[l]))
    pa_bf, pb_bf, wo_bf = p_a[l].astype(BF16), p_b[l].astype(BF16), w_o[l].astype(BF16)

    rows = jnp.concatenate([x_prompt[:, -1, :], x_sample[:, 0, :]], axis=0)
    xn_rows = rmsnorm_rows(rows, ln_w[l])
    shift_prompt = xn_rows[:B]
    xn_s = xn_rows[B:]
    xp = x_prompt.reshape(B * T, D_MODEL)
    z_p, zwa_p = input_projection(xp, ln_w[l], w_main, w_wa, normalize=True, n_cols=N_Z)
    z_s2, zwa_s2 = input_projection(jnp.concatenate([xn_s, state_shift[l]], axis=0), ln_w[l],
                                    w_main, w_wa, normalize=False, n_cols=N_MAIN)

    z_p3 = z_p.reshape(B, T, N_Z)
    o_a_p, wkv_p, kmean_s = rwkv_prompt(z_p3, zwa_p.reshape(B, T, 2 * LORA), mu_rw, mu_wa, vecs, w2p, a2p,
                                        cache_k, page_table, l)
    cos_p, sin_p = _rope_tables(jnp.arange(T))
    q_hm, k_hm, v_hm, k_rows_p, v_rows_p, kmean_p = attn_projection(xp, ln_w[l], w_main, cos_p, sin_p, B, T)
    nb_p = T // min(T, MOBA_BLOCK)
    o_b_p = moba_prompt(q_hm, k_hm, v_hm, kmean_p.reshape(B, nb_p, D_B), z_p3)
    y_prompt = merge_and_output(o_a_p.reshape(B * T, D_A), o_b_p.reshape(B * T, D_B), pa_bf, pb_bf, z_p,
                                wo_bf, xp, final_w).reshape(B, T, D_MODEL)

    z_s, zprev_s = z_s2[:DB], z_s2[DB:]
    r_s, kf_s, v_s, kk_s, b_s, w_s, g_s = rwkv_rows(
        z_s[:, :4 * D_A], zprev_s[:, :4 * D_A], zwa_s2[:DB], zwa_s2[DB:], mu_rw, mu_wa,
        vecs[0], vecs[1], vecs[2], vecs[3], w2p, a2p)
    hs = lambda a: a.reshape(-1, N_HEADS_A, 1, HEAD_A)
    col = lambda a: a.reshape(-1, N_HEADS_A, HEAD_A, 1)
    wkv_s, o_a_s = rwkv_step(state_wkv[l], hs(w_s), hs(kk_s), hs(b_s), hs(kf_s), hs(r_s),
                             col(v_s), col(g_s), hs(r_k[l])[0], col(lnx_w[l])[0], col(lnx_b[l])[0])
    o_a_s = o_a_s.reshape(DB, D_A).astype(BF16)

    cos_s, sin_s = _rope_tables(past + jnp.arange(TS))
    cos_s = jnp.broadcast_to(cos_s, (DB, HEAD_B))
    sin_s = jnp.broadcast_to(sin_s, (DB, HEAD_B))
    q_rows_s, k_rows_s = rope_rows(z_s[:, COL_Q:COL_Q + D_B], z_s[:, COL_KB:COL_KB + D_B], cos_s, sin_s)
    v_rows_s = z_s[:, COL_VB:COL_VB + D_B]
    if kmean_s is None:
        kmean_s = cache_kmean(cache_k, page_table, l)
    sel = sample_select(q_rows_s.reshape(DB, N_HEADS_B, HEAD_B), kmean_s)[:, :, :MOBA_TOPK]
    hv = lambda a: a.reshape(DB, N_HEADS_B, 1, HEAD_B)
    o_b_s = sample_attn(sel, page_table, hv(q_rows_s), hv(k_rows_s), hv(v_rows_s),
                        hv(z_s[:, COL_GB:COL_GB + D_B]), cache_k, cache_v, l)
    o_b_s = o_b_s.reshape(DB, D_B).astype(BF16)
    y_sample = merge_and_output(o_a_s, o_b_s, pa_bf, pb_bf, z_s, wo_bf, x_sample.reshape(DB, D_MODEL),
                                final_w)

    return (y_prompt,
            y_sample.reshape(DB, TS, D_MODEL),
            shift_prompt[None],
            wkv_p.reshape(1, B, N_HEADS_A, HEAD_A, HEAD_A),
            k_rows_p.reshape(1, B, T, N_HEADS_B, HEAD_B),
            v_rows_p.reshape(1, B, T, N_HEADS_B, HEAD_B),
            xn_s[None],
            wkv_s[None],
            k_rows_s.reshape(1, DB, TS, N_HEADS_B, HEAD_B),
            v_rows_s.reshape(1, DB, TS, N_HEADS_B, HEAD_B))
```

```python
import functools

import jax
import jax.numpy as jnp
from jax import lax
from jax.experimental import pallas as pl
from jax.experimental.pallas import tpu as pltpu

F32 = jnp.float32
BF16 = jnp.bfloat16

D_MODEL = 2048
D_A = D_MODEL // 2
HEAD_A = 64
N_HEADS_A = D_A // HEAD_A
LORA = 64
D_B = D_MODEL // 2
HEAD_B = 128
N_HEADS_B = D_B // HEAD_B
ROT_DIM = HEAD_B // 4
ROPE_THETA = 500000.0
MOBA_BLOCK = 256
MOBA_TOPK = 3
NORM_EPS = 1e-6
GN_EPS = 64e-5
NEG = -1e30
LOG2_E = 1.4426950408889634

COL_R, COL_K, COL_V, COL_G = 0, D_A, 2 * D_A, 3 * D_A
COL_GA = 4 * D_A
COL_GM = COL_GA + D_MODEL
COL_GB = COL_GM + D_MODEL
N_Z = COL_GB + D_B
COL_Q, COL_KB, COL_VB = N_Z, N_Z + D_B, N_Z + 2 * D_B
N_MAIN = COL_VB + D_B

GROUP = 256
HEADS_PER_GROUP = GROUP // HEAD_A
CHUNK = 64
assert CHUNK == HEAD_A
VMEM_LIMIT = 56 * 1024 * 1024


def _cparams(sem):
    return pltpu.CompilerParams(dimension_semantics=sem, vmem_limit_bytes=VMEM_LIMIT)


def _bdot(a, b):
    return jnp.dot(a.astype(BF16), b.astype(BF16), preferred_element_type=F32)


def _bdot_nt(a, b):
    return lax.dot_general(a.astype(BF16), b.astype(BF16), (((1,), (1,)), ((), ())),
                           preferred_element_type=F32)


def _bdot_tn(a, b):
    return lax.dot_general(a.astype(BF16), b.astype(BF16), (((0,), (0,)), ((), ())),
                           preferred_element_type=F32)


def _split3(x):
    hi = x.astype(BF16)
    r1 = x - hi.astype(F32)
    mid = r1.astype(BF16)
    lo = (r1 - mid.astype(F32)).astype(BF16)
    return hi, mid, lo


def _sigmoid(x):
    return 1.0 / (1.0 + jnp.exp(-x))


def _silu(x):
    return x * _sigmoid(x)


def _softplus(x):
    return jnp.maximum(x, 0.0) + jnp.log(1.0 + jnp.exp(-jnp.abs(x)))


def _rmsnorm_rows_kernel(x_ref, w_ref, o_ref):
    x = x_ref[...]
    ms = jnp.mean(x * x, axis=-1, keepdims=True)
    o_ref[...] = (x * lax.rsqrt(ms + NORM_EPS)) * w_ref[...]


def rmsnorm_rows(x, w):
    return pl.pallas_call(
        _rmsnorm_rows_kernel,
        out_shape=jax.ShapeDtypeStruct(x.shape, F32),
        name="rmsnorm_rows",
    )(x, w.reshape(1, -1))


W_TILE = 1024
LANES = 128


def _w_prep_kernel(w_hbm, o_hbm, in_buf, out_buf, in_sem, out_sem, *, layer, starts):
    n = len(starts)
    rows = in_buf.shape[1]

    def in_copy(j, slot):
        off = starts[j] % LANES
        width = W_TILE + (LANES if off else 0)
        return pltpu.make_async_copy(w_hbm.at[layer, :, pl.ds(starts[j] - off, width)],
                                     in_buf.at[slot, :, pl.ds(0, width)], in_sem.at[slot])

    def out_copy(j, slot):
        return pltpu.make_async_copy(out_buf.at[slot], o_hbm.at[:, pl.ds(j * W_TILE, W_TILE)],
                                     out_sem.at[slot])

    in_copy(0, 0).start()
    for j in range(n):
        slot = j % 2
        if j + 1 < n:
            in_copy(j + 1, 1 - slot).start()
        in_copy(j, slot).wait()
        if j >= 2:
            out_copy(j - 2, slot).wait()
        off = starts[j] % LANES
        rc = 256

        def chunk(c, _, slot=slot, off=off):
            r0 = pl.multiple_of(c * rc, rc)
            out_buf[slot, pl.ds(r0, rc), :] = in_buf[slot, pl.ds(r0, rc), off:off + W_TILE].astype(BF16)
            return 0

        lax.fori_loop(0, rows // rc, chunk, 0)
        out_copy(j, slot).start()
    for j in range(max(n - 2, 0), n):
        out_copy(j, j % 2).wait()


def prepare_projection_weight(w_in, layer, starts):
    rows = w_in.shape[1]
    assert rows % 256 == 0
    hbm = pl.BlockSpec(memory_space=pl.ANY)
    return pl.pallas_call(
        functools.partial(_w_prep_kernel, layer=layer, starts=tuple(starts)),
        in_specs=[hbm],
        out_specs=hbm,
        out_shape=jax.ShapeDtypeStruct((rows, len(starts) * W_TILE), BF16),
        scratch_shapes=[pltpu.VMEM((2, rows, W_TILE + LANES), F32),
                        pltpu.VMEM((2, rows, W_TILE), BF16),
                        pltpu.SemaphoreType.DMA((2,)), pltpu.SemaphoreType.DMA((2,))],
        compiler_params=pltpu.CompilerParams(vmem_limit_bytes=VMEM_LIMIT),
        name="prepare_projection_weight",
    )(w_in)


def _proj_kernel(x_ref, lnw_ref, w_ref, wwa_ref, z_ref, zwa_ref, xn_ref, *, normalize):
    @pl.when(pl.program_id(1) == 0)
    def _():
        x = x_ref[...]
        if normalize:
            ms = jnp.mean(x * x, axis=-1, keepdims=True)
            x = (x * lax.rsqrt(ms + NORM_EPS)) * lnw_ref[...]
        xn_ref[...] = x.astype(BF16)
        zwa_ref[...] = jnp.dot(xn_ref[...], wwa_ref[...].astype(BF16), preferred_element_type=F32)

    z_ref[...] = jnp.dot(xn_ref[...], w_ref[...], preferred_element_type=F32)


def input_projection(x, ln_w, w_main, w_wa, *, normalize, n_cols):
    m = x.shape[0]
    tm = min(m, 1024)
    if m < 1024:
        tn = 2048
    else:
        tn = 1536 if n_cols % 1536 == 0 else 1024
    assert m % tm == 0 and n_cols % tn == 0
    return pl.pallas_call(
        functools.partial(_proj_kernel, normalize=normalize),
        grid=(m // tm, n_cols // tn),
        in_specs=[
            pl.BlockSpec((tm, D_MODEL), lambda i, j: (i, 0)),
            pl.BlockSpec((1, D_MODEL), lambda i, j: (0, 0)),
            pl.BlockSpec((D_MODEL, tn), lambda i, j: (0, j)),
            pl.BlockSpec((D_MODEL, 2 * LORA), lambda i, j: (0, 0)),
        ],
        out_specs=[
            pl.BlockSpec((tm, tn), lambda i, j: (i, j)),
            pl.BlockSpec((tm, 2 * LORA), lambda i, j: (i, 0)),
        ],
        out_shape=[
            jax.ShapeDtypeStruct((m, n_cols), F32),
            jax.ShapeDtypeStruct((m, 2 * LORA), F32),
        ],
        scratch_shapes=[pltpu.VMEM((tm, D_MODEL), BF16)],
        compiler_params=_cparams(("arbitrary", "arbitrary")),
        name="input_projection",
    )(x, ln_w.reshape(1, -1), w_main, w_wa)


def _head_mask(n):
    r = lax.broadcasted_iota(jnp.int32, (n, n), 0) // HEAD_A
    c = lax.broadcasted_iota(jnp.int32, (n, n), 1) // HEAD_A
    return r == c


def _segsum(x, ones_bd):
    return jnp.dot(x.astype(BF16), ones_bd, preferred_element_type=F32)


def _rwkv_prep(zm_r, zm_k, zm_v, zm_wa, w0, a0, k_k, k_a, w2p, a2p, ones_bd):
    lane = lax.broadcasted_iota(jnp.int32, zm_wa.shape, 1)
    lora_in = jnp.where(lane < LORA, jnp.tanh(zm_wa), zm_wa)
    ww = _bdot(lora_in, w2p)
    aa = _bdot(lora_in, a2p)
    w_log = -_softplus(-(w0 + ww)) - 0.5
    logw = -jnp.exp(w_log)
    a = _sigmoid(a0 + aa)
    kk = zm_k * k_k
    ss = _segsum(kk * kk, ones_bd)
    kk = kk / jnp.maximum(jnp.sqrt(ss), 1e-12)
    kf = zm_k * (1.0 + (a - 1.0) * k_a)
    return zm_r, kf, zm_v, kk, kk * a, logw


def _rwkv_post(y, r, kf, v, zm_g, r_k, lnx_w, lnx_b, ones_bd):
    inv_n = 1.0 / HEAD_A
    mean = _segsum(y, ones_bd) * inv_n
    d = y - mean
    var = _segsum(d * d, ones_bd) * inv_n
    yn = d * lax.rsqrt(var + GN_EPS) * lnx_w + lnx_b
    bonus = _segsum(r * kf * r_k, ones_bd) * v
    return (yn + bonus) * _silu(zm_g)


def _stack_heads(x, lane_head):
    return jnp.concatenate(
        [jnp.where(lane_head == h, x, 0.0) for h in range(HEADS_PER_GROUP)], axis=0)


def _chunk_precompute(insts, consts, side_tasks=()):
    strict, incl, eye, lane_head, bd_mask = consts
    c = CHUNK
    n = len(insts)
    side_tasks = list(side_tasks)
    n_phases = 8
    per_phase = -(-len(side_tasks) // n_phases)

    def run_side(k=None):
        for _ in range(per_phase if k is None else k):
            if side_tasks:
                side_tasks.pop(0)()

    def bd(p):
        return jnp.where(bd_mask, jnp.concatenate([p] * HEADS_PER_GROUP, axis=0), 0.0).astype(BF16)

    lhs, wt, v_bd, bdkd, p_end = [], [], [], [], []
    for r, kf, v, kk, b, logw, cl in insts:
        cl_last = cl[c - 1:c, :]
        e_neg = jnp.exp(-cl)
        e_end = jnp.exp(cl_last - cl)
        lhs.append(jnp.concatenate([kk * jnp.exp(cl - logw), r * jnp.exp(cl)], axis=0).astype(BF16))
        wt.append(jnp.concatenate([_stack_heads(b * e_neg, lane_head),
                                   _stack_heads(kf * e_neg, lane_head)], axis=0).astype(BF16))
        v_bd.append(_stack_heads(v, lane_head).astype(BF16))
        bdkd.append(jnp.concatenate([b * e_end, kf * e_end], axis=0).astype(BF16))
        p_end.append(jnp.exp(cl_last))
    att = [_bdot_nt(lhs[i], wt[i]) for i in range(n)]
    run_side()
    a_ab = [jnp.where(strict, att[i][:c, :4 * c], 0.0) for i in range(n)]
    a_ak = [jnp.where(strict, att[i][:c, 4 * c:], 0.0) for i in range(n)]
    m_cat = [jnp.concatenate([jnp.where(incl, att[i][c:, :4 * c], 0.0),
                              jnp.where(incl, att[i][c:, 4 * c:], 0.0)], axis=1).astype(BF16)
             for i in range(n)]
    av = [_bdot(a_ak[i], v_bd[i]) for i in range(n)]
    run_side()

    x = [eye - a_ab[i] for i in range(n)]
    p = [_bdot(a_ab[i], bd(a_ab[i])) for i in range(n)]
    run_side()
    for _ in range(4):
        px = [_bdot(jnp.concatenate([p[i], x[i]], axis=0), bd(p[i])) for i in range(n)]
        p = [px[i][:c] for i in range(n)]
        x = [x[i] + px[i][c:] for i in range(n)]
        run_side()
    x = [(x[i] + _bdot(x[i], bd(p[i]))).astype(BF16) for i in range(n)]
    run_side(len(side_tasks))
    return [dict(lhs=lhs[i], av=av[i], t=x[i], m_cat=m_cat[i], v_bd=v_bd[i], v=insts[i][2],
                 bdkd=bdkd[i], p_end=p_end[i]) for i in range(n)]


def _chunk_state_step(states, pres, consts):
    lane_head, bd_mask = consts[3], consts[4]
    c = CHUNK
    n = len(states)
    sh = [_bdot_nt(pres[i]["lhs"], states[i]) for i in range(n)]
    rhs = [sh[i][:c] + pres[i]["av"] for i in range(n)]
    u = [-_bdot(pres[i]["t"], _stack_heads(rhs[i], lane_head)) for i in range(n)]
    y = [sh[i][c:] + _bdot(pres[i]["m_cat"],
                           jnp.concatenate([_stack_heads(u[i], lane_head).astype(BF16),
                                            pres[i]["v_bd"]], axis=0)) for i in range(n)]
    upd = [_bdot_tn(jnp.concatenate([u[i], pres[i]["v"]], axis=0), pres[i]["bdkd"]) for i in range(n)]
    new = [states[i] * pres[i]["p_end"] + jnp.where(bd_mask, upd[i], 0.0) for i in range(n)]
    return y, new


def _page_block_mean_tasks(page_refs, km_ref, pages_per_block):
    def task(j):
        def run():
            blk = jnp.sum(page_refs[j * pages_per_block][...], axis=0)
            for i in range(1, pages_per_block):
                blk = blk + jnp.sum(page_refs[j * pages_per_block + i][...], axis=0)
            km_ref[j] = blk * (1.0 / MOBA_BLOCK)
        return run
    return [task(j) for j in range(len(page_refs) // pages_per_block)]


def _rwkv_prompt_kernel(*refs, n_pages, pages_per_block):
    if n_pages:
        refs = refs[1:]
    (zr_ref, zk_ref, zv_ref, zg_ref, zwa_ref, mur_ref, muk_ref, muv_ref, mug_ref, muwa_ref,
     w0_ref, a0_ref, kk_ref, ka_ref, rk_ref, lnw_ref, lnb_ref, w2_ref, a2_ref) = refs[:19]
    page_refs = refs[19:19 + n_pages]
    refs = refs[19 + n_pages:]
    o_ref, s_out_ref = refs[:2]
    side_tasks = []
    if n_pages:
        side_tasks = _page_block_mean_tasks(page_refs, refs[2], pages_per_block)
        refs = refs[1:]
    s_ref, pr_ref, pk_ref, pv_ref, pg_ref, pwa_ref = refs[2:]
    t = pl.program_id(1)
    nseq, tr = zr_ref.shape[0], zr_ref.shape[1]

    @pl.when(t == 0)
    def _():
        for ref in (s_ref, pr_ref, pk_ref, pv_ref, pg_ref, pwa_ref):
            ref[...] = jnp.zeros_like(ref)

    def shifted(z_ref, prev_ref, mu_ref, i):
        z = z_ref[i]
        row = lax.broadcasted_iota(jnp.int32, z.shape, 0)
        prev = jnp.where(row == 0, prev_ref[i], pltpu.roll(z, 1, 0))
        prev_ref[i] = z[tr - 1:tr, :]
        return z + (prev - z) * mu_ref[...]

    bd_mask = _head_mask(GROUP)
    ones_bd = jnp.where(bd_mask, 1.0, 0.0).astype(BF16)
    ri = lax.broadcasted_iota(jnp.int32, (tr, tr), 0)
    ci = lax.broadcasted_iota(jnp.int32, (tr, tr), 1)
    tri = jnp.where((ri // CHUNK == ci // CHUNK) & (ci <= ri), 1.0, 0.0).astype(BF16)
    rc = lax.broadcasted_iota(jnp.int32, (CHUNK, GROUP), 0)
    lc = lax.broadcasted_iota(jnp.int32, (CHUNK, GROUP), 1)
    sc = lc % CHUNK
    consts = (sc < rc, sc <= rc, jnp.where(sc == rc, 1.0, 0.0), lc // HEAD_A, bd_mask)
    last = t == pl.num_programs(1) - 1

    n_ch = tr // CHUNK
    prepped, insts = [], []
    for i in range(nseq):
        zm_r = shifted(zr_ref, pr_ref, mur_ref, i)
        zm_k = shifted(zk_ref, pk_ref, muk_ref, i)
        zm_v = shifted(zv_ref, pv_ref, muv_ref, i)
        zm_g = shifted(zg_ref, pg_ref, mug_ref, i)
        zm_wa = shifted(zwa_ref, pwa_ref, muwa_ref, i)
        r, kf, v, kk, b, logw = _rwkv_prep(zm_r, zm_k, zm_v, zm_wa, w0_ref[...], a0_ref[...],
                                           kk_ref[...], ka_ref[...], w2_ref[...], a2_ref[...], ones_bd)
        cl = sum(jnp.dot(tri, piece, preferred_element_type=F32) for piece in _split3(logw))
        prepped.append((r, kf, v, zm_g))
        for ch in range(n_ch):
            sl = slice(ch * CHUNK, (ch + 1) * CHUNK)
            insts.append((r[sl], kf[sl], v[sl], kk[sl], b[sl], logw[sl], cl[sl]))
    pres = _chunk_precompute(insts, consts, side_tasks)

    states = [s_ref[i] for i in range(nseq)]
    ys = [[] for _ in range(nseq)]
    for ch in range(n_ch):
        y, states = _chunk_state_step(states, [pres[i * n_ch + ch] for i in range(nseq)], consts)
        for i in range(nseq):
            ys[i].append(y[i])
    for i in range(nseq):
        s_ref[i] = states[i]
        r, kf, v, zm_g = prepped[i]
        o_ref[i] = _rwkv_post(jnp.concatenate(ys[i], axis=0), r, kf, v, zm_g, rk_ref[...],
                              lnw_ref[...], lnb_ref[...], ones_bd).astype(o_ref.dtype)

    @pl.when(last)
    def _():
        for i in range(nseq):
            for h in range(HEADS_PER_GROUP):
                s_out_ref[i, h] = s_ref[i, h * HEAD_A:(h + 1) * HEAD_A, h * HEAD_A:(h + 1) * HEAD_A]


def rwkv_prompt(z_main, z_wa, mu_main, mu_wa, vecs, w2p, a2p, cache_k=None, page_table=None, layer=0):
    w0, a0, k_k, k_a, r_k, lnx_w, lnx_b = vecs
    batch, seq, _ = z_main.shape
    tr = min(seq, 512)
    assert seq % tr == 0 and tr % CHUNK == 0
    nt = seq // tr
    ng = D_A // GROUP

    n_pages = ppb = 0
    if cache_k is not None:
        page = cache_k.shape[2]
        ppb = MOBA_BLOCK // page
        total = page_table.size
        if total % (ng * nt) == 0 and (total // (ng * nt)) % ppb == 0:
            n_pages = total // (ng * nt)

    def zspec(col):
        return pl.BlockSpec((batch, tr, GROUP), lambda g, t, *_, c=col // GROUP: (0, t, c + g))

    def vspec(col=0):
        return pl.BlockSpec((1, GROUP), lambda g, t, *_, c=col // GROUP: (0, c + g))

    in_specs = [
        zspec(COL_R), zspec(COL_K), zspec(COL_V), zspec(COL_G),
        pl.BlockSpec((batch, tr, 2 * LORA), lambda g, t, *_: (0, t, 0)),
        vspec(COL_R), vspec(COL_K), vspec(COL_V), vspec(COL_G),
        pl.BlockSpec((1, 2 * LORA), lambda g, t, *_: (0, 0)),
        vspec(), vspec(), vspec(), vspec(), vspec(), vspec(), vspec(),
        pl.BlockSpec((2 * LORA, GROUP), lambda g, t, *_: (0, g)),
        pl.BlockSpec((2 * LORA, GROUP), lambda g, t, *_: (0, g)),
    ]
    out_specs = [
        pl.BlockSpec((batch, tr, GROUP), lambda g, t, *_: (0, t, g)),
        pl.BlockSpec((batch, HEADS_PER_GROUP, HEAD_A, HEAD_A), lambda g, t, *_: (0, g, 0, 0)),
    ]
    out_shape = [
        jax.ShapeDtypeStruct((batch, seq, D_A), BF16),
        jax.ShapeDtypeStruct((batch, N_HEADS_A, HEAD_A, HEAD_A), F32),
    ]
    operands = [z_main, z_main, z_main, z_main, z_wa,
                mu_main, mu_main, mu_main, mu_main, mu_wa,
                w0, a0, k_k, k_a, r_k, lnx_w, lnx_b, w2p, a2p]
    if n_pages:
        sq = pl.Squeezed()
        for i in range(n_pages):
            in_specs.append(pl.BlockSpec(
                (sq, sq, page, N_HEADS_B, HEAD_B),
                lambda g, t, pt, i=i: (layer, pt[(g * nt + t) * n_pages + i], 0, 0, 0)))
        operands += [cache_k] * n_pages
        out_specs.append(pl.BlockSpec((n_pages // ppb, N_HEADS_B, HEAD_B),
                                      lambda g, t, pt: (g * nt + t, 0, 0)))
        out_shape.append(jax.ShapeDtypeStruct((page_table.size // ppb, N_HEADS_B, HEAD_B), F32))
    row = lambda n: pltpu.VMEM((batch, 1, n), F32)
    scratch = [pltpu.VMEM((batch, GROUP, GROUP), F32),
               row(GROUP), row(GROUP), row(GROUP), row(GROUP), row(2 * LORA)]
    kern = functools.partial(_rwkv_prompt_kernel, n_pages=n_pages, pages_per_block=ppb)
    params = _cparams(("arbitrary", "arbitrary"))
    if not n_pages:
        o_a, wkv = pl.pallas_call(kern, grid=(ng, nt), in_specs=in_specs, out_specs=out_specs,
                                  out_shape=out_shape, scratch_shapes=scratch,
                                  compiler_params=params, name="rwkv_prompt")(*operands)
        return o_a, wkv, None
    o_a, wkv, kmean = pl.pallas_call(
        kern,
        grid_spec=pltpu.PrefetchScalarGridSpec(
            num_scalar_prefetch=1, grid=(ng, nt), in_specs=in_specs, out_specs=out_specs,
            scratch_shapes=scratch),
        out_shape=out_shape, compiler_params=params, name="rwkv_prompt",
    )(page_table.reshape(-1), *operands)
    n = page_table.shape[0]
    return o_a, wkv, kmean.reshape(n, -1, N_HEADS_B, HEAD_B)


def _rwkv_rows_kernel(z_ref, zp_ref, zwa_ref, zpwa_ref, mu_ref, muwa_ref,
                      w0_ref, a0_ref, kk_ref, ka_ref, w2_ref, a2_ref,
                      r_ref, kf_ref, v_ref, kkn_ref, b_ref, w_ref, g_ref):
    def shifted(z, zp, mu):
        return z + (zp - z) * mu

    mu = mu_ref[...]
    z = z_ref[...]
    zp = zp_ref[...]
    zm = [shifted(z[:, c:c + D_A], zp[:, c:c + D_A], mu[:, c:c + D_A])
          for c in (COL_R, COL_K, COL_V, COL_G)]
    zm_wa = shifted(zwa_ref[...], zpwa_ref[...], muwa_ref[...])
    ones_bd = jnp.where(_head_mask(D_A), 1.0, 0.0).astype(BF16)
    r, kf, v, kk, b, logw = _rwkv_prep(zm[0], zm[1], zm[2], zm_wa, w0_ref[...], a0_ref[...],
                                       kk_ref[...], ka_ref[...], w2_ref[...], a2_ref[...], ones_bd)
    r_ref[...] = r
    kf_ref[...] = kf
    kkn_ref[...] = kk
    b_ref[...] = b
    w_ref[...] = jnp.exp(logw)
    v_t = v.T
    g_t = zm[3].T
    for i in range(v_ref.shape[0]):
        v_ref[i] = v_t[:, i:i + 1]
        g_ref[i] = g_t[:, i:i + 1]


def rwkv_rows(z_rw, zp_rw, z_wa, zp_wa, mu_rw, mu_wa, w0, a0, k_k, k_a, w2p, a2p):
    n = z_rw.shape[0]
    out = jax.ShapeDtypeStruct((n, D_A), F32)
    col = jax.ShapeDtypeStruct((n, D_A, 1), F32)
    return pl.pallas_call(
        _rwkv_rows_kernel,
        out_shape=[out, out, col, out, out, out, col],
        compiler_params=pltpu.CompilerParams(vmem_limit_bytes=VMEM_LIMIT),
        name="rwkv_rows",
    )(z_rw, zp_rw, z_wa, zp_wa, mu_rw, mu_wa, w0, a0, k_k, k_a, w2p, a2p)


def _rwkv_step_kernel(s_ref, w_ref, kk_ref, b_ref, kf_ref, r_ref, v_ref, g_ref,
                      rk_ref, lnw_ref, lnb_ref, s_out_ref, o_ref):
    S = s_ref[...]
    w = w_ref[...]
    kk = kk_ref[...]
    b = b_ref[...]
    kf = kf_ref[...]
    r = r_ref[...]
    v = v_ref[...]
    sa = -jnp.sum(S * kk, axis=-1, keepdims=True)
    S = S * w + sa * b + v * kf
    s_out_ref[...] = S
    y = jnp.sum(S * r, axis=-1, keepdims=True)
    mean = jnp.mean(y, axis=1, keepdims=True)
    d = y - mean
    var = jnp.mean(d * d, axis=1, keepdims=True)
    yn = d * lax.rsqrt(var + GN_EPS) * lnw_ref[...] + lnb_ref[...]
    bonus = jnp.sum(r * kf * rk_ref[...], axis=-1, keepdims=True) * v
    o_ref[...] = (yn + bonus) * _silu(g_ref[...])


def rwkv_step(state, w, kk, b, kf, r, v, g, r_k, lnx_w, lnx_b):
    n = state.shape[0]
    per = 4 if n % 4 == 0 else 1
    h = per * N_HEADS_A
    flat = lambda a: a.reshape((n * N_HEADS_A,) + a.shape[2:])
    tile = lambda a: jnp.tile(a, (per, 1, 1))
    lane_vec = pl.BlockSpec((h, 1, HEAD_A), lambda i: (i, 0, 0))
    col_vec = pl.BlockSpec((h, HEAD_A, 1), lambda i: (i, 0, 0))
    s_new, o = pl.pallas_call(
        _rwkv_step_kernel,
        grid=(n // per,),
        in_specs=[pl.BlockSpec((h, HEAD_A, HEAD_A), lambda i: (i, 0, 0)),
                  lane_vec, lane_vec, lane_vec, lane_vec, lane_vec, col_vec, col_vec,
                  pl.BlockSpec((h, 1, HEAD_A), lambda i: (0, 0, 0)),
                  pl.BlockSpec((h, HEAD_A, 1), lambda i: (0, 0, 0)),
                  pl.BlockSpec((h, HEAD_A, 1), lambda i: (0, 0, 0))],
        out_specs=[pl.BlockSpec((h, HEAD_A, HEAD_A), lambda i: (i, 0, 0)), col_vec],
        out_shape=[jax.ShapeDtypeStruct((n * N_HEADS_A, HEAD_A, HEAD_A), F32),
                   jax.ShapeDtypeStruct((n * N_HEADS_A, HEAD_A, 1), F32)],
        compiler_params=_cparams(("arbitrary",)),
        name="rwkv_step",
    )(flat(state), flat(w), flat(kk), flat(b), flat(kf), flat(r), flat(v), flat(g),
      tile(r_k), tile(lnx_w), tile(lnx_b))
    return s_new.reshape(state.shape), o.reshape(n, N_HEADS_A, HEAD_A, 1)


def _rope(x, cos_t, sin_t, lane):
    partner = jnp.where(lane < ROT_DIM // 2, pltpu.roll(x, HEAD_B - ROT_DIM // 2, 1),
                        pltpu.roll(x, ROT_DIM // 2, 1))
    return x * cos_t + partner * sin_t


def _attn_proj_kernel(x_ref, lnw_ref, w_ref, cos_ref, sin_ref,
                      qh_ref, kh_ref, vh_ref, ko_ref, vo_ref, km_ref, xn_ref):
    tm = x_ref.shape[0]
    slab = 2 * HEAD_B

    x = x_ref[...]
    ms = jnp.mean(x * x, axis=-1, keepdims=True)
    xn_ref[...] = ((x * lax.rsqrt(ms + NORM_EPS)) * lnw_ref[...]).astype(BF16)

    cos_t = cos_ref[...]
    sin_t = sin_ref[...]
    lane = lax.broadcasted_iota(jnp.int32, cos_t.shape, 1)

    def slabs(part):
        for sidx in range(D_B // slab):
            c0 = part * D_B + sidx * slab
            z = jnp.dot(xn_ref[...], w_ref[:, c0:c0 + slab], preferred_element_type=F32)
            for hh in range(slab // HEAD_B):
                h = sidx * (slab // HEAD_B) + hh
                yield h, z[:, hh * HEAD_B:(hh + 1) * HEAD_B]

    for h, z in slabs(0):
        qh_ref[h] = _rope(z, cos_t, sin_t, lane).astype(qh_ref.dtype)

    for h, z in slabs(1):
        sl = slice(h * HEAD_B, (h + 1) * HEAD_B)
        k = _rope(z, cos_t, sin_t, lane)
        kh_ref[h] = k.astype(kh_ref.dtype)
        ko_ref[:, sl] = k
        for blk in range(km_ref.shape[0]):
            rows = tm // km_ref.shape[0]
            km_ref[blk, :, sl] = jnp.mean(k[blk * rows:(blk + 1) * rows], axis=0, keepdims=True)

    for h, z in slabs(2):
        vo_ref[:, h * HEAD_B:(h + 1) * HEAD_B] = z
        vh_ref[h] = z.astype(vh_ref.dtype)


def attn_projection(x, ln_w, w_main, cos_t, sin_t, batch, seq):
    m = batch * seq
    blk = min(seq, MOBA_BLOCK)
    tm = min(seq, 512)
    assert seq % tm == 0 and tm % blk == 0
    nt = seq // tm
    sq = pl.Squeezed()
    tab = pl.BlockSpec((tm, HEAD_B), lambda i: (i % nt, 0))
    hm = pl.BlockSpec((sq, N_HEADS_B, tm, HEAD_B), lambda i: (i // nt, 0, i % nt, 0))
    rows = pl.BlockSpec((tm, D_B), lambda i: (i, 0))
    hm_shape = jax.ShapeDtypeStruct((batch, N_HEADS_B, seq, HEAD_B), BF16)
    assert COL_Q % (3 * D_B) == 0
    return pl.pallas_call(
        _attn_proj_kernel,
        grid=(m // tm,),
        in_specs=[
            pl.BlockSpec((tm, D_MODEL), lambda i: (i, 0)),
            pl.BlockSpec((1, D_MODEL), lambda i: (0, 0)),
            pl.BlockSpec((D_MODEL, 3 * D_B), lambda i: (0, COL_Q // (3 * D_B)),
                         pipeline_mode=pl.Buffered(1)),
            tab, tab,
        ],
        out_specs=[hm, hm, hm, rows, rows,
                   pl.BlockSpec((tm // blk, 1, D_B), lambda i: (i, 0, 0))],
        out_shape=[hm_shape, hm_shape, hm_shape,
                   jax.ShapeDtypeStruct((m, D_B), F32), jax.ShapeDtypeStruct((m, D_B), F32),
                   jax.ShapeDtypeStruct((m // blk, 1, D_B), F32)],
        scratch_shapes=[pltpu.VMEM((tm, D_MODEL), BF16)],
        compiler_params=_cparams(("arbitrary",)),
        name="attn_projection",
    )(x, ln_w.reshape(1, -1), w_main, cos_t, sin_t)


def _rope_rows_kernel(zq_ref, zk_ref, cos_ref, sin_ref, q_ref, k_ref):
    cos_t = cos_ref[...]
    sin_t = sin_ref[...]
    lane = lax.broadcasted_iota(jnp.int32, cos_t.shape, 1)
    for h in range(N_HEADS_B):
        sl = slice(h * HEAD_B, (h + 1) * HEAD_B)
        q_ref[:, sl] = _rope(zq_ref[:, sl], cos_t, sin_t, lane).astype(BF16).astype(F32)
        k_ref[:, sl] = _rope(zk_ref[:, sl], cos_t, sin_t, lane)


def rope_rows(zq, zk, cos_t, sin_t):
    out = jax.ShapeDtypeStruct(zq.shape, F32)
    return pl.pallas_call(_rope_rows_kernel, out_shape=[out, out], name="rope_rows")(zq, zk, cos_t, sin_t)


MOBA_HEADS_PER_STEP = 8


def _moba_prompt_kernel(q_ref, k_ref, v_ref, km_ref, zg_ref, o_ref, sel_ref):
    qb = pl.program_id(1)
    nbatch, nhead, tq = q_ref.shape[0], q_ref.shape[1], q_ref.shape[2]
    nb = km_ref.shape[1]
    seqs = [(b, h) for b in range(nbatch) for h in range(nhead)]
    exp2_scale = HEAD_B ** -0.5 * LOG2_E
    qs = [q_ref[b, h].T for b, h in seqs]

    blk = lax.broadcasted_iota(jnp.int32, (nb, tq), 0)
    for i, (b, h) in enumerate(seqs):
        km = km_ref[b, :, h * HEAD_B:(h + 1) * HEAD_B]
        gate = sum(jnp.dot(piece, qs[i], preferred_element_type=F32)
                   for piece in _split3(km))
        gate = jnp.where(blk < qb, gate, -jnp.inf)
        for n in range(nb):
            g_n = gate[n:n + 1, :]
            tie = jnp.where(blk < n, 1.0, 0.0)
            beats = jnp.where(gate > g_n, 1.0, jnp.where(gate == g_n, tie, 0.0))
            cnt = jnp.sum(beats, axis=0, keepdims=True)
            sel_ref[i, n:n + 1, :] = jnp.where(cnt < MOBA_TOPK, 1.0, 0.0)

    def scores(i, n):
        b, h = seqs[i]
        start = pl.multiple_of(n * tq, tq)
        return jnp.dot(k_ref[b, h, pl.ds(start, tq), :], qs[i], preferred_element_type=F32)

    def softmax_step(s, m, l):
        m_new = jnp.maximum(m, jnp.max(s, axis=0, keepdims=True))
        alpha = jnp.exp2((m - m_new) * exp2_scale)
        p = jnp.exp2((s - m_new) * exp2_scale)
        return m_new, alpha, alpha * l + jnp.sum(p, axis=0, keepdims=True), p.astype(BF16)

    def pv(i, n, p):
        b, h = seqs[i]
        start = pl.multiple_of(n * tq, tq)
        return _bdot_tn(v_ref[b, h, pl.ds(start, tq), :], p)

    def step(n, masked, carry):
        sm = [softmax_step(masked[i], carry[i][0], carry[i][1]) for i in range(len(seqs))]
        acc = [sm[i][1] * carry[i][2] + pv(i, n, sm[i][3]) for i in range(len(seqs))]
        return [(sm[i][0], sm[i][2], acc[i]) for i in range(len(seqs))]

    def body(n, carry):
        s = [scores(i, n) for i in range(len(seqs))]
        masked = [jnp.where(sel_ref[i, pl.ds(n, 1), :] > 0.0, s[i], NEG)
                  for i in range(len(seqs))]
        return tuple(step(n, masked, carry))

    init = tuple((jnp.full((1, tq), NEG, F32), jnp.zeros((1, tq), F32),
                  jnp.zeros((HEAD_B, tq), F32)) for i in range(len(seqs)))
    carry = lax.fori_loop(0, qb, body, init)
    ki = lax.broadcasted_iota(jnp.int32, (tq, tq), 0)
    qi = lax.broadcasted_iota(jnp.int32, (tq, tq), 1)
    masked = [jnp.where(ki <= qi, scores(i, qb), NEG) for i in range(len(seqs))]
    final = step(qb, masked, list(carry))
    for i, (b, h) in enumerate(seqs):
        _, l, acc = final[i]
        sl = slice(h * HEAD_B, (h + 1) * HEAD_B)
        o_ref[b, :, sl] = ((acc / l).T * _silu(zg_ref[b, :, sl])).astype(o_ref.dtype)


def moba_prompt(q_hm, k_hm, v_hm, kmean, z_main):
    batch, _, seq, _ = k_hm.shape
    tq = min(seq, MOBA_BLOCK)
    nb = seq // tq
    hps = MOBA_HEADS_PER_STEP
    w = hps * HEAD_B
    return pl.pallas_call(
        _moba_prompt_kernel,
        grid=(N_HEADS_B // hps, nb),
        in_specs=[
            pl.BlockSpec((batch, hps, tq, HEAD_B), lambda h, i: (0, h, i, 0)),
            pl.BlockSpec((batch, hps, seq, HEAD_B), lambda h, i: (0, h, 0, 0),
                         pipeline_mode=pl.Buffered(1)),
            pl.BlockSpec((batch, hps, seq, HEAD_B), lambda h, i: (0, h, 0, 0),
                         pipeline_mode=pl.Buffered(1)),
            pl.BlockSpec((batch, nb, w), lambda h, i: (0, 0, h)),
            pl.BlockSpec((batch, tq, w), lambda h, i: (0, i, COL_GB // w + h)),
        ],
        out_specs=pl.BlockSpec((batch, tq, w), lambda h, i: (0, i, h)),
        out_shape=jax.ShapeDtypeStruct((batch, seq, D_B), BF16),
        scratch_shapes=[pltpu.VMEM((batch * hps, nb, tq), F32)],
        compiler_params=_cparams(("arbitrary", "arbitrary")),
        name="moba_prompt",
    )(q_hm, k_hm, v_hm, kmean, z_main)


KMEAN_PAGES = 8


def _cache_kmean_kernel(pt_ref, *refs, pages_per_block):
    del pt_ref
    for task in _page_block_mean_tasks(refs[:-1], refs[-1], pages_per_block):
        task()


def cache_kmean(cache_k, page_table, layer):
    n, n_pages = page_table.shape
    page = cache_k.shape[2]
    ppb = MOBA_BLOCK // page
    pps = min(KMEAN_PAGES, n_pages)
    assert n_pages % pps == 0 and pps % ppb == 0
    sq = pl.Squeezed()
    page_spec = lambda i: pl.BlockSpec(
        (sq, sq, page, N_HEADS_B, HEAD_B), lambda b, s, pt, i=i: (layer, pt[b, s * pps + i], 0, 0, 0))
    return pl.pallas_call(
        functools.partial(_cache_kmean_kernel, pages_per_block=ppb),
        grid_spec=pltpu.PrefetchScalarGridSpec(
            num_scalar_prefetch=1,
            grid=(n, n_pages // pps),
            in_specs=[page_spec(i) for i in range(pps)],
            out_specs=pl.BlockSpec((sq, pps // ppb, N_HEADS_B, HEAD_B), lambda b, s, pt: (b, s, 0, 0)),
        ),
        out_shape=jax.ShapeDtypeStruct((n, n_pages // ppb, N_HEADS_B, HEAD_B), F32),
        compiler_params=_cparams(("arbitrary", "arbitrary")),
        name="cache_kmean",
    )(page_table, *([cache_k] * pps))


def _sample_select_kernel(q_ref, km_ref, sel_ref):
    nb = km_ref.shape[0]
    ri = lax.broadcasted_iota(jnp.int32, (nb, nb), 0)
    ci = lax.broadcasted_iota(jnp.int32, (nb, nb), 1)
    lane = lax.broadcasted_iota(jnp.int32, (1, 128), 1)
    blk_row = lax.broadcasted_iota(jnp.int32, (1, nb), 1).astype(F32)
    for h in range(N_HEADS_B):
        km = km_ref[:, h, :]
        g_col = jnp.sum(km * q_ref[h:h + 1, :], axis=1, keepdims=True)
        g_row = jnp.sum(jnp.where(ri == ci, g_col, 0.0), axis=0, keepdims=True)
        beats = (g_col > g_row) | ((g_col == g_row) & (ri < ci))
        rank = jnp.sum(jnp.where(beats, 1.0, 0.0), axis=0, keepdims=True)
        out = jnp.zeros((1, 128), F32)
        for r in range(MOBA_TOPK):
            idx = jnp.sum(jnp.where(rank == float(r), blk_row, 0.0), axis=1, keepdims=True)
            out = jnp.where(lane == r, idx, out)
        sel_ref[h:h + 1, :] = out.astype(jnp.int32)


def sample_select(q_s, kmean_s):
    n, nb = kmean_s.shape[:2]
    sq = pl.Squeezed()
    return pl.pallas_call(
        _sample_select_kernel,
        grid=(n,),
        in_specs=[pl.BlockSpec((sq, N_HEADS_B, HEAD_B), lambda b: (b, 0, 0)),
                  pl.BlockSpec((sq, nb, N_HEADS_B, HEAD_B), lambda b: (b, 0, 0, 0))],
        out_specs=pl.BlockSpec((sq, N_HEADS_B, 128), lambda b: (b, 0, 0)),
        out_shape=jax.ShapeDtypeStruct((n, N_HEADS_B, 128), jnp.int32),
        compiler_params=_cparams(("arbitrary",)),
        name="sample_select",
    )(q_s, kmean_s)


def _sample_attn_kernel(sel_ref, pt_ref, q_ref, kn_ref, vn_ref, zg_ref, ck_ref, cv_ref, o_ref,
                        kbuf, vbuf, sem, *, layer, page, ppb):
    b = pl.program_id(0)
    nb_steps = pl.num_programs(0)
    n_sel = MOBA_TOPK * ppb
    scale = HEAD_B ** -0.5

    def copies(bb, slot):
        out = []
        for h in range(N_HEADS_B):
            for j in range(n_sel):
                pg = pt_ref[bb, sel_ref[bb, h, j // ppb] * ppb + j % ppb]
                dst = pl.ds(j * page, page)
                out.append(pltpu.make_async_copy(ck_ref.at[layer, pg, :, h, :],
                                                 kbuf.at[slot, h, dst, :], sem.at[slot, 0]))
                out.append(pltpu.make_async_copy(cv_ref.at[layer, pg, :, h, :],
                                                 vbuf.at[slot, h, dst, :], sem.at[slot, 1]))
        return out

    slot = b % 2

    @pl.when(b == 0)
    def _():
        for c in copies(0, 0):
            c.start()

    @pl.when(b + 1 < nb_steps)
    def _():
        for c in copies(b + 1, 1 - slot):
            c.start()

    for c in copies(b, slot):
        c.wait()

    for h in range(N_HEADS_B):
        q = q_ref[h]
        k = kbuf[slot, h]
        v = vbuf[slot, h]
        s = jnp.sum(k * q, axis=1, keepdims=True) * scale
        s_own = jnp.sum(kn_ref[h] * q, axis=1, keepdims=True) * scale
        m = jnp.maximum(jnp.max(s, axis=0, keepdims=True), s_own)
        p = jnp.exp(s - m)
        p_own = jnp.exp(s_own - m)
        l = jnp.sum(p, axis=0, keepdims=True) + p_own
        acc = jnp.sum(p * v, axis=0, keepdims=True) + p_own * vn_ref[h]
        o_ref[h] = (acc / l) * _silu(zg_ref[h])


def sample_attn(sel, page_table, q_s, k_new, v_new, zgb, cache_k, cache_v, layer):
    n = q_s.shape[0]
    page = cache_k.shape[2]
    ppb = MOBA_BLOCK // page
    rows = MOBA_TOPK * ppb * page
    sq = pl.Squeezed()
    vec = pl.BlockSpec((sq, N_HEADS_B, 1, HEAD_B), lambda b, sel, pt: (b, 0, 0, 0))
    hbm = pl.BlockSpec(memory_space=pl.ANY)
    return pl.pallas_call(
        functools.partial(_sample_attn_kernel, layer=layer, page=page, ppb=ppb),
        grid_spec=pltpu.PrefetchScalarGridSpec(
            num_scalar_prefetch=2,
            grid=(n,),
            in_specs=[vec, vec, vec, vec, hbm, hbm],
            out_specs=vec,
            scratch_shapes=[pltpu.VMEM((2, N_HEADS_B, rows, HEAD_B), F32),
                            pltpu.VMEM((2, N_HEADS_B, rows, HEAD_B), F32),
                            pltpu.SemaphoreType.DMA((2, 2))],
        ),
        out_shape=jax.ShapeDtypeStruct((n, N_HEADS_B, 1, HEAD_B), F32),
        compiler_params=_cparams(("arbitrary",)),
        name="sample_attn",
    )(sel, page_table, q_s, k_new, v_new, zgb, cache_k, cache_v)


def _merge_out_kernel(oa_ref, ob_ref, pa_ref, pb_ref, zga_ref, zgm_ref, wo_ref, x_ref, fw_ref, y_ref):
    ya = jnp.dot(oa_ref[...], pa_ref[...], preferred_element_type=F32)
    yb = jnp.dot(ob_ref[...], pb_ref[...], preferred_element_type=F32)
    merged = (_sigmoid(zga_ref[...]) * ya + _sigmoid(zgm_ref[...]) * yb).astype(BF16)
    h = x_ref[...] + jnp.dot(merged, wo_ref[...], preferred_element_type=F32)
    ms = jnp.mean(h * h, axis=-1, keepdims=True)
    y_ref[...] = (h * lax.rsqrt(ms + NORM_EPS)) * fw_ref[...]


def merge_and_output(o_a, o_b, p_a, p_b, z_main, w_o, x, final_w):
    m = x.shape[0]
    tm = min(m, 256)
    resident = pl.Buffered(1)
    return pl.pallas_call(
        _merge_out_kernel,
        grid=(m // tm,),
        in_specs=[
            pl.BlockSpec((tm, D_A), lambda i: (i, 0)),
            pl.BlockSpec((tm, D_B), lambda i: (i, 0)),
            pl.BlockSpec((D_A, D_MODEL), lambda i: (0, 0), pipeline_mode=resident),
            pl.BlockSpec((D_B, D_MODEL), lambda i: (0, 0), pipeline_mode=resident),
            pl.BlockSpec((tm, D_MODEL), lambda i: (i, COL_GA // D_MODEL)),
            pl.BlockSpec((tm, D_MODEL), lambda i: (i, COL_GM // D_MODEL)),
            pl.BlockSpec((D_MODEL, D_MODEL), lambda i: (0, 0), pipeline_mode=resident),
            pl.BlockSpec((tm, D_MODEL), lambda i: (i, 0)),
            pl.BlockSpec((1, D_MODEL), lambda i: (0, 0)),
        ],
        out_specs=pl.BlockSpec((tm, D_MODEL), lambda i: (i, 0)),
        out_shape=jax.ShapeDtypeStruct((m, D_MODEL), F32),
        compiler_params=_cparams(("arbitrary",)),
        name="merge_and_output",
    )(o_a, o_b, p_a, p_b, z_main, z_main, w_o, x, final_w.reshape(1, -1))


def _rope_tables(pos):
    half = ROT_DIM // 2
    inv = jnp.power(jnp.float32(ROPE_THETA), -jnp.arange(half, dtype=F32) * (2.0 / ROT_DIM))
    ang = pos.astype(F32)[:, None] * inv[None, :]
    cos, sin = jnp.cos(ang), jnp.sin(ang)
    n = pos.shape[0]
    rest = HEAD_B - ROT_DIM
    cos_t = jnp.concatenate([cos, cos, jnp.ones((n, rest), F32)], axis=1)
    sin_t = jnp.concatenate([-sin, sin, jnp.zeros((n, rest), F32)], axis=1)
    return cos_t, sin_t


def _reorder_rw(a):
    r, w_lo, k, v, a_lo, g = jnp.split(
        a, [D_A, D_A + LORA, 2 * D_A + LORA, 3 * D_A + LORA, 3 * D_A + 2 * LORA], axis=-1)
    return jnp.concatenate([r, k, v, g], axis=-1), jnp.concatenate([w_lo, a_lo], axis=-1)


def kernel(x_prompt, x_sample, state_shift, state_wkv, cache_k, cache_v, page_table, ln_w, w_in, mu,
           w0, w2, a0, a2, k_k, k_a, r_k, lnx_w, lnx_b, p_a, p_b, w_o, final_w):
    depth = ln_w.shape[0]
    assert depth == 1, "single-layer trunk"
    B, T, _ = x_prompt.shape
    DB, TS, _ = x_sample.shape
    assert TS == 1
    n_pages = page_table.shape[1]
    page = cache_k.shape[2]
    past = n_pages * page
    assert past % MOBA_BLOCK == 0 and MOBA_BLOCK % page == 0
    assert past // MOBA_BLOCK >= MOBA_TOPK, "sample attention expects at least top-k full past blocks"
    l = 0
    rw_cols = 4 * D_A + 2 * LORA

    att0 = rw_cols
    src_cols = [0, D_A + LORA, 2 * D_A + LORA, 3 * D_A + 2 * LORA]
    src_cols += [att0 + 4 * D_B + i * W_TILE for i in range(2 * D_MODEL // W_TILE)]
    src_cols += [att0 + 3 * D_B, att0, att0 + D_B, att0 + 2 * D_B]
    w_main = prepare_projection_weight(w_in, l, src_cols)
    w_wa = jnp.concatenate([w_in[l, :, D_A:D_A + LORA],
                            w_in[l, :, 3 * D_A + LORA:3 * D_A + 2 * LORA]], axis=1)
    mu_rw, mu_wa = _reorder_rw(mu[l][None, :])
    zeros = jnp.zeros((LORA, D_A), F32)
    w2p = jnp.concatenate([w2[l], zeros], axis=0).astype(BF16)
    a2p = jnp.concatenate([zeros, a2[l]], axis=0).astype(BF16)
    row = lambda a: a.reshape(1, -1)
    vecs = (row(w0[l]), row(a0[l]), row(k_k[l]), row(k_a[l]), row(r_k[l]), row(lnx_w[l]), row(lnx_b[l]))
    pa_bf, pb_bf, wo_bf = p_a[l].astype(BF16), p_b[l].astype(BF16), w_o[l].astype(BF16)

    rows = jnp.concatenate([x_prompt[:, -1, :], x_sample[:, 0, :]], axis=0)
    xn_rows = rmsnorm_rows(rows, ln_w[l])
    shift_prompt = xn_rows[:B]
    xn_s = xn_rows[B:]
    xp = x_prompt.reshape(B * T, D_MODEL)
    z_p, zwa_p = input_projection(xp, ln_w[l], w_main, w_wa, normalize=True, n_cols=N_Z)
    z_s2, zwa_s2 = input_projection(jnp.concatenate([xn_s, state_shift[l]], axis=0), ln_w[l],
                                    w_main, w_wa, normalize=False, n_cols=N_MAIN)

    z_p3 = z_p.reshape(B, T, N_Z)
    o_a_p, wkv_p, kmean_s = rwkv_prompt(z_p3, zwa_p.reshape(B, T, 2 * LORA), mu_rw, mu_wa, vecs, w2p, a2p,
                                        cache_k, page_table, l)
    cos_p, sin_p = _rope_tables(jnp.arange(T))
    q_hm, k_hm, v_hm, k_rows_p, v_rows_p, kmean_p = attn_projection(xp, ln_w[l], w_main, cos_p, sin_p, B, T)
    nb_p = T // min(T, MOBA_BLOCK)
    o_b_p = moba_prompt(q_hm, k_hm, v_hm, kmean_p.reshape(B, nb_p, D_B), z_p3)
    y_prompt = merge_and_output(o_a_p.reshape(B * T, D_A), o_b_p.reshape(B * T, D_B), pa_bf, pb_bf, z_p,
                                wo_bf, xp, final_w).reshape(B, T, D_MODEL)

    z_s, zprev_s = z_s2[:DB], z_s2[DB:]
    r_s, kf_s, v_s, kk_s, b_s, w_s, g_s = rwkv_rows(
        z_s[:, :4 * D_A], zprev_s[:, :4 * D_A], zwa_s2[:DB], zwa_s2[DB:], mu_rw, mu_wa,
        vecs[0], vecs[1], vecs[2], vecs[3], w2p, a2p)
    hs = lambda a: a.reshape(-1, N_HEADS_A, 1, HEAD_A)
    col = lambda a: a.reshape(-1, N_HEADS_A, HEAD_A, 1)
    wkv_s, o_a_s = rwkv_step(state_wkv[l], hs(w_s), hs(kk_s), hs(b_s), hs(kf_s), hs(r_s),
                             col(v_s), col(g_s), hs(r_k[l])[0], col(lnx_w[l])[0], col(lnx_b[l])[0])
    o_a_s = o_a_s.reshape(DB, D_A).astype(BF16)

    cos_s, sin_s = _rope_tables(past + jnp.arange(TS))
    cos_s = jnp.broadcast_to(cos_s, (DB, HEAD_B))
    sin_s = jnp.broadcast_to(sin_s, (DB, HEAD_B))
    q_rows_s, k_rows_s = rope_rows(z_s[:, COL_Q:COL_Q + D_B], z_s[:, COL_KB:COL_KB + D_B], cos_s, sin_s)
    v_rows_s = z_s[:, COL_VB:COL_VB + D_B]
    if kmean_s is None:
        kmean_s = cache_kmean(cache_k, page_table, l)
    sel = sample_select(q_rows_s.reshape(DB, N_HEADS_B, HEAD_B), kmean_s)[:, :, :MOBA_TOPK]
    hv = lambda a: a.reshape(DB, N_HEADS_B, 1, HEAD_B)
    o_b_s = sample_attn(sel, page_table, hv(q_rows_s), hv(k_rows_s), hv(v_rows_s),
                        hv(z_s[:, COL_GB:COL_GB + D_B]), cache_k, cache_v, l)
    o_b_s = o_b_s.reshape(DB, D_B).astype(BF16)
    y_sample = merge_and_output(o_a_s, o_b_s, pa_bf, pb_bf, z_s, wo_bf, x_sample.reshape(DB, D_MODEL),
                                final_w)

    return (y_prompt,
            y_sample.reshape(DB, TS, D_MODEL),
            shift_prompt[None],
            wkv_p.reshape(1, B, N_HEADS_A, HEAD_A, HEAD_A),
            k_rows_p.reshape(1, B, T, N_HEADS_B, HEAD_B),
            v_rows_p.reshape(1, B, T, N_HEADS_B, HEAD_B),
            xn_s[None],
            wkv_s[None],
            k_rows_s.reshape(1, DB, TS, N_HEADS_B, HEAD_B),
            v_rows_s.reshape(1, DB, TS, N_HEADS_B, HEAD_B))
```

```python
import functools

import jax
import jax.numpy as jnp
from jax import lax
from jax.experimental import pallas as pl
from jax.experimental.pallas import tpu as pltpu

F32 = jnp.float32
BF16 = jnp.bfloat16

D_MODEL = 2048
D_A = D_MODEL // 2
HEAD_A = 64
N_HEADS_A = D_A // HEAD_A
LORA = 64
D_B = D_MODEL // 2
HEAD_B = 128
N_HEADS_B = D_B // HEAD_B
ROT_DIM = HEAD_B // 4
ROPE_THETA = 500000.0
MOBA_BLOCK = 256
MOBA_TOPK = 3
NORM_EPS = 1e-6
GN_EPS = 64e-5
NEG = -1e30
LOG2_E = 1.4426950408889634

COL_R, COL_K, COL_V, COL_G = 0, D_A, 2 * D_A, 3 * D_A
COL_GA = 4 * D_A
COL_GM = COL_GA + D_MODEL
COL_GB = COL_GM + D_MODEL
N_Z = COL_GB + D_B
COL_Q, COL_KB, COL_VB = N_Z, N_Z + D_B, N_Z + 2 * D_B
N_MAIN = COL_VB + D_B

GROUP = 256
HEADS_PER_GROUP = GROUP // HEAD_A
CHUNK = 64
assert CHUNK == HEAD_A
VMEM_LIMIT = 56 * 1024 * 1024


def _cparams(sem):
    return pltpu.CompilerParams(dimension_semantics=sem, vmem_limit_bytes=VMEM_LIMIT)


def _bdot(a, b):
    return jnp.dot(a.astype(BF16), b.astype(BF16), preferred_element_type=F32)


def _bdot_nt(a, b):
    return lax.dot_general(a.astype(BF16), b.astype(BF16), (((1,), (1,)), ((), ())),
                           preferred_element_type=F32)


def _bdot_tn(a, b):
    return lax.dot_general(a.astype(BF16), b.astype(BF16), (((0,), (0,)), ((), ())),
                           preferred_element_type=F32)


def _split3(x):
    hi = x.astype(BF16)
    r1 = x - hi.astype(F32)
    mid = r1.astype(BF16)
    lo = (r1 - mid.astype(F32)).astype(BF16)
    return hi, mid, lo


def _sigmoid(x):
    return 1.0 / (1.0 + jnp.exp(-x))


def _silu(x):
    return x * _sigmoid(x)


def _softplus(x):
    return jnp.maximum(x, 0.0) + jnp.log(1.0 + jnp.exp(-jnp.abs(x)))


def _rmsnorm_rows_kernel(x_ref, w_ref, o_ref):
    x = x_ref[...]
    ms = jnp.mean(x * x, axis=-1, keepdims=True)
    o_ref[...] = (x * lax.rsqrt(ms + NORM_EPS)) * w_ref[...]


def rmsnorm_rows(x, w):
    return pl.pallas_call(
        _rmsnorm_rows_kernel,
        out_shape=jax.ShapeDtypeStruct(x.shape, F32),
        name="rmsnorm_rows",
    )(x, w.reshape(1, -1))


W_TILE = 1024
LANES = 128


def _w_prep_kernel(w_hbm, o_hbm, in_buf, out_buf, in_sem, out_sem, *, layer, starts):
    n = len(starts)
    rows = in_buf.shape[1]

    def in_copy(j, slot):
        off = starts[j] % LANES
        width = W_TILE + (LANES if off else 0)
        return pltpu.make_async_copy(w_hbm.at[layer, :, pl.ds(starts[j] - off, width)],
                                     in_buf.at[slot, :, pl.ds(0, width)], in_sem.at[slot])

    def out_copy(j, slot):
        return pltpu.make_async_copy(out_buf.at[slot], o_hbm.at[:, pl.ds(j * W_TILE, W_TILE)],
                                     out_sem.at[slot])

    in_copy(0, 0).start()
    for j in range(n):
        slot = j % 2
        if j + 1 < n:
            in_copy(j + 1, 1 - slot).start()
        in_copy(j, slot).wait()
        if j >= 2:
            out_copy(j - 2, slot).wait()
        off = starts[j] % LANES
        rc = 256

        def chunk(c, _, slot=slot, off=off):
            r0 = pl.multiple_of(c * rc, rc)
            out_buf[slot, pl.ds(r0, rc), :] = in_buf[slot, pl.ds(r0, rc), off:off + W_TILE].astype(BF16)
            return 0

        lax.fori_loop(0, rows // rc, chunk, 0)
        out_copy(j, slot).start()
    for j in range(max(n - 2, 0), n):
        out_copy(j, j % 2).wait()


def prepare_projection_weight(w_in, layer, starts):
    rows = w_in.shape[1]
    assert rows % 256 == 0
    hbm = pl.BlockSpec(memory_space=pl.ANY)
    return pl.pallas_call(
        functools.partial(_w_prep_kernel, layer=layer, starts=tuple(starts)),
        in_specs=[hbm],
        out_specs=hbm,
        out_shape=jax.ShapeDtypeStruct((rows, len(starts) * W_TILE), BF16),
        scratch_shapes=[pltpu.VMEM((2, rows, W_TILE + LANES), F32),
                        pltpu.VMEM((2, rows, W_TILE), BF16),
                        pltpu.SemaphoreType.DMA((2,)), pltpu.SemaphoreType.DMA((2,))],
        compiler_params=pltpu.CompilerParams(vmem_limit_bytes=VMEM_LIMIT),
        name="prepare_projection_weight",
    )(w_in)


def _proj_kernel(x_ref, lnw_ref, w_ref, wwa_ref, z_ref, zwa_ref, xn_ref, *, normalize):
    @pl.when(pl.program_id(1) == 0)
    def _():
        x = x_ref[...]
        if normalize:
            ms = jnp.mean(x * x, axis=-1, keepdims=True)
            x = (x * lax.rsqrt(ms + NORM_EPS)) * lnw_ref[...]
        xn_ref[...] = x.astype(BF16)
        zwa_ref[...] = jnp.dot(xn_ref[...], wwa_ref[...].astype(BF16), preferred_element_type=F32)

    z_ref[...] = jnp.dot(xn_ref[...], w_ref[...], preferred_element_type=F32)


def input_projection(x, ln_w, w_main, w_wa, *, normalize, n_cols):
    m = x.shape[0]
    tm = min(m, 1024)
    if m < 1024:
        tn = 2048
    else:
        tn = 1536 if n_cols % 1536 == 0 else 1024
    assert m % tm == 0 and n_cols % tn == 0
    return pl.pallas_call(
        functools.partial(_proj_kernel, normalize=normalize),
        grid=(m // tm, n_cols // tn),
        in_specs=[
            pl.BlockSpec((tm, D_MODEL), lambda i, j: (i, 0)),
            pl.BlockSpec((1, D_MODEL), lambda i, j: (0, 0)),
            pl.BlockSpec((D_MODEL, tn), lambda i, j: (0, j)),
            pl.BlockSpec((D_MODEL, 2 * LORA), lambda i, j: (0, 0)),
        ],
        out_specs=[
            pl.BlockSpec((tm, tn), lambda i, j: (i, j)),
            pl.BlockSpec((tm, 2 * LORA), lambda i, j: (i, 0)),
        ],
        out_shape=[
            jax.ShapeDtypeStruct((m, n_cols), F32),
            jax.ShapeDtypeStruct((m, 2 * LORA), F32),
        ],
        scratch_shapes=[pltpu.VMEM((tm, D_MODEL), BF16)],
        compiler_params=_cparams(("arbitrary", "arbitrary")),
        name="input_projection",
    )(x, ln_w.reshape(1, -1), w_main, w_wa)


def _head_mask(n):
    r = lax.broadcasted_iota(jnp.int32, (n, n), 0) // HEAD_A
    c = lax.broadcasted_iota(jnp.int32, (n, n), 1) // HEAD_A
    return r == c


def _segsum(x, ones_bd):
    return jnp.dot(x.astype(BF16), ones_bd, preferred_element_type=F32)


def _rwkv_prep(zm_r, zm_k, zm_v, zm_wa, w0, a0, k_k, k_a, w2p, a2p, ones_bd):
    lane = lax.broadcasted_iota(jnp.int32, zm_wa.shape, 1)
    lora_in = jnp.where(lane < LORA, jnp.tanh(zm_wa), zm_wa)
    ww = _bdot(lora_in, w2p)
    aa = _bdot(lora_in, a2p)
    w_log = -_softplus(-(w0 + ww)) - 0.5
    logw = -jnp.exp(w_log)
    a = _sigmoid(a0 + aa)
    kk = zm_k * k_k
    ss = _segsum(kk * kk, ones_bd)
    kk = kk / jnp.maximum(jnp.sqrt(ss), 1e-12)
    kf = zm_k * (1.0 + (a - 1.0) * k_a)
    return zm_r, kf, zm_v, kk, kk * a, logw


def _rwkv_post(y, r, kf, v, zm_g, r_k, lnx_w, lnx_b, ones_bd):
    inv_n = 1.0 / HEAD_A
    mean = _segsum(y, ones_bd) * inv_n
    d = y - mean
    var = _segsum(d * d, ones_bd) * inv_n
    yn = d * lax.rsqrt(var + GN_EPS) * lnx_w + lnx_b
    bonus = _segsum(r * kf * r_k, ones_bd) * v
    return (yn + bonus) * _silu(zm_g)


def _stack_heads(x, lane_head):
    return jnp.concatenate(
        [jnp.where(lane_head == h, x, 0.0) for h in range(HEADS_PER_GROUP)], axis=0)


def _chunk_precompute(insts, consts, side_tasks=()):
    strict, incl, eye, lane_head, bd_mask = consts
    c = CHUNK
    n = len(insts)
    side_tasks = list(side_tasks)
    n_phases = 7
    per_phase = -(-len(side_tasks) // n_phases)

    def run_side(k=None):
        for _ in range(per_phase if k is None else k):
            if side_tasks:
                side_tasks.pop(0)()

    def bd(p):
        return jnp.where(bd_mask, jnp.concatenate([p] * HEADS_PER_GROUP, axis=0), 0.0).astype(BF16)

    lhs, wt, v_bd, bdkd, p_end = [], [], [], [], []
    for r, kf, v, kk, b, logw, cl in insts:
        cl_last = cl[c - 1:c, :]
        e_neg = jnp.exp(-cl)
        e_end = jnp.exp(cl_last - cl)
        lhs.append(jnp.concatenate([kk * jnp.exp(cl - logw), r * jnp.exp(cl)], axis=0).astype(BF16))
        wt.append(jnp.concatenate([_stack_heads(b * e_neg, lane_head),
                                   _stack_heads(kf * e_neg, lane_head)], axis=0).astype(BF16))
        v_bd.append(_stack_heads(v, lane_head).astype(BF16))
        bdkd.append(jnp.concatenate([b * e_end, kf * e_end], axis=0).astype(BF16))
        p_end.append(jnp.exp(cl_last))
    att = [_bdot_nt(lhs[i], wt[i]) for i in range(n)]
    run_side()
    a_ab = [jnp.where(strict, att[i][:c, :4 * c], 0.0) for i in range(n)]
    a_ak = [jnp.where(strict, att[i][:c, 4 * c:], 0.0) for i in range(n)]
    m_cat = [jnp.concatenate([jnp.where(incl, att[i][c:, :4 * c], 0.0),
                              jnp.where(incl, att[i][c:, 4 * c:], 0.0)], axis=1).astype(BF16)
             for i in range(n)]
    av = [_bdot(a_ak[i], v_bd[i]) for i in range(n)]
    run_side()

    def pair(m):
        return (rc // (2 * m) == sc_ // (2 * m)) & ((rc // m) % 2 == 1) & ((sc_ // m) % 2 == 0)

    rc = lax.broadcasted_iota(jnp.int32, (c, HEADS_PER_GROUP * c), 0)
    sc_ = lax.broadcasted_iota(jnp.int32, (c, HEADS_PER_GROUP * c), 1) % c
    x = [eye - jnp.where(pair(1), a_ab[i], 0.0) for i in range(n)]
    m = 2
    while m < c:
        mask = pair(m)
        y = [_bdot(jnp.where(mask, a_ab[i], 0.0), bd(x[i])) for i in range(n)]
        x = [x[i] - _bdot(x[i], bd(y[i])) for i in range(n)]
        run_side()
        m *= 2
    x = [x[i].astype(BF16) for i in range(n)]
    run_side(len(side_tasks))
    return [dict(lhs=lhs[i], av=av[i], t=x[i], m_cat=m_cat[i], v_bd=v_bd[i], v=insts[i][2],
                 bdkd=bdkd[i], p_end=p_end[i]) for i in range(n)]


def _chunk_state_step(states, pres, consts):
    lane_head, bd_mask = consts[3], consts[4]
    c = CHUNK
    n = len(states)
    sh = [_bdot_nt(pres[i]["lhs"], states[i]) for i in range(n)]
    rhs = [sh[i][:c] + pres[i]["av"] for i in range(n)]
    u = [-_bdot(pres[i]["t"], _stack_heads(rhs[i], lane_head)) for i in range(n)]
    y = [sh[i][c:] + _bdot(pres[i]["m_cat"],
                           jnp.concatenate([_stack_heads(u[i], lane_head).astype(BF16),
                                            pres[i]["v_bd"]], axis=0)) for i in range(n)]
    upd = [_bdot_tn(jnp.concatenate([u[i], pres[i]["v"]], axis=0), pres[i]["bdkd"]) for i in range(n)]
    new = [states[i] * pres[i]["p_end"] + jnp.where(bd_mask, upd[i], 0.0) for i in range(n)]
    return y, new


def _page_block_mean_tasks(page_refs, km_ref, pages_per_block):
    def task(j):
        def run():
            blk = jnp.sum(page_refs[j * pages_per_block][...], axis=0)
            for i in range(1, pages_per_block):
                blk = blk + jnp.sum(page_refs[j * pages_per_block + i][...], axis=0)
            km_ref[j] = blk * (1.0 / MOBA_BLOCK)
        return run
    return [task(j) for j in range(len(page_refs) // pages_per_block)]


def _rwkv_prompt_kernel(*refs, n_pages, pages_per_block):
    if n_pages:
        refs = refs[1:]
    (zr_ref, zk_ref, zv_ref, zg_ref, zwa_ref, mur_ref, muk_ref, muv_ref, mug_ref, muwa_ref,
     w0_ref, a0_ref, kk_ref, ka_ref, rk_ref, lnw_ref, lnb_ref, w2_ref, a2_ref) = refs[:19]
    page_refs = refs[19:19 + n_pages]
    refs = refs[19 + n_pages:]
    o_ref, s_out_ref = refs[:2]
    side_tasks = []
    if n_pages:
        side_tasks = _page_block_mean_tasks(page_refs, refs[2], pages_per_block)
        refs = refs[1:]
    s_ref, pr_ref, pk_ref, pv_ref, pg_ref, pwa_ref = refs[2:]
    t = pl.program_id(1)
    nseq, tr = zr_ref.shape[0], zr_ref.shape[1]

    @pl.when(t == 0)
    def _():
        for ref in (s_ref, pr_ref, pk_ref, pv_ref, pg_ref, pwa_ref):
            ref[...] = jnp.zeros_like(ref)

    def shifted(z_ref, prev_ref, mu_ref, i):
        z = z_ref[i]
        row = lax.broadcasted_iota(jnp.int32, z.shape, 0)
        prev = jnp.where(row == 0, prev_ref[i], pltpu.roll(z, 1, 0))
        prev_ref[i] = z[tr - 1:tr, :]
        return z + (prev - z) * mu_ref[...]

    bd_mask = _head_mask(GROUP)
    ones_bd = jnp.where(bd_mask, 1.0, 0.0).astype(BF16)
    ri = lax.broadcasted_iota(jnp.int32, (tr, tr), 0)
    ci = lax.broadcasted_iota(jnp.int32, (tr, tr), 1)
    tri = jnp.where((ri // CHUNK == ci // CHUNK) & (ci <= ri), 1.0, 0.0).astype(BF16)
    rc = lax.broadcasted_iota(jnp.int32, (CHUNK, GROUP), 0)
    lc = lax.broadcasted_iota(jnp.int32, (CHUNK, GROUP), 1)
    sc = lc % CHUNK
    consts = (sc < rc, sc <= rc, jnp.where(sc == rc, 1.0, 0.0), lc // HEAD_A, bd_mask)
    last = t == pl.num_programs(1) - 1

    n_ch = tr // CHUNK
    prepped, insts = [], []
    for i in range(nseq):
        zm_r = shifted(zr_ref, pr_ref, mur_ref, i)
        zm_k = shifted(zk_ref, pk_ref, muk_ref, i)
        zm_v = shifted(zv_ref, pv_ref, muv_ref, i)
        zm_g = shifted(zg_ref, pg_ref, mug_ref, i)
        zm_wa = shifted(zwa_ref, pwa_ref, muwa_ref, i)
        r, kf, v, kk, b, logw = _rwkv_prep(zm_r, zm_k, zm_v, zm_wa, w0_ref[...], a0_ref[...],
                                           kk_ref[...], ka_ref[...], w2_ref[...], a2_ref[...], ones_bd)
        cl = sum(jnp.dot(tri, piece, preferred_element_type=F32) for piece in _split3(logw))
        prepped.append((r, kf, v, zm_g))
        for ch in range(n_ch):
            sl = slice(ch * CHUNK, (ch + 1) * CHUNK)
            insts.append((r[sl], kf[sl], v[sl], kk[sl], b[sl], logw[sl], cl[sl]))
    pres = _chunk_precompute(insts, consts, side_tasks)

    states = [s_ref[i] for i in range(nseq)]
    ys = [[] for _ in range(nseq)]
    for ch in range(n_ch):
        y, states = _chunk_state_step(states, [pres[i * n_ch + ch] for i in range(nseq)], consts)
        for i in range(nseq):
            ys[i].append(y[i])
    for i in range(nseq):
        s_ref[i] = states[i]
        r, kf, v, zm_g = prepped[i]
        o_ref[i] = _rwkv_post(jnp.concatenate(ys[i], axis=0), r, kf, v, zm_g, rk_ref[...],
                              lnw_ref[...], lnb_ref[...], ones_bd).astype(o_ref.dtype)

    @pl.when(last)
    def _():
        for i in range(nseq):
            for h in range(HEADS_PER_GROUP):
                s_out_ref[i, h] = s_ref[i, h * HEAD_A:(h + 1) * HEAD_A, h * HEAD_A:(h + 1) * HEAD_A]


def rwkv_prompt(z_main, z_wa, mu_main, mu_wa, vecs, w2p, a2p, cache_k=None, page_table=None, layer=0):
    w0, a0, k_k, k_a, r_k, lnx_w, lnx_b = vecs
    batch, seq, _ = z_main.shape
    tr = min(seq, 256)
    assert seq % tr == 0 and tr % CHUNK == 0
    nt = seq // tr
    ng = D_A // GROUP

    n_pages = ppb = 0
    if cache_k is not None:
        page = cache_k.shape[2]
        ppb = MOBA_BLOCK // page
        total = page_table.size
        if total % (ng * nt) == 0 and (total // (ng * nt)) % ppb == 0:
            n_pages = total // (ng * nt)

    def zspec(col):
        return pl.BlockSpec((batch, tr, GROUP), lambda g, t, *_, c=col // GROUP: (0, t, c + g))

    def vspec(col=0):
        return pl.BlockSpec((1, GROUP), lambda g, t, *_, c=col // GROUP: (0, c + g))

    in_specs = [
        zspec(COL_R), zspec(COL_K), zspec(COL_V), zspec(COL_G),
        pl.BlockSpec((batch, tr, 2 * LORA), lambda g, t, *_: (0, t, 0)),
        vspec(COL_R), vspec(COL_K), vspec(COL_V), vspec(COL_G),
        pl.BlockSpec((1, 2 * LORA), lambda g, t, *_: (0, 0)),
        vspec(), vspec(), vspec(), vspec(), vspec(), vspec(), vspec(),
        pl.BlockSpec((2 * LORA, GROUP), lambda g, t, *_: (0, g)),
        pl.BlockSpec((2 * LORA, GROUP), lambda g, t, *_: (0, g)),
    ]
    out_specs = [
        pl.BlockSpec((batch, tr, GROUP), lambda g, t, *_: (0, t, g)),
        pl.BlockSpec((batch, HEADS_PER_GROUP, HEAD_A, HEAD_A), lambda g, t, *_: (0, g, 0, 0)),
    ]
    out_shape = [
        jax.ShapeDtypeStruct((batch, seq, D_A), BF16),
        jax.ShapeDtypeStruct((batch, N_HEADS_A, HEAD_A, HEAD_A), F32),
    ]
    operands = [z_main, z_main, z_main, z_main, z_wa,
                mu_main, mu_main, mu_main, mu_main, mu_wa,
                w0, a0, k_k, k_a, r_k, lnx_w, lnx_b, w2p, a2p]
    if n_pages:
        sq = pl.Squeezed()
        for i in range(n_pages):
            in_specs.append(pl.BlockSpec(
                (sq, sq, page, N_HEADS_B, HEAD_B),
                lambda g, t, pt, i=i: (layer, pt[(g * nt + t) * n_pages + i], 0, 0, 0)))
        operands += [cache_k] * n_pages
        out_specs.append(pl.BlockSpec((n_pages // ppb, N_HEADS_B, HEAD_B),
                                      lambda g, t, pt: (g * nt + t, 0, 0)))
        out_shape.append(jax.ShapeDtypeStruct((page_table.size // ppb, N_HEADS_B, HEAD_B), F32))
    row = lambda n: pltpu.VMEM((batch, 1, n), F32)
    scratch = [pltpu.VMEM((batch, GROUP, GROUP), F32),
               row(GROUP), row(GROUP), row(GROUP), row(GROUP), row(2 * LORA)]
    kern = functools.partial(_rwkv_prompt_kernel, n_pages=n_pages, pages_per_block=ppb)
    params = _cparams(("arbitrary", "arbitrary"))
    if not n_pages:
        o_a, wkv = pl.pallas_call(kern, grid=(ng, nt), in_specs=in_specs, out_specs=out_specs,
                                  out_shape=out_shape, scratch_shapes=scratch,
                                  compiler_params=params, name="rwkv_prompt")(*operands)
        return o_a, wkv, None
    o_a, wkv, kmean = pl.pallas_call(
        kern,
        grid_spec=pltpu.PrefetchScalarGridSpec(
            num_scalar_prefetch=1, grid=(ng, nt), in_specs=in_specs, out_specs=out_specs,
            scratch_shapes=scratch),
        out_shape=out_shape, compiler_params=params, name="rwkv_prompt",
    )(page_table.reshape(-1), *operands)
    n = page_table.shape[0]
    return o_a, wkv, kmean.reshape(n, -1, N_HEADS_B, HEAD_B)


def _rwkv_rows_kernel(z_ref, zp_ref, zwa_ref, zpwa_ref, mu_ref, muwa_ref,
                      w0_ref, a0_ref, kk_ref, ka_ref, w2_ref, a2_ref,
                      r_ref, kf_ref, v_ref, kkn_ref, b_ref, w_ref, g_ref):
    def shifted(z, zp, mu):
        return z + (zp - z) * mu

    mu = mu_ref[...]
    z = z_ref[...]
    zp = zp_ref[...]
    zm = [shifted(z[:, c:c + D_A], zp[:, c:c + D_A], mu[:, c:c + D_A])
          for c in (COL_R, COL_K, COL_V, COL_G)]
    zm_wa = shifted(zwa_ref[...], zpwa_ref[...], muwa_ref[...])
    ones_bd = jnp.where(_head_mask(D_A), 1.0, 0.0).astype(BF16)
    r, kf, v, kk, b, logw = _rwkv_prep(zm[0], zm[1], zm[2], zm_wa, w0_ref[...], a0_ref[...],
                                       kk_ref[...], ka_ref[...], w2_ref[...], a2_ref[...], ones_bd)
    r_ref[...] = r
    kf_ref[...] = kf
    kkn_ref[...] = kk
    b_ref[...] = b
    w_ref[...] = jnp.exp(logw)
    v_t = v.T
    g_t = zm[3].T
    for i in range(v_ref.shape[0]):
        v_ref[i] = v_t[:, i:i + 1]
        g_ref[i] = g_t[:, i:i + 1]


def rwkv_rows(z_rw, zp_rw, z_wa, zp_wa, mu_rw, mu_wa, w0, a0, k_k, k_a, w2p, a2p):
    n = z_rw.shape[0]
    out = jax.ShapeDtypeStruct((n, D_A), F32)
    col = jax.ShapeDtypeStruct((n, D_A, 1), F32)
    return pl.pallas_call(
        _rwkv_rows_kernel,
        out_shape=[out, out, col, out, out, out, col],
        compiler_params=pltpu.CompilerParams(vmem_limit_bytes=VMEM_LIMIT),
        name="rwkv_rows",
    )(z_rw, zp_rw, z_wa, zp_wa, mu_rw, mu_wa, w0, a0, k_k, k_a, w2p, a2p)


def _rwkv_step_kernel(s_ref, w_ref, kk_ref, b_ref, kf_ref, r_ref, v_ref, g_ref,
                      rk_ref, lnw_ref, lnb_ref, s_out_ref, o_ref):
    S = s_ref[...]
    w = w_ref[...]
    kk = kk_ref[...]
    b = b_ref[...]
    kf = kf_ref[...]
    r = r_ref[...]
    v = v_ref[...]
    sa = -jnp.sum(S * kk, axis=-1, keepdims=True)
    S = S * w + sa * b + v * kf
    s_out_ref[...] = S
    y = jnp.sum(S * r, axis=-1, keepdims=True)
    mean = jnp.mean(y, axis=1, keepdims=True)
    d = y - mean
    var = jnp.mean(d * d, axis=1, keepdims=True)
    yn = d * lax.rsqrt(var + GN_EPS) * lnw_ref[...] + lnb_ref[...]
    bonus = jnp.sum(r * kf * rk_ref[...], axis=-1, keepdims=True) * v
    o_ref[...] = (yn + bonus) * _silu(g_ref[...])


def rwkv_step(state, w, kk, b, kf, r, v, g, r_k, lnx_w, lnx_b):
    n = state.shape[0]
    per = 4 if n % 4 == 0 else 1
    h = per * N_HEADS_A
    flat = lambda a: a.reshape((n * N_HEADS_A,) + a.shape[2:])
    tile = lambda a: jnp.tile(a, (per, 1, 1))
    lane_vec = pl.BlockSpec((h, 1, HEAD_A), lambda i: (i, 0, 0))
    col_vec = pl.BlockSpec((h, HEAD_A, 1), lambda i: (i, 0, 0))
    s_new, o = pl.pallas_call(
        _rwkv_step_kernel,
        grid=(n // per,),
        in_specs=[pl.BlockSpec((h, HEAD_A, HEAD_A), lambda i: (i, 0, 0)),
                  lane_vec, lane_vec, lane_vec, lane_vec, lane_vec, col_vec, col_vec,
                  pl.BlockSpec((h, 1, HEAD_A), lambda i: (0, 0, 0)),
                  pl.BlockSpec((h, HEAD_A, 1), lambda i: (0, 0, 0)),
                  pl.BlockSpec((h, HEAD_A, 1), lambda i: (0, 0, 0))],
        out_specs=[pl.BlockSpec((h, HEAD_A, HEAD_A), lambda i: (i, 0, 0)), col_vec],
        out_shape=[jax.ShapeDtypeStruct((n * N_HEADS_A, HEAD_A, HEAD_A), F32),
                   jax.ShapeDtypeStruct((n * N_HEADS_A, HEAD_A, 1), F32)],
        compiler_params=_cparams(("arbitrary",)),
        name="rwkv_step",
    )(flat(state), flat(w), flat(kk), flat(b), flat(kf), flat(r), flat(v), flat(g),
      tile(r_k), tile(lnx_w), tile(lnx_b))
    return s_new.reshape(state.shape), o.reshape(n, N_HEADS_A, HEAD_A, 1)


def _rope(x, cos_t, sin_t, lane):
    partner = jnp.where(lane < ROT_DIM // 2, pltpu.roll(x, HEAD_B - ROT_DIM // 2, 1),
                        pltpu.roll(x, ROT_DIM // 2, 1))
    return x * cos_t + partner * sin_t


def _attn_proj_kernel(x_ref, lnw_ref, w_ref, cos_ref, sin_ref,
                      qh_ref, kh_ref, vh_ref, ko_ref, vo_ref, km_ref, xn_ref):
    tm = x_ref.shape[0]
    slab = 2 * HEAD_B

    x = x_ref[...]
    ms = jnp.mean(x * x, axis=-1, keepdims=True)
    xn_ref[...] = ((x * lax.rsqrt(ms + NORM_EPS)) * lnw_ref[...]).astype(BF16)

    cos_t = cos_ref[...]
    sin_t = sin_ref[...]
    lane = lax.broadcasted_iota(jnp.int32, cos_t.shape, 1)

    def slabs(part):
        for sidx in range(D_B // slab):
            c0 = part * D_B + sidx * slab
            z = jnp.dot(xn_ref[...], w_ref[:, c0:c0 + slab], preferred_element_type=F32)
            for hh in range(slab // HEAD_B):
                h = sidx * (slab // HEAD_B) + hh
                yield h, z[:, hh * HEAD_B:(hh + 1) * HEAD_B]

    for h, z in slabs(0):
        qh_ref[h] = _rope(z, cos_t, sin_t, lane).astype(qh_ref.dtype)

    for h, z in slabs(1):
        sl = slice(h * HEAD_B, (h + 1) * HEAD_B)
        k = _rope(z, cos_t, sin_t, lane)
        kh_ref[h] = k.astype(kh_ref.dtype)
        ko_ref[:, sl] = k
        for blk in range(km_ref.shape[0]):
            rows = tm // km_ref.shape[0]
            km_ref[blk, :, sl] = jnp.mean(k[blk * rows:(blk + 1) * rows], axis=0, keepdims=True)

    for h, z in slabs(2):
        vo_ref[:, h * HEAD_B:(h + 1) * HEAD_B] = z
        vh_ref[h] = z.astype(vh_ref.dtype)


def attn_projection(x, ln_w, w_main, cos_t, sin_t, batch, seq):
    m = batch * seq
    blk = min(seq, MOBA_BLOCK)
    tm = min(seq, 512)
    assert seq % tm == 0 and tm % blk == 0
    nt = seq // tm
    sq = pl.Squeezed()
    tab = pl.BlockSpec((tm, HEAD_B), lambda i: (i % nt, 0))
    hm = pl.BlockSpec((sq, N_HEADS_B, tm, HEAD_B), lambda i: (i // nt, 0, i % nt, 0))
    rows = pl.BlockSpec((tm, D_B), lambda i: (i, 0))
    hm_shape = jax.ShapeDtypeStruct((batch, N_HEADS_B, seq, HEAD_B), BF16)
    assert COL_Q % (3 * D_B) == 0
    return pl.pallas_call(
        _attn_proj_kernel,
        grid=(m // tm,),
        in_specs=[
            pl.BlockSpec((tm, D_MODEL), lambda i: (i, 0)),
            pl.BlockSpec((1, D_MODEL), lambda i: (0, 0)),
            pl.BlockSpec((D_MODEL, 3 * D_B), lambda i: (0, COL_Q // (3 * D_B)),
                         pipeline_mode=pl.Buffered(1)),
            tab, tab,
        ],
        out_specs=[hm, hm, hm, rows, rows,
                   pl.BlockSpec((tm // blk, 1, D_B), lambda i: (i, 0, 0))],
        out_shape=[hm_shape, hm_shape, hm_shape,
                   jax.ShapeDtypeStruct((m, D_B), F32), jax.ShapeDtypeStruct((m, D_B), F32),
                   jax.ShapeDtypeStruct((m // blk, 1, D_B), F32)],
        scratch_shapes=[pltpu.VMEM((tm, D_MODEL), BF16)],
        compiler_params=_cparams(("arbitrary",)),
        name="attn_projection",
    )(x, ln_w.reshape(1, -1), w_main, cos_t, sin_t)


def _rope_rows_kernel(zq_ref, zk_ref, cos_ref, sin_ref, q_ref, k_ref):
    cos_t = cos_ref[...]
    sin_t = sin_ref[...]
    lane = lax.broadcasted_iota(jnp.int32, cos_t.shape, 1)
    for h in range(N_HEADS_B):
        sl = slice(h * HEAD_B, (h + 1) * HEAD_B)
        q_ref[:, sl] = _rope(zq_ref[:, sl], cos_t, sin_t, lane).astype(BF16).astype(F32)
        k_ref[:, sl] = _rope(zk_ref[:, sl], cos_t, sin_t, lane)


def rope_rows(zq, zk, cos_t, sin_t):
    out = jax.ShapeDtypeStruct(zq.shape, F32)
    return pl.pallas_call(_rope_rows_kernel, out_shape=[out, out], name="rope_rows")(zq, zk, cos_t, sin_t)


MOBA_HEADS_PER_STEP = 8


def _moba_prompt_kernel(q_ref, k_ref, v_ref, km_ref, zg_ref, o_ref, sel_ref):
    qb = pl.program_id(1)
    nbatch, nhead, tq = q_ref.shape[0], q_ref.shape[1], q_ref.shape[2]
    nb = km_ref.shape[1]
    seqs = [(b, h) for b in range(nbatch) for h in range(nhead)]
    exp2_scale = HEAD_B ** -0.5 * LOG2_E
    qs = [q_ref[b, h].T for b, h in seqs]

    blk = lax.broadcasted_iota(jnp.int32, (nb, tq), 0)
    for i, (b, h) in enumerate(seqs):
        km = km_ref[b, :, h * HEAD_B:(h + 1) * HEAD_B]
        gate = sum(jnp.dot(piece, qs[i], preferred_element_type=F32)
                   for piece in _split3(km))
        gate = jnp.where(blk < qb, gate, -jnp.inf)
        for n in range(nb):
            g_n = gate[n:n + 1, :]
            tie = jnp.where(blk < n, 1.0, 0.0)
            beats = jnp.where(gate > g_n, 1.0, jnp.where(gate == g_n, tie, 0.0))
            cnt = jnp.sum(beats, axis=0, keepdims=True)
            sel_ref[i, n:n + 1, :] = jnp.where(cnt < MOBA_TOPK, 1.0, 0.0)

    def scores(i, n):
        b, h = seqs[i]
        start = pl.multiple_of(n * tq, tq)
        return jnp.dot(k_ref[b, h, pl.ds(start, tq), :], qs[i], preferred_element_type=F32)

    def softmax_step(s, m, l):
        m_new = jnp.maximum(m, jnp.max(s, axis=0, keepdims=True))
        alpha = jnp.exp2((m - m_new) * exp2_scale)
        p = jnp.exp2((s - m_new) * exp2_scale)
        return m_new, alpha, alpha * l + jnp.sum(p, axis=0, keepdims=True), p.astype(BF16)

    def pv(i, n, p):
        b, h = seqs[i]
        start = pl.multiple_of(n * tq, tq)
        return _bdot_tn(v_ref[b, h, pl.ds(start, tq), :], p)

    def step(n, masked, carry):
        sm = [softmax_step(masked[i], carry[i][0], carry[i][1]) for i in range(len(seqs))]
        acc = [sm[i][1] * carry[i][2] + pv(i, n, sm[i][3]) for i in range(len(seqs))]
        return [(sm[i][0], sm[i][2], acc[i]) for i in range(len(seqs))]

    def body(n, carry):
        s = [scores(i, n) for i in range(len(seqs))]
        masked = [jnp.where(sel_ref[i, pl.ds(n, 1), :] > 0.0, s[i], NEG)
                  for i in range(len(seqs))]
        return tuple(step(n, masked, carry))

    init = tuple((jnp.full((1, tq), NEG, F32), jnp.zeros((1, tq), F32),
                  jnp.zeros((HEAD_B, tq), F32)) for i in range(len(seqs)))
    carry = lax.fori_loop(0, qb, body, init)
    ki = lax.broadcasted_iota(jnp.int32, (tq, tq), 0)
    qi = lax.broadcasted_iota(jnp.int32, (tq, tq), 1)
    masked = [jnp.where(ki <= qi, scores(i, qb), NEG) for i in range(len(seqs))]
    final = step(qb, masked, list(carry))
    for i, (b, h) in enumerate(seqs):
        _, l, acc = final[i]
        sl = slice(h * HEAD_B, (h + 1) * HEAD_B)
        o_ref[b, :, sl] = ((acc / l).T * _silu(zg_ref[b, :, sl])).astype(o_ref.dtype)


def moba_prompt(q_hm, k_hm, v_hm, kmean, z_main):
    batch, _, seq, _ = k_hm.shape
    tq = min(seq, MOBA_BLOCK)
    nb = seq // tq
    hps = MOBA_HEADS_PER_STEP
    w = hps * HEAD_B
    return pl.pallas_call(
        _moba_prompt_kernel,
        grid=(N_HEADS_B // hps, nb),
        in_specs=[
            pl.BlockSpec((batch, hps, tq, HEAD_B), lambda h, i: (0, h, i, 0)),
            pl.BlockSpec((batch, hps, seq, HEAD_B), lambda h, i: (0, h, 0, 0),
                         pipeline_mode=pl.Buffered(1)),
            pl.BlockSpec((batch, hps, seq, HEAD_B), lambda h, i: (0, h, 0, 0),
                         pipeline_mode=pl.Buffered(1)),
            pl.BlockSpec((batch, nb, w), lambda h, i: (0, 0, h)),
            pl.BlockSpec((batch, tq, w), lambda h, i: (0, i, COL_GB // w + h)),
        ],
        out_specs=pl.BlockSpec((batch, tq, w), lambda h, i: (0, i, h)),
        out_shape=jax.ShapeDtypeStruct((batch, seq, D_B), BF16),
        scratch_shapes=[pltpu.VMEM((batch * hps, nb, tq), F32)],
        compiler_params=_cparams(("arbitrary", "arbitrary")),
        name="moba_prompt",
    )(q_hm, k_hm, v_hm, kmean, z_main)


KMEAN_PAGES = 8


def _cache_kmean_kernel(pt_ref, *refs, pages_per_block):
    del pt_ref
    for task in _page_block_mean_tasks(refs[:-1], refs[-1], pages_per_block):
        task()


def cache_kmean(cache_k, page_table, layer):
    n, n_pages = page_table.shape
    page = cache_k.shape[2]
    ppb = MOBA_BLOCK // page
    pps = min(KMEAN_PAGES, n_pages)
    assert n_pages % pps == 0 and pps % ppb == 0
    sq = pl.Squeezed()
    page_spec = lambda i: pl.BlockSpec(
        (sq, sq, page, N_HEADS_B, HEAD_B), lambda b, s, pt, i=i: (layer, pt[b, s * pps + i], 0, 0, 0))
    return pl.pallas_call(
        functools.partial(_cache_kmean_kernel, pages_per_block=ppb),
        grid_spec=pltpu.PrefetchScalarGridSpec(
            num_scalar_prefetch=1,
            grid=(n, n_pages // pps),
            in_specs=[page_spec(i) for i in range(pps)],
            out_specs=pl.BlockSpec((sq, pps // ppb, N_HEADS_B, HEAD_B), lambda b, s, pt: (b, s, 0, 0)),
        ),
        out_shape=jax.ShapeDtypeStruct((n, n_pages // ppb, N_HEADS_B, HEAD_B), F32),
        compiler_params=_cparams(("arbitrary", "arbitrary")),
        name="cache_kmean",
    )(page_table, *([cache_k] * pps))


def _sample_select_kernel(q_ref, km_ref, sel_ref):
    nb = km_ref.shape[0]
    ri = lax.broadcasted_iota(jnp.int32, (nb, nb), 0)
    ci = lax.broadcasted_iota(jnp.int32, (nb, nb), 1)
    lane = lax.broadcasted_iota(jnp.int32, (1, 128), 1)
    blk_row = lax.broadcasted_iota(jnp.int32, (1, nb), 1).astype(F32)
    for h in range(N_HEADS_B):
        km = km_ref[:, h, :]
        g_col = jnp.sum(km * q_ref[h:h + 1, :], axis=1, keepdims=True)
        g_row = jnp.sum(jnp.where(ri == ci, g_col, 0.0), axis=0, keepdims=True)
        beats = (g_col > g_row) | ((g_col == g_row) & (ri < ci))
        rank = jnp.sum(jnp.where(beats, 1.0, 0.0), axis=0, keepdims=True)
        out = jnp.zeros((1, 128), F32)
        for r in range(MOBA_TOPK):
            idx = jnp.sum(jnp.where(rank == float(r), blk_row, 0.0), axis=1, keepdims=True)
            out = jnp.where(lane == r, idx, out)
        sel_ref[h:h + 1, :] = out.astype(jnp.int32)


def sample_select(q_s, kmean_s):
    n, nb = kmean_s.shape[:2]
    sq = pl.Squeezed()
    return pl.pallas_call(
        _sample_select_kernel,
        grid=(n,),
        in_specs=[pl.BlockSpec((sq, N_HEADS_B, HEAD_B), lambda b: (b, 0, 0)),
                  pl.BlockSpec((sq, nb, N_HEADS_B, HEAD_B), lambda b: (b, 0, 0, 0))],
        out_specs=pl.BlockSpec((sq, N_HEADS_B, 128), lambda b: (b, 0, 0)),
        out_shape=jax.ShapeDtypeStruct((n, N_HEADS_B, 128), jnp.int32),
        compiler_params=_cparams(("arbitrary",)),
        name="sample_select",
    )(q_s, kmean_s)


def _sample_attn_kernel(sel_ref, pt_ref, q_ref, kn_ref, vn_ref, zg_ref, ck_ref, cv_ref, o_ref,
                        kbuf, vbuf, sem, *, layer, page, ppb):
    b = pl.program_id(0)
    nb_steps = pl.num_programs(0)
    n_sel = MOBA_TOPK * ppb
    scale = HEAD_B ** -0.5

    def copies(bb, slot):
        out = []
        for h in range(N_HEADS_B):
            for j in range(n_sel):
                pg = pt_ref[bb, sel_ref[bb, h, j // ppb] * ppb + j % ppb]
                dst = pl.ds(j * page, page)
                out.append(pltpu.make_async_copy(ck_ref.at[layer, pg, :, h, :],
                                                 kbuf.at[slot, h, dst, :], sem.at[slot, 0]))
                out.append(pltpu.make_async_copy(cv_ref.at[layer, pg, :, h, :],
                                                 vbuf.at[slot, h, dst, :], sem.at[slot, 1]))
        return out

    slot = b % 2

    @pl.when(b == 0)
    def _():
        for c in copies(0, 0):
            c.start()

    @pl.when(b + 1 < nb_steps)
    def _():
        for c in copies(b + 1, 1 - slot):
            c.start()

    for c in copies(b, slot):
        c.wait()

    for h in range(N_HEADS_B):
        q = q_ref[h]
        k = kbuf[slot, h]
        v = vbuf[slot, h]
        s = jnp.sum(k * q, axis=1, keepdims=True) * scale
        s_own = jnp.sum(kn_ref[h] * q, axis=1, keepdims=True) * scale
        m = jnp.maximum(jnp.max(s, axis=0, keepdims=True), s_own)
        p = jnp.exp(s - m)
        p_own = jnp.exp(s_own - m)
        l = jnp.sum(p, axis=0, keepdims=True) + p_own
        acc = jnp.sum(p * v, axis=0, keepdims=True) + p_own * vn_ref[h]
        o_ref[h] = (acc / l) * _silu(zg_ref[h])


def sample_attn(sel, page_table, q_s, k_new, v_new, zgb, cache_k, cache_v, layer):
    n = q_s.shape[0]
    page = cache_k.shape[2]
    ppb = MOBA_BLOCK // page
    rows = MOBA_TOPK * ppb * page
    sq = pl.Squeezed()
    vec = pl.BlockSpec((sq, N_HEADS_B, 1, HEAD_B), lambda b, sel, pt: (b, 0, 0, 0))
    hbm = pl.BlockSpec(memory_space=pl.ANY)
    return pl.pallas_call(
        functools.partial(_sample_attn_kernel, layer=layer, page=page, ppb=ppb),
        grid_spec=pltpu.PrefetchScalarGridSpec(
            num_scalar_prefetch=2,
            grid=(n,),
            in_specs=[vec, vec, vec, vec, hbm, hbm],
            out_specs=vec,
            scratch_shapes=[pltpu.VMEM((2, N_HEADS_B, rows, HEAD_B), F32),
                            pltpu.VMEM((2, N_HEADS_B, rows, HEAD_B), F32),
                            pltpu.SemaphoreType.DMA((2, 2))],
        ),
        out_shape=jax.ShapeDtypeStruct((n, N_HEADS_B, 1, HEAD_B), F32),
        compiler_params=_cparams(("arbitrary",)),
        name="sample_attn",
    )(sel, page_table, q_s, k_new, v_new, zgb, cache_k, cache_v)


def _merge_out_kernel(oa_ref, ob_ref, pa_ref, pb_ref, zga_ref, zgm_ref, wo_ref, x_ref, fw_ref, y_ref):
    ya = jnp.dot(oa_ref[...], pa_ref[...], preferred_element_type=F32)
    yb = jnp.dot(ob_ref[...], pb_ref[...], preferred_element_type=F32)
    merged = (_sigmoid(zga_ref[...]) * ya + _sigmoid(zgm_ref[...]) * yb).astype(BF16)
    h = x_ref[...] + jnp.dot(merged, wo_ref[...], preferred_element_type=F32)
    ms = jnp.mean(h * h, axis=-1, keepdims=True)
    y_ref[...] = (h * lax.rsqrt(ms + NORM_EPS)) * fw_ref[...]


def merge_and_output(o_a, o_b, p_a, p_b, z_main, w_o, x, final_w):
    m = x.shape[0]
    tm = min(m, 256)
    resident = pl.Buffered(1)
    return pl.pallas_call(
        _merge_out_kernel,
        grid=(m // tm,),
        in_specs=[
            pl.BlockSpec((tm, D_A), lambda i: (i, 0)),
            pl.BlockSpec((tm, D_B), lambda i: (i, 0)),
            pl.BlockSpec((D_A, D_MODEL), lambda i: (0, 0), pipeline_mode=resident),
            pl.BlockSpec((D_B, D_MODEL), lambda i: (0, 0), pipeline_mode=resident),
            pl.BlockSpec((tm, D_MODEL), lambda i: (i, COL_GA // D_MODEL)),
            pl.BlockSpec((tm, D_MODEL), lambda i: (i, COL_GM // D_MODEL)),
            pl.BlockSpec((D_MODEL, D_MODEL), lambda i: (0, 0), pipeline_mode=resident),
            pl.BlockSpec((tm, D_MODEL), lambda i: (i, 0)),
            pl.BlockSpec((1, D_MODEL), lambda i: (0, 0)),
        ],
        out_specs=pl.BlockSpec((tm, D_MODEL), lambda i: (i, 0)),
        out_shape=jax.ShapeDtypeStruct((m, D_MODEL), F32),
        compiler_params=_cparams(("arbitrary",)),
        name="merge_and_output",
    )(o_a, o_b, p_a, p_b, z_main, z_main, w_o, x, final_w.reshape(1, -1))


def _rope_tables(pos):
    half = ROT_DIM // 2
    inv = jnp.power(jnp.float32(ROPE_THETA), -jnp.arange(half, dtype=F32) * (2.0 / ROT_DIM))
    ang = pos.astype(F32)[:, None] * inv[None, :]
    cos, sin = jnp.cos(ang), jnp.sin(ang)
    n = pos.shape[0]
    rest = HEAD_B - ROT_DIM
    cos_t = jnp.concatenate([cos, cos, jnp.ones((n, rest), F32)], axis=1)
    sin_t = jnp.concatenate([-sin, sin, jnp.zeros((n, rest), F32)], axis=1)
    return cos_t, sin_t


def _reorder_rw(a):
    r, w_lo, k, v, a_lo, g = jnp.split(
        a, [D_A, D_A + LORA, 2 * D_A + LORA, 3 * D_A + LORA, 3 * D_A + 2 * LORA], axis=-1)
    return jnp.concatenate([r, k, v, g], axis=-1), jnp.concatenate([w_lo, a_lo], axis=-1)


def kernel(x_prompt, x_sample, state_shift, state_wkv, cache_k, cache_v, page_table, ln_w, w_in, mu,
           w0, w2, a0, a2, k_k, k_a, r_k, lnx_w, lnx_b, p_a, p_b, w_o, final_w):
    depth = ln_w.shape[0]
    assert depth == 1, "single-layer trunk"
    B, T, _ = x_prompt.shape
    DB, TS, _ = x_sample.shape
    assert TS == 1
    n_pages = page_table.shape[1]
    page = cache_k.shape[2]
    past = n_pages * page
    assert past % MOBA_BLOCK == 0 and MOBA_BLOCK % page == 0
    assert past // MOBA_BLOCK >= MOBA_TOPK, "sample attention expects at least top-k full past blocks"
    l = 0
    rw_cols = 4 * D_A + 2 * LORA

    att0 = rw_cols
    src_cols = [0, D_A + LORA, 2 * D_A + LORA, 3 * D_A + 2 * LORA]
    src_cols += [att0 + 4 * D_B + i * W_TILE for i in range(2 * D_MODEL // W_TILE)]
    src_cols += [att0 + 3 * D_B, att0, att0 + D_B, att0 + 2 * D_B]
    w_main = prepare_projection_weight(w_in, l, src_cols)
    w_wa = jnp.concatenate([w_in[l, :, D_A:D_A + LORA],
                            w_in[l, :, 3 * D_A + LORA:3 * D_A + 2 * LORA]], axis=1)
    mu_rw, mu_wa = _reorder_rw(mu[l][None, :])
    zeros = jnp.zeros((LORA, D_A), F32)
    w2p = jnp.concatenate([w2[l], zeros], axis=0).astype(BF16)
    a2p = jnp.concatenate([zeros, a2[l]], axis=0).astype(BF16)
    row = lambda a: a.reshape(1, -1)
    vecs = (row(w0[l]), row(a0[l]), row(k_k[l]), row(k_a[l]), row(r_k[l]), row(lnx_w[l]), row(lnx_b[l]))
    pa_bf, pb_bf, wo_bf = p_a[l].astype(BF16), p_b[l].astype(BF16), w_o[l].astype(BF16)

    rows = jnp.concatenate([x_prompt[:, -1, :], x_sample[:, 0, :]], axis=0)
    xn_rows = rmsnorm_rows(rows, ln_w[l])
    shift_prompt = xn_rows[:B]
    xn_s = xn_rows[B:]
    xp = x_prompt.reshape(B * T, D_MODEL)
    z_p, zwa_p = input_projection(xp, ln_w[l], w_main, w_wa, normalize=True, n_cols=N_Z)
    z_s2, zwa_s2 = input_projection(jnp.concatenate([xn_s, state_shift[l]], axis=0), ln_w[l],
                                    w_main, w_wa, normalize=False, n_cols=N_MAIN)

    z_p3 = z_p.reshape(B, T, N_Z)
    o_a_p, wkv_p, kmean_s = rwkv_prompt(z_p3, zwa_p.reshape(B, T, 2 * LORA), mu_rw, mu_wa, vecs, w2p, a2p,
                                        cache_k, page_table, l)
    cos_p, sin_p = _rope_tables(jnp.arange(T))
    q_hm, k_hm, v_hm, k_rows_p, v_rows_p, kmean_p = attn_projection(xp, ln_w[l], w_main, cos_p, sin_p, B, T)
    nb_p = T // min(T, MOBA_BLOCK)
    o_b_p = moba_prompt(q_hm, k_hm, v_hm, kmean_p.reshape(B, nb_p, D_B), z_p3)
    y_prompt = merge_and_output(o_a_p.reshape(B * T, D_A), o_b_p.reshape(B * T, D_B), pa_bf, pb_bf, z_p,
                                wo_bf, xp, final_w).reshape(B, T, D_MODEL)

    z_s, zprev_s = z_s2[:DB], z_s2[DB:]
    r_s, kf_s, v_s, kk_s, b_s, w_s, g_s = rwkv_rows(
        z_s[:, :4 * D_A], zprev_s[:, :4 * D_A], zwa_s2[:DB], zwa_s2[DB:], mu_rw, mu_wa,
        vecs[0], vecs[1], vecs[2], vecs[3], w2p, a2p)
    hs = lambda a: a.reshape(-1, N_HEADS_A, 1, HEAD_A)
    col = lambda a: a.reshape(-1, N_HEADS_A, HEAD_A, 1)
    wkv_s, o_a_s = rwkv_step(state_wkv[l], hs(w_s), hs(kk_s), hs(b_s), hs(kf_s), hs(r_s),
                             col(v_s), col(g_s), hs(r_k[l])[0], col(lnx_w[l])[0], col(lnx_b[l])[0])
    o_a_s = o_a_s.reshape(DB, D_A).astype(BF16)

    cos_s, sin_s = _rope_tables(past + jnp.arange(TS))
    cos_s = jnp.broadcast_to(cos_s, (DB, HEAD_B))
    sin_s = jnp.broadcast_to(sin_s, (DB, HEAD_B))
    q_rows_s, k_rows_s = rope_rows(z_s[:, COL_Q:COL_Q + D_B], z_s[:, COL_KB:COL_KB + D_B], cos_s, sin_s)
    v_rows_s = z_s[:, COL_VB:COL_VB + D_B]
    if kmean_s is None:
        kmean_s = cache_kmean(cache_k, page_table, l)
    sel = sample_select(q_rows_s.reshape(DB, N_HEADS_B, HEAD_B), kmean_s)[:, :, :MOBA_TOPK]
    hv = lambda a: a.reshape(DB, N_HEADS_B, 1, HEAD_B)
    o_b_s = sample_attn(sel, page_table, hv(q_rows_s), hv(k_rows_s), hv(v_rows_s),
                        hv(z_s[:, COL_GB:COL_GB + D_B]), cache_k, cache_v, l)
    o_b_s = o_b_s.reshape(DB, D_B).astype(BF16)
    y_sample = merge_and_output(o_a_s, o_b_s, pa_bf, pb_bf, z_s, wo_bf, x_sample.reshape(DB, D_MODEL),
                                final_w)

    return (y_prompt,
            y_sample.reshape(DB, TS, D_MODEL),
            shift_prompt[None],
            wkv_p.reshape(1, B, N_HEADS_A, HEAD_A, HEAD_A),
            k_rows_p.reshape(1, B, T, N_HEADS_B, HEAD_B),
            v_rows_p.reshape(1, B, T, N_HEADS_B, HEAD_B),
            xn_s[None],
            wkv_s[None],
            k_rows_s.reshape(1, DB, TS, N_HEADS_B, HEAD_B),
            v_rows_s.reshape(1, DB, TS, N_HEADS_B, HEAD_B))
```

```python
import functools

import jax
import jax.numpy as jnp
from jax import lax
from jax.experimental import pallas as pl
from jax.experimental.pallas import tpu as pltpu

F32 = jnp.float32
BF16 = jnp.bfloat16

D_MODEL = 2048
D_A = D_MODEL // 2
HEAD_A = 64
N_HEADS_A = D_A // HEAD_A
LORA = 64
D_B = D_MODEL // 2
HEAD_B = 128
N_HEADS_B = D_B // HEAD_B
ROT_DIM = HEAD_B // 4
ROPE_THETA = 500000.0
MOBA_BLOCK = 256
MOBA_TOPK = 3
NORM_EPS = 1e-6
GN_EPS = 64e-5
NEG = -1e30
LOG2_E = 1.4426950408889634

COL_R, COL_K, COL_V, COL_G = 0, D_A, 2 * D_A, 3 * D_A
COL_GA = 4 * D_A
COL_GM = COL_GA + D_MODEL
COL_GB = COL_GM + D_MODEL
N_Z = COL_GB + D_B
COL_Q, COL_KB, COL_VB = N_Z, N_Z + D_B, N_Z + 2 * D_B
N_MAIN = COL_VB + D_B

GROUP = 256
HEADS_PER_GROUP = GROUP // HEAD_A
CHUNK = 64
assert CHUNK == HEAD_A
VMEM_LIMIT = 58 * 1024 * 1024


def _cparams(sem):
    return pltpu.CompilerParams(dimension_semantics=sem, vmem_limit_bytes=VMEM_LIMIT)


def _bdot(a, b):
    return jnp.dot(a.astype(BF16), b.astype(BF16), preferred_element_type=F32)


def _bdot_nt(a, b):
    return lax.dot_general(a.astype(BF16), b.astype(BF16), (((1,), (1,)), ((), ())),
                           preferred_element_type=F32)


def _bdot_tn(a, b):
    return lax.dot_general(a.astype(BF16), b.astype(BF16), (((0,), (0,)), ((), ())),
                           preferred_element_type=F32)


def _split3(x):
    hi = x.astype(BF16)
    r1 = x - hi.astype(F32)
    mid = r1.astype(BF16)
    lo = (r1 - mid.astype(F32)).astype(BF16)
    return hi, mid, lo


def _sigmoid(x):
    return 1.0 / (1.0 + jnp.exp(-x))


def _silu(x):
    return x * _sigmoid(x)


def _softplus(x):
    return jnp.maximum(x, 0.0) + jnp.log(1.0 + jnp.exp(-jnp.abs(x)))


def _rmsnorm_rows_kernel(x_ref, w_ref, o_ref):
    x = x_ref[...]
    ms = jnp.mean(x * x, axis=-1, keepdims=True)
    o_ref[...] = (x * lax.rsqrt(ms + NORM_EPS)) * w_ref[...]


def rmsnorm_rows(x, w):
    return pl.pallas_call(
        _rmsnorm_rows_kernel,
        out_shape=jax.ShapeDtypeStruct(x.shape, F32),
        name="rmsnorm_rows",
    )(x, w.reshape(1, -1))


W_TILE = 1024
LANES = 128


def _w_prep_kernel(w_hbm, o_hbm, in_buf, out_buf, in_sem, out_sem, *, layer, starts):
    n = len(starts)
    rows = in_buf.shape[1]

    def in_copy(j, slot):
        off = starts[j] % LANES
        width = W_TILE + (LANES if off else 0)
        return pltpu.make_async_copy(w_hbm.at[layer, :, pl.ds(starts[j] - off, width)],
                                     in_buf.at[slot, :, pl.ds(0, width)], in_sem.at[slot])

    def out_copy(j, slot):
        return pltpu.make_async_copy(out_buf.at[slot], o_hbm.at[:, pl.ds(j * W_TILE, W_TILE)],
                                     out_sem.at[slot])

    in_copy(0, 0).start()
    for j in range(n):
        slot = j % 2
        if j + 1 < n:
            in_copy(j + 1, 1 - slot).start()
        in_copy(j, slot).wait()
        if j >= 2:
            out_copy(j - 2, slot).wait()
        off = starts[j] % LANES
        rc = 256

        def chunk(c, _, slot=slot, off=off):
            r0 = pl.multiple_of(c * rc, rc)
            out_buf[slot, pl.ds(r0, rc), :] = in_buf[slot, pl.ds(r0, rc), off:off + W_TILE].astype(BF16)
            return 0

        lax.fori_loop(0, rows // rc, chunk, 0)
        out_copy(j, slot).start()
    for j in range(max(n - 2, 0), n):
        out_copy(j, j % 2).wait()


def prepare_projection_weight(w_in, layer, starts):
    rows = w_in.shape[1]
    assert rows % 256 == 0
    hbm = pl.BlockSpec(memory_space=pl.ANY)
    return pl.pallas_call(
        functools.partial(_w_prep_kernel, layer=layer, starts=tuple(starts)),
        in_specs=[hbm],
        out_specs=hbm,
        out_shape=jax.ShapeDtypeStruct((rows, len(starts) * W_TILE), BF16),
        scratch_shapes=[pltpu.VMEM((2, rows, W_TILE + LANES), F32),
                        pltpu.VMEM((2, rows, W_TILE), BF16),
                        pltpu.SemaphoreType.DMA((2,)), pltpu.SemaphoreType.DMA((2,))],
        compiler_params=pltpu.CompilerParams(vmem_limit_bytes=VMEM_LIMIT),
        name="prepare_projection_weight",
    )(w_in)


def _proj_kernel(x_ref, lnw_ref, w_ref, wwa_ref, z_ref, zwa_ref, xn_ref, *, normalize):
    @pl.when(pl.program_id(1) == 0)
    def _():
        x = x_ref[...]
        if normalize:
            ms = jnp.mean(x * x, axis=-1, keepdims=True)
            x = (x * lax.rsqrt(ms + NORM_EPS)) * lnw_ref[...]
        xn_ref[...] = x.astype(BF16)
        zwa_ref[...] = jnp.dot(xn_ref[...], wwa_ref[...].astype(BF16), preferred_element_type=F32)

    z_ref[...] = jnp.dot(xn_ref[...], w_ref[...], preferred_element_type=F32)


def input_projection(x, ln_w, w_main, w_wa, *, normalize, n_cols):
    m = x.shape[0]
    tm = min(m, 1024)
    if m < 1024:
        tn = 2048
    else:
        tn = 1536 if n_cols % 1536 == 0 else 1024
    assert m % tm == 0 and n_cols % tn == 0
    return pl.pallas_call(
        functools.partial(_proj_kernel, normalize=normalize),
        grid=(m // tm, n_cols // tn),
        in_specs=[
            pl.BlockSpec((tm, D_MODEL), lambda i, j: (i, 0)),
            pl.BlockSpec((1, D_MODEL), lambda i, j: (0, 0)),
            pl.BlockSpec((D_MODEL, tn), lambda i, j: (0, j)),
            pl.BlockSpec((D_MODEL, 2 * LORA), lambda i, j: (0, 0)),
        ],
        out_specs=[
            pl.BlockSpec((tm, tn), lambda i, j: (i, j)),
            pl.BlockSpec((tm, 2 * LORA), lambda i, j: (i, 0)),
        ],
        out_shape=[
            jax.ShapeDtypeStruct((m, n_cols), F32),
            jax.ShapeDtypeStruct((m, 2 * LORA), F32),
        ],
        scratch_shapes=[pltpu.VMEM((tm, D_MODEL), BF16)],
        compiler_params=_cparams(("arbitrary", "arbitrary")),
        name="input_projection",
    )(x, ln_w.reshape(1, -1), w_main, w_wa)


def _head_mask(n):
    r = lax.broadcasted_iota(jnp.int32, (n, n), 0) // HEAD_A
    c = lax.broadcasted_iota(jnp.int32, (n, n), 1) // HEAD_A
    return r == c


def _segsum(x, ones_bd):
    return jnp.dot(x.astype(BF16), ones_bd, preferred_element_type=F32)


def _rwkv_prep(zm_r, zm_k, zm_v, zm_wa, w0, a0, k_k, k_a, w2p, a2p, ones_bd):
    lane = lax.broadcasted_iota(jnp.int32, zm_wa.shape, 1)
    lora_in = jnp.where(lane < LORA, jnp.tanh(zm_wa), zm_wa)
    ww = _bdot(lora_in, w2p)
    aa = _bdot(lora_in, a2p)
    w_log = -_softplus(-(w0 + ww)) - 0.5
    logw = -jnp.exp(w_log)
    a = _sigmoid(a0 + aa)
    kk = zm_k * k_k
    ss = _segsum(kk * kk, ones_bd)
    kk = kk / jnp.maximum(jnp.sqrt(ss), 1e-12)
    kf = zm_k * (1.0 + (a - 1.0) * k_a)
    return zm_r, kf, zm_v, kk, kk * a, logw


def _rwkv_post(y, r, kf, v, zm_g, r_k, lnx_w, lnx_b, ones_bd):
    inv_n = 1.0 / HEAD_A
    mean = _segsum(y, ones_bd) * inv_n
    d = y - mean
    var = _segsum(d * d, ones_bd) * inv_n
    yn = d * lax.rsqrt(var + GN_EPS) * lnx_w + lnx_b
    bonus = _segsum(r * kf * r_k, ones_bd) * v
    return (yn + bonus) * _silu(zm_g)


def _stack_heads(x, lane_head):
    return jnp.concatenate(
        [jnp.where(lane_head == h, x, 0.0) for h in range(HEADS_PER_GROUP)], axis=0)


def _chunk_precompute(insts, consts, side_tasks=()):
    strict, incl, eye, lane_head, bd_mask = consts
    c = CHUNK
    n = len(insts)
    side_tasks = list(side_tasks)
    n_phases = 7
    per_phase = -(-len(side_tasks) // n_phases)

    def run_side(k=None):
        for _ in range(per_phase if k is None else k):
            if side_tasks:
                side_tasks.pop(0)()

    def bd(p):
        return jnp.where(bd_mask, jnp.concatenate([p] * HEADS_PER_GROUP, axis=0), 0.0).astype(BF16)

    lhs, wt, v_bd, bdkd, p_end = [], [], [], [], []
    for r, kf, v, kk, b, logw, cl in insts:
        cl_last = cl[c - 1:c, :]
        e_neg = jnp.exp(-cl)
        e_end = jnp.exp(cl_last - cl)
        lhs.append(jnp.concatenate([kk * jnp.exp(cl - logw), r * jnp.exp(cl)], axis=0).astype(BF16))
        wt.append(jnp.concatenate([_stack_heads(b * e_neg, lane_head),
                                   _stack_heads(kf * e_neg, lane_head)], axis=0).astype(BF16))
        v_bd.append(_stack_heads(v, lane_head).astype(BF16))
        bdkd.append(jnp.concatenate([b * e_end, kf * e_end], axis=0).astype(BF16))
        p_end.append(jnp.exp(cl_last))
    att = [_bdot_nt(lhs[i], wt[i]) for i in range(n)]
    run_side()
    a_ab = [jnp.where(strict, att[i][:c, :4 * c], 0.0) for i in range(n)]
    a_ak = [jnp.where(strict, att[i][:c, 4 * c:], 0.0) for i in range(n)]
    m_cat = [jnp.concatenate([jnp.where(incl, att[i][c:, :4 * c], 0.0),
                              jnp.where(incl, att[i][c:, 4 * c:], 0.0)], axis=1).astype(BF16)
             for i in range(n)]
    av = [_bdot(a_ak[i], v_bd[i]) for i in range(n)]
    run_side()

    def pair(m):
        return (rc // (2 * m) == sc_ // (2 * m)) & ((rc // m) % 2 == 1) & ((sc_ // m) % 2 == 0)

    rc = lax.broadcasted_iota(jnp.int32, (c, HEADS_PER_GROUP * c), 0)
    sc_ = lax.broadcasted_iota(jnp.int32, (c, HEADS_PER_GROUP * c), 1) % c
    x = [eye - jnp.where(pair(1), a_ab[i], 0.0) for i in range(n)]
    m = 2
    while m < c:
        mask = pair(m)
        y = [_bdot(jnp.where(mask, a_ab[i], 0.0), bd(x[i])) for i in range(n)]
        x = [x[i] - _bdot(x[i], bd(y[i])) for i in range(n)]
        run_side()
        m *= 2
    x = [x[i].astype(BF16) for i in range(n)]
    run_side(len(side_tasks))
    return [dict(lhs=lhs[i], av=av[i], t=x[i], m_cat=m_cat[i], v_bd=v_bd[i], v=insts[i][2],
                 bdkd=bdkd[i], p_end=p_end[i]) for i in range(n)]


def _chunk_state_step(states, pres, consts):
    lane_head, bd_mask = consts[3], consts[4]
    c = CHUNK
    n = len(states)
    sh = [_bdot_nt(pres[i]["lhs"], states[i]) for i in range(n)]
    rhs = [sh[i][:c] + pres[i]["av"] for i in range(n)]
    u = [-_bdot(pres[i]["t"], _stack_heads(rhs[i], lane_head)) for i in range(n)]
    y = [sh[i][c:] + _bdot(pres[i]["m_cat"],
                           jnp.concatenate([_stack_heads(u[i], lane_head).astype(BF16),
                                            pres[i]["v_bd"]], axis=0)) for i in range(n)]
    upd = [_bdot_tn(jnp.concatenate([u[i], pres[i]["v"]], axis=0), pres[i]["bdkd"]) for i in range(n)]
    new = [states[i] * pres[i]["p_end"] + jnp.where(bd_mask, upd[i], 0.0) for i in range(n)]
    return y, new


def _page_block_mean_tasks(page_refs, km_ref, pages_per_block):
    def task(j):
        def run():
            blk = jnp.sum(page_refs[j * pages_per_block][...], axis=0)
            for i in range(1, pages_per_block):
                blk = blk + jnp.sum(page_refs[j * pages_per_block + i][...], axis=0)
            km_ref[j] = blk * (1.0 / MOBA_BLOCK)
        return run
    return [task(j) for j in range(len(page_refs) // pages_per_block)]


def _rwkv_prompt_kernel(*refs, n_pages, pages_per_block):
    if n_pages:
        refs = refs[1:]
    (zr_ref, zk_ref, zv_ref, zg_ref, zwa_ref, mur_ref, muk_ref, muv_ref, mug_ref, muwa_ref,
     w0_ref, a0_ref, kk_ref, ka_ref, rk_ref, lnw_ref, lnb_ref, w2_ref, a2_ref) = refs[:19]
    page_refs = refs[19:19 + n_pages]
    refs = refs[19 + n_pages:]
    o_ref, s_out_ref = refs[:2]
    side_tasks = []
    if n_pages:
        side_tasks = _page_block_mean_tasks(page_refs, refs[2], pages_per_block)
        refs = refs[1:]
    s_ref, pr_ref, pk_ref, pv_ref, pg_ref, pwa_ref = refs[2:]
    t = pl.program_id(1)
    nseq, tr = zr_ref.shape[0], zr_ref.shape[1]

    @pl.when(t == 0)
    def _():
        for ref in (s_ref, pr_ref, pk_ref, pv_ref, pg_ref, pwa_ref):
            ref[...] = jnp.zeros_like(ref)

    def shifted(z_ref, prev_ref, mu_ref, i):
        z = z_ref[i]
        row = lax.broadcasted_iota(jnp.int32, z.shape, 0)
        prev = jnp.where(row == 0, prev_ref[i], pltpu.roll(z, 1, 0))
        prev_ref[i] = z[tr - 1:tr, :]
        return z + (prev - z) * mu_ref[...]

    bd_mask = _head_mask(GROUP)
    ones_bd = jnp.where(bd_mask, 1.0, 0.0).astype(BF16)
    ri = lax.broadcasted_iota(jnp.int32, (tr, tr), 0)
    ci = lax.broadcasted_iota(jnp.int32, (tr, tr), 1)
    tri = jnp.where((ri // CHUNK == ci // CHUNK) & (ci <= ri), 1.0, 0.0).astype(BF16)
    rc = lax.broadcasted_iota(jnp.int32, (CHUNK, GROUP), 0)
    lc = lax.broadcasted_iota(jnp.int32, (CHUNK, GROUP), 1)
    sc = lc % CHUNK
    consts = (sc < rc, sc <= rc, jnp.where(sc == rc, 1.0, 0.0), lc // HEAD_A, bd_mask)
    last = t == pl.num_programs(1) - 1

    n_ch = tr // CHUNK
    prepped, insts = [], []
    for i in range(nseq):
        zm_r = shifted(zr_ref, pr_ref, mur_ref, i)
        zm_k = shifted(zk_ref, pk_ref, muk_ref, i)
        zm_v = shifted(zv_ref, pv_ref, muv_ref, i)
        zm_g = shifted(zg_ref, pg_ref, mug_ref, i)
        zm_wa = shifted(zwa_ref, pwa_ref, muwa_ref, i)
        r, kf, v, kk, b, logw = _rwkv_prep(zm_r, zm_k, zm_v, zm_wa, w0_ref[...], a0_ref[...],
                                           kk_ref[...], ka_ref[...], w2_ref[...], a2_ref[...], ones_bd)
        cl = sum(jnp.dot(tri, piece, preferred_element_type=F32) for piece in _split3(logw))
        prepped.append((r, kf, v, zm_g))
        for ch in range(n_ch):
            sl = slice(ch * CHUNK, (ch + 1) * CHUNK)
            insts.append((r[sl], kf[sl], v[sl], kk[sl], b[sl], logw[sl], cl[sl]))
    pres = _chunk_precompute(insts, consts, side_tasks)

    states = [s_ref[i] for i in range(nseq)]
    ys = [[] for _ in range(nseq)]
    for ch in range(n_ch):
        y, states = _chunk_state_step(states, [pres[i * n_ch + ch] for i in range(nseq)], consts)
        for i in range(nseq):
            ys[i].append(y[i])
    for i in range(nseq):
        s_ref[i] = states[i]
        r, kf, v, zm_g = prepped[i]
        o_ref[i] = _rwkv_post(jnp.concatenate(ys[i], axis=0), r, kf, v, zm_g, rk_ref[...],
                              lnw_ref[...], lnb_ref[...], ones_bd).astype(o_ref.dtype)

    @pl.when(last)
    def _():
        for i in range(nseq):
            for h in range(HEADS_PER_GROUP):
                s_out_ref[i, h] = s_ref[i, h * HEAD_A:(h + 1) * HEAD_A, h * HEAD_A:(h + 1) * HEAD_A]


def rwkv_prompt(z_main, z_wa, mu_main, mu_wa, vecs, w2p, a2p, cache_k=None, page_table=None, layer=0):
    w0, a0, k_k, k_a, r_k, lnx_w, lnx_b = vecs
    batch, seq, _ = z_main.shape
    tr = min(seq, 512)
    assert seq % tr == 0 and tr % CHUNK == 0
    nt = seq // tr
    ng = D_A // GROUP

    n_pages = ppb = 0
    if cache_k is not None:
        page = cache_k.shape[2]
        ppb = MOBA_BLOCK // page
        total = page_table.size
        if total % (ng * nt) == 0 and (total // (ng * nt)) % ppb == 0:
            n_pages = total // (ng * nt)

    def zspec(col):
        return pl.BlockSpec((batch, tr, GROUP), lambda g, t, *_, c=col // GROUP: (0, t, c + g))

    def vspec(col=0):
        return pl.BlockSpec((1, GROUP), lambda g, t, *_, c=col // GROUP: (0, c + g))

    in_specs = [
        zspec(COL_R), zspec(COL_K), zspec(COL_V), zspec(COL_G),
        pl.BlockSpec((batch, tr, 2 * LORA), lambda g, t, *_: (0, t, 0)),
        vspec(COL_R), vspec(COL_K), vspec(COL_V), vspec(COL_G),
        pl.BlockSpec((1, 2 * LORA), lambda g, t, *_: (0, 0)),
        vspec(), vspec(), vspec(), vspec(), vspec(), vspec(), vspec(),
        pl.BlockSpec((2 * LORA, GROUP), lambda g, t, *_: (0, g)),
        pl.BlockSpec((2 * LORA, GROUP), lambda g, t, *_: (0, g)),
    ]
    out_specs = [
        pl.BlockSpec((batch, tr, GROUP), lambda g, t, *_: (0, t, g)),
        pl.BlockSpec((batch, HEADS_PER_GROUP, HEAD_A, HEAD_A), lambda g, t, *_: (0, g, 0, 0)),
    ]
    out_shape = [
        jax.ShapeDtypeStruct((batch, seq, D_A), BF16),
        jax.ShapeDtypeStruct((batch, N_HEADS_A, HEAD_A, HEAD_A), F32),
    ]
    operands = [z_main, z_main, z_main, z_main, z_wa,
                mu_main, mu_main, mu_main, mu_main, mu_wa,
                w0, a0, k_k, k_a, r_k, lnx_w, lnx_b, w2p, a2p]
    if n_pages:
        sq = pl.Squeezed()
        for i in range(n_pages):
            in_specs.append(pl.BlockSpec(
                (sq, sq, page, N_HEADS_B, HEAD_B),
                lambda g, t, pt, i=i: (layer, pt[(g * nt + t) * n_pages + i], 0, 0, 0)))
        operands += [cache_k] * n_pages
        out_specs.append(pl.BlockSpec((n_pages // ppb, N_HEADS_B, HEAD_B),
                                      lambda g, t, pt: (g * nt + t, 0, 0)))
        out_shape.append(jax.ShapeDtypeStruct((page_table.size // ppb, N_HEADS_B, HEAD_B), F32))
    row = lambda n: pltpu.VMEM((batch, 1, n), F32)
    scratch = [pltpu.VMEM((batch, GROUP, GROUP), F32),
               row(GROUP), row(GROUP), row(GROUP), row(GROUP), row(2 * LORA)]
    kern = functools.partial(_rwkv_prompt_kernel, n_pages=n_pages, pages_per_block=ppb)
    params = _cparams(("arbitrary", "arbitrary"))
    if not n_pages:
        o_a, wkv = pl.pallas_call(kern, grid=(ng, nt), in_specs=in_specs, out_specs=out_specs,
                                  out_shape=out_shape, scratch_shapes=scratch,
                                  compiler_params=params, name="rwkv_prompt")(*operands)
        return o_a, wkv, None
    o_a, wkv, kmean = pl.pallas_call(
        kern,
        grid_spec=pltpu.PrefetchScalarGridSpec(
            num_scalar_prefetch=1, grid=(ng, nt), in_specs=in_specs, out_specs=out_specs,
            scratch_shapes=scratch),
        out_shape=out_shape, compiler_params=params, name="rwkv_prompt",
    )(page_table.reshape(-1), *operands)
    n = page_table.shape[0]
    return o_a, wkv, kmean.reshape(n, -1, N_HEADS_B, HEAD_B)


def _rwkv_rows_kernel(z_ref, zp_ref, zwa_ref, zpwa_ref, mu_ref, muwa_ref,
                      w0_ref, a0_ref, kk_ref, ka_ref, w2_ref, a2_ref,
                      r_ref, kf_ref, v_ref, kkn_ref, b_ref, w_ref, g_ref):
    def shifted(z, zp, mu):
        return z + (zp - z) * mu

    mu = mu_ref[...]
    z = z_ref[...]
    zp = zp_ref[...]
    zm = [shifted(z[:, c:c + D_A], zp[:, c:c + D_A], mu[:, c:c + D_A])
          for c in (COL_R, COL_K, COL_V, COL_G)]
    zm_wa = shifted(zwa_ref[...], zpwa_ref[...], muwa_ref[...])
    ones_bd = jnp.where(_head_mask(D_A), 1.0, 0.0).astype(BF16)
    r, kf, v, kk, b, logw = _rwkv_prep(zm[0], zm[1], zm[2], zm_wa, w0_ref[...], a0_ref[...],
                                       kk_ref[...], ka_ref[...], w2_ref[...], a2_ref[...], ones_bd)
    r_ref[...] = r
    kf_ref[...] = kf
    kkn_ref[...] = kk
    b_ref[...] = b
    w_ref[...] = jnp.exp(logw)
    v_t = v.T
    g_t = zm[3].T
    for i in range(v_ref.shape[0]):
        v_ref[i] = v_t[:, i:i + 1]
        g_ref[i] = g_t[:, i:i + 1]


def rwkv_rows(z_rw, zp_rw, z_wa, zp_wa, mu_rw, mu_wa, w0, a0, k_k, k_a, w2p, a2p):
    n = z_rw.shape[0]
    out = jax.ShapeDtypeStruct((n, D_A), F32)
    col = jax.ShapeDtypeStruct((n, D_A, 1), F32)
    return pl.pallas_call(
        _rwkv_rows_kernel,
        out_shape=[out, out, col, out, out, out, col],
        compiler_params=pltpu.CompilerParams(vmem_limit_bytes=VMEM_LIMIT),
        name="rwkv_rows",
    )(z_rw, zp_rw, z_wa, zp_wa, mu_rw, mu_wa, w0, a0, k_k, k_a, w2p, a2p)


def _rwkv_step_kernel(s_ref, w_ref, kk_ref, b_ref, kf_ref, r_ref, v_ref, g_ref,
                      rk_ref, lnw_ref, lnb_ref, s_out_ref, o_ref):
    S = s_ref[...]
    w = w_ref[...]
    kk = kk_ref[...]
    b = b_ref[...]
    kf = kf_ref[...]
    r = r_ref[...]
    v = v_ref[...]
    sa = -jnp.sum(S * kk, axis=-1, keepdims=True)
    S = S * w + sa * b + v * kf
    s_out_ref[...] = S
    y = jnp.sum(S * r, axis=-1, keepdims=True)
    mean = jnp.mean(y, axis=1, keepdims=True)
    d = y - mean
    var = jnp.mean(d * d, axis=1, keepdims=True)
    yn = d * lax.rsqrt(var + GN_EPS) * lnw_ref[...] + lnb_ref[...]
    bonus = jnp.sum(r * kf * rk_ref[...], axis=-1, keepdims=True) * v
    o_ref[...] = (yn + bonus) * _silu(g_ref[...])


def rwkv_step(state, w, kk, b, kf, r, v, g, r_k, lnx_w, lnx_b):
    n = state.shape[0]
    per = 4 if n % 4 == 0 else 1
    h = per * N_HEADS_A
    flat = lambda a: a.reshape((n * N_HEADS_A,) + a.shape[2:])
    tile = lambda a: jnp.tile(a, (per, 1, 1))
    lane_vec = pl.BlockSpec((h, 1, HEAD_A), lambda i: (i, 0, 0))
    col_vec = pl.BlockSpec((h, HEAD_A, 1), lambda i: (i, 0, 0))
    s_new, o = pl.pallas_call(
        _rwkv_step_kernel,
        grid=(n // per,),
        in_specs=[pl.BlockSpec((h, HEAD_A, HEAD_A), lambda i: (i, 0, 0)),
                  lane_vec, lane_vec, lane_vec, lane_vec, lane_vec, col_vec, col_vec,
                  pl.BlockSpec((h, 1, HEAD_A), lambda i: (0, 0, 0)),
                  pl.BlockSpec((h, HEAD_A, 1), lambda i: (0, 0, 0)),
                  pl.BlockSpec((h, HEAD_A, 1), lambda i: (0, 0, 0))],
        out_specs=[pl.BlockSpec((h, HEAD_A, HEAD_A), lambda i: (i, 0, 0)), col_vec],
        out_shape=[jax.ShapeDtypeStruct((n * N_HEADS_A, HEAD_A, HEAD_A), F32),
                   jax.ShapeDtypeStruct((n * N_HEADS_A, HEAD_A, 1), F32)],
        compiler_params=_cparams(("arbitrary",)),
        name="rwkv_step",
    )(flat(state), flat(w), flat(kk), flat(b), flat(kf), flat(r), flat(v), flat(g),
      tile(r_k), tile(lnx_w), tile(lnx_b))
    return s_new.reshape(state.shape), o.reshape(n, N_HEADS_A, HEAD_A, 1)


def _rope(x, cos_t, sin_t, lane):
    partner = jnp.where(lane < ROT_DIM // 2, pltpu.roll(x, HEAD_B - ROT_DIM // 2, 1),
                        pltpu.roll(x, ROT_DIM // 2, 1))
    return x * cos_t + partner * sin_t


def _attn_proj_kernel(x_ref, lnw_ref, w_ref, cos_ref, sin_ref,
                      qh_ref, kh_ref, vh_ref, ko_ref, vo_ref, km_ref, xn_ref):
    tm = x_ref.shape[0]
    slab = 2 * HEAD_B

    x = x_ref[...]
    ms = jnp.mean(x * x, axis=-1, keepdims=True)
    xn_ref[...] = ((x * lax.rsqrt(ms + NORM_EPS)) * lnw_ref[...]).astype(BF16)

    cos_t = cos_ref[...]
    sin_t = sin_ref[...]
    lane = lax.broadcasted_iota(jnp.int32, cos_t.shape, 1)

    def slabs(part):
        for sidx in range(D_B // slab):
            c0 = part * D_B + sidx * slab
            z = jnp.dot(xn_ref[...], w_ref[:, c0:c0 + slab], preferred_element_type=F32)
            for hh in range(slab // HEAD_B):
                h = sidx * (slab // HEAD_B) + hh
                yield h, z[:, hh * HEAD_B:(hh + 1) * HEAD_B]

    for h, z in slabs(0):
        qh_ref[h] = _rope(z, cos_t, sin_t, lane).astype(qh_ref.dtype)

    for h, z in slabs(1):
        sl = slice(h * HEAD_B, (h + 1) * HEAD_B)
        k = _rope(z, cos_t, sin_t, lane)
        kh_ref[h] = k.astype(kh_ref.dtype)
        ko_ref[:, sl] = k
        for blk in range(km_ref.shape[0]):
            rows = tm // km_ref.shape[0]
            km_ref[blk, :, sl] = jnp.mean(k[blk * rows:(blk + 1) * rows], axis=0, keepdims=True)

    for h, z in slabs(2):
        vo_ref[:, h * HEAD_B:(h + 1) * HEAD_B] = z
        vh_ref[h] = z.astype(vh_ref.dtype)


def attn_projection(x, ln_w, w_main, cos_t, sin_t, batch, seq):
    m = batch * seq
    blk = min(seq, MOBA_BLOCK)
    tm = min(seq, 512)
    assert seq % tm == 0 and tm % blk == 0
    nt = seq // tm
    sq = pl.Squeezed()
    tab = pl.BlockSpec((tm, HEAD_B), lambda i: (i % nt, 0))
    hm = pl.BlockSpec((sq, N_HEADS_B, tm, HEAD_B), lambda i: (i // nt, 0, i % nt, 0))
    rows = pl.BlockSpec((tm, D_B), lambda i: (i, 0))
    hm_shape = jax.ShapeDtypeStruct((batch, N_HEADS_B, seq, HEAD_B), BF16)
    assert COL_Q % (3 * D_B) == 0
    return pl.pallas_call(
        _attn_proj_kernel,
        grid=(m // tm,),
        in_specs=[
            pl.BlockSpec((tm, D_MODEL), lambda i: (i, 0)),
            pl.BlockSpec((1, D_MODEL), lambda i: (0, 0)),
            pl.BlockSpec((D_MODEL, 3 * D_B), lambda i: (0, COL_Q // (3 * D_B)),
                         pipeline_mode=pl.Buffered(1)),
            tab, tab,
        ],
        out_specs=[hm, hm, hm, rows, rows,
                   pl.BlockSpec((tm // blk, 1, D_B), lambda i: (i, 0, 0))],
        out_shape=[hm_shape, hm_shape, hm_shape,
                   jax.ShapeDtypeStruct((m, D_B), F32), jax.ShapeDtypeStruct((m, D_B), F32),
                   jax.ShapeDtypeStruct((m // blk, 1, D_B), F32)],
        scratch_shapes=[pltpu.VMEM((tm, D_MODEL), BF16)],
        compiler_params=_cparams(("arbitrary",)),
        name="attn_projection",
    )(x, ln_w.reshape(1, -1), w_main, cos_t, sin_t)


def _rope_rows_kernel(zq_ref, zk_ref, cos_ref, sin_ref, q_ref, k_ref):
    cos_t = cos_ref[...]
    sin_t = sin_ref[...]
    lane = lax.broadcasted_iota(jnp.int32, cos_t.shape, 1)
    for h in range(N_HEADS_B):
        sl = slice(h * HEAD_B, (h + 1) * HEAD_B)
        q_ref[:, sl] = _rope(zq_ref[:, sl], cos_t, sin_t, lane).astype(BF16).astype(F32)
        k_ref[:, sl] = _rope(zk_ref[:, sl], cos_t, sin_t, lane)


def rope_rows(zq, zk, cos_t, sin_t):
    out = jax.ShapeDtypeStruct(zq.shape, F32)
    return pl.pallas_call(_rope_rows_kernel, out_shape=[out, out], name="rope_rows")(zq, zk, cos_t, sin_t)


MOBA_HEADS_PER_STEP = 8


def _moba_prompt_kernel(q_ref, k_ref, v_ref, km_ref, zg_ref, o_ref, sel_ref):
    qb = pl.program_id(1)
    nbatch, nhead, tq = q_ref.shape[0], q_ref.shape[1], q_ref.shape[2]
    nb = km_ref.shape[1]
    seqs = [(b, h) for b in range(nbatch) for h in range(nhead)]
    exp2_scale = HEAD_B ** -0.5 * LOG2_E
    qs = [q_ref[b, h].T for b, h in seqs]

    blk = lax.broadcasted_iota(jnp.int32, (nb, tq), 0)
    for i, (b, h) in enumerate(seqs):
        km = km_ref[b, :, h * HEAD_B:(h + 1) * HEAD_B]
        gate = sum(jnp.dot(piece, qs[i], preferred_element_type=F32)
                   for piece in _split3(km))
        gate = jnp.where(blk < qb, gate, -jnp.inf)
        for n in range(nb):
            g_n = gate[n:n + 1, :]
            tie = jnp.where(blk < n, 1.0, 0.0)
            beats = jnp.where(gate > g_n, 1.0, jnp.where(gate == g_n, tie, 0.0))
            cnt = jnp.sum(beats, axis=0, keepdims=True)
            sel_ref[i, n:n + 1, :] = jnp.where(cnt < MOBA_TOPK, 1.0, 0.0)

    def scores(i, n):
        b, h = seqs[i]
        start = pl.multiple_of(n * tq, tq)
        return jnp.dot(k_ref[b, h, pl.ds(start, tq), :], qs[i], preferred_element_type=F32)

    def softmax_step(s, m, l):
        m_new = jnp.maximum(m, jnp.max(s, axis=0, keepdims=True))
        alpha = jnp.exp2((m - m_new) * exp2_scale)
        p = jnp.exp2((s - m_new) * exp2_scale)
        return m_new, alpha, alpha * l + jnp.sum(p, axis=0, keepdims=True), p.astype(BF16)

    def pv(i, n, p):
        b, h = seqs[i]
        start = pl.multiple_of(n * tq, tq)
        return _bdot_tn(v_ref[b, h, pl.ds(start, tq), :], p)

    def step(n, masked, carry):
        sm = [softmax_step(masked[i], carry[i][0], carry[i][1]) for i in range(len(seqs))]
        acc = [sm[i][1] * carry[i][2] + pv(i, n, sm[i][3]) for i in range(len(seqs))]
        return [(sm[i][0], sm[i][2], acc[i]) for i in range(len(seqs))]

    def body(n, carry):
        s = [scores(i, n) for i in range(len(seqs))]
        masked = [jnp.where(sel_ref[i, pl.ds(n, 1), :] > 0.0, s[i], NEG)
                  for i in range(len(seqs))]
        return tuple(step(n, masked, carry))

    init = tuple((jnp.full((1, tq), NEG, F32), jnp.zeros((1, tq), F32),
                  jnp.zeros((HEAD_B, tq), F32)) for i in range(len(seqs)))
    carry = lax.fori_loop(0, qb, body, init)
    ki = lax.broadcasted_iota(jnp.int32, (tq, tq), 0)
    qi = lax.broadcasted_iota(jnp.int32, (tq, tq), 1)
    masked = [jnp.where(ki <= qi, scores(i, qb), NEG) for i in range(len(seqs))]
    final = step(qb, masked, list(carry))
    for i, (b, h) in enumerate(seqs):
        _, l, acc = final[i]
        sl = slice(h * HEAD_B, (h + 1) * HEAD_B)
        o_ref[b, :, sl] = ((acc / l).T * _silu(zg_ref[b, :, sl])).astype(o_ref.dtype)


def moba_prompt(q_hm, k_hm, v_hm, kmean, z_main):
    batch, _, seq, _ = k_hm.shape
    tq = min(seq, MOBA_BLOCK)
    nb = seq // tq
    hps = MOBA_HEADS_PER_STEP
    w = hps * HEAD_B
    return pl.pallas_call(
        _moba_prompt_kernel,
        grid=(N_HEADS_B // hps, nb),
        in_specs=[
            pl.BlockSpec((batch, hps, tq, HEAD_B), lambda h, i: (0, h, i, 0)),
            pl.BlockSpec((batch, hps, seq, HEAD_B), lambda h, i: (0, h, 0, 0),
                         pipeline_mode=pl.Buffered(1)),
            pl.BlockSpec((batch, hps, seq, HEAD_B), lambda h, i: (0, h, 0, 0),
                         pipeline_mode=pl.Buffered(1)),
            pl.BlockSpec((batch, nb, w), lambda h, i: (0, 0, h)),
            pl.BlockSpec((batch, tq, w), lambda h, i: (0, i, COL_GB // w + h)),
        ],
        out_specs=pl.BlockSpec((batch, tq, w), lambda h, i: (0, i, h)),
        out_shape=jax.ShapeDtypeStruct((batch, seq, D_B), BF16),
        scratch_shapes=[pltpu.VMEM((batch * hps, nb, tq), F32)],
        compiler_params=_cparams(("arbitrary", "arbitrary")),
        name="moba_prompt",
    )(q_hm, k_hm, v_hm, kmean, z_main)


KMEAN_PAGES = 8


def _cache_kmean_kernel(pt_ref, *refs, pages_per_block):
    del pt_ref
    for task in _page_block_mean_tasks(refs[:-1], refs[-1], pages_per_block):
        task()


def cache_kmean(cache_k, page_table, layer):
    n, n_pages = page_table.shape
    page = cache_k.shape[2]
    ppb = MOBA_BLOCK // page
    pps = min(KMEAN_PAGES, n_pages)
    assert n_pages % pps == 0 and pps % ppb == 0
    sq = pl.Squeezed()
    page_spec = lambda i: pl.BlockSpec(
        (sq, sq, page, N_HEADS_B, HEAD_B), lambda b, s, pt, i=i: (layer, pt[b, s * pps + i], 0, 0, 0))
    return pl.pallas_call(
        functools.partial(_cache_kmean_kernel, pages_per_block=ppb),
        grid_spec=pltpu.PrefetchScalarGridSpec(
            num_scalar_prefetch=1,
            grid=(n, n_pages // pps),
            in_specs=[page_spec(i) for i in range(pps)],
            out_specs=pl.BlockSpec((sq, pps // ppb, N_HEADS_B, HEAD_B), lambda b, s, pt: (b, s, 0, 0)),
        ),
        out_shape=jax.ShapeDtypeStruct((n, n_pages // ppb, N_HEADS_B, HEAD_B), F32),
        compiler_params=_cparams(("arbitrary", "arbitrary")),
        name="cache_kmean",
    )(page_table, *([cache_k] * pps))


def _sample_select_kernel(q_ref, km_ref, sel_ref):
    nb = km_ref.shape[0]
    ri = lax.broadcasted_iota(jnp.int32, (nb, nb), 0)
    ci = lax.broadcasted_iota(jnp.int32, (nb, nb), 1)
    lane = lax.broadcasted_iota(jnp.int32, (1, 128), 1)
    blk_row = lax.broadcasted_iota(jnp.int32, (1, nb), 1).astype(F32)
    for h in range(N_HEADS_B):
        km = km_ref[:, h, :]
        g_col = jnp.sum(km * q_ref[h:h + 1, :], axis=1, keepdims=True)
        g_row = jnp.sum(jnp.where(ri == ci, g_col, 0.0), axis=0, keepdims=True)
        beats = (g_col > g_row) | ((g_col == g_row) & (ri < ci))
        rank = jnp.sum(jnp.where(beats, 1.0, 0.0), axis=0, keepdims=True)
        out = jnp.zeros((1, 128), F32)
        for r in range(MOBA_TOPK):
            idx = jnp.sum(jnp.where(rank == float(r), blk_row, 0.0), axis=1, keepdims=True)
            out = jnp.where(lane == r, idx, out)
        sel_ref[h:h + 1, :] = out.astype(jnp.int32)


def sample_select(q_s, kmean_s):
    n, nb = kmean_s.shape[:2]
    sq = pl.Squeezed()
    return pl.pallas_call(
        _sample_select_kernel,
        grid=(n,),
        in_specs=[pl.BlockSpec((sq, N_HEADS_B, HEAD_B), lambda b: (b, 0, 0)),
                  pl.BlockSpec((sq, nb, N_HEADS_B, HEAD_B), lambda b: (b, 0, 0, 0))],
        out_specs=pl.BlockSpec((sq, N_HEADS_B, 128), lambda b: (b, 0, 0)),
        out_shape=jax.ShapeDtypeStruct((n, N_HEADS_B, 128), jnp.int32),
        compiler_params=_cparams(("arbitrary",)),
        name="sample_select",
    )(q_s, kmean_s)


def _sample_attn_kernel(sel_ref, pt_ref, q_ref, kn_ref, vn_ref, zg_ref, ck_ref, cv_ref, o_ref,
                        kbuf, vbuf, sem, *, layer, page, ppb):
    b = pl.program_id(0)
    nb_steps = pl.num_programs(0)
    n_sel = MOBA_TOPK * ppb
    scale = HEAD_B ** -0.5

    def copies(bb, slot):
        out = []
        for h in range(N_HEADS_B):
            for j in range(n_sel):
                pg = pt_ref[bb, sel_ref[bb, h, j // ppb] * ppb + j % ppb]
                dst = pl.ds(j * page, page)
                out.append(pltpu.make_async_copy(ck_ref.at[layer, pg, :, h, :],
                                                 kbuf.at[slot, h, dst, :], sem.at[slot, 0]))
                out.append(pltpu.make_async_copy(cv_ref.at[layer, pg, :, h, :],
                                                 vbuf.at[slot, h, dst, :], sem.at[slot, 1]))
        return out

    slot = b % 2

    @pl.when(b == 0)
    def _():
        for c in copies(0, 0):
            c.start()

    @pl.when(b + 1 < nb_steps)
    def _():
        for c in copies(b + 1, 1 - slot):
            c.start()

    for c in copies(b, slot):
        c.wait()

    for h in range(N_HEADS_B):
        q = q_ref[h]
        k = kbuf[slot, h]
        v = vbuf[slot, h]
        s = jnp.sum(k * q, axis=1, keepdims=True) * scale
        s_own = jnp.sum(kn_ref[h] * q, axis=1, keepdims=True) * scale
        m = jnp.maximum(jnp.max(s, axis=0, keepdims=True), s_own)
        p = jnp.exp(s - m)
        p_own = jnp.exp(s_own - m)
        l = jnp.sum(p, axis=0, keepdims=True) + p_own
        acc = jnp.sum(p * v, axis=0, keepdims=True) + p_own * vn_ref[h]
        o_ref[h] = (acc / l) * _silu(zg_ref[h])


def sample_attn(sel, page_table, q_s, k_new, v_new, zgb, cache_k, cache_v, layer):
    n = q_s.shape[0]
    page = cache_k.shape[2]
    ppb = MOBA_BLOCK // page
    rows = MOBA_TOPK * ppb * page
    sq = pl.Squeezed()
    vec = pl.BlockSpec((sq, N_HEADS_B, 1, HEAD_B), lambda b, sel, pt: (b, 0, 0, 0))
    hbm = pl.BlockSpec(memory_space=pl.ANY)
    return pl.pallas_call(
        functools.partial(_sample_attn_kernel, layer=layer, page=page, ppb=ppb),
        grid_spec=pltpu.PrefetchScalarGridSpec(
            num_scalar_prefetch=2,
            grid=(n,),
            in_specs=[vec, vec, vec, vec, hbm, hbm],
            out_specs=vec,
            scratch_shapes=[pltpu.VMEM((2, N_HEADS_B, rows, HEAD_B), F32),
                            pltpu.VMEM((2, N_HEADS_B, rows, HEAD_B), F32),
                            pltpu.SemaphoreType.DMA((2, 2))],
        ),
        out_shape=jax.ShapeDtypeStruct((n, N_HEADS_B, 1, HEAD_B), F32),
        compiler_params=_cparams(("arbitrary",)),
        name="sample_attn",
    )(sel, page_table, q_s, k_new, v_new, zgb, cache_k, cache_v)


def _merge_out_kernel(oa_ref, ob_ref, pa_ref, pb_ref, zga_ref, zgm_ref, wo_ref, x_ref, fw_ref, y_ref):
    ya = jnp.dot(oa_ref[...], pa_ref[...], preferred_element_type=F32)
    yb = jnp.dot(ob_ref[...], pb_ref[...], preferred_element_type=F32)
    merged = (_sigmoid(zga_ref[...]) * ya + _sigmoid(zgm_ref[...]) * yb).astype(BF16)
    h = x_ref[...] + jnp.dot(merged, wo_ref[...], preferred_element_type=F32)
    ms = jnp.mean(h * h, axis=-1, keepdims=True)
    y_ref[...] = (h * lax.rsqrt(ms + NORM_EPS)) * fw_ref[...]


def merge_and_output(o_a, o_b, p_a, p_b, z_main, w_o, x, final_w):
    m = x.shape[0]
    tm = min(m, 256)
    resident = pl.Buffered(1)
    return pl.pallas_call(
        _merge_out_kernel,
        grid=(m // tm,),
        in_specs=[
            pl.BlockSpec((tm, D_A), lambda i: (i, 0)),
            pl.BlockSpec((tm, D_B), lambda i: (i, 0)),
            pl.BlockSpec((D_A, D_MODEL), lambda i: (0, 0), pipeline_mode=resident),
            pl.BlockSpec((D_B, D_MODEL), lambda i: (0, 0), pipeline_mode=resident),
            pl.BlockSpec((tm, D_MODEL), lambda i: (i, COL_GA // D_MODEL)),
            pl.BlockSpec((tm, D_MODEL), lambda i: (i, COL_GM // D_MODEL)),
            pl.BlockSpec((D_MODEL, D_MODEL), lambda i: (0, 0), pipeline_mode=resident),
            pl.BlockSpec((tm, D_MODEL), lambda i: (i, 0)),
            pl.BlockSpec((1, D_MODEL), lambda i: (0, 0)),
        ],
        out_specs=pl.BlockSpec((tm, D_MODEL), lambda i: (i, 0)),
        out_shape=jax.ShapeDtypeStruct((m, D_MODEL), F32),
        compiler_params=_cparams(("arbitrary",)),
        name="merge_and_output",
    )(o_a, o_b, p_a, p_b, z_main, z_main, w_o, x, final_w.reshape(1, -1))


def _rope_tables(pos):
    half = ROT_DIM // 2
    inv = jnp.power(jnp.float32(ROPE_THETA), -jnp.arange(half, dtype=F32) * (2.0 / ROT_DIM))
    ang = pos.astype(F32)[:, None] * inv[None, :]
    cos, sin = jnp.cos(ang), jnp.sin(ang)
    n = pos.shape[0]
    rest = HEAD_B - ROT_DIM
    cos_t = jnp.concatenate([cos, cos, jnp.ones((n, rest), F32)], axis=1)
    sin_t = jnp.concatenate([-sin, sin, jnp.zeros((n, rest), F32)], axis=1)
    return cos_t, sin_t


def _reorder_rw(a):
    r, w_lo, k, v, a_lo, g = jnp.split(
        a, [D_A, D_A + LORA, 2 * D_A + LORA, 3 * D_A + LORA, 3 * D_A + 2 * LORA], axis=-1)
    return jnp.concatenate([r, k, v, g], axis=-1), jnp.concatenate([w_lo, a_lo], axis=-1)


def kernel(x_prompt, x_sample, state_shift, state_wkv, cache_k, cache_v, page_table, ln_w, w_in, mu,
           w0, w2, a0, a2, k_k, k_a, r_k, lnx_w, lnx_b, p_a, p_b, w_o, final_w):
    depth = ln_w.shape[0]
    assert depth == 1, "single-layer trunk"
    B, T, _ = x_prompt.shape
    DB, TS, _ = x_sample.shape
    assert TS == 1
    n_pages = page_table.shape[1]
    page = cache_k.shape[2]
    past = n_pages * page
    assert past % MOBA_BLOCK == 0 and MOBA_BLOCK % page == 0
    assert past // MOBA_BLOCK >= MOBA_TOPK, "sample attention expects at least top-k full past blocks"
    l = 0
    rw_cols = 4 * D_A + 2 * LORA

    att0 = rw_cols
    src_cols = [0, D_A + LORA, 2 * D_A + LORA, 3 * D_A + 2 * LORA]
    src_cols += [att0 + 4 * D_B + i * W_TILE for i in range(2 * D_MODEL // W_TILE)]
    src_cols += [att0 + 3 * D_B, att0, att0 + D_B, att0 + 2 * D_B]
    w_main = prepare_projection_weight(w_in, l, src_cols)
    w_wa = jnp.concatenate([w_in[l, :, D_A:D_A + LORA],
                            w_in[l, :, 3 * D_A + LORA:3 * D_A + 2 * LORA]], axis=1)
    mu_rw, mu_wa = _reorder_rw(mu[l][None, :])
    zeros = jnp.zeros((LORA, D_A), F32)
    w2p = jnp.concatenate([w2[l], zeros], axis=0).astype(BF16)
    a2p = jnp.concatenate([zeros, a2[l]], axis=0).astype(BF16)
    row = lambda a: a.reshape(1, -1)
    vecs = (row(w0[l]), row(a0[l]), row(k_k[l]), row(k_a[l]), row(r_k[l]), row(lnx_w[l]), row(lnx_b[l]))
    pa_bf, pb_bf, wo_bf = p_a[l].astype(BF16), p_b[l].astype(BF16), w_o[l].astype(BF16)

    rows = jnp.concatenate([x_prompt[:, -1, :], x_sample[:, 0, :]], axis=0)
    xn_rows = rmsnorm_rows(rows, ln_w[l])
    shift_prompt = xn_rows[:B]
    xn_s = xn_rows[B:]
    xp = x_prompt.reshape(B * T, D_MODEL)
    z_p, zwa_p = input_projection(xp, ln_w[l], w_main, w_wa, normalize=True, n_cols=N_Z)
    z_s2, zwa_s2 = input_projection(jnp.concatenate([xn_s, state_shift[l]], axis=0), ln_w[l],
                                    w_main, w_wa, normalize=False, n_cols=N_MAIN)

    z_p3 = z_p.reshape(B, T, N_Z)
    o_a_p, wkv_p, kmean_s = rwkv_prompt(z_p3, zwa_p.reshape(B, T, 2 * LORA), mu_rw, mu_wa, vecs, w2p, a2p,
                                        cache_k, page_table, l)
    cos_p, sin_p = _rope_tables(jnp.arange(T))
    q_hm, k_hm, v_hm, k_rows_p, v_rows_p, kmean_p = attn_projection(xp, ln_w[l], w_main, cos_p, sin_p, B, T)
    nb_p = T // min(T, MOBA_BLOCK)
    o_b_p = moba_prompt(q_hm, k_hm, v_hm, kmean_p.reshape(B, nb_p, D_B), z_p3)
    y_prompt = merge_and_output(o_a_p.reshape(B * T, D_A), o_b_p.reshape(B * T, D_B), pa_bf, pb_bf, z_p,
                                wo_bf, xp, final_w).reshape(B, T, D_MODEL)

    z_s, zprev_s = z_s2[:DB], z_s2[DB:]
    r_s, kf_s, v_s, kk_s, b_s, w_s, g_s = rwkv_rows(
        z_s[:, :4 * D_A], zprev_s[:, :4 * D_A], zwa_s2[:DB], zwa_s2[DB:], mu_rw, mu_wa,
        vecs[0], vecs[1], vecs[2], vecs[3], w2p, a2p)
    hs = lambda a: a.reshape(-1, N_HEADS_A, 1, HEAD_A)
    col = lambda a: a.reshape(-1, N_HEADS_A, HEAD_A, 1)
    wkv_s, o_a_s = rwkv_step(state_wkv[l], hs(w_s), hs(kk_s), hs(b_s), hs(kf_s), hs(r_s),
                             col(v_s), col(g_s), hs(r_k[l])[0], col(lnx_w[l])[0], col(lnx_b[l])[0])
    o_a_s = o_a_s.reshape(DB, D_A).astype(BF16)

    cos_s, sin_s = _rope_tables(past + jnp.arange(TS))
    cos_s = jnp.broadcast_to(cos_s, (DB, HEAD_B))
    sin_s = jnp.broadcast_to(sin_s, (DB, HEAD_B))
    q_rows_s, k_rows_s = rope_rows(z_s[:, COL_Q:COL_Q + D_B], z_s[:, COL_KB:COL_KB + D_B], cos_s, sin_s)
    v_rows_s = z_s[:, COL_VB:COL_VB + D_B]
    if kmean_s is None:
        kmean_s = cache_kmean(cache_k, page_table, l)
    sel = sample_select(q_rows_s.reshape(DB, N_HEADS_B, HEAD_B), kmean_s)[:, :, :MOBA_TOPK]
    hv = lambda a: a.reshape(DB, N_HEADS_B, 1, HEAD_B)
    o_b_s = sample_attn(sel, page_table, hv(q_rows_s), hv(k_rows_s), hv(v_rows_s),
                        hv(z_s[:, COL_GB:COL_GB + D_B]), cache_k, cache_v, l)
    o_b_s = o_b_s.reshape(DB, D_B).astype(BF16)
    y_sample = merge_and_output(o_a_s, o_b_s, pa_bf, pb_bf, z_s, wo_bf, x_sample.reshape(DB, D_MODEL),
                                final_w)

    return (y_prompt,
            y_sample.reshape(DB, TS, D_MODEL),
            shift_prompt[None],
            wkv_p.reshape(1, B, N_HEADS_A, HEAD_A, HEAD_A),
            k_rows_p.reshape(1, B, T, N_HEADS_B, HEAD_B),
            v_rows_p.reshape(1, B, T, N_HEADS_B, HEAD_B),
            xn_s[None],
            wkv_s[None],
            k_rows_s.reshape(1, DB, TS, N_HEADS_B, HEAD_B),
            v_rows_s.reshape(1, DB, TS, N_HEADS_B, HEAD_B))
```
